```python
import math
import jax
import jax.numpy as jnp
from jax import lax
import numpy as np

D_MODEL = 1024
BATCH = 1
SEQ = 16384
DEPTH = 2

GRID_W = 64
CTX_LEN = 256
HEAD_DIM = 64

NA_HEADS = 8
NA_WIDTH = NA_HEADS * HEAD_DIM
NA_KH = 8
NA_KW = 16
HY_WIDTH = D_MODEL - NA_WIDTH
HY_SHORT = 3
HY_BANDS = 16
HY_EMB = 1 + 2 * HY_BANDS
HY_FILT_HID = 64
HY_DECAY_PCT_MIN = 0.3
HY_DECAY_PCT_MAX = 1.5
HY_DECAY_TARGET = 1e-2
AB_IN = 3 * NA_WIDTH + 3 * HY_WIDTH

GQA_HEADS = 8
GQA_KV_HEADS = 2
GQA_WIDTH = GQA_HEADS * HEAD_DIM
GQA_KV_WIDTH = GQA_KV_HEADS * HEAD_DIM
Q_BLOCK = 128
ROPE_THETA = 10000.0
S5_WIDTH = D_MODEL - GQA_WIDTH
S5_GROUP = 16
S5_GROUPS = S5_WIDTH // S5_GROUP
S5_STATE = 64
CD_IN = GQA_WIDTH + 2 * GQA_KV_WIDTH + S5_WIDTH

N_EXPERTS = 256
TOP_K = 8
N_EXPERT_GROUPS = 8
TOPK_GROUPS = 4
EXPERT_FF = 256
ROUTED_SCALE = 2.5
EXPERT_BLOCK = 128

N_AB = (DEPTH + 1) // 2
N_CD = DEPTH // 2
DEEPNORM_ALPHA = (2.0 * DEPTH) ** 0.25
DEEPNORM_BETA = (8.0 * DEPTH) ** -0.25
LN_EPS = 1e-5
RMS_EPS = 1e-6

kernel_name = 'hybrid_na_hyena_s5_gqa_moe_dit'


def layer_norm(x, g, b):
    xf = x.astype(jnp.float32)
    mu = jnp.mean(xf, axis=-1, keepdims=True)
    var = jnp.mean(jnp.square(xf - mu), axis=-1, keepdims=True)
    return ((xf - mu) * lax.rsqrt(var + LN_EPS) * g + b).astype(x.dtype)


def rms_norm(x, g):
    xf = x.astype(jnp.float32)
    return (xf * lax.rsqrt(jnp.mean(xf * xf, axis=-1, keepdims=True) + RMS_EPS) * g).astype(x.dtype)


def split_heads(t, n_heads):
    return t.reshape(t.shape[:-1] + (n_heads, HEAD_DIM))


def modulation(cvec, w_mod, b_mod):
    return jnp.split(jax.nn.silu(cvec) @ w_mod + b_mod, 6, axis=-1)


def grid_positions(l):
    t = jnp.arange(l)
    return (t // GRID_W).astype(jnp.float32), (t % GRID_W).astype(jnp.float32)


def rope_2d(x, row, col):
    half = x.shape[-1] // 2
    inv_freq = ROPE_THETA ** (-jnp.arange(0, half, 2, dtype=jnp.float32) / half)

    def rotate(xp, pos):
        ang = pos[:, None] * inv_freq[None]
        cos = jnp.cos(ang)[None, :, None, :]
        sin = jnp.sin(ang)[None, :, None, :]
        x1 = xp[..., 0::2].astype(jnp.float32)
        x2 = xp[..., 1::2].astype(jnp.float32)
        return jnp.stack([x1 * cos - x2 * sin, x1 * sin + x2 * cos], axis=-1).reshape(xp.shape).astype(x.dtype)

    return jnp.concatenate([rotate(x[..., :half], row), rotate(x[..., half:], col)], axis=-1)


def context_attention(q, k, v):
    b, nq, hq, dh = q.shape
    hkv = k.shape[2]
    qg = q.reshape(b, nq, hkv, hq // hkv, dh) * dh ** -0.5
    s = jnp.einsum('bqgrd,bkgd->bgrqk', qg, k)
    p = jax.nn.softmax(s.astype(jnp.float32), axis=-1).astype(v.dtype)
    return jnp.einsum('bgrqk,bkgd->bqgrd', p, v).reshape(b, nq, hq * dh)


def neighbourhood_attention(q, k, v, k_ctx, v_ctx, rpb):
    b, l, h, dh = q.shape
    rows = l // GRID_W
    kh = min(NA_KH, rows)
    kw = NA_KW
    qg = q.reshape(b, rows, GRID_W, h, dh) * dh ** -0.5
    kg = k.reshape(b, rows, GRID_W, h, dh)
    vg = v.reshape(b, rows, GRID_W, h, dh)
    cols = jnp.arange(GRID_W)
    col_idx = jnp.clip(cols - kw // 2, 0, GRID_W - kw)[:, None] + jnp.arange(kw)[None]
    col_bias = rpb[:, :, col_idx - cols[:, None] + NA_KW - 1]
    n_win = kh * kw

    def query_row(args):
        r, q_r = args
        rs = jnp.clip(r - kh // 2, 0, rows - kh)
        k_rows = lax.dynamic_slice_in_dim(kg, rs, kh, axis=1)
        v_rows = lax.dynamic_slice_in_dim(vg, rs, kh, axis=1)
        k_win = k_rows[:, :, col_idx]
        v_win = v_rows[:, :, col_idx]
        bias = col_bias[:, rs + jnp.arange(kh) - r + NA_KH - 1]
        s_win = jnp.einsum('bchd,bicjhd->bhcij', q_r, k_win) + jnp.transpose(bias, (0, 2, 1, 3))[None]
        s_ctx = jnp.einsum('bchd,bnhd->bhcn', q_r, k_ctx)
        s = jnp.concatenate([s_win.reshape(b, h, GRID_W, n_win), s_ctx.astype(s_win.dtype)], axis=-1)
        p = jax.nn.softmax(s.astype(jnp.float32), axis=-1).astype(v.dtype)
        p_win = p[..., :n_win].reshape(b, h, GRID_W, kh, kw)
        return (jnp.einsum('bhcij,bicjhd->bchd', p_win, v_win)
                + jnp.einsum('bhcn,bnhd->bchd', p[..., n_win:], v_ctx))

    out = lax.map(query_row, (jnp.arange(rows), jnp.moveaxis(qg, 1, 0)))
    return jnp.moveaxis(out, 0, 1).reshape(b, l, h * dh)


def short_conv(u, w, bias):
    l = u.shape[1]
    pad = HY_SHORT // 2
    up = jnp.pad(u, ((0, 0), (pad, HY_SHORT - 1 - pad), (0, 0)))
    return sum(up[:, j:j + l] * w[j] for j in range(HY_SHORT)) + bias


def implicit_filter(l, f_w1, f_b1, f_freq, f_w2, f_b2, f_w3):
    pos = jnp.arange(l, dtype=jnp.float32)
    t = pos / max(l - 1, 1)
    bands = jnp.linspace(1e-4, HY_BANDS - 1, HY_BANDS, dtype=jnp.float32)
    ang = (2.0 * math.pi / l) * pos[:, None] * bands[None]
    z = jnp.concatenate([t[:, None], jnp.cos(ang), -jnp.sin(ang)], axis=-1)
    hid = jnp.sin(f_freq * (z @ f_w1 + f_b1))
    hid = jnp.sin(f_freq * (hid @ f_w2 + f_b2))
    taps = (hid @ f_w3).astype(jnp.float32)
    deltas = jnp.abs(jnp.linspace(math.log(HY_DECAY_TARGET) / HY_DECAY_PCT_MAX,
                                  math.log(HY_DECAY_TARGET) / HY_DECAY_PCT_MIN, HY_WIDTH, dtype=jnp.float32))
    window = jnp.exp(-t[:, None] * deltas[None])
    fwd = taps[:, :HY_WIDTH] * window
    bwd = taps[:, HY_WIDTH:] * window
    filt2 = jnp.concatenate([fwd, jnp.zeros((1, HY_WIDTH), jnp.float32), bwd[:0:-1]], axis=0)
    return filt2 / jnp.sum(jnp.abs(filt2), axis=0, keepdims=True)


def long_conv(u, filt2):
    l = u.shape[1]
    u_f = jnp.fft.rfft(u.astype(jnp.float32), n=2 * l, axis=1)
    f_f = jnp.fft.rfft(filt2, n=2 * l, axis=0)
    return jnp.fft.irfft(u_f * f_f[None], n=2 * l, axis=1)[:, :l].astype(u.dtype)


def hyena(u, conv_w, conv_b, f_w1, f_b1, f_freq, f_w2, f_b2, f_w3, skip):
    l = u.shape[1]
    x0, x1, v = jnp.split(short_conv(u, conv_w, conv_b), 3, axis=-1)
    z = x1 * v
    z = long_conv(z, implicit_filter(l, f_w1, f_b1, f_freq, f_w2, f_b2, f_w3)) + z * skip
    return x0 * z


def _ssm_combine(left, right):
    a_l, b_l = left
    a_r, b_r = right
    return a_r * a_l, a_r * b_l + b_r


def s5_scan(u, a_bar, b_bar, reverse, s0):
    bu = jnp.einsum('gpk,blgk->blgp', b_bar, u.astype(jnp.complex64))
    if s0 is not None:
        bu = bu.at[:, -1 if reverse else 0].add(a_bar * s0)
    a = jnp.broadcast_to(a_bar, bu.shape)
    _, s = lax.associative_scan(_ssm_combine, (a, bu), axis=1, reverse=reverse)
    return s


def s5_bidirectional(u_ctx, u_lat, a_re, a_im, log_dt, b_re, b_im, c_re, c_im, d_skip, glu_w, glu_b, need_ctx):
    lam = lax.complex(a_re.astype(jnp.float32), a_im.astype(jnp.float32))
    dt = jnp.exp(log_dt.astype(jnp.float32))[..., None]
    a_bar = jnp.exp(lam * dt)
    b_bar = ((a_bar - 1.0) / lam)[..., None] * lax.complex(b_re.astype(jnp.float32), b_im.astype(jnp.float32))
    c_mat = lax.complex(c_re.astype(jnp.float32), c_im.astype(jnp.float32))

    def groups(u):
        return u.reshape(u.shape[:2] + (S5_GROUPS, S5_GROUP)).astype(jnp.float32)

    def readout(u, s_fwd, s_bwd):
        y = (jnp.real(jnp.einsum('gkp,blgp->blgk', c_mat[0], s_fwd))
             + jnp.real(jnp.einsum('gkp,blgp->blgk', c_mat[1], s_bwd)))
        y = jax.nn.gelu(y.reshape(u.shape) + d_skip * u)
        return (y * jax.nn.sigmoid(y @ glu_w + glu_b)).astype(u.dtype)

    gc = groups(u_ctx)
    ctx_f = s5_scan(gc, a_bar[0], b_bar[0], False, None)
    ctx_b = s5_scan(gc, a_bar[1], b_bar[1], True, None)
    gl = groups(u_lat)
    lat_f = s5_scan(gl, a_bar[0], b_bar[0], False, ctx_f[:, -1])
    lat_b = s5_scan(gl, a_bar[1], b_bar[1], True, ctx_b[:, 0])
    y_lat = readout(u_lat, lat_f, lat_b)
    y_ctx = readout(u_ctx, ctx_f, ctx_b) if need_ctx else None
    return y_ctx, y_lat


def blocked_gqa(q, k, v, k_ctx, v_ctx):
    b, l, hq, dh = q.shape
    hkv = k.shape[2]
    k_all = jnp.concatenate([k, k_ctx], axis=1)
    v_all = jnp.concatenate([v, v_ctx], axis=1)
    qb = (q * dh ** -0.5).reshape(b, l // Q_BLOCK, Q_BLOCK, hkv, hq // hkv, dh)

    def query_block(q_i):
        s = jnp.einsum('bqgrd,bkgd->bgrqk', q_i, k_all)
        p = jax.nn.softmax(s.astype(jnp.float32), axis=-1).astype(v_all.dtype)
        return jnp.einsum('bgrqk,bkgd->bqgrd', p, v_all)

    out = lax.map(query_block, jnp.moveaxis(qb, 1, 0))
    return jnp.moveaxis(out, 0, 1).reshape(b, l, hq * dh)


def mixer_ab(h_ctx, h_lat, w_in, w_out, rpb, conv_w, conv_b, f_w1, f_b1, f_freq, f_w2, f_b2, f_w3, skip, need_ctx):
    filt = (conv_w, conv_b, f_w1, f_b1, f_freq, f_w2, f_b2, f_w3, skip)
    q_l, k_l, v_l, u_l = jnp.split(h_lat @ w_in, [NA_WIDTH, 2 * NA_WIDTH, 3 * NA_WIDTH], axis=-1)
    k_c, v_c = jnp.split(h_ctx @ w_in[:, NA_WIDTH:3 * NA_WIDTH], 2, axis=-1)
    k_c, v_c = split_heads(k_c, NA_HEADS), split_heads(v_c, NA_HEADS)
    a_lat = neighbourhood_attention(split_heads(q_l, NA_HEADS), split_heads(k_l, NA_HEADS),
                                    split_heads(v_l, NA_HEADS), k_c, v_c, rpb)
    y_lat = jnp.concatenate([a_lat, hyena(u_l, *filt)], axis=-1) @ w_out
    y_ctx = None
    if need_ctx:
        a_ctx = context_attention(split_heads(h_ctx @ w_in[:, :NA_WIDTH], NA_HEADS), k_c, v_c)
        y_ctx = jnp.concatenate([a_ctx, hyena(h_ctx @ w_in[:, 3 * NA_WIDTH:], *filt)], axis=-1) @ w_out
    return y_ctx, y_lat


def mixer_cd(h_ctx, h_lat, w_in, w_out, q_gain, k_gain, a_re, a_im, log_dt, b_re, b_im, c_re, c_im,
             d_skip, glu_w, glu_b, need_ctx):
    l = h_lat.shape[1]
    cuts = [GQA_WIDTH, GQA_WIDTH + GQA_KV_WIDTH, GQA_WIDTH + 2 * GQA_KV_WIDTH]
    q_l, k_l, v_l, u_l = jnp.split(h_lat @ w_in, cuts, axis=-1)
    k_c, v_c, u_c = jnp.split(h_ctx @ w_in[:, GQA_WIDTH:], [GQA_KV_WIDTH, 2 * GQA_KV_WIDTH], axis=-1)
    k_c = rms_norm(split_heads(k_c, GQA_KV_HEADS), k_gain)
    v_c = split_heads(v_c, GQA_KV_HEADS)
    row, col = grid_positions(l)
    q_l = rope_2d(rms_norm(split_heads(q_l, GQA_HEADS), q_gain), row, col)
    k_l = rope_2d(rms_norm(split_heads(k_l, GQA_KV_HEADS), k_gain), row, col)
    att_lat = blocked_gqa(q_l, k_l, split_heads(v_l, GQA_KV_HEADS), k_c, v_c)
    ssm_ctx, ssm_lat = s5_bidirectional(u_c, u_l, a_re, a_im, log_dt, b_re, b_im, c_re, c_im,
                                        d_skip, glu_w, glu_b, need_ctx)
    y_lat = jnp.concatenate([att_lat, ssm_lat], axis=-1) @ w_out
    y_ctx = None
    if need_ctx:
        q_c = rms_norm(split_heads(h_ctx @ w_in[:, :GQA_WIDTH], GQA_HEADS), q_gain)
        y_ctx = jnp.concatenate([context_attention(q_c, k_c, v_c), ssm_ctx], axis=-1) @ w_out
    return y_ctx, y_lat


def swiglu(x, w_gate, w_up, w_down):
    return (jax.nn.silu(x @ w_gate) * (x @ w_up)) @ w_down


def moe_ffn(h, router_w, router_bias, exp_w_gate, exp_w_up, exp_w_down, sh_w_gate, sh_w_up, sh_w_down):
    n, d = h.shape
    scores = jax.nn.sigmoid((h @ router_w).astype(jnp.float32))
    grouped = (scores + router_bias.astype(jnp.float32)).reshape(n, N_EXPERT_GROUPS, -1)
    group_score = jnp.sum(lax.top_k(grouped, 2)[0], axis=-1)
    _, top_groups = lax.top_k(group_score, TOPK_GROUPS)
    keep = jnp.sum(jax.nn.one_hot(top_groups, N_EXPERT_GROUPS, dtype=jnp.float32), axis=1) > 0
    masked = jnp.where(keep[..., None], grouped, -jnp.inf).reshape(n, N_EXPERTS)
    _, top_e = lax.top_k(masked, TOP_K)
    gate = jnp.take_along_axis(scores, top_e, axis=1)
    gate = ROUTED_SCALE * gate / jnp.sum(gate, axis=-1, keepdims=True)

    flat_e = top_e.reshape(-1)
    order = jnp.argsort(flat_e)
    sorted_e = flat_e[order]
    counts = jnp.bincount(flat_e, length=N_EXPERTS)
    padded = (counts + EXPERT_BLOCK - 1) // EXPERT_BLOCK * EXPERT_BLOCK
    pad_end = jnp.cumsum(padded)
    start = jnp.cumsum(counts) - counts
    dest = (pad_end - padded)[sorted_e] + jnp.arange(n * TOP_K) - start[sorted_e]
    n_blocks = -(-(n * TOP_K + N_EXPERTS * (EXPERT_BLOCK - 1)) // EXPERT_BLOCK)
    n_slots = n_blocks * EXPERT_BLOCK
    slot_tok = jnp.full((n_slots,), n, jnp.int32).at[dest].set((order // TOP_K).astype(jnp.int32))
    slot_gate = jnp.zeros((n_slots,), h.dtype).at[dest].set(gate.reshape(-1)[order].astype(h.dtype))
    block_e = jnp.minimum(jnp.searchsorted(pad_end, jnp.arange(n_blocks) * EXPERT_BLOCK, side='right'),
                          N_EXPERTS - 1)
    h_pad = jnp.concatenate([h, jnp.zeros((1, d), h.dtype)], axis=0)

    def expert_block(args):
        tok, e = args
        return swiglu(h_pad[tok], exp_w_gate[e], exp_w_up[e], exp_w_down[e])

    ys = lax.map(expert_block, (slot_tok.reshape(n_blocks, EXPERT_BLOCK), block_e))
    routed = jnp.zeros((n + 1, d), h.dtype).at[slot_tok].add(ys.reshape(-1, d) * slot_gate[:, None])[:n]
    return routed + swiglu(h, sh_w_gate, sh_w_up, sh_w_down)


def setup_inputs(seed: int = 0) -> dict:
    key = jax.random.key(seed)
    ks = iter(jax.random.split(key, 48))

    def nrm(shape, scale):
        return scale * jax.random.normal(next(ks), shape, jnp.float32)

    d = D_MODEL
    beta = DEEPNORM_BETA
    return {
        'x': nrm((BATCH, SEQ, d), 1.0),
        'c': nrm((BATCH, d), 1.0),
        'ctx': nrm((BATCH, CTX_LEN, d), 1.0),
        'c_ctx': nrm((d,), 1.0),
        'w_mod': nrm((DEPTH, d, 6 * d), 0.5 * d ** -0.5),
        'b_mod': nrm((DEPTH, 6 * d), 0.1),
        'ln_mix_g': 1.0 + nrm((DEPTH, d), 0.05),
        'ln_mix_b': nrm((DEPTH, d), 0.05),
        'ln_ffn_g': 1.0 + nrm((DEPTH, d), 0.05),
        'ln_ffn_b': nrm((DEPTH, d), 0.05),
        'ab_w_in': nrm((N_AB, d, AB_IN), d ** -0.5),
        'ab_w_out': nrm((N_AB, NA_WIDTH + HY_WIDTH, d), beta * (NA_WIDTH + HY_WIDTH) ** -0.5),
        'na_rpb': nrm((N_AB, NA_HEADS, 2 * NA_KH - 1, 2 * NA_KW - 1), 0.5),
        'hy_conv_w': nrm((N_AB, HY_SHORT, 3 * HY_WIDTH), 0.6),
        'hy_conv_b': nrm((N_AB, 3 * HY_WIDTH), 0.1),
        'hy_f_w1': nrm((N_AB, HY_EMB, HY_FILT_HID), 2.0 * HY_EMB ** -0.5),
        'hy_f_b1': nrm((N_AB, HY_FILT_HID), 0.5),
        'hy_f_freq': 1.0 + nrm((N_AB, HY_FILT_HID), 0.1),
        'hy_f_w2': nrm((N_AB, HY_FILT_HID, HY_FILT_HID), HY_FILT_HID ** -0.5),
        'hy_f_b2': nrm((N_AB, HY_FILT_HID), 0.5),
        'hy_f_w3': nrm((N_AB, HY_FILT_HID, 2 * HY_WIDTH), HY_FILT_HID ** -0.5),
        'hy_skip': nrm((N_AB, HY_WIDTH), 0.5),
        'cd_w_in': nrm((N_CD, d, CD_IN), d ** -0.5),
        'cd_w_out': nrm((N_CD, GQA_WIDTH + S5_WIDTH, d), beta * (GQA_WIDTH + S5_WIDTH) ** -0.5),
        'q_norm_g': 1.0 + nrm((N_CD, HEAD_DIM), 0.05),
        'k_norm_g': 1.0 + nrm((N_CD, HEAD_DIM), 0.05),
        's5_a_re': -0.5 + nrm((N_CD, 2, S5_GROUPS, S5_STATE), 0.01),
        's5_a_im': jnp.pi * jnp.arange(S5_STATE, dtype=jnp.float32) + nrm((N_CD, 2, S5_GROUPS, S5_STATE), 0.01),
        's5_log_dt': jax.random.uniform(next(ks), (N_CD, 2, S5_GROUPS), jnp.float32,
                                        math.log(1e-3), math.log(1e-1)),
        's5_b_re': nrm((N_CD, 2, S5_GROUPS, S5_STATE, S5_GROUP), (2.0 * S5_GROUP) ** -0.5),
        's5_b_im': nrm((N_CD, 2, S5_GROUPS, S5_STATE, S5_GROUP), (2.0 * S5_GROUP) ** -0.5),
        's5_c_re': nrm((N_CD, 2, S5_GROUPS, S5_GROUP, S5_STATE), S5_STATE ** -0.5),
        's5_c_im': nrm((N_CD, 2, S5_GROUPS, S5_GROUP, S5_STATE), S5_STATE ** -0.5),
        's5_d': nrm((N_CD, S5_WIDTH), 1.0),
        's5_glu_w': nrm((N_CD, S5_WIDTH, S5_WIDTH), S5_WIDTH ** -0.5),
        's5_glu_b': nrm((N_CD, S5_WIDTH), 0.1),
        'router_w': nrm((DEPTH, d, N_EXPERTS), d ** -0.5),
        'router_bias': nrm((DEPTH, N_EXPERTS), 0.01),
        'exp_w_gate': nrm((DEPTH, N_EXPERTS, d, EXPERT_FF), d ** -0.5),
        'exp_w_up': nrm((DEPTH, N_EXPERTS, d, EXPERT_FF), d ** -0.5),
        'exp_w_down': nrm((DEPTH, N_EXPERTS, EXPERT_FF, d), beta * EXPERT_FF ** -0.5),
        'sh_w_gate': nrm((DEPTH, d, EXPERT_FF), d ** -0.5),
        'sh_w_up': nrm((DEPTH, d, EXPERT_FF), d ** -0.5),
        'sh_w_down': nrm((DEPTH, EXPERT_FF, d), beta * EXPERT_FF ** -0.5),
    }


def reference(x, c, ctx, c_ctx, w_mod, b_mod, ln_mix_g, ln_mix_b, ln_ffn_g, ln_ffn_b,
              ab_w_in, ab_w_out, na_rpb, hy_conv_w, hy_conv_b, hy_f_w1, hy_f_b1, hy_f_freq, hy_f_w2, hy_f_b2,
              hy_f_w3, hy_skip, cd_w_in, cd_w_out, q_norm_g, k_norm_g, s5_a_re, s5_a_im, s5_log_dt,
              s5_b_re, s5_b_im, s5_c_re, s5_c_im, s5_d, s5_glu_w, s5_glu_b,
              router_w, router_bias, exp_w_gate, exp_w_up, exp_w_down, sh_w_gate, sh_w_up, sh_w_down):
    b, l, d = x.shape
    n_ctx = ctx.shape[1]
    for i in range(DEPTH):
        need_ctx = i < DEPTH - 1
        sh1, sc1, g1, sh2, sc2, g2 = [m[:, None] for m in modulation(c, w_mod[i], b_mod[i])]
        csh1, csc1, cg1, csh2, csc2, cg2 = modulation(c_ctx, w_mod[i], b_mod[i])
        h_lat = x * (1 + sc1) + sh1
        h_ctx = ctx * (1 + csc1) + csh1
        j = i // 2
        if i % 2 == 0:
            y_ctx, y_lat = mixer_ab(h_ctx, h_lat, ab_w_in[j], ab_w_out[j], na_rpb[j], hy_conv_w[j], hy_conv_b[j],
                                    hy_f_w1[j], hy_f_b1[j], hy_f_freq[j], hy_f_w2[j], hy_f_b2[j], hy_f_w3[j],
                                    hy_skip[j], need_ctx)
        else:
            y_ctx, y_lat = mixer_cd(h_ctx, h_lat, cd_w_in[j], cd_w_out[j], q_norm_g[j], k_norm_g[j],
                                    s5_a_re[j], s5_a_im[j], s5_log_dt[j], s5_b_re[j], s5_b_im[j],
                                    s5_c_re[j], s5_c_im[j], s5_d[j], s5_glu_w[j], s5_glu_b[j], need_ctx)
        x = layer_norm(DEEPNORM_ALPHA * x + g1 * y_lat, ln_mix_g[i], ln_mix_b[i])
        moe_w = (router_w[i], router_bias[i], exp_w_gate[i], exp_w_up[i], exp_w_down[i],
                 sh_w_gate[i], sh_w_up[i], sh_w_down[i])
        hf_lat = x * (1 + sc2) + sh2
        if need_ctx:
            ctx = layer_norm(DEEPNORM_ALPHA * ctx + cg1 * y_ctx, ln_mix_g[i], ln_mix_b[i])
            hf_ctx = ctx * (1 + csc2) + csh2
            toks = jnp.concatenate([hf_ctx, hf_lat], axis=1).reshape(-1, d)
            y = moe_ffn(toks, *moe_w).reshape(b, n_ctx + l, d)
            ctx = layer_norm(DEEPNORM_ALPHA * ctx + cg2 * y[:, :n_ctx], ln_ffn_g[i], ln_ffn_b[i])
            y_lat = y[:, n_ctx:]
        else:
            y_lat = moe_ffn(hf_lat.reshape(-1, d), *moe_w).reshape(b, l, d)
        x = layer_norm(DEEPNORM_ALPHA * x + g2 * y_lat, ln_ffn_g[i], ln_ffn_b[i])
    return x
```

```python
import functools
import math

import jax
import jax.numpy as jnp
from jax import lax
from jax.experimental import pallas as pl
from jax.experimental.pallas import tpu as pltpu

F32 = jnp.float32
BF16 = jnp.bfloat16
I32 = jnp.int32
HIGHEST = lax.Precision.HIGHEST

LANES = 128
SUBLANES = 8
VMEM_LIMIT = 56 * 1024 * 1024

D_MODEL = 1024
DEPTH = 2
GRID_W = 64
HEAD_DIM = 64
NA_HEADS = 8
NA_WIDTH = NA_HEADS * HEAD_DIM
NA_KH = 8
NA_KW = 16
HY_WIDTH = D_MODEL - NA_WIDTH
HY_BANDS = 16
HY_DECAY_PCT_MIN = 0.3
HY_DECAY_PCT_MAX = 1.5
HY_DECAY_TARGET = 1e-2
GQA_HEADS = 8
GQA_KV_HEADS = 2
GQA_WIDTH = GQA_HEADS * HEAD_DIM
GQA_KV_WIDTH = GQA_KV_HEADS * HEAD_DIM
ROPE_THETA = 10000.0
S5_WIDTH = D_MODEL - GQA_WIDTH
S5_GROUP = 16
S5_GROUPS = S5_WIDTH // S5_GROUP
S5_STATE = 64
N_EXPERTS = 256
TOP_K = 8
N_EXPERT_GROUPS = 8
TOPK_GROUPS = 4
EXPERT_FF = 256
ROUTED_SCALE = 2.5
EXPERT_BLOCK = 128
DEEPNORM_ALPHA = (2.0 * DEPTH) ** 0.25
LN_EPS = 1e-5
RMS_EPS = 1e-6

NEG_BIG = -1e30
NA_TILE_ROWS = 8
NA_KEY_ROWS = 16
NA_KEY_BLOCK_ROWS = 4
DFT_N1 = 128
S5_CHUNK = 16
MOE_TILE = 256
ROW_SLAB = D_MODEL // LANES

NT_DIMS = (((1,), (1,)), ((), ()))


def _cparams(sem, **kw):
    return pltpu.CompilerParams(dimension_semantics=sem, vmem_limit_bytes=VMEM_LIMIT, **kw)


def _dot(a, b, **kw):
    return jnp.dot(a, b, preferred_element_type=F32, **kw)


def _dot_nt(a, b):
    return lax.dot_general(a, b, NT_DIMS, preferred_element_type=F32)


def _row_tile(m, pref):
    return pref if m % pref == 0 else m


def _mod_kernel(c_ref, w_ref, b_ref, o_ref):
    cv = c_ref[...]
    s = cv * jax.nn.sigmoid(cv)
    o_ref[...] = _dot(s, w_ref[...], precision=HIGHEST) + b_ref[...]


def modulation_all(cmat, w_mod, b_mod):
    depth, d, n = w_mod.shape
    tn = 1536
    return pl.pallas_call(
        _mod_kernel,
        grid=(depth, n // tn),
        in_specs=[pl.BlockSpec((SUBLANES, d), lambda l, j: (0, 0)),
                  pl.BlockSpec((None, d, tn), lambda l, j: (l, 0, j)),
                  pl.BlockSpec((None, 1, tn), lambda l, j: (l, 0, j))],
        out_specs=pl.BlockSpec((None, SUBLANES, tn), lambda l, j: (l, 0, j)),
        out_shape=jax.ShapeDtypeStruct((depth, SUBLANES, n), F32),
        compiler_params=_cparams(("arbitrary", "arbitrary")),
        name="modulation",
    )(cmat, w_mod, b_mod.reshape(depth, 1, n))


def _proj_kernel(x_ref, sc_ref, sh_ref, w_ref, *o_refs, splits, scales):
    h = (x_ref[...] * (1.0 + sc_ref[...]) + sh_ref[...]).astype(BF16)
    off = 0
    for o_ref, wd, sc in zip(o_refs, splits, scales):
        y = _dot(h, w_ref[:, off:off + wd])
        if sc != 1.0:
            y = y * sc
        o_ref[...] = y.astype(o_ref.dtype)
        off += wd


def mod_project(x, sc, sh, w, splits, dtypes, scales=None):
    m, d = x.shape
    n = w.shape[1]
    assert sum(splits) == n
    scales = scales or (1.0,) * len(splits)
    tm = _row_tile(m, 512)
    kern = functools.partial(_proj_kernel, splits=tuple(splits), scales=tuple(scales))
    return pl.pallas_call(
        kern,
        grid=(m // tm,),
        in_specs=[pl.BlockSpec((tm, d), lambda i: (i, 0)),
                  pl.BlockSpec((1, d), lambda i: (0, 0)),
                  pl.BlockSpec((1, d), lambda i: (0, 0)),
                  pl.BlockSpec((d, n), lambda i: (0, 0))],
        out_specs=[pl.BlockSpec((tm, wd), lambda i: (i, 0)) for wd in splits],
        out_shape=[jax.ShapeDtypeStruct((m, wd), dt) for wd, dt in zip(splits, dtypes)],
        compiler_params=_cparams(("parallel",)),
        name="mod_project",
    )(x, sc, sh, w.astype(BF16))


def _layer_norm_rows(r, g, b):
    mu = jnp.mean(r, axis=-1, keepdims=True)
    c = r - mu
    var = jnp.mean(c * c, axis=-1, keepdims=True)
    return c * lax.rsqrt(var + LN_EPS) * g + b


def _outproj_ln_kernel(a_ref, b_ref, w_ref, x_ref, gate_ref, g_ref, beta_ref, o_ref):
    ka = a_ref.shape[1]
    y = _dot(a_ref[...], w_ref[:ka, :]) + _dot(b_ref[...], w_ref[ka:, :])
    r = DEEPNORM_ALPHA * x_ref[...] + gate_ref[...] * y
    o_ref[...] = _layer_norm_rows(r, g_ref[...], beta_ref[...])


def outproj_ln(a, b, w, x, gate, g, beta):
    m, d = x.shape
    ka, kb = a.shape[1], b.shape[1]
    tm = _row_tile(m, 512)
    row = lambda i: (i, 0)
    fixed = lambda i: (0, 0)
    return pl.pallas_call(
        _outproj_ln_kernel,
        grid=(m // tm,),
        in_specs=[pl.BlockSpec((tm, ka), row), pl.BlockSpec((tm, kb), row),
                  pl.BlockSpec((ka + kb, d), fixed), pl.BlockSpec((tm, d), row),
                  pl.BlockSpec((1, d), fixed), pl.BlockSpec((1, d), fixed), pl.BlockSpec((1, d), fixed)],
        out_specs=pl.BlockSpec((tm, d), row),
        out_shape=jax.ShapeDtypeStruct((m, d), F32),
        compiler_params=_cparams(("parallel",)),
        name="outproj_ln",
    )(a, b, w.astype(BF16), x, gate, g.reshape(1, d), beta.reshape(1, d))


def _na_bias_table(rpb, rows):
    h = rpb.shape[0]
    ri = jnp.arange(NA_TILE_ROWS)
    kr = jnp.arange(NA_KEY_ROWS)
    c = jnp.arange(GRID_W)
    cs = jnp.clip(c - NA_KW // 2, 0, GRID_W - NA_KW)
    vc = (c[None, :] >= cs[:, None]) & (c[None, :] < cs[:, None] + NA_KW)
    dc = jnp.clip(c[None, :] - c[:, None] + NA_KW - 1, 0, 2 * NA_KW - 2)

    def case(t):
        start = min(max(NA_TILE_ROWS * t - NA_KH // 2, 0), rows - NA_KEY_ROWS)
        r = NA_TILE_ROWS * t + ri
        rs = jnp.clip(r - NA_KH // 2, 0, rows - NA_KH)
        krow = start + kr
        vr = (krow[None, :] >= rs[:, None]) & (krow[None, :] < rs[:, None] + NA_KH)
        dr = jnp.clip(krow[None, :] - r[:, None] + NA_KH - 1, 0, 2 * NA_KH - 2)
        b = rpb[:, dr[:, None, :, None], dc[None, :, None, :]]
        valid = vr[:, None, :, None] & vc[None, :, None, :]
        b = jnp.where(valid[None], b, NEG_BIG)
        return b.reshape(h // 2, 2, NA_TILE_ROWS * GRID_W, NA_KEY_ROWS * GRID_W)

    n_tiles = rows // NA_TILE_ROWS
    return jnp.stack([case(0), case(1), case(n_tiles - 1)])


def _pair_masks(shape):
    lane = lax.broadcasted_iota(I32, shape, 1)
    return lane < HEAD_DIM


def _na_kernel(q_ref, k0, k1, k2, k3, v0, v1, v2, v3, kc_ref, vc_ref, bias_ref, o_ref):
    q = q_ref[...]
    lo = _pair_masks(q.shape)
    ks = (k0, k1, k2, k3)
    vs = (v0, v1, v2, v3)
    kb = k0.shape[0]
    outs = []
    for hh in range(2):
        qh = jnp.where(lo if hh == 0 else jnp.logical_not(lo), q, jnp.zeros_like(q))
        s = [_dot_nt(qh, ks[i][...]) + bias_ref[hh, :, i * kb:(i + 1) * kb] for i in range(4)]
        s.append(_dot_nt(qh, kc_ref[...]))
        m = s[0].max(axis=1, keepdims=True)
        for si in s[1:]:
            m = jnp.maximum(m, si.max(axis=1, keepdims=True))
        p = [jnp.exp(si - m) for si in s]
        l = p[0].sum(axis=1, keepdims=True)
        for pi in p[1:]:
            l = l + pi.sum(axis=1, keepdims=True)
        acc = _dot(p[4].astype(BF16), vc_ref[...])
        for i in range(4):
            acc = acc + _dot(p[i].astype(BF16), vs[i][...])
        outs.append(acc / l)
    o_ref[...] = jnp.where(lo, outs[0], outs[1]).astype(o_ref.dtype)


def neighbourhood_attention(q, k, v, k_ctx, v_ctx, rpb):
    l, w = q.shape
    rows = l // GRID_W
    n_tiles = rows // NA_TILE_ROWS
    assert n_tiles >= 3 and rows % NA_TILE_ROWS == 0
    n_ctx = k_ctx.shape[0]
    tq = NA_TILE_ROWS * GRID_W
    kb = NA_KEY_BLOCK_ROWS * GRID_W
    n_kblk = rows // NA_KEY_BLOCK_ROWS
    bias = _na_bias_table(rpb, rows)
    pair_w = 2 * HEAD_DIM

    def kv_spec(i):
        def imap(p, t):
            start = jnp.clip(2 * t - 1, 0, n_kblk - 4)
            return (start + i, p)
        return pl.BlockSpec((kb, pair_w), imap)

    def bias_map(p, t):
        case = jnp.where(t == 0, 0, jnp.where(t == n_tiles - 1, 2, 1))
        return (case, p, 0, 0, 0)

    return pl.pallas_call(
        _na_kernel,
        grid=(w // pair_w, n_tiles),
        in_specs=[pl.BlockSpec((tq, pair_w), lambda p, t: (t, p))]
                 + [kv_spec(i) for i in range(4)] + [kv_spec(i) for i in range(4)]
                 + [pl.BlockSpec((n_ctx, pair_w), lambda p, t: (0, p)),
                    pl.BlockSpec((n_ctx, pair_w), lambda p, t: (0, p)),
                    pl.BlockSpec((None, None, 2, tq, NA_KEY_ROWS * GRID_W), bias_map)],
        out_specs=pl.BlockSpec((tq, pair_w), lambda p, t: (t, p)),
        out_shape=jax.ShapeDtypeStruct((l, w), BF16),
        compiler_params=_cparams(("parallel", "parallel")),
        name="neighbourhood_attention",
    )(q, k, k, k, k, v, v, v, v, k_ctx, v_ctx, bias)


def _ctx_attn_kernel(q_ref, k_ref, v_ref, o_ref):
    q = q_ref[...]
    lo = _pair_masks(q.shape)
    outs = []
    for hh in range(2):
        qh = jnp.where(lo if hh == 0 else jnp.logical_not(lo), q, jnp.zeros_like(q))
        s = _dot_nt(qh, k_ref[...])
        p = jnp.exp(s - s.max(axis=1, keepdims=True))
        outs.append(_dot(p.astype(BF16), v_ref[...]) / p.sum(axis=1, keepdims=True))
    o_ref[...] = jnp.where(lo, outs[0], outs[1]).astype(o_ref.dtype)


def context_attention(q, k, v):
    n, w = q.shape
    pair_w = 2 * HEAD_DIM
    spec = pl.BlockSpec((n, pair_w), lambda p: (0, p))
    return pl.pallas_call(
        _ctx_attn_kernel, grid=(w // pair_w,), in_specs=[spec, spec, spec], out_specs=spec,
        out_shape=jax.ShapeDtypeStruct((n, w), BF16),
        compiler_params=_cparams(("parallel",)), name="context_attention",
    )(q, k, v)


def _shortconv_kernel(u_ref, up_ref, un_ref, w_ref, b_ref, x0_ref, z_ref, *, n_tiles):
    i = pl.program_id(0)
    u = u_ref[...]
    tm = u.shape[0]
    prev_row = jnp.where(i > 0, up_ref[SUBLANES - 1:SUBLANES, :], 0.0)
    next_row = jnp.where(i < n_tiles - 1, un_ref[0:1, :], 0.0)
    row = lax.broadcasted_iota(I32, u.shape, 0)
    u_dn = jnp.where(row == 0, prev_row, pltpu.roll(u, 1, 0))
    u_up = jnp.where(row == tm - 1, next_row, pltpu.roll(u, tm - 1, 0))
    y = u_dn * w_ref[0:1, :] + u * w_ref[1:2, :] + u_up * w_ref[2:3, :] + b_ref[...]
    c = HY_WIDTH
    x0_ref[...] = y[:, :c]
    z_ref[...] = y[:, c:2 * c] * y[:, 2 * c:]


def hyena_gate(u, conv_w, conv_b):
    l, w3 = u.shape
    tm = _row_tile(l, 512)
    n_tiles = l // tm
    per = tm // SUBLANES
    last = l // SUBLANES - 1
    kern = functools.partial(_shortconv_kernel, n_tiles=n_tiles)
    return pl.pallas_call(
        kern,
        grid=(n_tiles,),
        in_specs=[pl.BlockSpec((tm, w3), lambda i: (i, 0)),
                  pl.BlockSpec((SUBLANES, w3), lambda i: (jnp.maximum(i * per - 1, 0), 0)),
                  pl.BlockSpec((SUBLANES, w3), lambda i: (jnp.minimum((i + 1) * per, last), 0)),
                  pl.BlockSpec((3, w3), lambda i: (0, 0)),
                  pl.BlockSpec((1, w3), lambda i: (0, 0))],
        out_specs=[pl.BlockSpec((tm, HY_WIDTH), lambda i: (i, 0))] * 2,
        out_shape=[jax.ShapeDtypeStruct((l, HY_WIDTH), F32)] * 2,
        compiler_params=_cparams(("parallel",)),
        name="hyena_gate",
    )(u, u, u, conv_w, conv_b.reshape(1, w3))


def _filter_kernel(bands_ref, w1t_ref, w1c_ref, w1s_ref, b1_ref, fr_ref, w2_ref, b2_ref, w3_ref, dl_ref,
                   taps_ref, asum_ref, *, l, tp):
    i = pl.program_id(0)
    hid_w = w2_ref.shape[0]
    c = HY_WIDTH
    denom = float(max(l - 1, 1))

    def pos(width):
        return (lax.broadcasted_iota(I32, (tp, width), 0) + i * tp).astype(F32)

    ang = (2.0 * math.pi / l) * pos(HY_BANDS) * bands_ref[...]
    pre = ((pos(hid_w) / denom) * w1t_ref[...]
           + _dot(jnp.cos(ang), w1c_ref[...], precision=HIGHEST)
           + _dot(-jnp.sin(ang), w1s_ref[...], precision=HIGHEST) + b1_ref[...])
    hid = jnp.sin(fr_ref[...] * pre)
    hid = jnp.sin(fr_ref[...] * (_dot(hid, w2_ref[...], precision=HIGHEST) + b2_ref[...]))
    taps = _dot(hid, w3_ref[...], precision=HIGHEST)
    pc = pos(c)
    window = jnp.exp(-(pc / denom) * dl_ref[...])
    fwd = taps[:, :c] * window
    bwd = jnp.where(pc == 0.0, 0.0, taps[:, c:] * window)
    taps_ref[:, :c] = fwd
    taps_ref[:, c:] = bwd

    @pl.when(i == 0)
    def _():
        asum_ref[...] = jnp.zeros_like(asum_ref)

    asum_ref[...] += jnp.sum(jnp.abs(fwd) + jnp.abs(bwd), axis=0, keepdims=True)


def hyena_filter_taps(l, f_w1, f_b1, f_freq, f_w2, f_b2, f_w3):
    c = HY_WIDTH
    hid = f_w2.shape[0]
    tp = _row_tile(l, 1024)
    bands = jnp.linspace(1e-4, HY_BANDS - 1, HY_BANDS, dtype=F32).reshape(1, HY_BANDS)
    deltas = jnp.abs(jnp.linspace(math.log(HY_DECAY_TARGET) / HY_DECAY_PCT_MAX,
                                  math.log(HY_DECAY_TARGET) / HY_DECAY_PCT_MIN, c, dtype=F32)).reshape(1, c)
    fixed = lambda i: (0, 0)
    full = lambda a: pl.BlockSpec(a.shape, fixed)
    args = (bands, f_w1[0:1], f_w1[1:1 + HY_BANDS], f_w1[1 + HY_BANDS:], f_b1.reshape(1, hid),
            f_freq.reshape(1, hid), f_w2, f_b2.reshape(1, hid), f_w3, deltas)
    kern = functools.partial(_filter_kernel, l=l, tp=tp)
    return pl.pallas_call(
        kern,
        grid=(l // tp,),
        in_specs=[full(a) for a in args],
        out_specs=[pl.BlockSpec((tp, 2 * c), lambda i: (i, 0)), pl.BlockSpec((1, c), fixed)],
        out_shape=[jax.ShapeDtypeStruct((l, 2 * c), F32), jax.ShapeDtypeStruct((1, c), F32)],
        compiler_params=_cparams(("arbitrary",)),
        name="hyena_filter",
    )(*args)


def _dft_tables(l):
    n = 2 * l
    n1 = DFT_N1
    n2 = n // n1
    k1 = jnp.arange(n1)[:, None]
    m1 = jnp.arange(n1 // 2)[None, :]
    ph1 = (2.0 * math.pi / n1) * ((k1 * m1) % n1).astype(F32)
    d1 = jnp.stack([jnp.cos(ph1), -jnp.sin(ph1)], axis=1).reshape(2 * n1, n1 // 2)
    d1_inv = d1.T
    kk = (jnp.arange(n2)[None, :, None] * n1 + jnp.arange(n1)[:, None, None])
    th = (2.0 * math.pi / n) * ((kk * jnp.arange(n2)[None, None, :]) % n).astype(F32)
    cs, sn = jnp.cos(th), jnp.sin(th)
    m2 = jnp.concatenate([jnp.concatenate([cs, sn], axis=2), jnp.concatenate([-sn, cs], axis=2)], axis=1)
    m2_inv = jnp.swapaxes(m2, 1, 2)
    return d1.astype(BF16), d1_inv.astype(BF16), m2.astype(BF16), m2_inv.astype(BF16)


def _dft1_kernel(d_ref, x_ref, o_ref):
    o_ref[...] = _dot(d_ref[...], x_ref[...].astype(BF16)).astype(o_ref.dtype)


def dft_stage1(x, d1, n2):
    l, c = x.shape
    n1h = d1.shape[1]
    cols = n2 * c
    tc = min(cols, 4096)
    out = pl.pallas_call(
        _dft1_kernel,
        grid=(cols // tc,),
        in_specs=[pl.BlockSpec(d1.shape, lambda j: (0, 0)), pl.BlockSpec((n1h, tc), lambda j: (0, j))],
        out_specs=pl.BlockSpec((d1.shape[0], tc), lambda j: (0, j)),
        out_shape=jax.ShapeDtypeStruct((d1.shape[0], cols), BF16),
        compiler_params=_cparams(("parallel",)),
        name="dft_stage1",
    )(d1, x.reshape(n1h, cols))
    return out.reshape(d1.shape[0] // 2, 2, n2, c)


def _filter_spectrum_kernel(m_ref, a_ref, h_ref):
    n2 = a_ref.shape[1]
    c = h_ref.shape[2]
    x = _dot(m_ref[...], a_ref[...].reshape(2 * n2, 2 * c))
    h_ref[0] = x[:n2, :c] + x[:n2, c:]
    h_ref[1] = x[n2:, :c] - x[n2:, c:]


def filter_spectrum(a_taps, m2):
    n1, _, n2, c2 = a_taps.shape
    c = c2 // 2
    return pl.pallas_call(
        _filter_spectrum_kernel,
        grid=(n1,),
        in_specs=[pl.BlockSpec((None, 2 * n2, 2 * n2), lambda i: (i, 0, 0)),
                  pl.BlockSpec((None, 2, n2, c2), lambda i: (i, 0, 0, 0))],
        out_specs=pl.BlockSpec((None, 2, n2, c), lambda i: (i, 0, 0, 0)),
        out_shape=jax.ShapeDtypeStruct((n1, 2, n2, c), F32),
        compiler_params=_cparams(("parallel",)),
        name="filter_spectrum",
    )(m2, a_taps)


def _spectral_mix_kernel(m_ref, mi_ref, a_ref, h_ref, o_ref):
    n2 = a_ref.shape[1]
    c = a_ref.shape[2]
    x = _dot(m_ref[...], a_ref[...].reshape(2 * n2, c))
    xr, xi = x[:n2], x[n2:]
    hr, hi = h_ref[0], h_ref[1]
    y = jnp.concatenate([xr * hr - xi * hi, xr * hi + xi * hr], axis=0).astype(BF16)
    o_ref[...] = _dot(mi_ref[...], y).reshape(2, n2, c).astype(o_ref.dtype)


def spectral_mix(a_z, h, m2, m2_inv):
    n1, _, n2, c = a_z.shape
    blk = pl.BlockSpec((None, 2, n2, c), lambda i: (i, 0, 0, 0))
    mat = pl.BlockSpec((None, 2 * n2, 2 * n2), lambda i: (i, 0, 0))
    return pl.pallas_call(
        _spectral_mix_kernel,
        grid=(n1,),
        in_specs=[mat, mat, blk, blk],
        out_specs=blk,
        out_shape=jax.ShapeDtypeStruct((n1, 2, n2, c), BF16),
        compiler_params=_cparams(("parallel",)),
        name="spectral_mix",
    )(m2, m2_inv, a_z, h)


def _hyena_out_kernel(di_ref, b_ref, x0_ref, z_ref, inv_ref, skip_ref, o_ref, *, inv_n):
    conv = _dot(di_ref[...], b_ref[...]) * inv_n
    o_ref[...] = (x0_ref[...] * (conv * inv_ref[...] + z_ref[...] * skip_ref[...])).astype(o_ref.dtype)


def hyena_output(b, d1_inv, x0, z, inv_norm, skip):
    n1, _, n2, c = b.shape
    l = x0.shape[0]
    cols = n2 * c
    tc = min(cols, 4096)
    n1h = n1 // 2
    rep = tc // c
    kern = functools.partial(_hyena_out_kernel, inv_n=1.0 / (2 * l))
    tile = pl.BlockSpec((n1h, tc), lambda j: (0, j))
    out = pl.pallas_call(
        kern,
        grid=(cols // tc,),
        in_specs=[pl.BlockSpec(d1_inv.shape, lambda j: (0, 0)),
                  pl.BlockSpec((2 * n1, tc), lambda j: (0, j)),
                  tile, tile,
                  pl.BlockSpec((1, tc), lambda j: (0, 0)), pl.BlockSpec((1, tc), lambda j: (0, 0))],
        out_specs=tile,
        out_shape=jax.ShapeDtypeStruct((n1h, cols), BF16),
        compiler_params=_cparams(("parallel",)),
        name="hyena_output",
    )(d1_inv, b.reshape(2 * n1, cols), x0.reshape(n1h, cols), z.reshape(n1h, cols),
      jnp.tile(inv_norm, (1, rep)), jnp.tile(skip.reshape(1, c), (1, rep)))
    return out.reshape(l, c)


def _small_conv_kernel(d_ref, di_ref, z_ref, taps_ref, x0_ref, inv_ref, skip_ref, o_ref, *, inv_n):
    c = z_ref.shape[1]
    n = d_ref.shape[0] // 2
    zs = _dot(d_ref[...], z_ref[...], precision=HIGHEST)
    ts = _dot(d_ref[...], taps_ref[...], precision=HIGHEST)
    hr = ts[:n, :c] + ts[:n, c:]
    hi = ts[n:, :c] - ts[n:, c:]
    zr, zi = zs[:n], zs[n:]
    y = jnp.concatenate([zr * hr - zi * hi, zr * hi + zi * hr], axis=0)
    conv = _dot(di_ref[...], y, precision=HIGHEST) * inv_n
    o_ref[...] = (x0_ref[...] * (conv * inv_ref[...] + z_ref[...] * skip_ref[...])).astype(o_ref.dtype)


def hyena_output_short(z, taps, x0, inv_norm, skip):
    l, c = z.shape
    n = 2 * l
    ph = (2.0 * math.pi / n) * ((jnp.arange(n)[:, None] * jnp.arange(l)[None, :]) % n).astype(F32)
    d = jnp.concatenate([jnp.cos(ph), -jnp.sin(ph)], axis=0)
    di = jnp.concatenate([jnp.cos(ph), -jnp.sin(ph)], axis=0).T
    args = (d, di, z, taps, x0, inv_norm, skip.reshape(1, c))
    kern = functools.partial(_small_conv_kernel, inv_n=1.0 / n)
    return pl.pallas_call(
        kern,
        grid=(1,),
        in_specs=[pl.BlockSpec(a.shape, lambda i: (0, 0)) for a in args],
        out_specs=pl.BlockSpec((l, c), lambda i: (0, 0)),
        out_shape=jax.ShapeDtypeStruct((l, c), BF16),
        compiler_params=_cparams(("arbitrary",)),
        name="hyena_output_short",
    )(*args)


def hyena_long(u, conv_w, conv_b, f_w1, f_b1, f_freq, f_w2, f_b2, f_w3, skip):
    l = u.shape[0]
    x0, z = hyena_gate(u, conv_w, conv_b)
    taps, asum = hyena_filter_taps(l, f_w1, f_b1, f_freq, f_w2, f_b2, f_w3)
    inv_norm = 1.0 / asum
    if 2 * l < DFT_N1 * SUBLANES * 2:
        return hyena_output_short(z, taps, x0, inv_norm, skip)
    n2 = 2 * l // DFT_N1
    d1, d1_inv, m2, m2_inv = _dft_tables(l)
    h = filter_spectrum(dft_stage1(taps, d1, n2), m2)
    b = spectral_mix(dft_stage1(z, d1, n2), h, m2, m2_inv)
    return hyena_output(b, d1_inv, x0, z, inv_norm, skip)


def _head_sumsq(x, bd):
    sq = x * x
    hi = sq.astype(BF16)
    lo = (sq - hi.astype(F32)).astype(BF16)
    return _dot(hi, bd) + _dot(lo, bd)


def _qk_prep_kernel(x_ref, gain_ref, bd_ref, *rest, rope, scale):
    x = x_ref[...]
    w = x.shape[1]
    ms = _head_sumsq(x, bd_ref[...]) * (1.0 / HEAD_DIM)
    xn = x * lax.rsqrt(ms + RMS_EPS) * gain_ref[...]
    if rope:
        cos_ref, sin_ref, o_ref = rest
        reps = w // cos_ref.shape[1]
        cos = jnp.tile(cos_ref[...], (1, reps)) if reps > 1 else cos_ref[...]
        sin = jnp.tile(sin_ref[...], (1, reps)) if reps > 1 else sin_ref[...]
        lane = lax.broadcasted_iota(I32, x.shape, 1)
        partner = jnp.where(lane % 2 == 0, pltpu.roll(xn, w - 1, 1), pltpu.roll(xn, 1, 1))
        xn = xn * cos + partner * sin
    else:
        (o_ref,) = rest
    if scale != 1.0:
        xn = xn * scale
    o_ref[...] = xn.astype(o_ref.dtype)


def _rope_tables(l):
    half = HEAD_DIM // 2
    inv_freq = ROPE_THETA ** (-jnp.arange(0, half, 2, dtype=F32) / half)
    t = jnp.arange(l)
    row = (t // GRID_W).astype(F32)
    col = (t % GRID_W).astype(F32)
    ang = jnp.concatenate([jnp.repeat(row[:, None] * inv_freq[None], 2, axis=1),
                           jnp.repeat(col[:, None] * inv_freq[None], 2, axis=1)], axis=1)
    sign = jnp.where(jnp.arange(HEAD_DIM) % 2 == 0, -1.0, 1.0).astype(F32)
    cos = jnp.tile(jnp.cos(ang), (1, 2))
    sin = jnp.tile(jnp.sin(ang) * sign[None], (1, 2))
    return cos, sin


def qk_prep(x, gain, rope_tabs, scale):
    l, w = x.shape
    tm = _row_tile(l, 512)
    head = jnp.arange(w) // HEAD_DIM
    bd = (head[:, None] == head[None, :]).astype(BF16)
    gain_t = jnp.tile(gain.reshape(1, HEAD_DIM), (1, w // HEAD_DIM))
    row = lambda i: (i, 0)
    fixed = lambda i: (0, 0)
    in_specs = [pl.BlockSpec((tm, w), row), pl.BlockSpec((1, w), fixed), pl.BlockSpec((w, w), fixed)]
    args = [x, gain_t, bd]
    if rope_tabs is not None:
        in_specs += [pl.BlockSpec((tm, 2 * HEAD_DIM), row)] * 2
        args += list(rope_tabs)
    kern = functools.partial(_qk_prep_kernel, rope=rope_tabs is not None, scale=scale)
    return pl.pallas_call(
        kern, grid=(l // tm,), in_specs=in_specs, out_specs=pl.BlockSpec((tm, w), row),
        out_shape=jax.ShapeDtypeStruct((l, w), BF16),
        compiler_params=_cparams(("parallel",)), name="qk_prep",
    )(*args)


def _flash_kernel(q_ref, k_ref, v_ref, o_ref, m_ref, l_ref, acc_ref, *, tk, nk):
    hq, tq, dh = q_ref.shape
    q = q_ref[...].reshape(hq * tq, dh)
    m_ref[...] = jnp.full(m_ref.shape, NEG_BIG, F32)
    l_ref[...] = jnp.zeros(l_ref.shape, F32)
    acc_ref[...] = jnp.zeros(acc_ref.shape, F32)

    def body(j, carry):
        start = pl.multiple_of(j * tk, tk)
        kj = k_ref[pl.ds(start, tk), :]
        vj = v_ref[pl.ds(start, tk), :]
        s = _dot_nt(q, kj)
        m_old = m_ref[...]
        m_new = jnp.maximum(m_old, s.max(axis=1, keepdims=True))
        alpha = jnp.exp(m_old - m_new)
        p = jnp.exp(s - m_new)
        l_ref[...] = alpha * l_ref[...] + p.sum(axis=1, keepdims=True)
        acc_ref[...] = alpha * acc_ref[...] + _dot(p.astype(BF16), vj)
        m_ref[...] = m_new
        return carry

    lax.fori_loop(0, nk, body, 0)
    o_ref[...] = (acc_ref[...] / l_ref[...]).reshape(hq, tq, dh).astype(o_ref.dtype)


def _kv_chunk(lk):
    for tk in (640, 512, 384, 256, 128):
        if lk % tk == 0:
            return tk
    raise ValueError(lk)


def gqa_attention(q_hm, k_hm, v_hm):
    hq, l, dh = q_hm.shape
    hkv, lk, _ = k_hm.shape
    rep = hq // hkv
    tq = _row_tile(l, 256)
    tk = _kv_chunk(lk)
    kern = functools.partial(_flash_kernel, tk=tk, nk=lk // tk)
    rows = rep * tq
    return pl.pallas_call(
        kern,
        grid=(hkv, l // tq),
        in_specs=[pl.BlockSpec((rep, tq, dh), lambda g, i: (g, i, 0)),
                  pl.BlockSpec((None, lk, dh), lambda g, i: (g, 0, 0)),
                  pl.BlockSpec((None, lk, dh), lambda g, i: (g, 0, 0))],
        out_specs=pl.BlockSpec((rep, tq, dh), lambda g, i: (g, i, 0)),
        out_shape=jax.ShapeDtypeStruct((hq, l, dh), BF16),
        scratch_shapes=[pltpu.VMEM((rows, 1), F32), pltpu.VMEM((rows, 1), F32), pltpu.VMEM((rows, dh), F32)],
        compiler_params=_cparams(("parallel", "parallel")),
        name="gqa_attention",
    )(q_hm, k_hm, v_hm)


def _cmul(ar, ai, br, bi):
    return ar * br - ai * bi, ar * bi + ai * br


def _s5_operators(a_re, a_im, log_dt, b_re, b_im, c_re, c_im):
    t = S5_CHUNK
    gs = S5_GROUP
    hp = dict(precision=HIGHEST)
    dt = jnp.exp(log_dt)[..., None]
    zr, zi = a_re * dt, a_im * dt
    er = jnp.exp(zr)
    abr, abi = er * jnp.cos(zi), er * jnp.sin(zi)
    den = a_re * a_re + a_im * a_im
    fr = ((abr - 1.0) * a_re + abi * a_im) / den
    fi = (abi * a_re - (abr - 1.0) * a_im) / den
    bbr, bbi = _cmul(fr[..., None], fi[..., None], b_re, b_im)
    tau = jnp.arange(t + 1, dtype=F32)
    pr = jnp.exp(zr[..., None] * tau) * jnp.cos(zi[..., None] * tau)
    pi = jnp.exp(zr[..., None] * tau) * jnp.sin(zi[..., None] * tau)
    car, cai = _cmul(c_re[..., None], c_im[..., None], pr[:, :, None, :, :t], pi[:, :, None, :, :t])
    ktap = (jnp.einsum('dgqpt,dgpk->dgtqk', car, bbr, **hp) - jnp.einsum('dgqpt,dgpk->dgtqk', cai, bbi, **hp))
    i_idx = jnp.arange(t)[:, None]
    j_idx = jnp.arange(t)[None, :]
    lag = j_idx - i_idx
    blocks = ktap[:, :, jnp.clip(lag, 0, t - 1)]
    blocks = jnp.where((lag >= 0)[None, None, :, :, None, None], blocks, 0.0)
    m_op = blocks.transpose(0, 1, 2, 5, 3, 4).reshape(2, S5_GROUPS, t * gs, t * gs)
    rev = t - 1 - jnp.arange(t)
    wr, wi = _cmul(pr[..., rev][:, :, :, :, None], pi[..., rev][:, :, :, :, None], bbr[:, :, :, None, :], bbi[:, :, :, None, :])
    w_re = wr.transpose(0, 1, 3, 4, 2).reshape(2, S5_GROUPS, t * gs, S5_STATE)
    w_im = wi.transpose(0, 1, 3, 4, 2).reshape(2, S5_GROUPS, t * gs, S5_STATE)
    vr, vi = _cmul(c_re[..., None], c_im[..., None], pr[:, :, None, :, 1:], pi[:, :, None, :, 1:])
    v_re = vr.transpose(0, 1, 3, 4, 2).reshape(2, S5_GROUPS, S5_STATE, t * gs)
    v_im = (-vi).transpose(0, 1, 3, 4, 2).reshape(2, S5_GROUPS, S5_STATE, t * gs)
    return m_op, w_re, w_im, v_re, v_im, pr[..., t], pi[..., t]


def _pair_blockdiag(x):
    d, g, r, c = x.shape
    x = x.reshape(d, g // 2, 2, r, c)
    z = jnp.zeros_like(x[:, :, 0])
    top = jnp.concatenate([x[:, :, 0], z], axis=-1)
    bot = jnp.concatenate([z, x[:, :, 1]], axis=-1)
    return jnp.concatenate([top, bot], axis=-2)


def _s5_state_in_kernel(u_ref, w_ref, er_ref, ei_ref):
    e = _dot(u_ref[...], w_ref[...])
    half = e.shape[1] // 2
    er_ref[...] = e[:, :half]
    ei_ref[...] = e[:, half:]


def _s5_scan_kernel(ar_ref, ai_ref, er_ref, ei_ref, sr_ref, si_ref, cr_ref, ci_ref):
    @pl.when(pl.program_id(0) == 0)
    def _():
        cr_ref[...] = jnp.zeros_like(cr_ref)
        ci_ref[...] = jnp.zeros_like(ci_ref)

    ar, ai = ar_ref[...], ai_ref[...]

    def body(c, carry):
        sr, si = carry
        sr_ref[c] = sr
        si_ref[c] = si
        nr = ar * sr - ai * si + er_ref[c]
        ni = ar * si + ai * sr + ei_ref[c]
        return nr, ni

    sr, si = lax.fori_loop(0, er_ref.shape[0], body, (cr_ref[...], ci_ref[...]))
    cr_ref[...] = sr
    ci_ref[...] = si


def _s5_out_kernel(u_ref, sr_ref, si_ref, m_ref, vr_ref, vi_ref, y_ref):
    y = _dot(u_ref[...], m_ref[...])
    y = y + _dot(sr_ref[...].astype(BF16), vr_ref[...]) + _dot(si_ref[...].astype(BF16), vi_ref[...])
    y_ref[...] = y


def _largest_divisor(n, cap):
    for d in range(min(cap, n), 0, -1):
        if n % d == 0 and (d % SUBLANES == 0 or d == n):
            return d
    return n


def s5_scan_outputs(u_ctx, u_lat, a_re, a_im, log_dt, b_re, b_im, c_re, c_im):
    t, gs, g = S5_CHUNK, S5_GROUP, S5_GROUPS
    n_ctx, l = u_ctx.shape[0], u_lat.shape[0]
    n_tok = n_ctx + l
    nch = n_tok // t
    pairs = g // 2
    m_op, w_re, w_im, v_re, v_im, atr, ati = _s5_operators(a_re, a_im, log_dt, b_re, b_im, c_re, c_im)
    m_bd = _pair_blockdiag(m_op).astype(BF16)
    w_bd = jnp.concatenate([_pair_blockdiag(w_re), _pair_blockdiag(w_im)], axis=-1).astype(BF16)
    vr_bd = _pair_blockdiag(v_re).astype(BF16)
    vi_bd = _pair_blockdiag(v_im).astype(BF16)

    def chunked(seq):
        return seq.reshape(nch, t, g, gs).transpose(0, 2, 1, 3).reshape(nch, g * t * gs)

    fwd = jnp.concatenate([u_ctx, u_lat], axis=0)
    bwd = jnp.concatenate([u_ctx[::-1], u_lat[::-1]], axis=0)
    u_ch = jnp.stack([chunked(fwd), chunked(bwd)]).astype(BF16)
    pw = 2 * t * gs
    sw = 2 * S5_STATE
    n_state = g * S5_STATE

    e_re, e_im = pl.pallas_call(
        _s5_state_in_kernel,
        grid=(2, pairs),
        in_specs=[pl.BlockSpec((None, nch, pw), lambda d, p: (d, 0, p)),
                  pl.BlockSpec((None, None, pw, 2 * sw), lambda d, p: (d, p, 0, 0))],
        out_specs=[pl.BlockSpec((nch, sw), lambda d, p: (0, d * pairs + p))] * 2,
        out_shape=[jax.ShapeDtypeStruct((nch, 2 * n_state), F32)] * 2,
        compiler_params=_cparams(("parallel", "parallel")),
        name="s5_state_in",
    )(u_ch, w_bd)

    slab = 2 * n_state // SUBLANES
    tc = _largest_divisor(nch, 128)
    blk = pl.BlockSpec((tc, SUBLANES, slab), lambda i: (i, 0, 0))
    vec = pl.BlockSpec((SUBLANES, slab), lambda i: (0, 0))
    s_re, s_im = pl.pallas_call(
        _s5_scan_kernel,
        grid=(nch // tc,),
        in_specs=[vec, vec, blk, blk],
        out_specs=[blk, blk],
        out_shape=[jax.ShapeDtypeStruct((nch, SUBLANES, slab), F32)] * 2,
        scratch_shapes=[pltpu.VMEM((SUBLANES, slab), F32)] * 2,
        compiler_params=_cparams(("arbitrary",)),
        name="s5_scan",
    )(atr.reshape(SUBLANES, slab), ati.reshape(SUBLANES, slab),
      e_re.reshape(nch, SUBLANES, slab), e_im.reshape(nch, SUBLANES, slab))

    y = pl.pallas_call(
        _s5_out_kernel,
        grid=(2, pairs),
        in_specs=[pl.BlockSpec((None, nch, pw), lambda d, p: (d, 0, p)),
                  pl.BlockSpec((nch, sw), lambda d, p: (0, d * pairs + p)),
                  pl.BlockSpec((nch, sw), lambda d, p: (0, d * pairs + p)),
                  pl.BlockSpec((None, None, pw, pw), lambda d, p: (d, p, 0, 0)),
                  pl.BlockSpec((None, None, sw, pw), lambda d, p: (d, p, 0, 0)),
                  pl.BlockSpec((None, None, sw, pw), lambda d, p: (d, p, 0, 0))],
        out_specs=pl.BlockSpec((None, nch, pw), lambda d, p: (d, 0, p)),
        out_shape=jax.ShapeDtypeStruct((2, nch, g * t * gs), F32),
        compiler_params=_cparams(("parallel", "parallel")),
        name="s5_out",
    )(u_ch, s_re.reshape(nch, 2 * n_state), s_im.reshape(nch, 2 * n_state), m_bd, vr_bd, vi_bd)

    y = y.reshape(2, nch, g, t, gs).transpose(0, 1, 3, 2, 4).reshape(2, n_tok, g * gs)[:, n_ctx:]
    return y[0], y[1, ::-1]


def _s5_readout_kernel(yf_ref, yb_ref, u_ref, d_ref, w_ref, b_ref, o_ref):
    y = yf_ref[...] + yb_ref[...] + d_ref[...] * u_ref[...]
    y = 0.5 * y * (1.0 + jnp.tanh(math.sqrt(2.0 / math.pi) * (y + 0.044715 * (y * y * y))))
    gate = jax.nn.sigmoid(_dot(y.astype(BF16), w_ref[...]) + b_ref[...])
    o_ref[...] = (y * gate).astype(o_ref.dtype)


def s5_readout(y_f, y_b, u, d_skip, glu_w, glu_b):
    l, w = u.shape
    tm = _row_tile(l, 512)
    row = lambda i: (i, 0)
    fixed = lambda i: (0, 0)
    return pl.pallas_call(
        _s5_readout_kernel,
        grid=(l // tm,),
        in_specs=[pl.BlockSpec((tm, w), row)] * 3
                 + [pl.BlockSpec((1, w), fixed), pl.BlockSpec((w, w), fixed), pl.BlockSpec((1, w), fixed)],
        out_specs=pl.BlockSpec((tm, w), row),
        out_shape=jax.ShapeDtypeStruct((l, w), BF16),
        compiler_params=_cparams(("parallel",)),
        name="s5_readout",
    )(y_f, y_b, u, d_skip.reshape(1, w), glu_w.astype(BF16), glu_b.reshape(1, w))


def _first_max(vals, lane):
    m = vals.max(axis=1, keepdims=True)
    idx = jnp.where(vals == m, lane, jnp.int32(1 << 20)).min(axis=1, keepdims=True)
    return m, idx


def _router_kernel(x_ref, sc_ref, sh_ref, rw_ref, rb_ref, tri_ref,
                   hf_ref, te_ref, gt_ref, rk_ref, cnt_ref, run_ref):
    @pl.when(pl.program_id(0) == 0)
    def _():
        run_ref[...] = jnp.zeros_like(run_ref)

    hf = x_ref[...] * (1.0 + sc_ref[...]) + sh_ref[...]
    hf_ref[...] = hf
    tm = hf.shape[0]
    scores = jax.nn.sigmoid(_dot(hf, rw_ref[...], precision=HIGHEST))
    biased = scores + rb_ref[...]
    lane = lax.broadcasted_iota(I32, (tm, N_EXPERTS), 1)
    grp = lane // (N_EXPERTS // N_EXPERT_GROUPS)
    lane_o = lax.broadcasted_iota(I32, (tm, LANES), 1)
    neg = jnp.float32(-jnp.inf)

    group_score = jnp.full((tm, LANES), neg, F32)
    for g in range(N_EXPERT_GROUPS):
        vals = jnp.where(grp == g, biased, neg)
        m1, i1 = _first_max(vals, lane)
        m2 = jnp.where(lane == i1, neg, vals).max(axis=1, keepdims=True)
        group_score = jnp.where(lane_o == g, m1 + m2, group_score)
    keep = jnp.zeros((tm, N_EXPERTS), F32)
    for _ in range(TOPK_GROUPS):
        _, gi = _first_max(group_score, lane_o)
        keep = jnp.where(grp == gi, 1.0, keep)
        group_score = jnp.where(lane_o == gi, neg, group_score)

    masked = jnp.where(keep > 0.0, biased, neg)
    member = jnp.zeros((tm, N_EXPERTS), F32)
    e_cols, g_cols = [], []
    for _ in range(TOP_K):
        _, ei = _first_max(masked, lane)
        hit = lane == ei
        g_cols.append(jnp.where(hit, scores, 0.0).sum(axis=1, keepdims=True))
        masked = jnp.where(hit, neg, masked)
        member = jnp.where(hit, 1.0, member)
        e_cols.append(ei)
    g_sum = g_cols[0]
    for gk in g_cols[1:]:
        g_sum = g_sum + gk

    before = _dot(tri_ref[...], member.astype(BF16)) + run_ref[...]
    te = jnp.zeros((tm, LANES), I32)
    rk = jnp.zeros((tm, LANES), I32)
    gt = jnp.zeros((tm, LANES), F32)
    for k in range(TOP_K):
        rank = jnp.where(lane == e_cols[k], before, 0.0).sum(axis=1, keepdims=True)
        te = jnp.where(lane_o == k, e_cols[k], te)
        rk = jnp.where(lane_o == k, rank.astype(I32), rk)
        gt = jnp.where(lane_o == k, ROUTED_SCALE * g_cols[k] / g_sum, gt)
    te_ref[...] = te
    rk_ref[...] = rk
    gt_ref[...] = gt
    run_ref[...] += member.sum(axis=0, keepdims=True)
    cnt_ref[...] = run_ref[...]


def moe_route(x, sc2, sh2, n_ctx_tiles, router_w, router_bias):
    n, d = x.shape
    tm = MOE_TILE
    tri = (jnp.arange(tm)[None, :] < jnp.arange(tm)[:, None]).astype(BF16)
    row = lambda i: (i, 0)
    fixed = lambda i: (0, 0)
    seg = lambda i: (jnp.where(i < n_ctx_tiles, 1, 0), 0, 0)
    return pl.pallas_call(
        _router_kernel,
        grid=(n // tm,),
        in_specs=[pl.BlockSpec((tm, d), row), pl.BlockSpec((None, 1, d), seg), pl.BlockSpec((None, 1, d), seg),
                  pl.BlockSpec((d, N_EXPERTS), fixed), pl.BlockSpec((1, N_EXPERTS), fixed),
                  pl.BlockSpec((tm, tm), fixed)],
        out_specs=[pl.BlockSpec((tm, d), row), pl.BlockSpec((tm, LANES), row), pl.BlockSpec((tm, LANES), row),
                   pl.BlockSpec((tm, LANES), row), pl.BlockSpec((1, N_EXPERTS), fixed)],
        out_shape=[jax.ShapeDtypeStruct((n, d), F32), jax.ShapeDtypeStruct((n, LANES), I32),
                   jax.ShapeDtypeStruct((n, LANES), F32), jax.ShapeDtypeStruct((n, LANES), I32),
                   jax.ShapeDtypeStruct((1, N_EXPERTS), F32)],
        scratch_shapes=[pltpu.VMEM((1, N_EXPERTS), F32)],
        compiler_params=_cparams(("arbitrary",)),
        name="moe_route",
    )(x, sc2, sh2, router_w, router_bias.reshape(1, N_EXPERTS), tri)


IDX_TILE = 1024


def _row_gather(idx_ref, base, count, src_hbm, dst, sem):
    def body(j, carry):
        tok = idx_ref[base + j]
        pltpu.make_async_copy(src_hbm.at[tok], dst.at[pl.ds(j * ROW_SLAB, ROW_SLAB), :], sem).start()
        return carry
    lax.fori_loop(0, count, body, 0)


def _wait_rows(dst, sem):
    pltpu.make_async_copy(dst, dst, sem).wait()


def _gathered_rows(buf, first, rows, stride):
    return jnp.concatenate(
        [buf[pl.ds(first * ROW_SLAB + s, rows, stride=stride * ROW_SLAB), :] for s in range(ROW_SLAB)], axis=1)


def _expert_kernel(be_ref, nu_ref, cur_ref, nxt_ref, hf_hbm, wg_ref, wu_ref, wd_ref, y_ref, xbuf, sem):
    i = pl.program_id(0)
    n_used = nu_ref[0]
    per = IDX_TILE // EXPERT_BLOCK
    slot = i % 2

    @pl.when(i == 0)
    def _():
        _row_gather(cur_ref, 0, EXPERT_BLOCK, hf_hbm, xbuf.at[0], sem.at[0])

    @pl.when(i + 1 < n_used)
    def _():
        base = ((i + 1) % per) * EXPERT_BLOCK
        _row_gather(nxt_ref, base, EXPERT_BLOCK, hf_hbm, xbuf.at[1 - slot], sem.at[1 - slot])

    @pl.when(i < n_used)
    def _():
        _wait_rows(xbuf.at[slot], sem.at[slot])
        x = _gathered_rows(xbuf.at[slot], 0, EXPERT_BLOCK, 1).astype(BF16)
        gate = _dot(x, wg_ref[...].astype(BF16))
        up = _dot(x, wu_ref[...].astype(BF16))
        act = (gate * jax.nn.sigmoid(gate) * up).astype(BF16)
        y_ref[...] = _dot(act, wd_ref[...].astype(BF16))

    @pl.when(i >= n_used)
    def _():
        y_ref[...] = jnp.zeros_like(y_ref)


def moe_experts(hf, slot_tok, block_e, n_used, w_gate, w_up, w_down):
    n, d = hf.shape
    n_blocks = block_e.shape[0]
    ff = w_gate.shape[2]
    per = IDX_TILE // EXPERT_BLOCK
    n_idx_tiles = n_blocks // per
    grid_spec = pltpu.PrefetchScalarGridSpec(
        num_scalar_prefetch=2,
        grid=(n_blocks,),
        in_specs=[pl.BlockSpec((IDX_TILE,), lambda i, be, nu: (i // per,), memory_space=pltpu.SMEM),
                  pl.BlockSpec((IDX_TILE,), lambda i, be, nu: (jnp.minimum((i + 1) // per, n_idx_tiles - 1),),
                               memory_space=pltpu.SMEM),
                  pl.BlockSpec(memory_space=pl.ANY),
                  pl.BlockSpec((None, d, ff), lambda i, be, nu: (be[i], 0, 0)),
                  pl.BlockSpec((None, d, ff), lambda i, be, nu: (be[i], 0, 0)),
                  pl.BlockSpec((None, ff, d), lambda i, be, nu: (be[i], 0, 0))],
        out_specs=pl.BlockSpec((EXPERT_BLOCK, d), lambda i, be, nu: (i, 0)),
        scratch_shapes=[pltpu.VMEM((2, EXPERT_BLOCK * ROW_SLAB, LANES), F32), pltpu.SemaphoreType.DMA((2,))],
    )
    return pl.pallas_call(
        _expert_kernel,
        grid_spec=grid_spec,
        out_shape=jax.ShapeDtypeStruct((n_blocks * EXPERT_BLOCK, d), F32),
        compiler_params=_cparams(("arbitrary",)),
        name="moe_experts",
    )(block_e, n_used, slot_tok, slot_tok, hf.reshape(n, ROW_SLAB, LANES), w_gate, w_up, w_down)


def _combine_kernel(cur_ref, nxt_ref, ys_hbm, x_ref, hf_ref, gt_ref, g2_ref, sg_ref, su_ref, sd_ref,
                    lg_ref, lb_ref, o_ref, ybuf, sem):
    i = pl.program_id(0)
    n_tiles = pl.num_programs(0)
    tm = x_ref.shape[0]
    n_rows = tm * TOP_K
    slot = i % 2

    @pl.when(i == 0)
    def _():
        _row_gather(cur_ref, 0, n_rows, ys_hbm, ybuf.at[0], sem.at[0])

    @pl.when(i + 1 < n_tiles)
    def _():
        _row_gather(nxt_ref, 0, n_rows, ys_hbm, ybuf.at[1 - slot], sem.at[1 - slot])

    hf = hf_ref[...].astype(BF16)
    gate = _dot(hf, sg_ref[...])
    up = _dot(hf, su_ref[...])
    y = _dot((gate * jax.nn.sigmoid(gate) * up).astype(BF16), sd_ref[...])

    _wait_rows(ybuf.at[slot], sem.at[slot])
    for k in range(TOP_K):
        y = y + gt_ref[:, k:k + 1] * _gathered_rows(ybuf.at[slot], k, tm, TOP_K)
    r = DEEPNORM_ALPHA * x_ref[...] + g2_ref[...] * y
    o_ref[...] = _layer_norm_rows(r, lg_ref[...], lb_ref[...])


def moe_combine(dest, ys, x, hf, gate, g2, n_ctx_tiles, sh_gate, sh_up, sh_down, ln_g, ln_b):
    n, d = x.shape
    tm = MOE_TILE
    n_tiles = n // tm
    n_rows = tm * TOP_K
    ff = sh_gate.shape[1]
    row = lambda i: (i, 0)
    fixed = lambda i: (0, 0)
    seg = lambda i: (jnp.where(i < n_ctx_tiles, 1, 0), 0, 0)
    return pl.pallas_call(
        _combine_kernel,
        grid=(n_tiles,),
        in_specs=[pl.BlockSpec((n_rows,), lambda i: (i,), memory_space=pltpu.SMEM),
                  pl.BlockSpec((n_rows,), lambda i: (jnp.minimum(i + 1, n_tiles - 1),), memory_space=pltpu.SMEM),
                  pl.BlockSpec(memory_space=pl.ANY),
                  pl.BlockSpec((tm, d), row), pl.BlockSpec((tm, d), row), pl.BlockSpec((tm, LANES), row),
                  pl.BlockSpec((None, 1, d), seg),
                  pl.BlockSpec((d, ff), fixed), pl.BlockSpec((d, ff), fixed), pl.BlockSpec((ff, d), fixed),
                  pl.BlockSpec((1, d), fixed), pl.BlockSpec((1, d), fixed)],
        out_specs=pl.BlockSpec((tm, d), row),
        out_shape=jax.ShapeDtypeStruct((n, d), F32),
        scratch_shapes=[pltpu.VMEM((2, n_rows * ROW_SLAB, LANES), F32), pltpu.SemaphoreType.DMA((2,))],
        compiler_params=_cparams(("arbitrary",)),
        name="moe_combine",
    )(dest, dest, ys.reshape(ys.shape[0], ROW_SLAB, LANES), x, hf, gate, g2,
      sh_gate.astype(BF16), sh_up.astype(BF16), sh_down.astype(BF16), ln_g.reshape(1, d), ln_b.reshape(1, d))


def moe_layer(x, sc2, sh2, g2, n_ctx_tiles, router_w, router_bias, w_gate, w_up, w_down,
              sh_gate, sh_up, sh_down, ln_g, ln_b):
    n, _ = x.shape
    hf, te, gt, rk, cnt = moe_route(x, sc2, sh2, n_ctx_tiles, router_w, router_bias)
    counts = cnt[0].astype(I32)
    padded = (counts + EXPERT_BLOCK - 1) // EXPERT_BLOCK * EXPERT_BLOCK
    pad_end = jnp.cumsum(padded)
    pad_start = pad_end - padded
    dest = (pad_start[te[:, :TOP_K]] + rk[:, :TOP_K]).reshape(-1)
    per = IDX_TILE // EXPERT_BLOCK
    n_blocks = -(-(n * TOP_K + N_EXPERTS * (EXPERT_BLOCK - 1)) // EXPERT_BLOCK)
    n_blocks = -(-n_blocks // per) * per
    slot_tok = jnp.zeros((n_blocks * EXPERT_BLOCK,), I32).at[dest].set(jnp.arange(n * TOP_K, dtype=I32) // TOP_K)
    n_used = pad_end[-1] // EXPERT_BLOCK
    blk = jnp.arange(n_blocks, dtype=I32)
    block_e = jnp.minimum(jnp.searchsorted(pad_end, blk * EXPERT_BLOCK, side='right'), N_EXPERTS - 1).astype(I32)
    block_e = jnp.where(blk < n_used, block_e, block_e[jnp.maximum(n_used - 1, 0)])
    ys = moe_experts(hf, slot_tok, block_e, n_used.reshape(1).astype(I32), w_gate, w_up, w_down)
    return moe_combine(dest, ys, x, hf, gt, g2, n_ctx_tiles, sh_gate, sh_up, sh_down, ln_g, ln_b)


def kernel(x, c, ctx, c_ctx, w_mod, b_mod, ln_mix_g, ln_mix_b, ln_ffn_g, ln_ffn_b, ab_w_in, ab_w_out, na_rpb,
           hy_conv_w, hy_conv_b, hy_f_w1, hy_f_b1, hy_f_freq, hy_f_w2, hy_f_b2, hy_f_w3, hy_skip, cd_w_in, cd_w_out,
           q_norm_g, k_norm_g, s5_a_re, s5_a_im, s5_log_dt, s5_b_re, s5_b_im, s5_c_re, s5_c_im, s5_d, s5_glu_w,
           s5_glu_b, router_w, router_bias, exp_w_gate, exp_w_up, exp_w_down, sh_w_gate, sh_w_up, sh_w_down):
    b, l, d = x.shape
    assert b == 1
    n_ctx = ctx.shape[1]
    assert n_ctx == MOE_TILE
    xs = x[0]
    cs = ctx[0]
    cmat = jnp.zeros((SUBLANES, d), F32).at[0].set(c[0]).at[1].set(c_ctx)
    mods = modulation_all(cmat, w_mod, b_mod).reshape(DEPTH, SUBLANES, 6, d)
    qscale = HEAD_DIM ** -0.5

    for i in range(DEPTH):
        need_ctx = i < DEPTH - 1
        m = mods[i]
        sh1, sc1, g1, sh2, sc2, g2 = [m[0:1, t] for t in range(6)]
        csh1, csc1, cg1, csh2, csc2, cg2 = [m[1:2, t] for t in range(6)]
        j = i // 2
        if i % 2 == 0:
            filt = (hy_conv_w[j], hy_conv_b[j], hy_f_w1[j], hy_f_b1[j], hy_f_freq[j], hy_f_w2[j], hy_f_b2[j],
                    hy_f_w3[j], hy_skip[j])
            splits = (NA_WIDTH, NA_WIDTH, NA_WIDTH, 3 * HY_WIDTH)
            dts = (BF16, BF16, BF16, F32)
            scl = (qscale, 1.0, 1.0, 1.0)
            q_l, k_l, v_l, u_l = mod_project(xs, sc1, sh1, ab_w_in[j], splits, dts, scl)
            q_c, k_c, v_c, u_c = mod_project(cs, csc1, csh1, ab_w_in[j], splits, dts, scl)
            a_lat = neighbourhood_attention(q_l, k_l, v_l, k_c, v_c, na_rpb[j])
            y_hy = hyena_long(u_l, *filt)
            xs_new = outproj_ln(a_lat, y_hy, ab_w_out[j], xs, g1, ln_mix_g[i], ln_mix_b[i])
            if need_ctx:
                a_ctx = context_attention(q_c, k_c, v_c)
                yc_hy = hyena_long(u_c, *filt)
                cs = outproj_ln(a_ctx, yc_hy, ab_w_out[j], cs, cg1, ln_mix_g[i], ln_mix_b[i])
            xs = xs_new
        else:
            splits = (GQA_WIDTH, GQA_KV_WIDTH, GQA_KV_WIDTH, S5_WIDTH)
            q_l, k_l, v_l, u_l = mod_project(xs, sc1, sh1, cd_w_in[j], splits, (F32, F32, BF16, F32))
            k_c, v_c, u_c = mod_project(cs, csc1, csh1, cd_w_in[j][:, GQA_WIDTH:], splits[1:], (F32, BF16, F32))
            tabs = _rope_tables(l)
            qn = qk_prep(q_l, q_norm_g[j], tabs, qscale)
            kn = qk_prep(k_l, k_norm_g[j], tabs, 1.0)
            kcn = qk_prep(k_c, k_norm_g[j], None, 1.0)
            heads = lambda t, h: t.reshape(t.shape[0], h, HEAD_DIM).transpose(1, 0, 2)
            att = gqa_attention(heads(qn, GQA_HEADS), heads(jnp.concatenate([kn, kcn], axis=0), GQA_KV_HEADS),
                                heads(jnp.concatenate([v_l, v_c], axis=0), GQA_KV_HEADS))
            att = att.transpose(1, 0, 2).reshape(l, GQA_WIDTH)
            y_f, y_b = s5_scan_outputs(u_c, u_l, s5_a_re[j], s5_a_im[j], s5_log_dt[j], s5_b_re[j], s5_b_im[j],
                                       s5_c_re[j], s5_c_im[j])
            ssm = s5_readout(y_f, y_b, u_l, s5_d[j], s5_glu_w[j], s5_glu_b[j])
            xs = outproj_ln(att, ssm, cd_w_out[j], xs, g1, ln_mix_g[i], ln_mix_b[i])
            assert not need_ctx

        moe_w = (router_w[i], router_bias[i], exp_w_gate[i], exp_w_up[i], exp_w_down[i],
                 sh_w_gate[i], sh_w_up[i], sh_w_down[i], ln_ffn_g[i], ln_ffn_b[i])
        stack2 = lambda lat, cx: jnp.stack([lat, cx])
        if need_ctx:
            toks = jnp.concatenate([cs, xs], axis=0)
            out = moe_layer(toks, stack2(sc2, csc2), stack2(sh2, csh2), stack2(g2, cg2), n_ctx // MOE_TILE, *moe_w)
            cs, xs = out[:n_ctx], out[n_ctx:]
        else:
            xs = moe_layer(xs, stack2(sc2, csc2), stack2(sh2, csh2), stack2(g2, cg2), 0, *moe_w)
    return xs.reshape(b, l, d)
```

```python
import functools
import math

import jax
import jax.numpy as jnp
import numpy as np
from jax import lax
from jax.experimental import pallas as pl
from jax.experimental.pallas import tpu as pltpu

F32 = jnp.float32
BF16 = jnp.bfloat16
I32 = jnp.int32
HIGHEST = lax.Precision.HIGHEST

LANES = 128
SUBLANES = 8
VMEM_LIMIT = 56 * 1024 * 1024

D_MODEL = 1024
DEPTH = 2
GRID_W = 64
HEAD_DIM = 64
NA_HEADS = 8
NA_WIDTH = NA_HEADS * HEAD_DIM
NA_KH = 8
NA_KW = 16
HY_WIDTH = D_MODEL - NA_WIDTH
HY_BANDS = 16
HY_DECAY_PCT_MIN = 0.3
HY_DECAY_PCT_MAX = 1.5
HY_DECAY_TARGET = 1e-2
GQA_HEADS = 8
GQA_KV_HEADS = 2
GQA_WIDTH = GQA_HEADS * HEAD_DIM
GQA_KV_WIDTH = GQA_KV_HEADS * HEAD_DIM
ROPE_THETA = 10000.0
S5_WIDTH = D_MODEL - GQA_WIDTH
S5_GROUP = 16
S5_GROUPS = S5_WIDTH // S5_GROUP
S5_STATE = 64
N_EXPERTS = 256
TOP_K = 8
N_EXPERT_GROUPS = 8
TOPK_GROUPS = 4
EXPERT_FF = 256
ROUTED_SCALE = 2.5
EXPERT_BLOCK = 128
DEEPNORM_ALPHA = (2.0 * DEPTH) ** 0.25
LN_EPS = 1e-5
RMS_EPS = 1e-6

NEG_BIG = -1e30
NA_TILE_ROWS = 8
NA_KEY_ROWS = 16
NA_KEY_BLOCK_ROWS = 4
DFT_N1 = 128
S5_CHUNK = 16
MOE_TILE = 256
ROW_SLAB = D_MODEL // LANES

NT_DIMS = (((1,), (1,)), ((), ()))


def _cparams(sem, **kw):
    return pltpu.CompilerParams(dimension_semantics=sem, vmem_limit_bytes=VMEM_LIMIT, **kw)


def _dot(a, b, **kw):
    return jnp.dot(a, b, preferred_element_type=F32, **kw)


def _dot_nt(a, b):
    return lax.dot_general(a, b, NT_DIMS, preferred_element_type=F32)


def _row_tile(m, pref):
    return pref if m % pref == 0 else m


def _mod_kernel(c_ref, w_ref, b_ref, o_ref):
    cv = c_ref[...]
    s = cv * jax.nn.sigmoid(cv)
    o_ref[...] = _dot(s, w_ref[...], precision=HIGHEST) + b_ref[...]


def modulation_all(cmat, w_mod, b_mod):
    depth, d, n = w_mod.shape
    tn = 1536
    return pl.pallas_call(
        _mod_kernel,
        grid=(depth, n // tn),
        in_specs=[pl.BlockSpec((SUBLANES, d), lambda l, j: (0, 0)),
                  pl.BlockSpec((None, d, tn), lambda l, j: (l, 0, j)),
                  pl.BlockSpec((None, 1, tn), lambda l, j: (l, 0, j))],
        out_specs=pl.BlockSpec((None, SUBLANES, tn), lambda l, j: (l, 0, j)),
        out_shape=jax.ShapeDtypeStruct((depth, SUBLANES, n), F32),
        compiler_params=_cparams(("arbitrary", "arbitrary")),
        name="modulation",
    )(cmat, w_mod, b_mod.reshape(depth, 1, n))


def _proj_kernel(x_ref, sc_ref, sh_ref, w_ref, *o_refs, splits, scales):
    h = (x_ref[...] * (1.0 + sc_ref[...]) + sh_ref[...]).astype(BF16)
    off = 0
    for o_ref, wd, sc in zip(o_refs, splits, scales):
        y = _dot(h, w_ref[:, off:off + wd])
        if sc != 1.0:
            y = y * sc
        o_ref[...] = y.astype(o_ref.dtype)
        off += wd


def mod_project(x, sc, sh, w, splits, dtypes, scales=None):
    m, d = x.shape
    n = w.shape[1]
    assert sum(splits) == n
    scales = scales or (1.0,) * len(splits)
    tm = _row_tile(m, 512)
    kern = functools.partial(_proj_kernel, splits=tuple(splits), scales=tuple(scales))
    return pl.pallas_call(
        kern,
        grid=(m // tm,),
        in_specs=[pl.BlockSpec((tm, d), lambda i: (i, 0)),
                  pl.BlockSpec((1, d), lambda i: (0, 0)),
                  pl.BlockSpec((1, d), lambda i: (0, 0)),
                  pl.BlockSpec((d, n), lambda i: (0, 0))],
        out_specs=[pl.BlockSpec((tm, wd), lambda i: (i, 0)) for wd in splits],
        out_shape=[jax.ShapeDtypeStruct((m, wd), dt) for wd, dt in zip(splits, dtypes)],
        compiler_params=_cparams(("parallel",)),
        name="mod_project",
    )(x, sc, sh, w.astype(BF16))


def _layer_norm_rows(r, g, b):
    mu = jnp.mean(r, axis=-1, keepdims=True)
    c = r - mu
    var = jnp.mean(c * c, axis=-1, keepdims=True)
    return c * lax.rsqrt(var + LN_EPS) * g + b


def _outproj_ln_kernel(a_ref, b_ref, w_ref, x_ref, gate_ref, g_ref, beta_ref, o_ref):
    ka = a_ref.shape[1]
    y = _dot(a_ref[...], w_ref[:ka, :]) + _dot(b_ref[...], w_ref[ka:, :])
    r = DEEPNORM_ALPHA * x_ref[...] + gate_ref[...] * y
    o_ref[...] = _layer_norm_rows(r, g_ref[...], beta_ref[...])


def outproj_ln(a, b, w, x, gate, g, beta):
    m, d = x.shape
    ka, kb = a.shape[1], b.shape[1]
    tm = _row_tile(m, 512)
    row = lambda i: (i, 0)
    fixed = lambda i: (0, 0)
    return pl.pallas_call(
        _outproj_ln_kernel,
        grid=(m // tm,),
        in_specs=[pl.BlockSpec((tm, ka), row), pl.BlockSpec((tm, kb), row),
                  pl.BlockSpec((ka + kb, d), fixed), pl.BlockSpec((tm, d), row),
                  pl.BlockSpec((1, d), fixed), pl.BlockSpec((1, d), fixed), pl.BlockSpec((1, d), fixed)],
        out_specs=pl.BlockSpec((tm, d), row),
        out_shape=jax.ShapeDtypeStruct((m, d), F32),
        compiler_params=_cparams(("parallel",)),
        name="outproj_ln",
    )(a, b, w.astype(BF16), x, gate, g.reshape(1, d), beta.reshape(1, d))


def _na_bias_table(rpb, rows):
    h = rpb.shape[0]
    ri = np.arange(NA_TILE_ROWS)
    kr = np.arange(NA_KEY_ROWS)
    c = np.arange(GRID_W)
    cs = np.clip(c - NA_KW // 2, 0, GRID_W - NA_KW)
    vc = (c[None, :] >= cs[:, None]) & (c[None, :] < cs[:, None] + NA_KW)
    dc = np.clip(c[None, :] - c[:, None] + NA_KW - 1, 0, 2 * NA_KW - 2)
    pick = (dc.reshape(-1)[:, None] == np.arange(2 * NA_KW - 1)[None, :]).astype(np.float32)
    colb = jnp.einsum('qb,hab->haq', pick, rpb, precision=HIGHEST).reshape(h, 2 * NA_KH - 1, GRID_W, GRID_W)
    colb = jnp.where(vc[None, None], colb, NEG_BIG)

    def case(t):
        start = min(max(NA_TILE_ROWS * t - NA_KH // 2, 0), rows - NA_KEY_ROWS)
        r = NA_TILE_ROWS * t + ri
        rs = np.clip(r - NA_KH // 2, 0, rows - NA_KH)
        krow = start + kr
        vr = (krow[None, :] >= rs[:, None]) & (krow[None, :] < rs[:, None] + NA_KH)
        dr = np.clip(krow[None, :] - r[:, None] + NA_KH - 1, 0, 2 * NA_KH - 2)
        b = jnp.take(colb, dr.reshape(-1), axis=1).reshape(h, NA_TILE_ROWS, NA_KEY_ROWS, GRID_W, GRID_W)
        b = jnp.where(vr[None, :, :, None, None], b, NEG_BIG)
        b = b.transpose(0, 1, 3, 2, 4)
        return b.reshape(h // 2, 2, NA_TILE_ROWS * GRID_W, NA_KEY_ROWS * GRID_W)

    n_tiles = rows // NA_TILE_ROWS
    return jnp.stack([case(0), case(1), case(n_tiles - 1)])


def _pair_masks(shape):
    lane = lax.broadcasted_iota(I32, shape, 1)
    return lane < HEAD_DIM


def _na_kernel(q_ref, k0, k1, k2, k3, v0, v1, v2, v3, kc_ref, vc_ref, bias_ref, o_ref):
    q = q_ref[...]
    lo = _pair_masks(q.shape)
    ks = (k0, k1, k2, k3)
    vs = (v0, v1, v2, v3)
    kb = k0.shape[0]
    outs = []
    for hh in range(2):
        qh = jnp.where(lo if hh == 0 else jnp.logical_not(lo), q, jnp.zeros_like(q))
        s = [_dot_nt(qh, ks[i][...]) + bias_ref[hh, :, i * kb:(i + 1) * kb] for i in range(4)]
        s.append(_dot_nt(qh, kc_ref[...]))
        m = s[0].max(axis=1, keepdims=True)
        for si in s[1:]:
            m = jnp.maximum(m, si.max(axis=1, keepdims=True))
        p = [jnp.exp(si - m) for si in s]
        l = p[0].sum(axis=1, keepdims=True)
        for pi in p[1:]:
            l = l + pi.sum(axis=1, keepdims=True)
        acc = _dot(p[4].astype(BF16), vc_ref[...])
        for i in range(4):
            acc = acc + _dot(p[i].astype(BF16), vs[i][...])
        outs.append(acc / l)
    o_ref[...] = jnp.where(lo, outs[0], outs[1]).astype(o_ref.dtype)


def neighbourhood_attention(q, k, v, k_ctx, v_ctx, rpb):
    l, w = q.shape
    rows = l // GRID_W
    n_tiles = rows // NA_TILE_ROWS
    assert n_tiles >= 3 and rows % NA_TILE_ROWS == 0
    n_ctx = k_ctx.shape[0]
    tq = NA_TILE_ROWS * GRID_W
    kb = NA_KEY_BLOCK_ROWS * GRID_W
    n_kblk = rows // NA_KEY_BLOCK_ROWS
    bias = _na_bias_table(rpb, rows)
    pair_w = 2 * HEAD_DIM

    def kv_spec(i):
        def imap(p, t):
            start = jnp.clip(2 * t - 1, 0, n_kblk - 4)
            return (start + i, p)
        return pl.BlockSpec((kb, pair_w), imap)

    def bias_map(p, t):
        case = jnp.where(t == 0, 0, jnp.where(t == n_tiles - 1, 2, 1))
        return (case, p, 0, 0, 0)

    return pl.pallas_call(
        _na_kernel,
        grid=(w // pair_w, n_tiles),
        in_specs=[pl.BlockSpec((tq, pair_w), lambda p, t: (t, p))]
                 + [kv_spec(i) for i in range(4)] + [kv_spec(i) for i in range(4)]
                 + [pl.BlockSpec((n_ctx, pair_w), lambda p, t: (0, p)),
                    pl.BlockSpec((n_ctx, pair_w), lambda p, t: (0, p)),
                    pl.BlockSpec((None, None, 2, tq, NA_KEY_ROWS * GRID_W), bias_map)],
        out_specs=pl.BlockSpec((tq, pair_w), lambda p, t: (t, p)),
        out_shape=jax.ShapeDtypeStruct((l, w), BF16),
        compiler_params=_cparams(("parallel", "parallel")),
        name="neighbourhood_attention",
    )(q, k, k, k, k, v, v, v, v, k_ctx, v_ctx, bias)


def _ctx_attn_kernel(q_ref, k_ref, v_ref, o_ref):
    q = q_ref[...]
    lo = _pair_masks(q.shape)
    outs = []
    for hh in range(2):
        qh = jnp.where(lo if hh == 0 else jnp.logical_not(lo), q, jnp.zeros_like(q))
        s = _dot_nt(qh, k_ref[...])
        p = jnp.exp(s - s.max(axis=1, keepdims=True))
        outs.append(_dot(p.astype(BF16), v_ref[...]) / p.sum(axis=1, keepdims=True))
    o_ref[...] = jnp.where(lo, outs[0], outs[1]).astype(o_ref.dtype)


def context_attention(q, k, v):
    n, w = q.shape
    pair_w = 2 * HEAD_DIM
    spec = pl.BlockSpec((n, pair_w), lambda p: (0, p))
    return pl.pallas_call(
        _ctx_attn_kernel, grid=(w // pair_w,), in_specs=[spec, spec, spec], out_specs=spec,
        out_shape=jax.ShapeDtypeStruct((n, w), BF16),
        compiler_params=_cparams(("parallel",)), name="context_attention",
    )(q, k, v)


def _shortconv_kernel(u_ref, up_ref, un_ref, w_ref, b_ref, x0_ref, z_ref, *, n_tiles):
    i = pl.program_id(0)
    u = u_ref[...]
    tm = u.shape[0]
    prev_row = jnp.where(i > 0, up_ref[SUBLANES - 1:SUBLANES, :], 0.0)
    next_row = jnp.where(i < n_tiles - 1, un_ref[0:1, :], 0.0)
    row = lax.broadcasted_iota(I32, u.shape, 0)
    u_dn = jnp.where(row == 0, prev_row, pltpu.roll(u, 1, 0))
    u_up = jnp.where(row == tm - 1, next_row, pltpu.roll(u, tm - 1, 0))
    y = u_dn * w_ref[0:1, :] + u * w_ref[1:2, :] + u_up * w_ref[2:3, :] + b_ref[...]
    c = HY_WIDTH
    x0_ref[...] = y[:, :c]
    z_ref[...] = y[:, c:2 * c] * y[:, 2 * c:]


def hyena_gate(u, conv_w, conv_b):
    l, w3 = u.shape
    tm = _row_tile(l, 512)
    n_tiles = l // tm
    per = tm // SUBLANES
    last = l // SUBLANES - 1
    kern = functools.partial(_shortconv_kernel, n_tiles=n_tiles)
    return pl.pallas_call(
        kern,
        grid=(n_tiles,),
        in_specs=[pl.BlockSpec((tm, w3), lambda i: (i, 0)),
                  pl.BlockSpec((SUBLANES, w3), lambda i: (jnp.maximum(i * per - 1, 0), 0)),
                  pl.BlockSpec((SUBLANES, w3), lambda i: (jnp.minimum((i + 1) * per, last), 0)),
                  pl.BlockSpec((3, w3), lambda i: (0, 0)),
                  pl.BlockSpec((1, w3), lambda i: (0, 0))],
        out_specs=[pl.BlockSpec((tm, HY_WIDTH), lambda i: (i, 0))] * 2,
        out_shape=[jax.ShapeDtypeStruct((l, HY_WIDTH), F32)] * 2,
        compiler_params=_cparams(("parallel",)),
        name="hyena_gate",
    )(u, u, u, conv_w, conv_b.reshape(1, w3))


def _filter_kernel(bands_ref, w1t_ref, w1c_ref, w1s_ref, b1_ref, fr_ref, w2_ref, b2_ref, w3_ref, dl_ref,
                   taps_ref, asum_ref, *, l, tp):
    i = pl.program_id(0)
    hid_w = w2_ref.shape[0]
    c = HY_WIDTH
    denom = float(max(l - 1, 1))

    def pos(width):
        return (lax.broadcasted_iota(I32, (tp, width), 0) + i * tp).astype(F32)

    ang = (2.0 * math.pi / l) * pos(HY_BANDS) * bands_ref[...]
    pre = ((pos(hid_w) / denom) * w1t_ref[...]
           + _dot(jnp.cos(ang), w1c_ref[...], precision=HIGHEST)
           + _dot(-jnp.sin(ang), w1s_ref[...], precision=HIGHEST) + b1_ref[...])
    hid = jnp.sin(fr_ref[...] * pre)
    hid = jnp.sin(fr_ref[...] * (_dot(hid, w2_ref[...], precision=HIGHEST) + b2_ref[...]))
    taps = _dot(hid, w3_ref[...], precision=HIGHEST)
    pc = pos(c)
    window = jnp.exp(-(pc / denom) * dl_ref[...])
    fwd = taps[:, :c] * window
    bwd = jnp.where(pc == 0.0, 0.0, taps[:, c:] * window)
    taps_ref[:, :c] = fwd
    taps_ref[:, c:] = bwd

    @pl.when(i == 0)
    def _():
        asum_ref[...] = jnp.zeros_like(asum_ref)

    asum_ref[...] += jnp.sum(jnp.abs(fwd) + jnp.abs(bwd), axis=0, keepdims=True)


def hyena_filter_taps(l, f_w1, f_b1, f_freq, f_w2, f_b2, f_w3):
    c = HY_WIDTH
    hid = f_w2.shape[0]
    tp = _row_tile(l, 1024)
    bands = jnp.linspace(1e-4, HY_BANDS - 1, HY_BANDS, dtype=F32).reshape(1, HY_BANDS)
    deltas = jnp.abs(jnp.linspace(math.log(HY_DECAY_TARGET) / HY_DECAY_PCT_MAX,
                                  math.log(HY_DECAY_TARGET) / HY_DECAY_PCT_MIN, c, dtype=F32)).reshape(1, c)
    fixed = lambda i: (0, 0)
    full = lambda a: pl.BlockSpec(a.shape, fixed)
    args = (bands, f_w1[0:1], f_w1[1:1 + HY_BANDS], f_w1[1 + HY_BANDS:], f_b1.reshape(1, hid),
            f_freq.reshape(1, hid), f_w2, f_b2.reshape(1, hid), f_w3, deltas)
    kern = functools.partial(_filter_kernel, l=l, tp=tp)
    return pl.pallas_call(
        kern,
        grid=(l // tp,),
        in_specs=[full(a) for a in args],
        out_specs=[pl.BlockSpec((tp, 2 * c), lambda i: (i, 0)), pl.BlockSpec((1, c), fixed)],
        out_shape=[jax.ShapeDtypeStruct((l, 2 * c), F32), jax.ShapeDtypeStruct((1, c), F32)],
        compiler_params=_cparams(("arbitrary",)),
        name="hyena_filter",
    )(*args)


def _dft_tables(l):
    n = 2 * l
    n1 = DFT_N1
    n2 = n // n1
    k1 = jnp.arange(n1)[:, None]
    m1 = jnp.arange(n1 // 2)[None, :]
    ph1 = (2.0 * math.pi / n1) * ((k1 * m1) % n1).astype(F32)
    d1 = jnp.stack([jnp.cos(ph1), -jnp.sin(ph1)], axis=1).reshape(2 * n1, n1 // 2)
    d1_inv = d1.T
    kk = (jnp.arange(n2)[None, :, None] * n1 + jnp.arange(n1)[:, None, None])
    th = (2.0 * math.pi / n) * ((kk * jnp.arange(n2)[None, None, :]) % n).astype(F32)
    cs, sn = jnp.cos(th), jnp.sin(th)
    m2 = jnp.concatenate([jnp.concatenate([cs, sn], axis=2), jnp.concatenate([-sn, cs], axis=2)], axis=1)
    m2_inv = jnp.swapaxes(m2, 1, 2)
    return d1.astype(BF16), d1_inv.astype(BF16), m2.astype(BF16), m2_inv.astype(BF16)


def _dft1_kernel(d_ref, x_ref, o_ref):
    o_ref[...] = _dot(d_ref[...], x_ref[...].astype(BF16)).astype(o_ref.dtype)


def dft_stage1(x, d1, n2):
    l, c = x.shape
    n1h = d1.shape[1]
    cols = n2 * c
    tc = min(cols, 4096)
    out = pl.pallas_call(
        _dft1_kernel,
        grid=(cols // tc,),
        in_specs=[pl.BlockSpec(d1.shape, lambda j: (0, 0)), pl.BlockSpec((n1h, tc), lambda j: (0, j))],
        out_specs=pl.BlockSpec((d1.shape[0], tc), lambda j: (0, j)),
        out_shape=jax.ShapeDtypeStruct((d1.shape[0], cols), BF16),
        compiler_params=_cparams(("parallel",)),
        name="dft_stage1",
    )(d1, x.reshape(n1h, cols))
    return out.reshape(d1.shape[0] // 2, 2, n2, c)


def _filter_spectrum_kernel(m_ref, a_ref, h_ref):
    n2 = a_ref.shape[1]
    c = h_ref.shape[2]
    x = _dot(m_ref[...], a_ref[...].reshape(2 * n2, 2 * c))
    h_ref[0] = x[:n2, :c] + x[:n2, c:]
    h_ref[1] = x[n2:, :c] - x[n2:, c:]


def filter_spectrum(a_taps, m2):
    n1, _, n2, c2 = a_taps.shape
    c = c2 // 2
    return pl.pallas_call(
        _filter_spectrum_kernel,
        grid=(n1,),
        in_specs=[pl.BlockSpec((None, 2 * n2, 2 * n2), lambda i: (i, 0, 0)),
                  pl.BlockSpec((None, 2, n2, c2), lambda i: (i, 0, 0, 0))],
        out_specs=pl.BlockSpec((None, 2, n2, c), lambda i: (i, 0, 0, 0)),
        out_shape=jax.ShapeDtypeStruct((n1, 2, n2, c), F32),
        compiler_params=_cparams(("parallel",)),
        name="filter_spectrum",
    )(m2, a_taps)


def _spectral_mix_kernel(m_ref, mi_ref, a_ref, h_ref, o_ref):
    n2 = a_ref.shape[1]
    c = a_ref.shape[2]
    x = _dot(m_ref[...], a_ref[...].reshape(2 * n2, c))
    xr, xi = x[:n2], x[n2:]
    hr, hi = h_ref[0], h_ref[1]
    y = jnp.concatenate([xr * hr - xi * hi, xr * hi + xi * hr], axis=0).astype(BF16)
    o_ref[...] = _dot(mi_ref[...], y).reshape(2, n2, c).astype(o_ref.dtype)


def spectral_mix(a_z, h, m2, m2_inv):
    n1, _, n2, c = a_z.shape
    blk = pl.BlockSpec((None, 2, n2, c), lambda i: (i, 0, 0, 0))
    mat = pl.BlockSpec((None, 2 * n2, 2 * n2), lambda i: (i, 0, 0))
    return pl.pallas_call(
        _spectral_mix_kernel,
        grid=(n1,),
        in_specs=[mat, mat, blk, blk],
        out_specs=blk,
        out_shape=jax.ShapeDtypeStruct((n1, 2, n2, c), BF16),
        compiler_params=_cparams(("parallel",)),
        name="spectral_mix",
    )(m2, m2_inv, a_z, h)


def _hyena_out_kernel(di_ref, b_ref, x0_ref, z_ref, inv_ref, skip_ref, o_ref, *, inv_n):
    conv = _dot(di_ref[...], b_ref[...]) * inv_n
    o_ref[...] = (x0_ref[...] * (conv * inv_ref[...] + z_ref[...] * skip_ref[...])).astype(o_ref.dtype)


def hyena_output(b, d1_inv, x0, z, inv_norm, skip):
    n1, _, n2, c = b.shape
    l = x0.shape[0]
    cols = n2 * c
    tc = min(cols, 4096)
    n1h = n1 // 2
    rep = tc // c
    kern = functools.partial(_hyena_out_kernel, inv_n=1.0 / (2 * l))
    tile = pl.BlockSpec((n1h, tc), lambda j: (0, j))
    out = pl.pallas_call(
        kern,
        grid=(cols // tc,),
        in_specs=[pl.BlockSpec(d1_inv.shape, lambda j: (0, 0)),
                  pl.BlockSpec((2 * n1, tc), lambda j: (0, j)),
                  tile, tile,
                  pl.BlockSpec((1, tc), lambda j: (0, 0)), pl.BlockSpec((1, tc), lambda j: (0, 0))],
        out_specs=tile,
        out_shape=jax.ShapeDtypeStruct((n1h, cols), BF16),
        compiler_params=_cparams(("parallel",)),
        name="hyena_output",
    )(d1_inv, b.reshape(2 * n1, cols), x0.reshape(n1h, cols), z.reshape(n1h, cols),
      jnp.tile(inv_norm, (1, rep)), jnp.tile(skip.reshape(1, c), (1, rep)))
    return out.reshape(l, c)


def _small_conv_kernel(d_ref, di_ref, z_ref, taps_ref, x0_ref, inv_ref, skip_ref, o_ref, *, inv_n):
    c = z_ref.shape[1]
    n = d_ref.shape[0] // 2
    zs = _dot(d_ref[...], z_ref[...], precision=HIGHEST)
    ts = _dot(d_ref[...], taps_ref[...], precision=HIGHEST)
    hr = ts[:n, :c] + ts[:n, c:]
    hi = ts[n:, :c] - ts[n:, c:]
    zr, zi = zs[:n], zs[n:]
    y = jnp.concatenate([zr * hr - zi * hi, zr * hi + zi * hr], axis=0)
    conv = _dot(di_ref[...], y, precision=HIGHEST) * inv_n
    o_ref[...] = (x0_ref[...] * (conv * inv_ref[...] + z_ref[...] * skip_ref[...])).astype(o_ref.dtype)


def hyena_output_short(z, taps, x0, inv_norm, skip):
    l, c = z.shape
    n = 2 * l
    ph = (2.0 * math.pi / n) * ((jnp.arange(n)[:, None] * jnp.arange(l)[None, :]) % n).astype(F32)
    d = jnp.concatenate([jnp.cos(ph), -jnp.sin(ph)], axis=0)
    di = jnp.concatenate([jnp.cos(ph), -jnp.sin(ph)], axis=0).T
    args = (d, di, z, taps, x0, inv_norm, skip.reshape(1, c))
    kern = functools.partial(_small_conv_kernel, inv_n=1.0 / n)
    return pl.pallas_call(
        kern,
        grid=(1,),
        in_specs=[pl.BlockSpec(a.shape, lambda i: (0, 0)) for a in args],
        out_specs=pl.BlockSpec((l, c), lambda i: (0, 0)),
        out_shape=jax.ShapeDtypeStruct((l, c), BF16),
        compiler_params=_cparams(("arbitrary",)),
        name="hyena_output_short",
    )(*args)


def hyena_long(u, conv_w, conv_b, f_w1, f_b1, f_freq, f_w2, f_b2, f_w3, skip):
    l = u.shape[0]
    x0, z = hyena_gate(u, conv_w, conv_b)
    taps, asum = hyena_filter_taps(l, f_w1, f_b1, f_freq, f_w2, f_b2, f_w3)
    inv_norm = 1.0 / asum
    if 2 * l < DFT_N1 * SUBLANES * 2:
        return hyena_output_short(z, taps, x0, inv_norm, skip)
    n2 = 2 * l // DFT_N1
    d1, d1_inv, m2, m2_inv = _dft_tables(l)
    h = filter_spectrum(dft_stage1(taps, d1, n2), m2)
    b = spectral_mix(dft_stage1(z, d1, n2), h, m2, m2_inv)
    return hyena_output(b, d1_inv, x0, z, inv_norm, skip)


def _head_sumsq(x, bd):
    sq = x * x
    hi = sq.astype(BF16)
    lo = (sq - hi.astype(F32)).astype(BF16)
    return _dot(hi, bd) + _dot(lo, bd)


def _qk_prep_kernel(x_ref, gain_ref, bd_ref, *rest, rope, scale):
    x = x_ref[...]
    w = x.shape[1]
    ms = _head_sumsq(x, bd_ref[...]) * (1.0 / HEAD_DIM)
    xn = x * lax.rsqrt(ms + RMS_EPS) * gain_ref[...]
    if rope:
        cos_ref, sin_ref, o_ref = rest
        reps = w // cos_ref.shape[1]
        cos = jnp.tile(cos_ref[...], (1, reps)) if reps > 1 else cos_ref[...]
        sin = jnp.tile(sin_ref[...], (1, reps)) if reps > 1 else sin_ref[...]
        lane = lax.broadcasted_iota(I32, x.shape, 1)
        partner = jnp.where(lane % 2 == 0, pltpu.roll(xn, w - 1, 1), pltpu.roll(xn, 1, 1))
        xn = xn * cos + partner * sin
    else:
        (o_ref,) = rest
    if scale != 1.0:
        xn = xn * scale
    o_ref[...] = xn.astype(o_ref.dtype)


def _rope_tables(l):
    half = HEAD_DIM // 2
    inv_freq = ROPE_THETA ** (-jnp.arange(0, half, 2, dtype=F32) / half)
    t = jnp.arange(l)
    row = (t // GRID_W).astype(F32)
    col = (t % GRID_W).astype(F32)
    ang = jnp.concatenate([jnp.repeat(row[:, None] * inv_freq[None], 2, axis=1),
                           jnp.repeat(col[:, None] * inv_freq[None], 2, axis=1)], axis=1)
    sign = jnp.where(jnp.arange(HEAD_DIM) % 2 == 0, -1.0, 1.0).astype(F32)
    cos = jnp.tile(jnp.cos(ang), (1, 2))
    sin = jnp.tile(jnp.sin(ang) * sign[None], (1, 2))
    return cos, sin


def qk_prep(x, gain, rope_tabs, scale):
    l, w = x.shape
    tm = _row_tile(l, 512)
    head = jnp.arange(w) // HEAD_DIM
    bd = (head[:, None] == head[None, :]).astype(BF16)
    gain_t = jnp.tile(gain.reshape(1, HEAD_DIM), (1, w // HEAD_DIM))
    row = lambda i: (i, 0)
    fixed = lambda i: (0, 0)
    in_specs = [pl.BlockSpec((tm, w), row), pl.BlockSpec((1, w), fixed), pl.BlockSpec((w, w), fixed)]
    args = [x, gain_t, bd]
    if rope_tabs is not None:
        in_specs += [pl.BlockSpec((tm, 2 * HEAD_DIM), row)] * 2
        args += list(rope_tabs)
    kern = functools.partial(_qk_prep_kernel, rope=rope_tabs is not None, scale=scale)
    return pl.pallas_call(
        kern, grid=(l // tm,), in_specs=in_specs, out_specs=pl.BlockSpec((tm, w), row),
        out_shape=jax.ShapeDtypeStruct((l, w), BF16),
        compiler_params=_cparams(("parallel",)), name="qk_prep",
    )(*args)


def _flash_kernel(q_ref, k_ref, v_ref, o_ref, m_ref, l_ref, acc_ref, *, tk, nk):
    hq, tq, dh = q_ref.shape
    q = q_ref[...].reshape(hq * tq, dh)
    m_ref[...] = jnp.full(m_ref.shape, NEG_BIG, F32)
    l_ref[...] = jnp.zeros(l_ref.shape, F32)
    acc_ref[...] = jnp.zeros(acc_ref.shape, F32)

    def body(j, carry):
        start = pl.multiple_of(j * tk, tk)
        kj = k_ref[pl.ds(start, tk), :]
        vj = v_ref[pl.ds(start, tk), :]
        s = _dot_nt(q, kj)
        m_old = m_ref[...]
        m_new = jnp.maximum(m_old, s.max(axis=1, keepdims=True))
        alpha = jnp.exp(m_old - m_new)
        p = jnp.exp(s - m_new)
        l_ref[...] = alpha * l_ref[...] + p.sum(axis=1, keepdims=True)
        acc_ref[...] = alpha * acc_ref[...] + _dot(p.astype(BF16), vj)
        m_ref[...] = m_new
        return carry

    lax.fori_loop(0, nk, body, 0)
    o_ref[...] = (acc_ref[...] / l_ref[...]).reshape(hq, tq, dh).astype(o_ref.dtype)


def _kv_chunk(lk):
    for tk in (640, 512, 384, 256, 128):
        if lk % tk == 0:
            return tk
    raise ValueError(lk)


def gqa_attention(q_hm, k_hm, v_hm):
    hq, l, dh = q_hm.shape
    hkv, lk, _ = k_hm.shape
    rep = hq // hkv
    tq = _row_tile(l, 256)
    tk = _kv_chunk(lk)
    kern = functools.partial(_flash_kernel, tk=tk, nk=lk // tk)
    rows = rep * tq
    return pl.pallas_call(
        kern,
        grid=(hkv, l // tq),
        in_specs=[pl.BlockSpec((rep, tq, dh), lambda g, i: (g, i, 0)),
                  pl.BlockSpec((None, lk, dh), lambda g, i: (g, 0, 0)),
                  pl.BlockSpec((None, lk, dh), lambda g, i: (g, 0, 0))],
        out_specs=pl.BlockSpec((rep, tq, dh), lambda g, i: (g, i, 0)),
        out_shape=jax.ShapeDtypeStruct((hq, l, dh), BF16),
        scratch_shapes=[pltpu.VMEM((rows, 1), F32), pltpu.VMEM((rows, 1), F32), pltpu.VMEM((rows, dh), F32)],
        compiler_params=_cparams(("parallel", "parallel")),
        name="gqa_attention",
    )(q_hm, k_hm, v_hm)


def _cmul(ar, ai, br, bi):
    return ar * br - ai * bi, ar * bi + ai * br


def _s5_operators(a_re, a_im, log_dt, b_re, b_im, c_re, c_im):
    t = S5_CHUNK
    gs = S5_GROUP
    hp = dict(precision=HIGHEST)
    dt = jnp.exp(log_dt)[..., None]
    zr, zi = a_re * dt, a_im * dt
    er = jnp.exp(zr)
    abr, abi = er * jnp.cos(zi), er * jnp.sin(zi)
    den = a_re * a_re + a_im * a_im
    fr = ((abr - 1.0) * a_re + abi * a_im) / den
    fi = (abi * a_re - (abr - 1.0) * a_im) / den
    bbr, bbi = _cmul(fr[..., None], fi[..., None], b_re, b_im)
    tau = jnp.arange(t + 1, dtype=F32)
    pr = jnp.exp(zr[..., None] * tau) * jnp.cos(zi[..., None] * tau)
    pi = jnp.exp(zr[..., None] * tau) * jnp.sin(zi[..., None] * tau)
    car, cai = _cmul(c_re[..., None], c_im[..., None], pr[:, :, None, :, :t], pi[:, :, None, :, :t])
    ktap = (jnp.einsum('dgqpt,dgpk->dgtqk', car, bbr, **hp) - jnp.einsum('dgqpt,dgpk->dgtqk', cai, bbi, **hp))
    ktp = jnp.concatenate([jnp.zeros_like(ktap), ktap], axis=2)
    blocks = jnp.stack([ktp[:, :, t - i:2 * t - i] for i in range(t)], axis=2)
    m_op = blocks.transpose(0, 1, 2, 5, 3, 4).reshape(2, S5_GROUPS, t * gs, t * gs)
    prr, pir = pr[..., t - 1::-1], pi[..., t - 1::-1]
    wr, wi = _cmul(prr[:, :, :, :, None], pir[:, :, :, :, None], bbr[:, :, :, None, :], bbi[:, :, :, None, :])
    w_re = wr.transpose(0, 1, 3, 4, 2).reshape(2, S5_GROUPS, t * gs, S5_STATE)
    w_im = wi.transpose(0, 1, 3, 4, 2).reshape(2, S5_GROUPS, t * gs, S5_STATE)
    vr, vi = _cmul(c_re[..., None], c_im[..., None], pr[:, :, None, :, 1:], pi[:, :, None, :, 1:])
    v_re = vr.transpose(0, 1, 3, 4, 2).reshape(2, S5_GROUPS, S5_STATE, t * gs)
    v_im = (-vi).transpose(0, 1, 3, 4, 2).reshape(2, S5_GROUPS, S5_STATE, t * gs)
    return m_op, w_re, w_im, v_re, v_im, pr[..., t], pi[..., t]


def _pair_blockdiag(x):
    d, g, r, c = x.shape
    x = x.reshape(d, g // 2, 2, r, c)
    z = jnp.zeros_like(x[:, :, 0])
    top = jnp.concatenate([x[:, :, 0], z], axis=-1)
    bot = jnp.concatenate([z, x[:, :, 1]], axis=-1)
    return jnp.concatenate([top, bot], axis=-2)


def _s5_state_in_kernel(u_ref, w_ref, er_ref, ei_ref):
    e = _dot(u_ref[...], w_ref[...])
    half = e.shape[1] // 2
    er_ref[...] = e[:, :half]
    ei_ref[...] = e[:, half:]


def _s5_scan_kernel(ar_ref, ai_ref, er_ref, ei_ref, sr_ref, si_ref, cr_ref, ci_ref):
    @pl.when(pl.program_id(0) == 0)
    def _():
        cr_ref[...] = jnp.zeros_like(cr_ref)
        ci_ref[...] = jnp.zeros_like(ci_ref)

    ar, ai = ar_ref[...], ai_ref[...]

    def body(c, carry):
        sr, si = carry
        sr_ref[c] = sr
        si_ref[c] = si
        nr = ar * sr - ai * si + er_ref[c]
        ni = ar * si + ai * sr + ei_ref[c]
        return nr, ni

    sr, si = lax.fori_loop(0, er_ref.shape[0], body, (cr_ref[...], ci_ref[...]))
    cr_ref[...] = sr
    ci_ref[...] = si


def _s5_out_kernel(u_ref, sr_ref, si_ref, m_ref, vr_ref, vi_ref, y_ref):
    y = _dot(u_ref[...], m_ref[...])
    y = y + _dot(sr_ref[...].astype(BF16), vr_ref[...]) + _dot(si_ref[...].astype(BF16), vi_ref[...])
    y_ref[...] = y


def _largest_divisor(n, cap):
    for d in range(min(cap, n), 0, -1):
        if n % d == 0 and (d % SUBLANES == 0 or d == n):
            return d
    return n


def s5_scan_outputs(u_ctx, u_lat, a_re, a_im, log_dt, b_re, b_im, c_re, c_im):
    t, gs, g = S5_CHUNK, S5_GROUP, S5_GROUPS
    n_ctx, l = u_ctx.shape[0], u_lat.shape[0]
    n_tok = n_ctx + l
    nch = n_tok // t
    pairs = g // 2
    m_op, w_re, w_im, v_re, v_im, atr, ati = _s5_operators(a_re, a_im, log_dt, b_re, b_im, c_re, c_im)
    m_bd = _pair_blockdiag(m_op).astype(BF16)
    w_bd = jnp.concatenate([_pair_blockdiag(w_re), _pair_blockdiag(w_im)], axis=-1).astype(BF16)
    vr_bd = _pair_blockdiag(v_re).astype(BF16)
    vi_bd = _pair_blockdiag(v_im).astype(BF16)

    def chunked(seq):
        return seq.reshape(nch, t, g, gs).transpose(0, 2, 1, 3).reshape(nch, g * t * gs)

    fwd = jnp.concatenate([u_ctx, u_lat], axis=0)
    bwd = jnp.concatenate([u_ctx[::-1], u_lat[::-1]], axis=0)
    u_ch = jnp.stack([chunked(fwd), chunked(bwd)]).astype(BF16)
    pw = 2 * t * gs
    sw = 2 * S5_STATE
    n_state = g * S5_STATE

    e_re, e_im = pl.pallas_call(
        _s5_state_in_kernel,
        grid=(2, pairs),
        in_specs=[pl.BlockSpec((None, nch, pw), lambda d, p: (d, 0, p)),
                  pl.BlockSpec((None, None, pw, 2 * sw), lambda d, p: (d, p, 0, 0))],
        out_specs=[pl.BlockSpec((nch, sw), lambda d, p: (0, d * pairs + p))] * 2,
        out_shape=[jax.ShapeDtypeStruct((nch, 2 * n_state), F32)] * 2,
        compiler_params=_cparams(("parallel", "parallel")),
        name="s5_state_in",
    )(u_ch, w_bd)

    slab = 2 * n_state // SUBLANES
    tc = _largest_divisor(nch, 128)
    blk = pl.BlockSpec((tc, SUBLANES, slab), lambda i: (i, 0, 0))
    vec = pl.BlockSpec((SUBLANES, slab), lambda i: (0, 0))
    s_re, s_im = pl.pallas_call(
        _s5_scan_kernel,
        grid=(nch // tc,),
        in_specs=[vec, vec, blk, blk],
        out_specs=[blk, blk],
        out_shape=[jax.ShapeDtypeStruct((nch, SUBLANES, slab), F32)] * 2,
        scratch_shapes=[pltpu.VMEM((SUBLANES, slab), F32)] * 2,
        compiler_params=_cparams(("arbitrary",)),
        name="s5_scan",
    )(atr.reshape(SUBLANES, slab), ati.reshape(SUBLANES, slab),
      e_re.reshape(nch, SUBLANES, slab), e_im.reshape(nch, SUBLANES, slab))

    y = pl.pallas_call(
        _s5_out_kernel,
        grid=(2, pairs),
        in_specs=[pl.BlockSpec((None, nch, pw), lambda d, p: (d, 0, p)),
                  pl.BlockSpec((nch, sw), lambda d, p: (0, d * pairs + p)),
                  pl.BlockSpec((nch, sw), lambda d, p: (0, d * pairs + p)),
                  pl.BlockSpec((None, None, pw, pw), lambda d, p: (d, p, 0, 0)),
                  pl.BlockSpec((None, None, sw, pw), lambda d, p: (d, p, 0, 0)),
                  pl.BlockSpec((None, None, sw, pw), lambda d, p: (d, p, 0, 0))],
        out_specs=pl.BlockSpec((None, nch, pw), lambda d, p: (d, 0, p)),
        out_shape=jax.ShapeDtypeStruct((2, nch, g * t * gs), F32),
        compiler_params=_cparams(("parallel", "parallel")),
        name="s5_out",
    )(u_ch, s_re.reshape(nch, 2 * n_state), s_im.reshape(nch, 2 * n_state), m_bd, vr_bd, vi_bd)

    y = y.reshape(2, nch, g, t, gs).transpose(0, 1, 3, 2, 4).reshape(2, n_tok, g * gs)[:, n_ctx:]
    return y[0], y[1, ::-1]


def _s5_readout_kernel(yf_ref, yb_ref, u_ref, d_ref, w_ref, b_ref, o_ref):
    y = yf_ref[...] + yb_ref[...] + d_ref[...] * u_ref[...]
    y = 0.5 * y * (1.0 + jnp.tanh(math.sqrt(2.0 / math.pi) * (y + 0.044715 * (y * y * y))))
    gate = jax.nn.sigmoid(_dot(y.astype(BF16), w_ref[...]) + b_ref[...])
    o_ref[...] = (y * gate).astype(o_ref.dtype)


def s5_readout(y_f, y_b, u, d_skip, glu_w, glu_b):
    l, w = u.shape
    tm = _row_tile(l, 512)
    row = lambda i: (i, 0)
    fixed = lambda i: (0, 0)
    return pl.pallas_call(
        _s5_readout_kernel,
        grid=(l // tm,),
        in_specs=[pl.BlockSpec((tm, w), row)] * 3
                 + [pl.BlockSpec((1, w), fixed), pl.BlockSpec((w, w), fixed), pl.BlockSpec((1, w), fixed)],
        out_specs=pl.BlockSpec((tm, w), row),
        out_shape=jax.ShapeDtypeStruct((l, w), BF16),
        compiler_params=_cparams(("parallel",)),
        name="s5_readout",
    )(y_f, y_b, u, d_skip.reshape(1, w), glu_w.astype(BF16), glu_b.reshape(1, w))


def _first_max(vals, lane):
    m = vals.max(axis=1, keepdims=True)
    idx = jnp.where(vals == m, lane, jnp.int32(1 << 20)).min(axis=1, keepdims=True)
    return m, idx


def _stream_specs(n_ctx_tiles, tm, d):
    return (pl.BlockSpec((tm, d), lambda i: (jnp.clip(i, 0, max(n_ctx_tiles - 1, 0)), 0)),
            pl.BlockSpec((tm, d), lambda i: (jnp.maximum(i - n_ctx_tiles, 0), 0)))


def _stream_tile(xc_ref, xl_ref, n_ctx_tiles):
    if n_ctx_tiles == 0:
        return xl_ref[...]
    return jnp.where(pl.program_id(0) < n_ctx_tiles, xc_ref[...], xl_ref[...])


def _store_row_slabs(ref, x):
    rows = x.shape[0]
    for s in range(ROW_SLAB):
        ref[pl.ds(s, rows, stride=ROW_SLAB), :] = x[:, s * LANES:(s + 1) * LANES]


def _router_kernel(xc_ref, xl_ref, sc_ref, sh_ref, rw_ref, rb_ref, tri_ref,
                   hf_ref, te_ref, gt_ref, rk_ref, cnt_ref, run_ref, *, n_ctx_tiles):
    @pl.when(pl.program_id(0) == 0)
    def _():
        run_ref[...] = jnp.zeros_like(run_ref)

    hf = _stream_tile(xc_ref, xl_ref, n_ctx_tiles) * (1.0 + sc_ref[...]) + sh_ref[...]
    _store_row_slabs(hf_ref, hf)
    tm = hf.shape[0]
    scores = jax.nn.sigmoid(_dot(hf, rw_ref[...], precision=HIGHEST))
    biased = scores + rb_ref[...]
    lane = lax.broadcasted_iota(I32, (tm, N_EXPERTS), 1)
    grp = lane // (N_EXPERTS // N_EXPERT_GROUPS)
    lane_o = lax.broadcasted_iota(I32, (tm, LANES), 1)
    neg = jnp.float32(-jnp.inf)

    group_score = jnp.full((tm, LANES), neg, F32)
    for g in range(N_EXPERT_GROUPS):
        vals = jnp.where(grp == g, biased, neg)
        m1, i1 = _first_max(vals, lane)
        m2 = jnp.where(lane == i1, neg, vals).max(axis=1, keepdims=True)
        group_score = jnp.where(lane_o == g, m1 + m2, group_score)
    keep = jnp.zeros((tm, N_EXPERTS), F32)
    for _ in range(TOPK_GROUPS):
        _, gi = _first_max(group_score, lane_o)
        keep = jnp.where(grp == gi, 1.0, keep)
        group_score = jnp.where(lane_o == gi, neg, group_score)

    masked = jnp.where(keep > 0.0, biased, neg)
    member = jnp.zeros((tm, N_EXPERTS), F32)
    e_cols, g_cols = [], []
    for _ in range(TOP_K):
        _, ei = _first_max(masked, lane)
        hit = lane == ei
        g_cols.append(jnp.where(hit, scores, 0.0).sum(axis=1, keepdims=True))
        masked = jnp.where(hit, neg, masked)
        member = jnp.where(hit, 1.0, member)
        e_cols.append(ei)
    g_sum = g_cols[0]
    for gk in g_cols[1:]:
        g_sum = g_sum + gk

    before = _dot(tri_ref[...], member.astype(BF16)) + run_ref[...]
    te = jnp.zeros((tm, LANES), I32)
    rk = jnp.zeros((tm, LANES), I32)
    gt = jnp.zeros((tm, LANES), F32)
    for k in range(TOP_K):
        rank = jnp.where(lane == e_cols[k], before, 0.0).sum(axis=1, keepdims=True)
        te = jnp.where(lane_o == k, e_cols[k], te)
        rk = jnp.where(lane_o == k, rank.astype(I32), rk)
        gt = jnp.where(lane_o == k, ROUTED_SCALE * g_cols[k] / g_sum, gt)
    te_ref[...] = te
    rk_ref[...] = rk
    gt_ref[...] = gt
    run_ref[...] += member.sum(axis=0, keepdims=True)
    cnt_ref[...] = run_ref[...]


def moe_route(x_ctx, x_lat, sc2, sh2, n_ctx_tiles, router_w, router_bias):
    d = x_lat.shape[1]
    tm = MOE_TILE
    n = n_ctx_tiles * tm + x_lat.shape[0]
    tri = (jnp.arange(tm)[None, :] < jnp.arange(tm)[:, None]).astype(BF16)
    row = lambda i: (i, 0)
    fixed = lambda i: (0, 0)
    seg = lambda i: (jnp.where(i < n_ctx_tiles, 1, 0), 0, 0)
    kern = functools.partial(_router_kernel, n_ctx_tiles=n_ctx_tiles)
    return pl.pallas_call(
        kern,
        grid=(n // tm,),
        in_specs=[*_stream_specs(n_ctx_tiles, tm, d), pl.BlockSpec((None, 1, d), seg), pl.BlockSpec((None, 1, d), seg),
                  pl.BlockSpec((d, N_EXPERTS), fixed), pl.BlockSpec((1, N_EXPERTS), fixed),
                  pl.BlockSpec((tm, tm), fixed)],
        out_specs=[pl.BlockSpec((tm * ROW_SLAB, LANES), row), pl.BlockSpec((tm, LANES), row),
                   pl.BlockSpec((tm, LANES), row), pl.BlockSpec((tm, LANES), row),
                   pl.BlockSpec((1, N_EXPERTS), fixed)],
        out_shape=[jax.ShapeDtypeStruct((n * ROW_SLAB, LANES), F32), jax.ShapeDtypeStruct((n, LANES), I32),
                   jax.ShapeDtypeStruct((n, LANES), F32), jax.ShapeDtypeStruct((n, LANES), I32),
                   jax.ShapeDtypeStruct((1, N_EXPERTS), F32)],
        scratch_shapes=[pltpu.VMEM((1, N_EXPERTS), F32)],
        compiler_params=_cparams(("arbitrary",)),
        name="moe_route",
    )(x_ctx, x_lat, sc2, sh2, router_w, router_bias.reshape(1, N_EXPERTS), tri)


IDX_TILE = 1024


def _row_gather(idx_ref, base, count, src_hbm, dst, sem):
    def body(j, carry):
        tok = idx_ref[base + j]
        pltpu.make_async_copy(src_hbm.at[tok], dst.at[pl.ds(j * ROW_SLAB, ROW_SLAB), :], sem).start()
        return carry
    lax.fori_loop(0, count, body, 0, unroll=8)


def _wait_rows(dst, sem):
    pltpu.make_async_copy(dst, dst, sem).wait()


def _gathered_rows(buf, first, rows, stride):
    return jnp.concatenate(
        [buf[pl.ds(first * ROW_SLAB + s, rows, stride=stride * ROW_SLAB), :] for s in range(ROW_SLAB)], axis=1)


def _swiglu(x, wg, wu, wd):
    gate = _dot(x, wg)
    up = _dot(x, wu)
    return _dot((gate * jax.nn.sigmoid(gate) * up).astype(BF16), wd)


def _expert_kernel(be_ref, nu_ref, cur_ref, nxt_ref, hf_hbm, wg_ref, wu_ref, wd_ref, y_ref, xbuf, sem):
    i = pl.program_id(0)
    n_used = nu_ref[0]
    per = IDX_TILE // EXPERT_BLOCK
    slot = i % 2

    @pl.when(i == 0)
    def _():
        _row_gather(cur_ref, 0, EXPERT_BLOCK, hf_hbm, xbuf.at[0], sem.at[0])

    @pl.when(i + 1 < n_used)
    def _():
        base = ((i + 1) % per) * EXPERT_BLOCK
        _row_gather(nxt_ref, base, EXPERT_BLOCK, hf_hbm, xbuf.at[1 - slot], sem.at[1 - slot])

    @pl.when(i < n_used)
    def _():
        _wait_rows(xbuf.at[slot], sem.at[slot])
        x = _gathered_rows(xbuf.at[slot], 0, EXPERT_BLOCK, 1).astype(BF16)
        y = _swiglu(x, wg_ref[...].astype(BF16), wu_ref[...].astype(BF16), wd_ref[...].astype(BF16))
        _store_row_slabs(y_ref, y)

    @pl.when(i >= n_used)
    def _():
        y_ref[...] = jnp.zeros_like(y_ref)


def moe_experts(hf_slabs, slot_tok, block_e, n_used, w_gate, w_up, w_down):
    n_blocks = block_e.shape[0]
    d, ff = w_gate.shape[1:]
    per = IDX_TILE // EXPERT_BLOCK
    n_idx_tiles = n_blocks // per
    blk_rows = EXPERT_BLOCK * ROW_SLAB
    grid_spec = pltpu.PrefetchScalarGridSpec(
        num_scalar_prefetch=2,
        grid=(n_blocks,),
        in_specs=[pl.BlockSpec((IDX_TILE,), lambda i, be, nu: (i // per,), memory_space=pltpu.SMEM),
                  pl.BlockSpec((IDX_TILE,), lambda i, be, nu: (jnp.minimum((i + 1) // per, n_idx_tiles - 1),),
                               memory_space=pltpu.SMEM),
                  pl.BlockSpec(memory_space=pl.ANY),
                  pl.BlockSpec((None, d, ff), lambda i, be, nu: (be[i], 0, 0)),
                  pl.BlockSpec((None, d, ff), lambda i, be, nu: (be[i], 0, 0)),
                  pl.BlockSpec((None, ff, d), lambda i, be, nu: (be[i], 0, 0))],
        out_specs=pl.BlockSpec((blk_rows, LANES), lambda i, be, nu: (i, 0)),
        scratch_shapes=[pltpu.VMEM((2, blk_rows, LANES), F32), pltpu.SemaphoreType.DMA((2,))],
    )
    return pl.pallas_call(
        _expert_kernel,
        grid_spec=grid_spec,
        out_shape=jax.ShapeDtypeStruct((n_blocks * blk_rows, LANES), F32),
        compiler_params=_cparams(("arbitrary",)),
        name="moe_experts",
    )(block_e, n_used, slot_tok, slot_tok, hf_slabs.reshape(-1, ROW_SLAB, LANES), w_gate, w_up, w_down)


def _combine_kernel(cur_ref, nxt_ref, ys_hbm, xc_ref, xl_ref, hf_ref, gt_ref, g2_ref, sg_ref, su_ref, sd_ref,
                    lg_ref, lb_ref, *rest, n_ctx_tiles):
    *o_refs, ybuf, sem = rest
    i = pl.program_id(0)
    n_tiles = pl.num_programs(0)
    tm = xl_ref.shape[0]
    n_rows = tm * TOP_K
    slot = i % 2

    @pl.when(i == 0)
    def _():
        _row_gather(cur_ref, 0, n_rows, ys_hbm, ybuf.at[0], sem.at[0])

    @pl.when(i + 1 < n_tiles)
    def _():
        _row_gather(nxt_ref, 0, n_rows, ys_hbm, ybuf.at[1 - slot], sem.at[1 - slot])

    hf = _gathered_rows(hf_ref, 0, tm, 1).astype(BF16)
    y = _swiglu(hf, sg_ref[...], su_ref[...], sd_ref[...])

    _wait_rows(ybuf.at[slot], sem.at[slot])
    for k in range(TOP_K):
        y = y + gt_ref[:, k:k + 1] * _gathered_rows(ybuf.at[slot], k, tm, TOP_K)
    r = DEEPNORM_ALPHA * _stream_tile(xc_ref, xl_ref, n_ctx_tiles) + g2_ref[...] * y
    res = _layer_norm_rows(r, lg_ref[...], lb_ref[...])
    if n_ctx_tiles == 0:
        o_refs[0][...] = res
    else:
        oc_ref, ol_ref = o_refs

        @pl.when(i < n_ctx_tiles)
        def _():
            oc_ref[...] = res

        @pl.when(i >= n_ctx_tiles)
        def _():
            ol_ref[...] = res


def moe_combine(dest, ys_slabs, x_ctx, x_lat, hf_slabs, gate, g2, n_ctx_tiles, sh_gate, sh_up, sh_down, ln_g, ln_b):
    d = x_lat.shape[1]
    tm = MOE_TILE
    n_tiles = n_ctx_tiles + x_lat.shape[0] // tm
    n_rows = tm * TOP_K
    ff = sh_gate.shape[1]
    row = lambda i: (i, 0)
    fixed = lambda i: (0, 0)
    seg = lambda i: (jnp.where(i < n_ctx_tiles, 1, 0), 0, 0)
    ctx_spec, lat_spec = _stream_specs(n_ctx_tiles, tm, d)
    lat_out = jax.ShapeDtypeStruct(x_lat.shape, F32)
    if n_ctx_tiles == 0:
        out_specs, out_shape = [lat_spec], [lat_out]
    else:
        out_specs, out_shape = [ctx_spec, lat_spec], [jax.ShapeDtypeStruct(x_ctx.shape, F32), lat_out]
    kern = functools.partial(_combine_kernel, n_ctx_tiles=n_ctx_tiles)
    return pl.pallas_call(
        kern,
        grid=(n_tiles,),
        in_specs=[pl.BlockSpec((n_rows,), lambda i: (i,), memory_space=pltpu.SMEM),
                  pl.BlockSpec((n_rows,), lambda i: (jnp.minimum(i + 1, n_tiles - 1),), memory_space=pltpu.SMEM),
                  pl.BlockSpec(memory_space=pl.ANY),
                  ctx_spec, lat_spec, pl.BlockSpec((tm * ROW_SLAB, LANES), row), pl.BlockSpec((tm, LANES), row),
                  pl.BlockSpec((None, 1, d), seg),
                  pl.BlockSpec((d, ff), fixed), pl.BlockSpec((d, ff), fixed), pl.BlockSpec((ff, d), fixed),
                  pl.BlockSpec((1, d), fixed), pl.BlockSpec((1, d), fixed)],
        out_specs=out_specs,
        out_shape=out_shape,
        scratch_shapes=[pltpu.VMEM((2, n_rows * ROW_SLAB, LANES), F32), pltpu.SemaphoreType.DMA((2,))],
        compiler_params=_cparams(("arbitrary",)),
        name="moe_combine",
    )(dest, dest, ys_slabs.reshape(-1, ROW_SLAB, LANES), x_ctx, x_lat, hf_slabs, gate, g2,
      sh_gate.astype(BF16), sh_up.astype(BF16), sh_down.astype(BF16), ln_g.reshape(1, d), ln_b.reshape(1, d))


def moe_layer(x_ctx, x_lat, sc2, sh2, g2, n_ctx_tiles, router_w, router_bias, w_gate, w_up, w_down,
              sh_gate, sh_up, sh_down, ln_g, ln_b):
    n = n_ctx_tiles * MOE_TILE + x_lat.shape[0]
    hf, te, gt, rk, cnt = moe_route(x_ctx, x_lat, sc2, sh2, n_ctx_tiles, router_w, router_bias)
    counts = cnt[0].astype(I32)
    padded = (counts + EXPERT_BLOCK - 1) // EXPERT_BLOCK * EXPERT_BLOCK
    pad_end = jnp.cumsum(padded)
    pad_start = pad_end - padded
    dest = (pad_start[te[:, :TOP_K]] + rk[:, :TOP_K]).reshape(-1)
    per = IDX_TILE // EXPERT_BLOCK
    n_blocks = -(-(n * TOP_K + N_EXPERTS * (EXPERT_BLOCK - 1)) // EXPERT_BLOCK)
    n_blocks = -(-n_blocks // per) * per
    slot_tok = jnp.zeros((n_blocks * EXPERT_BLOCK,), I32).at[dest].set(jnp.arange(n * TOP_K, dtype=I32) // TOP_K)
    n_used = pad_end[-1] // EXPERT_BLOCK
    blk = jnp.arange(n_blocks, dtype=I32)
    last = jnp.maximum(n_used - 1, 0)
    block_e = jnp.sum((pad_end[None, :] <= (jnp.minimum(blk, last) * EXPERT_BLOCK)[:, None]).astype(I32), axis=1)
    block_e = jnp.minimum(block_e, N_EXPERTS - 1)
    ys = moe_experts(hf, slot_tok, block_e, n_used.reshape(1).astype(I32), w_gate, w_up, w_down)
    return moe_combine(dest, ys, x_ctx, x_lat, hf, gt, g2, n_ctx_tiles, sh_gate, sh_up, sh_down, ln_g, ln_b)


def kernel(x, c, ctx, c_ctx, w_mod, b_mod, ln_mix_g, ln_mix_b, ln_ffn_g, ln_ffn_b, ab_w_in, ab_w_out, na_rpb,
           hy_conv_w, hy_conv_b, hy_f_w1, hy_f_b1, hy_f_freq, hy_f_w2, hy_f_b2, hy_f_w3, hy_skip, cd_w_in, cd_w_out,
           q_norm_g, k_norm_g, s5_a_re, s5_a_im, s5_log_dt, s5_b_re, s5_b_im, s5_c_re, s5_c_im, s5_d, s5_glu_w,
           s5_glu_b, router_w, router_bias, exp_w_gate, exp_w_up, exp_w_down, sh_w_gate, sh_w_up, sh_w_down):
    b, l, d = x.shape
    assert b == 1
    n_ctx = ctx.shape[1]
    assert n_ctx == MOE_TILE
    xs = x[0]
    cs = ctx[0]
    cmat = jnp.zeros((SUBLANES, d), F32).at[0].set(c[0]).at[1].set(c_ctx)
    mods = modulation_all(cmat, w_mod, b_mod).reshape(DEPTH, SUBLANES, 6, d)
    qscale = HEAD_DIM ** -0.5

    for i in range(DEPTH):
        need_ctx = i < DEPTH - 1
        m = mods[i]
        sh1, sc1, g1, sh2, sc2, g2 = [m[0:1, t] for t in range(6)]
        csh1, csc1, cg1, csh2, csc2, cg2 = [m[1:2, t] for t in range(6)]
        j = i // 2
        if i % 2 == 0:
            filt = (hy_conv_w[j], hy_conv_b[j], hy_f_w1[j], hy_f_b1[j], hy_f_freq[j], hy_f_w2[j], hy_f_b2[j],
                    hy_f_w3[j], hy_skip[j])
            splits = (NA_WIDTH, NA_WIDTH, NA_WIDTH, 3 * HY_WIDTH)
            dts = (BF16, BF16, BF16, F32)
            scl = (qscale, 1.0, 1.0, 1.0)
            q_l, k_l, v_l, u_l = mod_project(xs, sc1, sh1, ab_w_in[j], splits, dts, scl)
            q_c, k_c, v_c, u_c = mod_project(cs, csc1, csh1, ab_w_in[j], splits, dts, scl)
            a_lat = neighbourhood_attention(q_l, k_l, v_l, k_c, v_c, na_rpb[j])
            y_hy = hyena_long(u_l, *filt)
            xs_new = outproj_ln(a_lat, y_hy, ab_w_out[j], xs, g1, ln_mix_g[i], ln_mix_b[i])
            if need_ctx:
                a_ctx = context_attention(q_c, k_c, v_c)
                yc_hy = hyena_long(u_c, *filt)
                cs = outproj_ln(a_ctx, yc_hy, ab_w_out[j], cs, cg1, ln_mix_g[i], ln_mix_b[i])
            xs = xs_new
        else:
            splits = (GQA_WIDTH, GQA_KV_WIDTH, GQA_KV_WIDTH, S5_WIDTH)
            q_l, k_l, v_l, u_l = mod_project(xs, sc1, sh1, cd_w_in[j], splits, (F32, F32, BF16, F32))
            k_c, v_c, u_c = mod_project(cs, csc1, csh1, cd_w_in[j][:, GQA_WIDTH:], splits[1:], (F32, BF16, F32))
            tabs = _rope_tables(l)
            qn = qk_prep(q_l, q_norm_g[j], tabs, qscale)
            kn = qk_prep(k_l, k_norm_g[j], tabs, 1.0)
            kcn = qk_prep(k_c, k_norm_g[j], None, 1.0)
            heads = lambda t, h: t.reshape(t.shape[0], h, HEAD_DIM).transpose(1, 0, 2)
            att = gqa_attention(heads(qn, GQA_HEADS), heads(jnp.concatenate([kn, kcn], axis=0), GQA_KV_HEADS),
                                heads(jnp.concatenate([v_l, v_c], axis=0), GQA_KV_HEADS))
            att = att.transpose(1, 0, 2).reshape(l, GQA_WIDTH)
            y_f, y_b = s5_scan_outputs(u_c, u_l, s5_a_re[j], s5_a_im[j], s5_log_dt[j], s5_b_re[j], s5_b_im[j],
                                       s5_c_re[j], s5_c_im[j])
            ssm = s5_readout(y_f, y_b, u_l, s5_d[j], s5_glu_w[j], s5_glu_b[j])
            xs = outproj_ln(att, ssm, cd_w_out[j], xs, g1, ln_mix_g[i], ln_mix_b[i])
            assert not need_ctx

        moe_w = (router_w[i], router_bias[i], exp_w_gate[i], exp_w_up[i], exp_w_down[i],
                 sh_w_gate[i], sh_w_up[i], sh_w_down[i], ln_ffn_g[i], ln_ffn_b[i])
        stack2 = lambda lat, cx: jnp.stack([lat, cx])
        mod2 = (stack2(sc2, csc2), stack2(sh2, csh2), stack2(g2, cg2))
        if need_ctx:
            cs, xs = moe_layer(cs, xs, *mod2, n_ctx // MOE_TILE, *moe_w)
        else:
            (xs,) = moe_layer(xs, xs, *mod2, 0, *moe_w)
    return xs.reshape(b, l, d)
```

```python
import functools
import math

import jax
import jax.numpy as jnp
import numpy as np
from jax import lax
from jax.experimental import pallas as pl
from jax.experimental.pallas import tpu as pltpu

F32 = jnp.float32
BF16 = jnp.bfloat16
I32 = jnp.int32
HIGHEST = lax.Precision.HIGHEST

LANES = 128
SUBLANES = 8
VMEM_LIMIT = 56 * 1024 * 1024

D_MODEL = 1024
DEPTH = 2
GRID_W = 64
HEAD_DIM = 64
NA_HEADS = 8
NA_WIDTH = NA_HEADS * HEAD_DIM
NA_KH = 8
NA_KW = 16
HY_WIDTH = D_MODEL - NA_WIDTH
HY_BANDS = 16
HY_DECAY_PCT_MIN = 0.3
HY_DECAY_PCT_MAX = 1.5
HY_DECAY_TARGET = 1e-2
GQA_HEADS = 8
GQA_KV_HEADS = 2
GQA_WIDTH = GQA_HEADS * HEAD_DIM
GQA_KV_WIDTH = GQA_KV_HEADS * HEAD_DIM
ROPE_THETA = 10000.0
S5_WIDTH = D_MODEL - GQA_WIDTH
S5_GROUP = 16
S5_GROUPS = S5_WIDTH // S5_GROUP
S5_STATE = 64
N_EXPERTS = 256
TOP_K = 8
N_EXPERT_GROUPS = 8
TOPK_GROUPS = 4
EXPERT_FF = 256
ROUTED_SCALE = 2.5
EXPERT_BLOCK = 128
DEEPNORM_ALPHA = (2.0 * DEPTH) ** 0.25
LN_EPS = 1e-5
RMS_EPS = 1e-6

NEG_BIG = -1e30
NA_TILE_ROWS = 8
NA_KEY_ROWS = 16
NA_KEY_BLOCK_ROWS = 4
DFT_N1 = 128
S5_CHUNK = 16
MOE_TILE = 256
ROW_SLAB = D_MODEL // LANES

NT_DIMS = (((1,), (1,)), ((), ()))


def _cparams(sem, **kw):
    return pltpu.CompilerParams(dimension_semantics=sem, vmem_limit_bytes=VMEM_LIMIT, **kw)


def _dot(a, b, **kw):
    return jnp.dot(a, b, preferred_element_type=F32, **kw)


def _dot_nt(a, b):
    return lax.dot_general(a, b, NT_DIMS, preferred_element_type=F32)


def _row_tile(m, pref):
    return pref if m % pref == 0 else m


def _mod_kernel(c_ref, w_ref, b_ref, o_ref):
    cv = c_ref[...]
    s = cv * jax.nn.sigmoid(cv)
    o_ref[...] = _dot(s, w_ref[...], precision=HIGHEST) + b_ref[...]


def modulation_all(cmat, w_mod, b_mod):
    depth, d, n = w_mod.shape
    tn = 1536
    return pl.pallas_call(
        _mod_kernel,
        grid=(depth, n // tn),
        in_specs=[pl.BlockSpec((SUBLANES, d), lambda l, j: (0, 0)),
                  pl.BlockSpec((None, d, tn), lambda l, j: (l, 0, j)),
                  pl.BlockSpec((None, 1, tn), lambda l, j: (l, 0, j))],
        out_specs=pl.BlockSpec((None, SUBLANES, tn), lambda l, j: (l, 0, j)),
        out_shape=jax.ShapeDtypeStruct((depth, SUBLANES, n), F32),
        compiler_params=_cparams(("arbitrary", "arbitrary")),
        name="modulation",
    )(cmat, w_mod, b_mod.reshape(depth, 1, n))


def _proj_kernel(x_ref, sc_ref, sh_ref, w_ref, *o_refs, splits, scales):
    h = (x_ref[...] * (1.0 + sc_ref[...]) + sh_ref[...]).astype(BF16)
    off = 0
    for o_ref, wd, sc in zip(o_refs, splits, scales):
        y = _dot(h, w_ref[:, off:off + wd])
        if sc != 1.0:
            y = y * sc
        o_ref[...] = y.astype(o_ref.dtype)
        off += wd


def mod_project(x, sc, sh, w, splits, dtypes, scales=None):
    m, d = x.shape
    n = w.shape[1]
    assert sum(splits) == n
    scales = scales or (1.0,) * len(splits)
    tm = _row_tile(m, 512)
    kern = functools.partial(_proj_kernel, splits=tuple(splits), scales=tuple(scales))
    return pl.pallas_call(
        kern,
        grid=(m // tm,),
        in_specs=[pl.BlockSpec((tm, d), lambda i: (i, 0)),
                  pl.BlockSpec((1, d), lambda i: (0, 0)),
                  pl.BlockSpec((1, d), lambda i: (0, 0)),
                  pl.BlockSpec((d, n), lambda i: (0, 0))],
        out_specs=[pl.BlockSpec((tm, wd), lambda i: (i, 0)) for wd in splits],
        out_shape=[jax.ShapeDtypeStruct((m, wd), dt) for wd, dt in zip(splits, dtypes)],
        compiler_params=_cparams(("parallel",)),
        name="mod_project",
    )(x, sc, sh, w.astype(BF16))


def _layer_norm_rows(r, g, b):
    mu = jnp.mean(r, axis=-1, keepdims=True)
    c = r - mu
    var = jnp.mean(c * c, axis=-1, keepdims=True)
    return c * lax.rsqrt(var + LN_EPS) * g + b


def _outproj_ln_kernel(a_ref, b_ref, w_ref, x_ref, gate_ref, g_ref, beta_ref, o_ref):
    ka = a_ref.shape[1]
    y = _dot(a_ref[...], w_ref[:ka, :]) + _dot(b_ref[...], w_ref[ka:, :])
    r = DEEPNORM_ALPHA * x_ref[...] + gate_ref[...] * y
    o_ref[...] = _layer_norm_rows(r, g_ref[...], beta_ref[...])


def outproj_ln(a, b, w, x, gate, g, beta):
    m, d = x.shape
    ka, kb = a.shape[1], b.shape[1]
    tm = _row_tile(m, 512)
    row = lambda i: (i, 0)
    fixed = lambda i: (0, 0)
    return pl.pallas_call(
        _outproj_ln_kernel,
        grid=(m // tm,),
        in_specs=[pl.BlockSpec((tm, ka), row), pl.BlockSpec((tm, kb), row),
                  pl.BlockSpec((ka + kb, d), fixed), pl.BlockSpec((tm, d), row),
                  pl.BlockSpec((1, d), fixed), pl.BlockSpec((1, d), fixed), pl.BlockSpec((1, d), fixed)],
        out_specs=pl.BlockSpec((tm, d), row),
        out_shape=jax.ShapeDtypeStruct((m, d), F32),
        compiler_params=_cparams(("parallel",)),
        name="outproj_ln",
    )(a, b, w.astype(BF16), x, gate, g.reshape(1, d), beta.reshape(1, d))


def _na_bias_table(rpb, rows):
    h = rpb.shape[0]
    ri = np.arange(NA_TILE_ROWS)
    kr = np.arange(NA_KEY_ROWS)
    c = np.arange(GRID_W)
    cs = np.clip(c - NA_KW // 2, 0, GRID_W - NA_KW)
    vc = (c[None, :] >= cs[:, None]) & (c[None, :] < cs[:, None] + NA_KW)
    dc = np.clip(c[None, :] - c[:, None] + NA_KW - 1, 0, 2 * NA_KW - 2)
    pick = (dc.reshape(-1)[:, None] == np.arange(2 * NA_KW - 1)[None, :]).astype(np.float32)
    colb = jnp.einsum('qb,hab->haq', pick, rpb, precision=HIGHEST).reshape(h, 2 * NA_KH - 1, GRID_W, GRID_W)
    colb = jnp.where(vc[None, None], colb, NEG_BIG)

    def case(t):
        start = min(max(NA_TILE_ROWS * t - NA_KH // 2, 0), rows - NA_KEY_ROWS)
        r = NA_TILE_ROWS * t + ri
        rs = np.clip(r - NA_KH // 2, 0, rows - NA_KH)
        krow = start + kr
        vr = (krow[None, :] >= rs[:, None]) & (krow[None, :] < rs[:, None] + NA_KH)
        dr = np.clip(krow[None, :] - r[:, None] + NA_KH - 1, 0, 2 * NA_KH - 2)
        b = jnp.take(colb, dr.reshape(-1), axis=1).reshape(h, NA_TILE_ROWS, NA_KEY_ROWS, GRID_W, GRID_W)
        b = jnp.where(vr[None, :, :, None, None], b, NEG_BIG)
        b = b.transpose(0, 1, 3, 2, 4)
        return b.reshape(h // 2, 2, NA_TILE_ROWS * GRID_W, NA_KEY_ROWS * GRID_W)

    n_tiles = rows // NA_TILE_ROWS
    return jnp.stack([case(0), case(1), case(n_tiles - 1)])


def _pair_masks(shape):
    lane = lax.broadcasted_iota(I32, shape, 1)
    return lane < HEAD_DIM


def _na_kernel(q_ref, k0, k1, k2, k3, v0, v1, v2, v3, kc_ref, vc_ref, bias_ref, o_ref):
    q = q_ref[...]
    lo = _pair_masks(q.shape)
    ks = (k0, k1, k2, k3)
    vs = (v0, v1, v2, v3)
    kb = k0.shape[0]
    outs = []
    for hh in range(2):
        qh = jnp.where(lo if hh == 0 else jnp.logical_not(lo), q, jnp.zeros_like(q))
        s = [_dot_nt(qh, ks[i][...]) + bias_ref[hh, :, i * kb:(i + 1) * kb] for i in range(4)]
        s.append(_dot_nt(qh, kc_ref[...]))
        m = s[0].max(axis=1, keepdims=True)
        for si in s[1:]:
            m = jnp.maximum(m, si.max(axis=1, keepdims=True))
        p = [jnp.exp(si - m) for si in s]
        l = p[0].sum(axis=1, keepdims=True)
        for pi in p[1:]:
            l = l + pi.sum(axis=1, keepdims=True)
        acc = _dot(p[4].astype(BF16), vc_ref[...])
        for i in range(4):
            acc = acc + _dot(p[i].astype(BF16), vs[i][...])
        outs.append(acc / l)
    o_ref[...] = jnp.where(lo, outs[0], outs[1]).astype(o_ref.dtype)


def neighbourhood_attention(q, k, v, k_ctx, v_ctx, rpb):
    l, w = q.shape
    rows = l // GRID_W
    n_tiles = rows // NA_TILE_ROWS
    assert n_tiles >= 3 and rows % NA_TILE_ROWS == 0
    n_ctx = k_ctx.shape[0]
    tq = NA_TILE_ROWS * GRID_W
    kb = NA_KEY_BLOCK_ROWS * GRID_W
    n_kblk = rows // NA_KEY_BLOCK_ROWS
    bias = _na_bias_table(rpb, rows)
    pair_w = 2 * HEAD_DIM

    def kv_spec(i):
        def imap(p, t):
            start = jnp.clip(2 * t - 1, 0, n_kblk - 4)
            return (start + i, p)
        return pl.BlockSpec((kb, pair_w), imap)

    def bias_map(p, t):
        case = jnp.where(t == 0, 0, jnp.where(t == n_tiles - 1, 2, 1))
        return (case, p, 0, 0, 0)

    return pl.pallas_call(
        _na_kernel,
        grid=(w // pair_w, n_tiles),
        in_specs=[pl.BlockSpec((tq, pair_w), lambda p, t: (t, p))]
                 + [kv_spec(i) for i in range(4)] + [kv_spec(i) for i in range(4)]
                 + [pl.BlockSpec((n_ctx, pair_w), lambda p, t: (0, p)),
                    pl.BlockSpec((n_ctx, pair_w), lambda p, t: (0, p)),
                    pl.BlockSpec((None, None, 2, tq, NA_KEY_ROWS * GRID_W), bias_map)],
        out_specs=pl.BlockSpec((tq, pair_w), lambda p, t: (t, p)),
        out_shape=jax.ShapeDtypeStruct((l, w), BF16),
        compiler_params=_cparams(("parallel", "parallel")),
        name="neighbourhood_attention",
    )(q, k, k, k, k, v, v, v, v, k_ctx, v_ctx, bias)


def _ctx_attn_kernel(q_ref, k_ref, v_ref, o_ref):
    q = q_ref[...]
    lo = _pair_masks(q.shape)
    outs = []
    for hh in range(2):
        qh = jnp.where(lo if hh == 0 else jnp.logical_not(lo), q, jnp.zeros_like(q))
        s = _dot_nt(qh, k_ref[...])
        p = jnp.exp(s - s.max(axis=1, keepdims=True))
        outs.append(_dot(p.astype(BF16), v_ref[...]) / p.sum(axis=1, keepdims=True))
    o_ref[...] = jnp.where(lo, outs[0], outs[1]).astype(o_ref.dtype)


def context_attention(q, k, v):
    n, w = q.shape
    pair_w = 2 * HEAD_DIM
    spec = pl.BlockSpec((n, pair_w), lambda p: (0, p))
    return pl.pallas_call(
        _ctx_attn_kernel, grid=(w // pair_w,), in_specs=[spec, spec, spec], out_specs=spec,
        out_shape=jax.ShapeDtypeStruct((n, w), BF16),
        compiler_params=_cparams(("parallel",)), name="context_attention",
    )(q, k, v)


def _shortconv_kernel(u_ref, up_ref, un_ref, w_ref, b_ref, x0_ref, z_ref, *, n_tiles):
    i = pl.program_id(0)
    u = u_ref[...]
    tm = u.shape[0]
    prev_row = jnp.where(i > 0, up_ref[SUBLANES - 1:SUBLANES, :], 0.0)
    next_row = jnp.where(i < n_tiles - 1, un_ref[0:1, :], 0.0)
    row = lax.broadcasted_iota(I32, u.shape, 0)
    u_dn = jnp.where(row == 0, prev_row, pltpu.roll(u, 1, 0))
    u_up = jnp.where(row == tm - 1, next_row, pltpu.roll(u, tm - 1, 0))
    y = u_dn * w_ref[0:1, :] + u * w_ref[1:2, :] + u_up * w_ref[2:3, :] + b_ref[...]
    c = HY_WIDTH
    x0_ref[...] = y[:, :c]
    z_ref[...] = y[:, c:2 * c] * y[:, 2 * c:]


def hyena_gate(u, conv_w, conv_b):
    l, w3 = u.shape
    tm = _row_tile(l, 512)
    n_tiles = l // tm
    per = tm // SUBLANES
    last = l // SUBLANES - 1
    kern = functools.partial(_shortconv_kernel, n_tiles=n_tiles)
    return pl.pallas_call(
        kern,
        grid=(n_tiles,),
        in_specs=[pl.BlockSpec((tm, w3), lambda i: (i, 0)),
                  pl.BlockSpec((SUBLANES, w3), lambda i: (jnp.maximum(i * per - 1, 0), 0)),
                  pl.BlockSpec((SUBLANES, w3), lambda i: (jnp.minimum((i + 1) * per, last), 0)),
                  pl.BlockSpec((3, w3), lambda i: (0, 0)),
                  pl.BlockSpec((1, w3), lambda i: (0, 0))],
        out_specs=[pl.BlockSpec((tm, HY_WIDTH), lambda i: (i, 0))] * 2,
        out_shape=[jax.ShapeDtypeStruct((l, HY_WIDTH), F32)] * 2,
        compiler_params=_cparams(("parallel",)),
        name="hyena_gate",
    )(u, u, u, conv_w, conv_b.reshape(1, w3))


def _filter_kernel(bands_ref, w1t_ref, w1c_ref, w1s_ref, b1_ref, fr_ref, w2_ref, b2_ref, w3_ref, dl_ref,
                   taps_ref, asum_ref, *, l, tp):
    i = pl.program_id(0)
    hid_w = w2_ref.shape[0]
    c = HY_WIDTH
    denom = float(max(l - 1, 1))

    def pos(width):
        return (lax.broadcasted_iota(I32, (tp, width), 0) + i * tp).astype(F32)

    ang = (2.0 * math.pi / l) * pos(HY_BANDS) * bands_ref[...]
    pre = ((pos(hid_w) / denom) * w1t_ref[...]
           + _dot(jnp.cos(ang), w1c_ref[...], precision=HIGHEST)
           + _dot(-jnp.sin(ang), w1s_ref[...], precision=HIGHEST) + b1_ref[...])
    hid = jnp.sin(fr_ref[...] * pre)
    hid = jnp.sin(fr_ref[...] * (_dot(hid, w2_ref[...], precision=HIGHEST) + b2_ref[...]))
    taps = _dot(hid, w3_ref[...], precision=HIGHEST)
    pc = pos(c)
    window = jnp.exp(-(pc / denom) * dl_ref[...])
    fwd = taps[:, :c] * window
    bwd = jnp.where(pc == 0.0, 0.0, taps[:, c:] * window)
    taps_ref[:, :c] = fwd
    taps_ref[:, c:] = bwd

    @pl.when(i == 0)
    def _():
        asum_ref[...] = jnp.zeros_like(asum_ref)

    asum_ref[...] += jnp.sum(jnp.abs(fwd) + jnp.abs(bwd), axis=0, keepdims=True)


def hyena_filter_taps(l, f_w1, f_b1, f_freq, f_w2, f_b2, f_w3):
    c = HY_WIDTH
    hid = f_w2.shape[0]
    tp = _row_tile(l, 1024)
    bands = jnp.linspace(1e-4, HY_BANDS - 1, HY_BANDS, dtype=F32).reshape(1, HY_BANDS)
    deltas = jnp.abs(jnp.linspace(math.log(HY_DECAY_TARGET) / HY_DECAY_PCT_MAX,
                                  math.log(HY_DECAY_TARGET) / HY_DECAY_PCT_MIN, c, dtype=F32)).reshape(1, c)
    fixed = lambda i: (0, 0)
    full = lambda a: pl.BlockSpec(a.shape, fixed)
    args = (bands, f_w1[0:1], f_w1[1:1 + HY_BANDS], f_w1[1 + HY_BANDS:], f_b1.reshape(1, hid),
            f_freq.reshape(1, hid), f_w2, f_b2.reshape(1, hid), f_w3, deltas)
    kern = functools.partial(_filter_kernel, l=l, tp=tp)
    return pl.pallas_call(
        kern,
        grid=(l // tp,),
        in_specs=[full(a) for a in args],
        out_specs=[pl.BlockSpec((tp, 2 * c), lambda i: (i, 0)), pl.BlockSpec((1, c), fixed)],
        out_shape=[jax.ShapeDtypeStruct((l, 2 * c), F32), jax.ShapeDtypeStruct((1, c), F32)],
        compiler_params=_cparams(("arbitrary",)),
        name="hyena_filter",
    )(*args)


def _dft_tables(l):
    n = 2 * l
    n1 = DFT_N1
    n2 = n // n1
    k1 = jnp.arange(n1)[:, None]
    m1 = jnp.arange(n1 // 2)[None, :]
    ph1 = (2.0 * math.pi / n1) * ((k1 * m1) % n1).astype(F32)
    d1 = jnp.stack([jnp.cos(ph1), -jnp.sin(ph1)], axis=1).reshape(2 * n1, n1 // 2)
    d1_inv = d1.T
    kk = (jnp.arange(n2)[None, :, None] * n1 + jnp.arange(n1)[:, None, None])
    th = (2.0 * math.pi / n) * ((kk * jnp.arange(n2)[None, None, :]) % n).astype(F32)
    cs, sn = jnp.cos(th), jnp.sin(th)
    m2 = jnp.concatenate([jnp.concatenate([cs, sn], axis=2), jnp.concatenate([-sn, cs], axis=2)], axis=1)
    m2_inv = jnp.swapaxes(m2, 1, 2)
    return d1.astype(BF16), d1_inv.astype(BF16), m2.astype(BF16), m2_inv.astype(BF16)


def _dft1_kernel(d_ref, x_ref, o_ref):
    o_ref[...] = _dot(d_ref[...], x_ref[...].astype(BF16)).astype(o_ref.dtype)


def dft_stage1(x, d1, n2):
    l, c = x.shape
    n1h = d1.shape[1]
    cols = n2 * c
    tc = min(cols, 4096)
    out = pl.pallas_call(
        _dft1_kernel,
        grid=(cols // tc,),
        in_specs=[pl.BlockSpec(d1.shape, lambda j: (0, 0)), pl.BlockSpec((n1h, tc), lambda j: (0, j))],
        out_specs=pl.BlockSpec((d1.shape[0], tc), lambda j: (0, j)),
        out_shape=jax.ShapeDtypeStruct((d1.shape[0], cols), BF16),
        compiler_params=_cparams(("parallel",)),
        name="dft_stage1",
    )(d1, x.reshape(n1h, cols))
    return out.reshape(d1.shape[0] // 2, 2, n2, c)


def _filter_spectrum_kernel(m_ref, a_ref, h_ref):
    n2 = a_ref.shape[1]
    c = h_ref.shape[2]
    x = _dot(m_ref[...], a_ref[...].reshape(2 * n2, 2 * c))
    h_ref[0] = x[:n2, :c] + x[:n2, c:]
    h_ref[1] = x[n2:, :c] - x[n2:, c:]


def filter_spectrum(a_taps, m2):
    n1, _, n2, c2 = a_taps.shape
    c = c2 // 2
    return pl.pallas_call(
        _filter_spectrum_kernel,
        grid=(n1,),
        in_specs=[pl.BlockSpec((None, 2 * n2, 2 * n2), lambda i: (i, 0, 0)),
                  pl.BlockSpec((None, 2, n2, c2), lambda i: (i, 0, 0, 0))],
        out_specs=pl.BlockSpec((None, 2, n2, c), lambda i: (i, 0, 0, 0)),
        out_shape=jax.ShapeDtypeStruct((n1, 2, n2, c), F32),
        compiler_params=_cparams(("parallel",)),
        name="filter_spectrum",
    )(m2, a_taps)


def _spectral_mix_kernel(m_ref, mi_ref, a_ref, h_ref, o_ref):
    n2 = a_ref.shape[1]
    c = a_ref.shape[2]
    x = _dot(m_ref[...], a_ref[...].reshape(2 * n2, c))
    xr, xi = x[:n2], x[n2:]
    hr, hi = h_ref[0], h_ref[1]
    y = jnp.concatenate([xr * hr - xi * hi, xr * hi + xi * hr], axis=0).astype(BF16)
    o_ref[...] = _dot(mi_ref[...], y).reshape(2, n2, c).astype(o_ref.dtype)


def spectral_mix(a_z, h, m2, m2_inv):
    n1, _, n2, c = a_z.shape
    blk = pl.BlockSpec((None, 2, n2, c), lambda i: (i, 0, 0, 0))
    mat = pl.BlockSpec((None, 2 * n2, 2 * n2), lambda i: (i, 0, 0))
    return pl.pallas_call(
        _spectral_mix_kernel,
        grid=(n1,),
        in_specs=[mat, mat, blk, blk],
        out_specs=blk,
        out_shape=jax.ShapeDtypeStruct((n1, 2, n2, c), BF16),
        compiler_params=_cparams(("parallel",)),
        name="spectral_mix",
    )(m2, m2_inv, a_z, h)


def _hyena_out_kernel(di_ref, b_ref, x0_ref, z_ref, inv_ref, skip_ref, o_ref, *, inv_n):
    conv = _dot(di_ref[...], b_ref[...]) * inv_n
    o_ref[...] = (x0_ref[...] * (conv * inv_ref[...] + z_ref[...] * skip_ref[...])).astype(o_ref.dtype)


def hyena_output(b, d1_inv, x0, z, inv_norm, skip):
    n1, _, n2, c = b.shape
    l = x0.shape[0]
    cols = n2 * c
    tc = min(cols, 4096)
    n1h = n1 // 2
    rep = tc // c
    kern = functools.partial(_hyena_out_kernel, inv_n=1.0 / (2 * l))
    tile = pl.BlockSpec((n1h, tc), lambda j: (0, j))
    out = pl.pallas_call(
        kern,
        grid=(cols // tc,),
        in_specs=[pl.BlockSpec(d1_inv.shape, lambda j: (0, 0)),
                  pl.BlockSpec((2 * n1, tc), lambda j: (0, j)),
                  tile, tile,
                  pl.BlockSpec((1, tc), lambda j: (0, 0)), pl.BlockSpec((1, tc), lambda j: (0, 0))],
        out_specs=tile,
        out_shape=jax.ShapeDtypeStruct((n1h, cols), BF16),
        compiler_params=_cparams(("parallel",)),
        name="hyena_output",
    )(d1_inv, b.reshape(2 * n1, cols), x0.reshape(n1h, cols), z.reshape(n1h, cols),
      jnp.tile(inv_norm, (1, rep)), jnp.tile(skip.reshape(1, c), (1, rep)))
    return out.reshape(l, c)


def _small_conv_kernel(d_ref, di_ref, z_ref, taps_ref, x0_ref, inv_ref, skip_ref, o_ref, *, inv_n):
    c = z_ref.shape[1]
    n = d_ref.shape[0] // 2
    zs = _dot(d_ref[...], z_ref[...], precision=HIGHEST)
    ts = _dot(d_ref[...], taps_ref[...], precision=HIGHEST)
    hr = ts[:n, :c] + ts[:n, c:]
    hi = ts[n:, :c] - ts[n:, c:]
    zr, zi = zs[:n], zs[n:]
    y = jnp.concatenate([zr * hr - zi * hi, zr * hi + zi * hr], axis=0)
    conv = _dot(di_ref[...], y, precision=HIGHEST) * inv_n
    o_ref[...] = (x0_ref[...] * (conv * inv_ref[...] + z_ref[...] * skip_ref[...])).astype(o_ref.dtype)


def hyena_output_short(z, taps, x0, inv_norm, skip):
    l, c = z.shape
    n = 2 * l
    ph = (2.0 * math.pi / n) * ((jnp.arange(n)[:, None] * jnp.arange(l)[None, :]) % n).astype(F32)
    d = jnp.concatenate([jnp.cos(ph), -jnp.sin(ph)], axis=0)
    di = jnp.concatenate([jnp.cos(ph), -jnp.sin(ph)], axis=0).T
    args = (d, di, z, taps, x0, inv_norm, skip.reshape(1, c))
    kern = functools.partial(_small_conv_kernel, inv_n=1.0 / n)
    return pl.pallas_call(
        kern,
        grid=(1,),
        in_specs=[pl.BlockSpec(a.shape, lambda i: (0, 0)) for a in args],
        out_specs=pl.BlockSpec((l, c), lambda i: (0, 0)),
        out_shape=jax.ShapeDtypeStruct((l, c), BF16),
        compiler_params=_cparams(("arbitrary",)),
        name="hyena_output_short",
    )(*args)


def hyena_long(u, conv_w, conv_b, f_w1, f_b1, f_freq, f_w2, f_b2, f_w3, skip):
    l = u.shape[0]
    x0, z = hyena_gate(u, conv_w, conv_b)
    taps, asum = hyena_filter_taps(l, f_w1, f_b1, f_freq, f_w2, f_b2, f_w3)
    inv_norm = 1.0 / asum
    if 2 * l < DFT_N1 * SUBLANES * 2:
        return hyena_output_short(z, taps, x0, inv_norm, skip)
    n2 = 2 * l // DFT_N1
    d1, d1_inv, m2, m2_inv = _dft_tables(l)
    h = filter_spectrum(dft_stage1(taps, d1, n2), m2)
    b = spectral_mix(dft_stage1(z, d1, n2), h, m2, m2_inv)
    return hyena_output(b, d1_inv, x0, z, inv_norm, skip)


def _head_sumsq(x, bd):
    sq = x * x
    hi = sq.astype(BF16)
    lo = (sq - hi.astype(F32)).astype(BF16)
    return _dot(hi, bd) + _dot(lo, bd)


def _qk_prep_kernel(x_ref, gain_ref, bd_ref, *rest, rope, scale, transposed):
    x = x_ref[...]
    w = x.shape[1]
    ms = _head_sumsq(x, bd_ref[...]) * (1.0 / HEAD_DIM)
    xn = x * lax.rsqrt(ms + RMS_EPS) * gain_ref[...]
    if rope:
        cos_ref, sin_ref, o_ref = rest
        reps = w // cos_ref.shape[1]
        cos = jnp.tile(cos_ref[...], (1, reps)) if reps > 1 else cos_ref[...]
        sin = jnp.tile(sin_ref[...], (1, reps)) if reps > 1 else sin_ref[...]
        lane = lax.broadcasted_iota(I32, x.shape, 1)
        partner = jnp.where(lane % 2 == 0, pltpu.roll(xn, w - 1, 1), pltpu.roll(xn, 1, 1))
        xn = xn * cos + partner * sin
    else:
        (o_ref,) = rest
    if scale != 1.0:
        xn = xn * scale
    if transposed:
        xn = xn.T
    o_ref[...] = xn.astype(o_ref.dtype)


def _rope_tables(l):
    half = HEAD_DIM // 2
    inv_freq = ROPE_THETA ** (-jnp.arange(0, half, 2, dtype=F32) / half)
    t = jnp.arange(l)
    row = (t // GRID_W).astype(F32)
    col = (t % GRID_W).astype(F32)
    ang = jnp.concatenate([jnp.repeat(row[:, None] * inv_freq[None], 2, axis=1),
                           jnp.repeat(col[:, None] * inv_freq[None], 2, axis=1)], axis=1)
    sign = jnp.where(jnp.arange(HEAD_DIM) % 2 == 0, -1.0, 1.0).astype(F32)
    cos = jnp.tile(jnp.cos(ang), (1, 2))
    sin = jnp.tile(jnp.sin(ang) * sign[None], (1, 2))
    return cos, sin


def qk_prep(x, gain, rope_tabs, scale, transposed=False):
    l, w = x.shape
    tm = _row_tile(l, 512)
    head = jnp.arange(w) // HEAD_DIM
    bd = (head[:, None] == head[None, :]).astype(BF16)
    gain_t = jnp.tile(gain.reshape(1, HEAD_DIM), (1, w // HEAD_DIM))
    row = lambda i: (i, 0)
    fixed = lambda i: (0, 0)
    in_specs = [pl.BlockSpec((tm, w), row), pl.BlockSpec((1, w), fixed), pl.BlockSpec((w, w), fixed)]
    args = [x, gain_t, bd]
    if rope_tabs is not None:
        in_specs += [pl.BlockSpec((tm, 2 * HEAD_DIM), row)] * 2
        args += list(rope_tabs)
    kern = functools.partial(_qk_prep_kernel, rope=rope_tabs is not None, scale=scale, transposed=transposed)
    if transposed:
        out_spec, out_shape = pl.BlockSpec((w, tm), lambda i: (0, i)), (w, l)
    else:
        out_spec, out_shape = pl.BlockSpec((tm, w), row), (l, w)
    return pl.pallas_call(
        kern, grid=(l // tm,), in_specs=in_specs, out_specs=out_spec,
        out_shape=jax.ShapeDtypeStruct(out_shape, BF16),
        compiler_params=_cparams(("parallel",)), name="qk_prep",
    )(*args)


def _flash_kernel(qt_ref, k_ref, vt_ref, o_ref, qs_ref, s_ref, m_ref, l_ref, acc_ref, *, tk, nk):
    dh = HEAD_DIM
    rep = qt_ref.shape[0] // dh
    tq = qt_ref.shape[1]
    for r in range(rep):
        qs_ref[:, r * tq:(r + 1) * tq] = qt_ref[r * dh:(r + 1) * dh, :]
    m_ref[...] = jnp.full(m_ref.shape, NEG_BIG, F32)
    l_ref[...] = jnp.zeros(l_ref.shape, F32)
    acc_ref[...] = jnp.zeros(acc_ref.shape, F32)

    def scores(j):
        start = pl.multiple_of(jnp.minimum(j, nk - 1) * tk, tk)
        return _dot(k_ref[pl.ds(start, tk), :], qs_ref[...])

    def absorb(j, s):
        start = pl.multiple_of(j * tk, tk)
        m_old = m_ref[...]
        m_new = jnp.maximum(m_old, s.max(axis=0, keepdims=True))
        alpha = jnp.exp2(m_old - m_new)
        p = jnp.exp2(s - m_new)
        l_ref[...] = alpha * l_ref[...] + p.sum(axis=0, keepdims=True)
        acc_ref[...] = alpha * acc_ref[...] + _dot(vt_ref[:, pl.ds(start, tk)], p.astype(BF16))
        m_ref[...] = m_new

    s_ref[0] = scores(0)

    def body(i, carry):
        j = 2 * i
        s_ref[1] = scores(j + 1)
        absorb(j, s_ref[0])
        s_ref[0] = scores(j + 2)
        absorb(j + 1, s_ref[1])
        return carry

    lax.fori_loop(0, nk // 2, body, 0)
    if nk % 2:
        absorb(nk - 1, s_ref[0])
    out = acc_ref[...] / l_ref[...]
    for r in range(rep):
        o_ref[:, r * dh:(r + 1) * dh] = out[:, r * tq:(r + 1) * tq].T.astype(o_ref.dtype)


def _kv_chunk(lk):
    for tiles in (5, 4, 3, 2, 1):
        if lk % (tiles * 256) == 0:
            return tiles * 256
    raise ValueError(lk)


def gqa_attention(q_t, k_hm, v_t):
    wq, l = q_t.shape
    hkv, lk, dh = k_hm.shape
    wg = wq // hkv
    rep = wg // dh
    tq = _row_tile(l, 256)
    tk = _kv_chunk(lk)
    kern = functools.partial(_flash_kernel, tk=tk, nk=lk // tk)
    return pl.pallas_call(
        kern,
        grid=(hkv, l // tq),
        in_specs=[pl.BlockSpec((wg, tq), lambda g, i: (g, i)),
                  pl.BlockSpec((None, lk, dh), lambda g, i: (g, 0, 0)),
                  pl.BlockSpec((None, dh, lk), lambda g, i: (g, 0, 0))],
        out_specs=pl.BlockSpec((tq, wg), lambda g, i: (i, g)),
        out_shape=jax.ShapeDtypeStruct((l, wq), BF16),
        scratch_shapes=[pltpu.VMEM((dh, rep * tq), BF16), pltpu.VMEM((2, tk, rep * tq), F32),
                        pltpu.VMEM((1, rep * tq), F32), pltpu.VMEM((1, rep * tq), F32),
                        pltpu.VMEM((dh, rep * tq), F32)],
        compiler_params=_cparams(("parallel", "parallel")),
        name="gqa_attention",
    )(q_t, k_hm, v_t)


def _cmul(ar, ai, br, bi):
    return ar * br - ai * bi, ar * bi + ai * br


def _s5_operators(a_re, a_im, log_dt, b_re, b_im, c_re, c_im):
    t = S5_CHUNK
    gs = S5_GROUP
    hp = dict(precision=HIGHEST)
    dt = jnp.exp(log_dt)[..., None]
    zr, zi = a_re * dt, a_im * dt
    er = jnp.exp(zr)
    abr, abi = er * jnp.cos(zi), er * jnp.sin(zi)
    den = a_re * a_re + a_im * a_im
    fr = ((abr - 1.0) * a_re + abi * a_im) / den
    fi = (abi * a_re - (abr - 1.0) * a_im) / den
    bbr, bbi = _cmul(fr[..., None], fi[..., None], b_re, b_im)
    tau = jnp.arange(t + 1, dtype=F32)
    pr = jnp.exp(zr[..., None] * tau) * jnp.cos(zi[..., None] * tau)
    pi = jnp.exp(zr[..., None] * tau) * jnp.sin(zi[..., None] * tau)
    car, cai = _cmul(c_re[..., None], c_im[..., None], pr[:, :, None, :, :t], pi[:, :, None, :, :t])
    ktap = (jnp.einsum('dgqpt,dgpk->dgtqk', car, bbr, **hp) - jnp.einsum('dgqpt,dgpk->dgtqk', cai, bbi, **hp))
    ktp = jnp.concatenate([jnp.zeros_like(ktap), ktap], axis=2)
    blocks = jnp.stack([ktp[:, :, t - i:2 * t - i] for i in range(t)], axis=2)
    m_op = blocks.transpose(0, 1, 2, 5, 3, 4).reshape(2, S5_GROUPS, t * gs, t * gs)
    prr, pir = pr[..., t - 1::-1], pi[..., t - 1::-1]
    wr, wi = _cmul(prr[:, :, :, :, None], pir[:, :, :, :, None], bbr[:, :, :, None, :], bbi[:, :, :, None, :])
    w_re = wr.transpose(0, 1, 3, 4, 2).reshape(2, S5_GROUPS, t * gs, S5_STATE)
    w_im = wi.transpose(0, 1, 3, 4, 2).reshape(2, S5_GROUPS, t * gs, S5_STATE)
    vr, vi = _cmul(c_re[..., None], c_im[..., None], pr[:, :, None, :, 1:], pi[:, :, None, :, 1:])
    v_re = vr.transpose(0, 1, 3, 4, 2).reshape(2, S5_GROUPS, S5_STATE, t * gs)
    v_im = (-vi).transpose(0, 1, 3, 4, 2).reshape(2, S5_GROUPS, S5_STATE, t * gs)
    return m_op, w_re, w_im, v_re, v_im, pr[..., t], pi[..., t]


def _pair_blockdiag(x):
    d, g, r, c = x.shape
    x = x.reshape(d, g // 2, 2, r, c)
    z = jnp.zeros_like(x[:, :, 0])
    top = jnp.concatenate([x[:, :, 0], z], axis=-1)
    bot = jnp.concatenate([z, x[:, :, 1]], axis=-1)
    return jnp.concatenate([top, bot], axis=-2)


def _s5_state_in_kernel(u_ref, w_ref, er_ref, ei_ref):
    e = _dot(u_ref[...], w_ref[...])
    half = e.shape[1] // 2
    er_ref[...] = e[:, :half]
    ei_ref[...] = e[:, half:]


def _s5_scan_kernel(ar_ref, ai_ref, er_ref, ei_ref, sr_ref, si_ref, cr_ref, ci_ref):
    @pl.when(pl.program_id(0) == 0)
    def _():
        cr_ref[...] = jnp.zeros_like(cr_ref)
        ci_ref[...] = jnp.zeros_like(ci_ref)

    ar, ai = ar_ref[...], ai_ref[...]

    def body(c, carry):
        sr, si = carry
        sr_ref[c] = sr
        si_ref[c] = si
        nr = ar * sr - ai * si + er_ref[c]
        ni = ar * si + ai * sr + ei_ref[c]
        return nr, ni

    sr, si = lax.fori_loop(0, er_ref.shape[0], body, (cr_ref[...], ci_ref[...]))
    cr_ref[...] = sr
    ci_ref[...] = si


def _s5_out_kernel(u_ref, sr_ref, si_ref, m_ref, vr_ref, vi_ref, y_ref):
    y = _dot(u_ref[...], m_ref[...])
    y = y + _dot(sr_ref[...].astype(BF16), vr_ref[...]) + _dot(si_ref[...].astype(BF16), vi_ref[...])
    y_ref[...] = y


def _largest_divisor(n, cap):
    for d in range(min(cap, n), 0, -1):
        if n % d == 0 and (d % SUBLANES == 0 or d == n):
            return d
    return n


def s5_scan_outputs(u_ctx, u_lat, a_re, a_im, log_dt, b_re, b_im, c_re, c_im):
    t, gs, g = S5_CHUNK, S5_GROUP, S5_GROUPS
    n_ctx, l = u_ctx.shape[0], u_lat.shape[0]
    n_tok = n_ctx + l
    nch = n_tok // t
    pairs = g // 2
    m_op, w_re, w_im, v_re, v_im, atr, ati = _s5_operators(a_re, a_im, log_dt, b_re, b_im, c_re, c_im)
    m_bd = _pair_blockdiag(m_op).astype(BF16)
    w_bd = jnp.concatenate([_pair_blockdiag(w_re), _pair_blockdiag(w_im)], axis=-1).astype(BF16)
    vr_bd = _pair_blockdiag(v_re).astype(BF16)
    vi_bd = _pair_blockdiag(v_im).astype(BF16)

    def chunked(seq):
        return seq.reshape(nch, t, g, gs).transpose(0, 2, 1, 3).reshape(nch, g * t * gs)

    fwd = jnp.concatenate([u_ctx, u_lat], axis=0)
    bwd = jnp.concatenate([u_ctx[::-1], u_lat[::-1]], axis=0)
    u_ch = jnp.stack([chunked(fwd), chunked(bwd)]).astype(BF16)
    pw = 2 * t * gs
    sw = 2 * S5_STATE
    n_state = g * S5_STATE

    e_re, e_im = pl.pallas_call(
        _s5_state_in_kernel,
        grid=(2, pairs),
        in_specs=[pl.BlockSpec((None, nch, pw), lambda d, p: (d, 0, p)),
                  pl.BlockSpec((None, None, pw, 2 * sw), lambda d, p: (d, p, 0, 0))],
        out_specs=[pl.BlockSpec((nch, sw), lambda d, p: (0, d * pairs + p))] * 2,
        out_shape=[jax.ShapeDtypeStruct((nch, 2 * n_state), F32)] * 2,
        compiler_params=_cparams(("parallel", "parallel")),
        name="s5_state_in",
    )(u_ch, w_bd)

    slab = 2 * n_state // SUBLANES
    tc = _largest_divisor(nch, 128)
    blk = pl.BlockSpec((tc, SUBLANES, slab), lambda i: (i, 0, 0))
    vec = pl.BlockSpec((SUBLANES, slab), lambda i: (0, 0))
    s_re, s_im = pl.pallas_call(
        _s5_scan_kernel,
        grid=(nch // tc,),
        in_specs=[vec, vec, blk, blk],
        out_specs=[blk, blk],
        out_shape=[jax.ShapeDtypeStruct((nch, SUBLANES, slab), F32)] * 2,
        scratch_shapes=[pltpu.VMEM((SUBLANES, slab), F32)] * 2,
        compiler_params=_cparams(("arbitrary",)),
        name="s5_scan",
    )(atr.reshape(SUBLANES, slab), ati.reshape(SUBLANES, slab),
      e_re.reshape(nch, SUBLANES, slab), e_im.reshape(nch, SUBLANES, slab))

    y = pl.pallas_call(
        _s5_out_kernel,
        grid=(2, pairs),
        in_specs=[pl.BlockSpec((None, nch, pw), lambda d, p: (d, 0, p)),
                  pl.BlockSpec((nch, sw), lambda d, p: (0, d * pairs + p)),
                  pl.BlockSpec((nch, sw), lambda d, p: (0, d * pairs + p)),
                  pl.BlockSpec((None, None, pw, pw), lambda d, p: (d, p, 0, 0)),
                  pl.BlockSpec((None, None, sw, pw), lambda d, p: (d, p, 0, 0)),
                  pl.BlockSpec((None, None, sw, pw), lambda d, p: (d, p, 0, 0))],
        out_specs=pl.BlockSpec((None, nch, pw), lambda d, p: (d, 0, p)),
        out_shape=jax.ShapeDtypeStruct((2, nch, g * t * gs), F32),
        compiler_params=_cparams(("parallel", "parallel")),
        name="s5_out",
    )(u_ch, s_re.reshape(nch, 2 * n_state), s_im.reshape(nch, 2 * n_state), m_bd, vr_bd, vi_bd)

    y = y.reshape(2, nch, g, t, gs).transpose(0, 1, 3, 2, 4).reshape(2, n_tok, g * gs)[:, n_ctx:]
    return y[0], y[1, ::-1]


def _s5_readout_kernel(yf_ref, yb_ref, u_ref, d_ref, w_ref, b_ref, o_ref):
    y = yf_ref[...] + yb_ref[...] + d_ref[...] * u_ref[...]
    y = 0.5 * y * (1.0 + jnp.tanh(math.sqrt(2.0 / math.pi) * (y + 0.044715 * (y * y * y))))
    gate = jax.nn.sigmoid(_dot(y.astype(BF16), w_ref[...]) + b_ref[...])
    o_ref[...] = (y * gate).astype(o_ref.dtype)


def s5_readout(y_f, y_b, u, d_skip, glu_w, glu_b):
    l, w = u.shape
    tm = _row_tile(l, 512)
    row = lambda i: (i, 0)
    fixed = lambda i: (0, 0)
    return pl.pallas_call(
        _s5_readout_kernel,
        grid=(l // tm,),
        in_specs=[pl.BlockSpec((tm, w), row)] * 3
                 + [pl.BlockSpec((1, w), fixed), pl.BlockSpec((w, w), fixed), pl.BlockSpec((1, w), fixed)],
        out_specs=pl.BlockSpec((tm, w), row),
        out_shape=jax.ShapeDtypeStruct((l, w), BF16),
        compiler_params=_cparams(("parallel",)),
        name="s5_readout",
    )(y_f, y_b, u, d_skip.reshape(1, w), glu_w.astype(BF16), glu_b.reshape(1, w))


def _first_max(vals, lane):
    m = vals.max(axis=1, keepdims=True)
    idx = jnp.where(vals == m, lane, jnp.int32(1 << 20)).min(axis=1, keepdims=True)
    return m, idx


def _stream_specs(n_ctx_tiles, tm, d):
    return (pl.BlockSpec((tm, d), lambda i: (jnp.clip(i, 0, max(n_ctx_tiles - 1, 0)), 0)),
            pl.BlockSpec((tm, d), lambda i: (jnp.maximum(i - n_ctx_tiles, 0), 0)))


def _stream_tile(xc_ref, xl_ref, n_ctx_tiles):
    if n_ctx_tiles == 0:
        return xl_ref[...]
    return jnp.where(pl.program_id(0) < n_ctx_tiles, xc_ref[...], xl_ref[...])


def _store_row_slabs(ref, x):
    rows = x.shape[0]
    for s in range(ROW_SLAB):
        ref[pl.ds(s, rows, stride=ROW_SLAB), :] = x[:, s * LANES:(s + 1) * LANES]


def _router_kernel(xc_ref, xl_ref, sc_ref, sh_ref, rw_ref, rb_ref, tri_ref,
                   hf_ref, te_ref, gt_ref, rk_ref, cnt_ref, run_ref, *, n_ctx_tiles):
    @pl.when(pl.program_id(0) == 0)
    def _():
        run_ref[...] = jnp.zeros_like(run_ref)

    hf = _stream_tile(xc_ref, xl_ref, n_ctx_tiles) * (1.0 + sc_ref[...]) + sh_ref[...]
    _store_row_slabs(hf_ref, hf)
    tm = hf.shape[0]
    scores = jax.nn.sigmoid(_dot(hf, rw_ref[...], precision=HIGHEST))
    biased = scores + rb_ref[...]
    lane = lax.broadcasted_iota(I32, (tm, N_EXPERTS), 1)
    grp = lane // (N_EXPERTS // N_EXPERT_GROUPS)
    lane_o = lax.broadcasted_iota(I32, (tm, LANES), 1)
    neg = jnp.float32(-jnp.inf)

    group_score = jnp.full((tm, LANES), neg, F32)
    for g in range(N_EXPERT_GROUPS):
        vals = jnp.where(grp == g, biased, neg)
        m1, i1 = _first_max(vals, lane)
        m2 = jnp.where(lane == i1, neg, vals).max(axis=1, keepdims=True)
        group_score = jnp.where(lane_o == g, m1 + m2, group_score)
    keep = jnp.zeros((tm, N_EXPERTS), F32)
    for _ in range(TOPK_GROUPS):
        _, gi = _first_max(group_score, lane_o)
        keep = jnp.where(grp == gi, 1.0, keep)
        group_score = jnp.where(lane_o == gi, neg, group_score)

    masked = jnp.where(keep > 0.0, biased, neg)
    member = jnp.zeros((tm, N_EXPERTS), F32)
    e_cols, g_cols = [], []
    for _ in range(TOP_K):
        _, ei = _first_max(masked, lane)
        hit = lane == ei
        g_cols.append(jnp.where(hit, scores, 0.0).sum(axis=1, keepdims=True))
        masked = jnp.where(hit, neg, masked)
        member = jnp.where(hit, 1.0, member)
        e_cols.append(ei)
    g_sum = g_cols[0]
    for gk in g_cols[1:]:
        g_sum = g_sum + gk

    before = _dot(tri_ref[...], member.astype(BF16)) + run_ref[...]
    te = jnp.zeros((tm, LANES), I32)
    rk = jnp.zeros((tm, LANES), I32)
    gt = jnp.zeros((tm, LANES), F32)
    for k in range(TOP_K):
        rank = jnp.where(lane == e_cols[k], before, 0.0).sum(axis=1, keepdims=True)
        te = jnp.where(lane_o == k, e_cols[k], te)
        rk = jnp.where(lane_o == k, rank.astype(I32), rk)
        gt = jnp.where(lane_o == k, ROUTED_SCALE * g_cols[k] / g_sum, gt)
    te_ref[...] = te
    rk_ref[...] = rk
    gt_ref[...] = gt
    run_ref[...] += member.sum(axis=0, keepdims=True)
    cnt_ref[...] = run_ref[...]


def moe_route(x_ctx, x_lat, sc2, sh2, n_ctx_tiles, router_w, router_bias):
    d = x_lat.shape[1]
    tm = MOE_TILE
    n = n_ctx_tiles * tm + x_lat.shape[0]
    tri = (jnp.arange(tm)[None, :] < jnp.arange(tm)[:, None]).astype(BF16)
    row = lambda i: (i, 0)
    fixed = lambda i: (0, 0)
    seg = lambda i: (jnp.where(i < n_ctx_tiles, 1, 0), 0, 0)
    kern = functools.partial(_router_kernel, n_ctx_tiles=n_ctx_tiles)
    return pl.pallas_call(
        kern,
        grid=(n // tm,),
        in_specs=[*_stream_specs(n_ctx_tiles, tm, d), pl.BlockSpec((None, 1, d), seg), pl.BlockSpec((None, 1, d), seg),
                  pl.BlockSpec((d, N_EXPERTS), fixed), pl.BlockSpec((1, N_EXPERTS), fixed),
                  pl.BlockSpec((tm, tm), fixed)],
        out_specs=[pl.BlockSpec((tm * ROW_SLAB, LANES), row), pl.BlockSpec((tm, LANES), row),
                   pl.BlockSpec((tm, LANES), row), pl.BlockSpec((tm, LANES), row),
                   pl.BlockSpec((1, N_EXPERTS), fixed)],
        out_shape=[jax.ShapeDtypeStruct((n * ROW_SLAB, LANES), F32), jax.ShapeDtypeStruct((n, LANES), I32),
                   jax.ShapeDtypeStruct((n, LANES), F32), jax.ShapeDtypeStruct((n, LANES), I32),
                   jax.ShapeDtypeStruct((1, N_EXPERTS), F32)],
        scratch_shapes=[pltpu.VMEM((1, N_EXPERTS), F32)],
        compiler_params=_cparams(("arbitrary",)),
        name="moe_route",
    )(x_ctx, x_lat, sc2, sh2, router_w, router_bias.reshape(1, N_EXPERTS), tri)


def _slab_rows(ref, row, n_rows):
    first = row * ROW_SLAB
    if not isinstance(first, int):
        first = pl.multiple_of(first, ROW_SLAB)
    return ref.at[pl.ds(first, n_rows * ROW_SLAB), :]


def _dispatch_kernel(dest_ref, hf_ref, xs_hbm, zbuf, sem, zsem, *, n_assign):
    n_rows = dest_ref.shape[0]

    @pl.when(pl.program_id(0) == 0)
    def _():
        zbuf[...] = jnp.zeros_like(zbuf)
        tail = pltpu.make_async_copy(zbuf, _slab_rows(xs_hbm, n_assign, EXPERT_BLOCK), zsem)
        tail.start()
        tail.wait()

    def body(r, carry):
        src = _slab_rows(hf_ref, r, 1)
        for k in range(TOP_K):
            pltpu.make_async_copy(src, _slab_rows(xs_hbm, dest_ref[r * TOP_K + k], 1), sem).start()
        return carry
    lax.fori_loop(0, n_rows // TOP_K, body, 0)
    for _ in range(TOP_K):
        pltpu.make_async_copy(hf_ref, hf_ref, sem).wait()


def moe_dispatch(dest, hf_slabs):
    n_assign = dest.shape[0]
    tm = MOE_TILE
    n_rows = tm * TOP_K
    kern = functools.partial(_dispatch_kernel, n_assign=n_assign)
    return pl.pallas_call(
        kern,
        grid=(n_assign // n_rows,),
        in_specs=[pl.BlockSpec((n_rows,), lambda i: (i,), memory_space=pltpu.SMEM),
                  pl.BlockSpec((tm * ROW_SLAB, LANES), lambda i: (i, 0))],
        out_specs=pl.BlockSpec(memory_space=pl.ANY),
        out_shape=jax.ShapeDtypeStruct(((n_assign + EXPERT_BLOCK) * ROW_SLAB, LANES), F32),
        scratch_shapes=[pltpu.VMEM((EXPERT_BLOCK * ROW_SLAB, LANES), F32), pltpu.SemaphoreType.DMA(()),
                        pltpu.SemaphoreType.DMA(())],
        compiler_params=_cparams(("arbitrary",)),
        name="moe_dispatch",
    )(dest, hf_slabs)


def _row_gather(idx_ref, base, count, src_hbm, dst, sem):
    def body(j, carry):
        pltpu.make_async_copy(_slab_rows(src_hbm, idx_ref[base + j], 1), _slab_rows(dst, j, 1), sem).start()
        return carry
    lax.fori_loop(0, count, body, 0, unroll=8)


def _wait_rows(dst, sem):
    pltpu.make_async_copy(dst, dst, sem).wait()


def _gathered_rows(buf, first, rows, stride):
    return jnp.concatenate(
        [buf[pl.ds(first * ROW_SLAB + s, rows, stride=stride * ROW_SLAB), :] for s in range(ROW_SLAB)], axis=1)


def _swiglu(x, wg, wu, wd):
    gate = _dot(x, wg)
    up = _dot(x, wu)
    return _dot((gate * jax.nn.sigmoid(gate) * up).astype(BF16), wd)


def _valid_row_copies(ybuf, ys_hbm, row0, valid, sem):
    out = [(valid == EXPERT_BLOCK,
            pltpu.make_async_copy(_slab_rows(ybuf, 0, EXPERT_BLOCK), _slab_rows(ys_hbm, row0, EXPERT_BLOCK), sem))]
    part = valid < EXPERT_BLOCK
    size = EXPERT_BLOCK // 2
    while size >= 1:
        off = valid & ~(2 * size - 1)
        out.append((part & ((valid & size) != 0),
                    pltpu.make_async_copy(_slab_rows(ybuf, off, size), _slab_rows(ys_hbm, row0 + off, size), sem)))
        size //= 2
    return out


def _expert_kernel(be_ref, r0_ref, nv_ref, xs_hbm, wg_ref, wu_ref, wd_ref, ys_hbm,
                   xbuf, ybuf, wg_bf, wu_bf, wd_bf, sem_in, sem_out):
    b = pl.program_id(0)
    n_blocks = pl.num_programs(0)
    slot = b % 2

    def fetch(blk, sl):
        return pltpu.make_async_copy(_slab_rows(xs_hbm, r0_ref[blk], EXPERT_BLOCK), xbuf.at[sl], sem_in.at[sl])

    def drain(blk, sl):
        for cond, cp in _valid_row_copies(ybuf.at[sl], ys_hbm, r0_ref[blk], nv_ref[blk], sem_out.at[sl]):
            pl.when(cond)(cp.wait)

    @pl.when(b == 0)
    def _():
        fetch(0, 0).start()

    nxt = jnp.minimum(b + 1, n_blocks - 1)

    @pl.when((b + 1 < n_blocks) & (nv_ref[nxt] > 0))
    def _():
        fetch(nxt, 1 - slot).start()

    @pl.when(b >= 2)
    def _():
        drain(jnp.maximum(b - 2, 0), slot)

    @pl.when((b == 0) | (be_ref[b] != be_ref[jnp.maximum(b - 1, 0)]))
    def _():
        wg_bf[...] = wg_ref[...].astype(BF16)
        wu_bf[...] = wu_ref[...].astype(BF16)
        wd_bf[...] = wd_ref[...].astype(BF16)

    @pl.when(nv_ref[b] > 0)
    def _():
        fetch(b, slot).wait()
        x = _gathered_rows(xbuf.at[slot], 0, EXPERT_BLOCK, 1).astype(BF16)
        y = _swiglu(x, wg_bf[...], wu_bf[...], wd_bf[...])
        _store_row_slabs(ybuf.at[slot], y)
        for cond, cp in _valid_row_copies(ybuf.at[slot], ys_hbm, r0_ref[b], nv_ref[b], sem_out.at[slot]):
            pl.when(cond)(cp.start)

    @pl.when(b == n_blocks - 1)
    def _():
        drain(jnp.maximum(b - 1, 0), 1 - slot)
        drain(b, slot)


def moe_experts(xs_slabs, block_e, block_row0, block_valid, w_gate, w_up, w_down):
    n_blocks = block_e.shape[0]
    d, ff = w_gate.shape[1:]
    blk_rows = EXPERT_BLOCK * ROW_SLAB
    n_assign = xs_slabs.shape[0] // ROW_SLAB - EXPERT_BLOCK
    wmap = lambda i, be, r0, nv: (be[i], 0, 0)
    grid_spec = pltpu.PrefetchScalarGridSpec(
        num_scalar_prefetch=3,
        grid=(n_blocks,),
        in_specs=[pl.BlockSpec(memory_space=pl.ANY),
                  pl.BlockSpec((None, d, ff), wmap), pl.BlockSpec((None, d, ff), wmap),
                  pl.BlockSpec((None, ff, d), wmap)],
        out_specs=pl.BlockSpec(memory_space=pl.ANY),
        scratch_shapes=[pltpu.VMEM((2, blk_rows, LANES), F32), pltpu.VMEM((2, blk_rows, LANES), F32),
                        pltpu.VMEM((d, ff), BF16), pltpu.VMEM((d, ff), BF16), pltpu.VMEM((ff, d), BF16),
                        pltpu.SemaphoreType.DMA((2,)), pltpu.SemaphoreType.DMA((2,))],
    )
    return pl.pallas_call(
        _expert_kernel,
        grid_spec=grid_spec,
        out_shape=jax.ShapeDtypeStruct((n_assign * ROW_SLAB, LANES), F32),
        compiler_params=_cparams(("arbitrary",)),
        name="moe_experts",
    )(block_e, block_row0, block_valid, xs_slabs, w_gate, w_up, w_down)


def _combine_kernel(cur_ref, nxt_ref, ys_hbm, xc_ref, xl_ref, hf_ref, gt_ref, g2_ref, sg_ref, su_ref, sd_ref,
                    lg_ref, lb_ref, *rest, n_ctx_tiles):
    *o_refs, ybuf, sem = rest
    i = pl.program_id(0)
    n_tiles = pl.num_programs(0)
    tm = xl_ref.shape[0]
    n_rows = tm * TOP_K
    slot = i % 2

    @pl.when(i == 0)
    def _():
        _row_gather(cur_ref, 0, n_rows, ys_hbm, ybuf.at[0], sem.at[0])

    @pl.when(i + 1 < n_tiles)
    def _():
        _row_gather(nxt_ref, 0, n_rows, ys_hbm, ybuf.at[1 - slot], sem.at[1 - slot])

    hf = _gathered_rows(hf_ref, 0, tm, 1).astype(BF16)
    y = _swiglu(hf, sg_ref[...], su_ref[...], sd_ref[...])

    _wait_rows(ybuf.at[slot], sem.at[slot])
    for k in range(TOP_K):
        y = y + gt_ref[:, k:k + 1] * _gathered_rows(ybuf.at[slot], k, tm, TOP_K)
    r = DEEPNORM_ALPHA * _stream_tile(xc_ref, xl_ref, n_ctx_tiles) + g2_ref[...] * y
    res = _layer_norm_rows(r, lg_ref[...], lb_ref[...])
    if n_ctx_tiles == 0:
        o_refs[0][...] = res
    else:
        oc_ref, ol_ref = o_refs

        @pl.when(i < n_ctx_tiles)
        def _():
            oc_ref[...] = res

        @pl.when(i >= n_ctx_tiles)
        def _():
            ol_ref[...] = res


def moe_combine(dest, ys_slabs, x_ctx, x_lat, hf_slabs, gate, g2, n_ctx_tiles, sh_gate, sh_up, sh_down, ln_g, ln_b):
    d = x_lat.shape[1]
    tm = MOE_TILE
    n_tiles = n_ctx_tiles + x_lat.shape[0] // tm
    n_rows = tm * TOP_K
    ff = sh_gate.shape[1]
    row = lambda i: (i, 0)
    fixed = lambda i: (0, 0)
    seg = lambda i: (jnp.where(i < n_ctx_tiles, 1, 0), 0, 0)
    ctx_spec, lat_spec = _stream_specs(n_ctx_tiles, tm, d)
    lat_out = jax.ShapeDtypeStruct(x_lat.shape, F32)
    if n_ctx_tiles == 0:
        out_specs, out_shape = [lat_spec], [lat_out]
    else:
        out_specs, out_shape = [ctx_spec, lat_spec], [jax.ShapeDtypeStruct(x_ctx.shape, F32), lat_out]
    kern = functools.partial(_combine_kernel, n_ctx_tiles=n_ctx_tiles)
    return pl.pallas_call(
        kern,
        grid=(n_tiles,),
        in_specs=[pl.BlockSpec((n_rows,), lambda i: (i,), memory_space=pltpu.SMEM),
                  pl.BlockSpec((n_rows,), lambda i: (jnp.minimum(i + 1, n_tiles - 1),), memory_space=pltpu.SMEM),
                  pl.BlockSpec(memory_space=pl.ANY),
                  ctx_spec, lat_spec, pl.BlockSpec((tm * ROW_SLAB, LANES), row), pl.BlockSpec((tm, LANES), row),
                  pl.BlockSpec((None, 1, d), seg),
                  pl.BlockSpec((d, ff), fixed), pl.BlockSpec((d, ff), fixed), pl.BlockSpec((ff, d), fixed),
                  pl.BlockSpec((1, d), fixed), pl.BlockSpec((1, d), fixed)],
        out_specs=out_specs,
        out_shape=out_shape,
        scratch_shapes=[pltpu.VMEM((2, n_rows * ROW_SLAB, LANES), F32), pltpu.SemaphoreType.DMA((2,))],
        compiler_params=_cparams(("arbitrary",)),
        name="moe_combine",
    )(dest, dest, ys_slabs, x_ctx, x_lat, hf_slabs, gate, g2,
      sh_gate.astype(BF16), sh_up.astype(BF16), sh_down.astype(BF16), ln_g.reshape(1, d), ln_b.reshape(1, d))


def moe_layer(x_ctx, x_lat, sc2, sh2, g2, n_ctx_tiles, router_w, router_bias, w_gate, w_up, w_down,
              sh_gate, sh_up, sh_down, ln_g, ln_b):
    n = n_ctx_tiles * MOE_TILE + x_lat.shape[0]
    hf, te, gt, rk, cnt = moe_route(x_ctx, x_lat, sc2, sh2, n_ctx_tiles, router_w, router_bias)
    counts = cnt[0].astype(I32)
    start = jnp.cumsum(counts) - counts
    experts = jnp.arange(N_EXPERTS, dtype=I32)
    pick = te[:, :TOP_K, None] == experts
    dest = (jnp.sum(jnp.where(pick, start, 0), axis=-1) + rk[:, :TOP_K]).reshape(-1)
    nb = (counts + EXPERT_BLOCK - 1) // EXPERT_BLOCK
    blk_end = jnp.cumsum(nb)
    blk_start = blk_end - nb
    n_blocks = n * TOP_K // EXPERT_BLOCK + N_EXPERTS
    blk = jnp.arange(n_blocks, dtype=I32)
    bb = jnp.minimum(blk, blk_end[-1] - 1)[:, None]
    own = (blk_start[None, :] <= bb) & (bb < blk_end[None, :])
    sel = lambda v: jnp.sum(jnp.where(own, v[None, :], 0), axis=1)
    j = bb[:, 0] - sel(blk_start)
    block_e = sel(experts)
    block_row0 = sel(start) + j * EXPERT_BLOCK
    block_valid = jnp.where(blk < blk_end[-1], jnp.clip(sel(counts) - j * EXPERT_BLOCK, 0, EXPERT_BLOCK), 0)
    xs = moe_dispatch(dest, hf)
    ys = moe_experts(xs, block_e, block_row0, block_valid, w_gate, w_up, w_down)
    return moe_combine(dest, ys, x_ctx, x_lat, hf, gt, g2, n_ctx_tiles, sh_gate, sh_up, sh_down, ln_g, ln_b)


def kernel(x, c, ctx, c_ctx, w_mod, b_mod, ln_mix_g, ln_mix_b, ln_ffn_g, ln_ffn_b, ab_w_in, ab_w_out, na_rpb,
           hy_conv_w, hy_conv_b, hy_f_w1, hy_f_b1, hy_f_freq, hy_f_w2, hy_f_b2, hy_f_w3, hy_skip, cd_w_in, cd_w_out,
           q_norm_g, k_norm_g, s5_a_re, s5_a_im, s5_log_dt, s5_b_re, s5_b_im, s5_c_re, s5_c_im, s5_d, s5_glu_w,
           s5_glu_b, router_w, router_bias, exp_w_gate, exp_w_up, exp_w_down, sh_w_gate, sh_w_up, sh_w_down):
    b, l, d = x.shape
    assert b == 1
    n_ctx = ctx.shape[1]
    assert n_ctx == MOE_TILE
    xs = x[0]
    cs = ctx[0]
    cmat = jnp.zeros((SUBLANES, d), F32).at[0].set(c[0]).at[1].set(c_ctx)
    mods = modulation_all(cmat, w_mod, b_mod).reshape(DEPTH, SUBLANES, 6, d)
    qscale = HEAD_DIM ** -0.5

    for i in range(DEPTH):
        need_ctx = i < DEPTH - 1
        m = mods[i]
        sh1, sc1, g1, sh2, sc2, g2 = [m[0:1, t] for t in range(6)]
        csh1, csc1, cg1, csh2, csc2, cg2 = [m[1:2, t] for t in range(6)]
        j = i // 2
        if i % 2 == 0:
            filt = (hy_conv_w[j], hy_conv_b[j], hy_f_w1[j], hy_f_b1[j], hy_f_freq[j], hy_f_w2[j], hy_f_b2[j],
                    hy_f_w3[j], hy_skip[j])
            splits = (NA_WIDTH, NA_WIDTH, NA_WIDTH, 3 * HY_WIDTH)
            dts = (BF16, BF16, BF16, F32)
            scl = (qscale, 1.0, 1.0, 1.0)
            q_l, k_l, v_l, u_l = mod_project(xs, sc1, sh1, ab_w_in[j], splits, dts, scl)
            q_c, k_c, v_c, u_c = mod_project(cs, csc1, csh1, ab_w_in[j], splits, dts, scl)
            a_lat = neighbourhood_attention(q_l, k_l, v_l, k_c, v_c, na_rpb[j])
            y_hy = hyena_long(u_l, *filt)
            xs_new = outproj_ln(a_lat, y_hy, ab_w_out[j], xs, g1, ln_mix_g[i], ln_mix_b[i])
            if need_ctx:
                a_ctx = context_attention(q_c, k_c, v_c)
                yc_hy = hyena_long(u_c, *filt)
                cs = outproj_ln(a_ctx, yc_hy, ab_w_out[j], cs, cg1, ln_mix_g[i], ln_mix_b[i])
            xs = xs_new
        else:
            splits = (GQA_WIDTH, GQA_KV_WIDTH, GQA_KV_WIDTH, S5_WIDTH)
            q_l, k_l, v_l, u_l = mod_project(xs, sc1, sh1, cd_w_in[j], splits, (F32, F32, BF16, F32))
            k_c, v_c, u_c = mod_project(cs, csc1, csh1, cd_w_in[j][:, GQA_WIDTH:], splits[1:], (F32, BF16, F32))
            tabs = _rope_tables(l)
            qn_t = qk_prep(q_l, q_norm_g[j], tabs, qscale * math.log2(math.e), transposed=True)
            kn = qk_prep(k_l, k_norm_g[j], tabs, 1.0)
            kcn = qk_prep(k_c, k_norm_g[j], None, 1.0)
            k_all = jnp.concatenate([kn, kcn], axis=0)
            v_all = jnp.concatenate([v_l, v_c], axis=0)
            k_hm = k_all.reshape(-1, GQA_KV_HEADS, HEAD_DIM).transpose(1, 0, 2)
            v_t = v_all.T.reshape(GQA_KV_HEADS, HEAD_DIM, -1)
            att = gqa_attention(qn_t, k_hm, v_t)
            y_f, y_b = s5_scan_outputs(u_c, u_l, s5_a_re[j], s5_a_im[j], s5_log_dt[j], s5_b_re[j], s5_b_im[j],
                                       s5_c_re[j], s5_c_im[j])
            ssm = s5_readout(y_f, y_b, u_l, s5_d[j], s5_glu_w[j], s5_glu_b[j])
            xs = outproj_ln(att, ssm, cd_w_out[j], xs, g1, ln_mix_g[i], ln_mix_b[i])
            assert not need_ctx

        moe_w = (router_w[i], router_bias[i], exp_w_gate[i], exp_w_up[i], exp_w_down[i],
                 sh_w_gate[i], sh_w_up[i], sh_w_down[i], ln_ffn_g[i], ln_ffn_b[i])
        stack2 = lambda lat, cx: jnp.stack([lat, cx])
        mod2 = (stack2(sc2, csc2), stack2(sh2, csh2), stack2(g2, cg2))
        if need_ctx:
            cs, xs = moe_layer(cs, xs, *mod2, n_ctx // MOE_TILE, *moe_w)
        else:
            (xs,) = moe_layer(xs, xs, *mod2, 0, *moe_w)
    return xs.reshape(b, l, d)
```

```python
import functools
import math

import jax
import jax.numpy as jnp
import numpy as np
from jax import lax
from jax.experimental import pallas as pl
from jax.experimental.pallas import tpu as pltpu

F32 = jnp.float32
BF16 = jnp.bfloat16
I32 = jnp.int32
HIGHEST = lax.Precision.HIGHEST

LANES = 128
SUBLANES = 8
VMEM_LIMIT = 56 * 1024 * 1024

D_MODEL = 1024
DEPTH = 2
GRID_W = 64
HEAD_DIM = 64
NA_HEADS = 8
NA_WIDTH = NA_HEADS * HEAD_DIM
NA_KH = 8
NA_KW = 16
HY_WIDTH = D_MODEL - NA_WIDTH
HY_BANDS = 16
HY_DECAY_PCT_MIN = 0.3
HY_DECAY_PCT_MAX = 1.5
HY_DECAY_TARGET = 1e-2
GQA_HEADS = 8
GQA_KV_HEADS = 2
GQA_WIDTH = GQA_HEADS * HEAD_DIM
GQA_KV_WIDTH = GQA_KV_HEADS * HEAD_DIM
ROPE_THETA = 10000.0
S5_WIDTH = D_MODEL - GQA_WIDTH
S5_GROUP = 16
S5_GROUPS = S5_WIDTH // S5_GROUP
S5_STATE = 64
N_EXPERTS = 256
TOP_K = 8
N_EXPERT_GROUPS = 8
TOPK_GROUPS = 4
EXPERT_FF = 256
ROUTED_SCALE = 2.5
EXPERT_BLOCK = 128
DEEPNORM_ALPHA = (2.0 * DEPTH) ** 0.25
LN_EPS = 1e-5
RMS_EPS = 1e-6

NEG_BIG = -1e30
NA_TILE_ROWS = 8
NA_KEY_ROWS = 16
NA_KEY_BLOCK_ROWS = 4
DFT_N1 = 128
S5_CHUNK = 16
S5_LANE_GROUPS = LANES // S5_GROUP
MOE_TILE = 256
ROW_SLAB = D_MODEL // LANES

NT_DIMS = (((1,), (1,)), ((), ()))


def _cparams(sem, **kw):
    return pltpu.CompilerParams(dimension_semantics=sem, vmem_limit_bytes=VMEM_LIMIT, **kw)


def _dot(a, b, **kw):
    return jnp.dot(a, b, preferred_element_type=F32, **kw)


def _dot_nt(a, b):
    return lax.dot_general(a, b, NT_DIMS, preferred_element_type=F32)


def _row_tile(m, pref):
    return pref if m % pref == 0 else m


def _mod_kernel(c_ref, w_ref, b_ref, o_ref):
    cv = c_ref[...]
    s = cv * jax.nn.sigmoid(cv)
    o_ref[...] = _dot(s, w_ref[...], precision=HIGHEST) + b_ref[...]


def modulation_all(cmat, w_mod, b_mod):
    depth, d, n = w_mod.shape
    tn = 1536
    return pl.pallas_call(
        _mod_kernel,
        grid=(depth, n // tn),
        in_specs=[pl.BlockSpec((SUBLANES, d), lambda l, j: (0, 0)),
                  pl.BlockSpec((None, d, tn), lambda l, j: (l, 0, j)),
                  pl.BlockSpec((None, 1, tn), lambda l, j: (l, 0, j))],
        out_specs=pl.BlockSpec((None, SUBLANES, tn), lambda l, j: (l, 0, j)),
        out_shape=jax.ShapeDtypeStruct((depth, SUBLANES, n), F32),
        compiler_params=_cparams(("arbitrary", "arbitrary")),
        name="modulation",
    )(cmat, w_mod, b_mod.reshape(depth, 1, n))


def _proj_kernel(x_ref, sc_ref, sh_ref, w_ref, *o_refs, splits, scales):
    h = (x_ref[...] * (1.0 + sc_ref[...]) + sh_ref[...]).astype(BF16)
    off = 0
    for o_ref, wd, sc in zip(o_refs, splits, scales):
        y = _dot(h, w_ref[:, off:off + wd])
        if sc != 1.0:
            y = y * sc
        o_ref[...] = y.astype(o_ref.dtype)
        off += wd


def mod_project(x, sc, sh, w, splits, dtypes, scales=None):
    m, d = x.shape
    n = w.shape[1]
    assert sum(splits) == n
    scales = scales or (1.0,) * len(splits)
    tm = _row_tile(m, 512)
    kern = functools.partial(_proj_kernel, splits=tuple(splits), scales=tuple(scales))
    return pl.pallas_call(
        kern,
        grid=(m // tm,),
        in_specs=[pl.BlockSpec((tm, d), lambda i: (i, 0)),
                  pl.BlockSpec((1, d), lambda i: (0, 0)),
                  pl.BlockSpec((1, d), lambda i: (0, 0)),
                  pl.BlockSpec((d, n), lambda i: (0, 0))],
        out_specs=[pl.BlockSpec((tm, wd), lambda i: (i, 0)) for wd in splits],
        out_shape=[jax.ShapeDtypeStruct((m, wd), dt) for wd, dt in zip(splits, dtypes)],
        compiler_params=_cparams(("parallel",)),
        name="mod_project",
    )(x, sc, sh, w.astype(BF16))


def _layer_norm_rows(r, g, b):
    mu = jnp.mean(r, axis=-1, keepdims=True)
    c = r - mu
    var = jnp.mean(c * c, axis=-1, keepdims=True)
    return c * lax.rsqrt(var + LN_EPS) * g + b


def _outproj_ln_kernel(a_ref, b_ref, w_ref, x_ref, gate_ref, g_ref, beta_ref, o_ref):
    ka = a_ref.shape[1]
    y = _dot(a_ref[...], w_ref[:ka, :]) + _dot(b_ref[...], w_ref[ka:, :])
    r = DEEPNORM_ALPHA * x_ref[...] + gate_ref[...] * y
    o_ref[...] = _layer_norm_rows(r, g_ref[...], beta_ref[...])


def outproj_ln(a, b, w, x, gate, g, beta):
    m, d = x.shape
    ka, kb = a.shape[1], b.shape[1]
    tm = _row_tile(m, 512)
    row = lambda i: (i, 0)
    fixed = lambda i: (0, 0)
    return pl.pallas_call(
        _outproj_ln_kernel,
        grid=(m // tm,),
        in_specs=[pl.BlockSpec((tm, ka), row), pl.BlockSpec((tm, kb), row),
                  pl.BlockSpec((ka + kb, d), fixed), pl.BlockSpec((tm, d), row),
                  pl.BlockSpec((1, d), fixed), pl.BlockSpec((1, d), fixed), pl.BlockSpec((1, d), fixed)],
        out_specs=pl.BlockSpec((tm, d), row),
        out_shape=jax.ShapeDtypeStruct((m, d), F32),
        compiler_params=_cparams(("parallel",)),
        name="outproj_ln",
    )(a, b, w.astype(BF16), x, gate, g.reshape(1, d), beta.reshape(1, d))


def _na_bias_table(rpb, rows):
    h = rpb.shape[0]
    ri = np.arange(NA_TILE_ROWS)
    kr = np.arange(NA_KEY_ROWS)
    c = np.arange(GRID_W)
    cs = np.clip(c - NA_KW // 2, 0, GRID_W - NA_KW)
    vc = (c[None, :] >= cs[:, None]) & (c[None, :] < cs[:, None] + NA_KW)
    dc = np.clip(c[None, :] - c[:, None] + NA_KW - 1, 0, 2 * NA_KW - 2)
    pick = (dc.reshape(-1)[:, None] == np.arange(2 * NA_KW - 1)[None, :]).astype(np.float32)
    colb = jnp.einsum('qb,hab->haq', pick, rpb, precision=HIGHEST).reshape(h, 2 * NA_KH - 1, GRID_W, GRID_W)
    colb = jnp.where(vc[None, None], colb, NEG_BIG)

    def case(t):
        start = min(max(NA_TILE_ROWS * t - NA_KH // 2, 0), rows - NA_KEY_ROWS)
        r = NA_TILE_ROWS * t + ri
        rs = np.clip(r - NA_KH // 2, 0, rows - NA_KH)
        krow = start + kr
        vr = (krow[None, :] >= rs[:, None]) & (krow[None, :] < rs[:, None] + NA_KH)
        dr = np.clip(krow[None, :] - r[:, None] + NA_KH - 1, 0, 2 * NA_KH - 2)
        b = jnp.take(colb, dr.reshape(-1), axis=1).reshape(h, NA_TILE_ROWS, NA_KEY_ROWS, GRID_W, GRID_W)
        b = jnp.where(vr[None, :, :, None, None], b, NEG_BIG)
        b = b.transpose(0, 1, 3, 2, 4)
        return b.reshape(h // 2, 2, NA_TILE_ROWS * GRID_W, NA_KEY_ROWS * GRID_W)

    n_tiles = rows // NA_TILE_ROWS
    return jnp.stack([case(0), case(1), case(n_tiles - 1)])


def _pair_masks(shape):
    lane = lax.broadcasted_iota(I32, shape, 1)
    return lane < HEAD_DIM


def _na_kernel(q_ref, k0, k1, k2, k3, v0, v1, v2, v3, kc_ref, vc_ref, bias_ref, o_ref):
    q = q_ref[...]
    lo = _pair_masks(q.shape)
    ks = (k0, k1, k2, k3)
    vs = (v0, v1, v2, v3)
    kb = k0.shape[0]
    outs = []
    for hh in range(2):
        qh = jnp.where(lo if hh == 0 else jnp.logical_not(lo), q, jnp.zeros_like(q))
        s = [_dot_nt(qh, ks[i][...]) + bias_ref[hh, :, i * kb:(i + 1) * kb] for i in range(4)]
        s.append(_dot_nt(qh, kc_ref[...]))
        m = s[0].max(axis=1, keepdims=True)
        for si in s[1:]:
            m = jnp.maximum(m, si.max(axis=1, keepdims=True))
        p = [jnp.exp(si - m) for si in s]
        l = p[0].sum(axis=1, keepdims=True)
        for pi in p[1:]:
            l = l + pi.sum(axis=1, keepdims=True)
        acc = _dot(p[4].astype(BF16), vc_ref[...])
        for i in range(4):
            acc = acc + _dot(p[i].astype(BF16), vs[i][...])
        outs.append(acc / l)
    o_ref[...] = jnp.where(lo, outs[0], outs[1]).astype(o_ref.dtype)


def neighbourhood_attention(q, k, v, k_ctx, v_ctx, rpb):
    l, w = q.shape
    rows = l // GRID_W
    n_tiles = rows // NA_TILE_ROWS
    assert n_tiles >= 3 and rows % NA_TILE_ROWS == 0
    n_ctx = k_ctx.shape[0]
    tq = NA_TILE_ROWS * GRID_W
    kb = NA_KEY_BLOCK_ROWS * GRID_W
    n_kblk = rows // NA_KEY_BLOCK_ROWS
    bias = _na_bias_table(rpb, rows)
    pair_w = 2 * HEAD_DIM

    def kv_spec(i):
        def imap(p, t):
            start = jnp.clip(2 * t - 1, 0, n_kblk - 4)
            return (start + i, p)
        return pl.BlockSpec((kb, pair_w), imap)

    def bias_map(p, t):
        case = jnp.where(t == 0, 0, jnp.where(t == n_tiles - 1, 2, 1))
        return (case, p, 0, 0, 0)

    return pl.pallas_call(
        _na_kernel,
        grid=(w // pair_w, n_tiles),
        in_specs=[pl.BlockSpec((tq, pair_w), lambda p, t: (t, p))]
                 + [kv_spec(i) for i in range(4)] + [kv_spec(i) for i in range(4)]
                 + [pl.BlockSpec((n_ctx, pair_w), lambda p, t: (0, p)),
                    pl.BlockSpec((n_ctx, pair_w), lambda p, t: (0, p)),
                    pl.BlockSpec((None, None, 2, tq, NA_KEY_ROWS * GRID_W), bias_map)],
        out_specs=pl.BlockSpec((tq, pair_w), lambda p, t: (t, p)),
        out_shape=jax.ShapeDtypeStruct((l, w), BF16),
        compiler_params=_cparams(("parallel", "parallel")),
        name="neighbourhood_attention",
    )(q, k, k, k, k, v, v, v, v, k_ctx, v_ctx, bias)


def _ctx_attn_kernel(q_ref, k_ref, v_ref, o_ref):
    q = q_ref[...]
    lo = _pair_masks(q.shape)
    outs = []
    for hh in range(2):
        qh = jnp.where(lo if hh == 0 else jnp.logical_not(lo), q, jnp.zeros_like(q))
        s = _dot_nt(qh, k_ref[...])
        p = jnp.exp(s - s.max(axis=1, keepdims=True))
        outs.append(_dot(p.astype(BF16), v_ref[...]) / p.sum(axis=1, keepdims=True))
    o_ref[...] = jnp.where(lo, outs[0], outs[1]).astype(o_ref.dtype)


def context_attention(q, k, v):
    n, w = q.shape
    pair_w = 2 * HEAD_DIM
    spec = pl.BlockSpec((n, pair_w), lambda p: (0, p))
    return pl.pallas_call(
        _ctx_attn_kernel, grid=(w // pair_w,), in_specs=[spec, spec, spec], out_specs=spec,
        out_shape=jax.ShapeDtypeStruct((n, w), BF16),
        compiler_params=_cparams(("parallel",)), name="context_attention",
    )(q, k, v)


def _shortconv_kernel(u_ref, up_ref, un_ref, w_ref, b_ref, x0_ref, z_ref, *, n_tiles):
    i = pl.program_id(0)
    u = u_ref[...]
    tm = u.shape[0]
    prev_row = jnp.where(i > 0, up_ref[SUBLANES - 1:SUBLANES, :], 0.0)
    next_row = jnp.where(i < n_tiles - 1, un_ref[0:1, :], 0.0)
    row = lax.broadcasted_iota(I32, u.shape, 0)
    u_dn = jnp.where(row == 0, prev_row, pltpu.roll(u, 1, 0))
    u_up = jnp.where(row == tm - 1, next_row, pltpu.roll(u, tm - 1, 0))
    y = u_dn * w_ref[0:1, :] + u * w_ref[1:2, :] + u_up * w_ref[2:3, :] + b_ref[...]
    c = HY_WIDTH
    x0_ref[...] = y[:, :c]
    z_ref[...] = y[:, c:2 * c] * y[:, 2 * c:]


def hyena_gate(u, conv_w, conv_b):
    l, w3 = u.shape
    tm = _row_tile(l, 512)
    n_tiles = l // tm
    per = tm // SUBLANES
    last = l // SUBLANES - 1
    kern = functools.partial(_shortconv_kernel, n_tiles=n_tiles)
    return pl.pallas_call(
        kern,
        grid=(n_tiles,),
        in_specs=[pl.BlockSpec((tm, w3), lambda i: (i, 0)),
                  pl.BlockSpec((SUBLANES, w3), lambda i: (jnp.maximum(i * per - 1, 0), 0)),
                  pl.BlockSpec((SUBLANES, w3), lambda i: (jnp.minimum((i + 1) * per, last), 0)),
                  pl.BlockSpec((3, w3), lambda i: (0, 0)),
                  pl.BlockSpec((1, w3), lambda i: (0, 0))],
        out_specs=[pl.BlockSpec((tm, HY_WIDTH), lambda i: (i, 0))] * 2,
        out_shape=[jax.ShapeDtypeStruct((l, HY_WIDTH), F32)] * 2,
        compiler_params=_cparams(("parallel",)),
        name="hyena_gate",
    )(u, u, u, conv_w, conv_b.reshape(1, w3))


def _filter_kernel(bands_ref, w1t_ref, w1c_ref, w1s_ref, b1_ref, fr_ref, w2_ref, b2_ref, w3_ref, dl_ref,
                   taps_ref, asum_ref, *, l, tp):
    i = pl.program_id(0)
    hid_w = w2_ref.shape[0]
    c = HY_WIDTH
    denom = float(max(l - 1, 1))

    def pos(width):
        return (lax.broadcasted_iota(I32, (tp, width), 0) + i * tp).astype(F32)

    ang = (2.0 * math.pi / l) * pos(HY_BANDS) * bands_ref[...]
    pre = ((pos(hid_w) / denom) * w1t_ref[...]
           + _dot(jnp.cos(ang), w1c_ref[...], precision=HIGHEST)
           + _dot(-jnp.sin(ang), w1s_ref[...], precision=HIGHEST) + b1_ref[...])
    hid = jnp.sin(fr_ref[...] * pre)
    hid = jnp.sin(fr_ref[...] * (_dot(hid, w2_ref[...], precision=HIGHEST) + b2_ref[...]))
    taps = _dot(hid, w3_ref[...], precision=HIGHEST)
    pc = pos(c)
    window = jnp.exp(-(pc / denom) * dl_ref[...])
    fwd = taps[:, :c] * window
    bwd = jnp.where(pc == 0.0, 0.0, taps[:, c:] * window)
    taps_ref[:, :c] = fwd
    taps_ref[:, c:] = bwd

    @pl.when(i == 0)
    def _():
        asum_ref[...] = jnp.zeros_like(asum_ref)

    asum_ref[...] += jnp.sum(jnp.abs(fwd) + jnp.abs(bwd), axis=0, keepdims=True)


def hyena_filter_taps(l, f_w1, f_b1, f_freq, f_w2, f_b2, f_w3):
    c = HY_WIDTH
    hid = f_w2.shape[0]
    tp = _row_tile(l, 1024)
    bands = jnp.linspace(1e-4, HY_BANDS - 1, HY_BANDS, dtype=F32).reshape(1, HY_BANDS)
    deltas = jnp.abs(jnp.linspace(math.log(HY_DECAY_TARGET) / HY_DECAY_PCT_MAX,
                                  math.log(HY_DECAY_TARGET) / HY_DECAY_PCT_MIN, c, dtype=F32)).reshape(1, c)
    fixed = lambda i: (0, 0)
    full = lambda a: pl.BlockSpec(a.shape, fixed)
    args = (bands, f_w1[0:1], f_w1[1:1 + HY_BANDS], f_w1[1 + HY_BANDS:], f_b1.reshape(1, hid),
            f_freq.reshape(1, hid), f_w2, f_b2.reshape(1, hid), f_w3, deltas)
    kern = functools.partial(_filter_kernel, l=l, tp=tp)
    return pl.pallas_call(
        kern,
        grid=(l // tp,),
        in_specs=[full(a) for a in args],
        out_specs=[pl.BlockSpec((tp, 2 * c), lambda i: (i, 0)), pl.BlockSpec((1, c), fixed)],
        out_shape=[jax.ShapeDtypeStruct((l, 2 * c), F32), jax.ShapeDtypeStruct((1, c), F32)],
        compiler_params=_cparams(("arbitrary",)),
        name="hyena_filter",
    )(*args)


def _dft_tables(l):
    n = 2 * l
    n1 = DFT_N1
    n2 = n // n1
    k1 = jnp.arange(n1)[:, None]
    m1 = jnp.arange(n1 // 2)[None, :]
    ph1 = (2.0 * math.pi / n1) * ((k1 * m1) % n1).astype(F32)
    d1 = jnp.stack([jnp.cos(ph1), -jnp.sin(ph1)], axis=1).reshape(2 * n1, n1 // 2)
    d1_inv = d1.T
    j2 = jnp.arange(n2)
    ph2 = (2.0 * math.pi / n2) * ((j2[:, None] * j2[None, :]) % n2).astype(F32)
    cs, sn = jnp.cos(ph2), jnp.sin(ph2)
    f2 = jnp.concatenate([jnp.concatenate([cs, sn], axis=1), jnp.concatenate([-sn, cs], axis=1)], axis=0)
    pht = (2.0 * math.pi / n) * ((jnp.arange(n1)[:, None] * j2[None, :]) % n).astype(F32)
    lanes = lambda x: jnp.broadcast_to(x[:, :, None], (n1, n2, LANES))
    return d1.astype(BF16), d1_inv.astype(BF16), f2.astype(BF16), f2.T.astype(BF16), lanes(jnp.cos(pht)), lanes(-jnp.sin(pht))


def _dft1_kernel(d_ref, x_ref, o_ref):
    o_ref[...] = _dot(d_ref[...], x_ref[...].astype(BF16)).astype(o_ref.dtype)


def dft_stage1(x, d1, n2):
    l, c = x.shape
    n1h = d1.shape[1]
    cols = n2 * c
    tc = min(cols, 4096)
    out = pl.pallas_call(
        _dft1_kernel,
        grid=(cols // tc,),
        in_specs=[pl.BlockSpec(d1.shape, lambda j: (0, 0)), pl.BlockSpec((n1h, tc), lambda j: (0, j))],
        out_specs=pl.BlockSpec((d1.shape[0], tc), lambda j: (0, j)),
        out_shape=jax.ShapeDtypeStruct((d1.shape[0], cols), BF16),
        compiler_params=_cparams(("parallel",)),
        name="dft_stage1",
    )(d1, x.reshape(n1h, cols))
    return out.reshape(d1.shape[0] // 2, 2, n2, c)


def _twiddled_stage2(f_ref, a_ref, twr_ref, twi_ref):
    c = a_ref.shape[2]
    reps = c // LANES
    twr = jnp.tile(twr_ref[...], (1, reps))
    twi = jnp.tile(twi_ref[...], (1, reps))
    ar, ai = a_ref[0].astype(F32), a_ref[1].astype(F32)
    a = jnp.concatenate([ar * twr - ai * twi, ar * twi + ai * twr], axis=0).astype(BF16)
    return _dot(f_ref[...], a), twr, twi


def _filter_spectrum_kernel(f_ref, a_ref, twr_ref, twi_ref, h_ref):
    n2 = a_ref.shape[1]
    c = h_ref.shape[2]
    x, _, _ = _twiddled_stage2(f_ref, a_ref, twr_ref, twi_ref)
    h_ref[0] = x[:n2, :c] + x[:n2, c:]
    h_ref[1] = x[n2:, :c] - x[n2:, c:]


def filter_spectrum(a_taps, f2, twr, twi):
    n1, _, n2, c2 = a_taps.shape
    c = c2 // 2
    tw = pl.BlockSpec((None, n2, LANES), lambda i: (i, 0, 0))
    return pl.pallas_call(
        _filter_spectrum_kernel,
        grid=(n1,),
        in_specs=[pl.BlockSpec((2 * n2, 2 * n2), lambda i: (0, 0)),
                  pl.BlockSpec((None, 2, n2, c2), lambda i: (i, 0, 0, 0)), tw, tw],
        out_specs=pl.BlockSpec((None, 2, n2, c), lambda i: (i, 0, 0, 0)),
        out_shape=jax.ShapeDtypeStruct((n1, 2, n2, c), F32),
        compiler_params=_cparams(("parallel",)),
        name="filter_spectrum",
    )(f2, a_taps, twr, twi)


def _spectral_mix_kernel(f_ref, fi_ref, a_ref, h_ref, twr_ref, twi_ref, o_ref):
    n2 = a_ref.shape[1]
    x, twr, twi = _twiddled_stage2(f_ref, a_ref, twr_ref, twi_ref)
    xr, xi = x[:n2], x[n2:]
    hr, hi = h_ref[0], h_ref[1]
    y = jnp.concatenate([xr * hr - xi * hi, xr * hi + xi * hr], axis=0).astype(BF16)
    b = _dot(fi_ref[...], y)
    br, bi = b[:n2], b[n2:]
    o_ref[0] = (br * twr + bi * twi).astype(o_ref.dtype)
    o_ref[1] = (bi * twr - br * twi).astype(o_ref.dtype)


def spectral_mix(a_z, h, f2, f2_inv, twr, twi):
    n1, _, n2, c = a_z.shape
    blk = pl.BlockSpec((None, 2, n2, c), lambda i: (i, 0, 0, 0))
    mat = pl.BlockSpec((2 * n2, 2 * n2), lambda i: (0, 0))
    tw = pl.BlockSpec((None, n2, LANES), lambda i: (i, 0, 0))
    return pl.pallas_call(
        _spectral_mix_kernel,
        grid=(n1,),
        in_specs=[mat, mat, blk, blk, tw, tw],
        out_specs=blk,
        out_shape=jax.ShapeDtypeStruct((n1, 2, n2, c), BF16),
        compiler_params=_cparams(("parallel",)),
        name="spectral_mix",
    )(f2, f2_inv, a_z, h, twr, twi)


def _hyena_out_kernel(di_ref, b_ref, x0_ref, z_ref, inv_ref, skip_ref, o_ref, *, inv_n):
    conv = _dot(di_ref[...], b_ref[...]) * inv_n
    o_ref[...] = (x0_ref[...] * (conv * inv_ref[...] + z_ref[...] * skip_ref[...])).astype(o_ref.dtype)


def hyena_output(b, d1_inv, x0, z, inv_norm, skip):
    n1, _, n2, c = b.shape
    l = x0.shape[0]
    cols = n2 * c
    tc = min(cols, 4096)
    n1h = n1 // 2
    rep = tc // c
    kern = functools.partial(_hyena_out_kernel, inv_n=1.0 / (2 * l))
    tile = pl.BlockSpec((n1h, tc), lambda j: (0, j))
    out = pl.pallas_call(
        kern,
        grid=(cols // tc,),
        in_specs=[pl.BlockSpec(d1_inv.shape, lambda j: (0, 0)),
                  pl.BlockSpec((2 * n1, tc), lambda j: (0, j)),
                  tile, tile,
                  pl.BlockSpec((1, tc), lambda j: (0, 0)), pl.BlockSpec((1, tc), lambda j: (0, 0))],
        out_specs=tile,
        out_shape=jax.ShapeDtypeStruct((n1h, cols), BF16),
        compiler_params=_cparams(("parallel",)),
        name="hyena_output",
    )(d1_inv, b.reshape(2 * n1, cols), x0.reshape(n1h, cols), z.reshape(n1h, cols),
      jnp.tile(inv_norm, (1, rep)), jnp.tile(skip.reshape(1, c), (1, rep)))
    return out.reshape(l, c)


def _small_conv_kernel(d_ref, di_ref, z_ref, taps_ref, x0_ref, inv_ref, skip_ref, o_ref, *, inv_n):
    c = z_ref.shape[1]
    n = d_ref.shape[0] // 2
    zs = _dot(d_ref[...], z_ref[...], precision=HIGHEST)
    ts = _dot(d_ref[...], taps_ref[...], precision=HIGHEST)
    hr = ts[:n, :c] + ts[:n, c:]
    hi = ts[n:, :c] - ts[n:, c:]
    zr, zi = zs[:n], zs[n:]
    y = jnp.concatenate([zr * hr - zi * hi, zr * hi + zi * hr], axis=0)
    conv = _dot(di_ref[...], y, precision=HIGHEST) * inv_n
    o_ref[...] = (x0_ref[...] * (conv * inv_ref[...] + z_ref[...] * skip_ref[...])).astype(o_ref.dtype)


def hyena_output_short(z, taps, x0, inv_norm, skip):
    l, c = z.shape
    n = 2 * l
    ph = (2.0 * math.pi / n) * ((jnp.arange(n)[:, None] * jnp.arange(l)[None, :]) % n).astype(F32)
    d = jnp.concatenate([jnp.cos(ph), -jnp.sin(ph)], axis=0)
    di = jnp.concatenate([jnp.cos(ph), -jnp.sin(ph)], axis=0).T
    args = (d, di, z, taps, x0, inv_norm, skip.reshape(1, c))
    kern = functools.partial(_small_conv_kernel, inv_n=1.0 / n)
    return pl.pallas_call(
        kern,
        grid=(1,),
        in_specs=[pl.BlockSpec(a.shape, lambda i: (0, 0)) for a in args],
        out_specs=pl.BlockSpec((l, c), lambda i: (0, 0)),
        out_shape=jax.ShapeDtypeStruct((l, c), BF16),
        compiler_params=_cparams(("arbitrary",)),
        name="hyena_output_short",
    )(*args)


def hyena_long(u, conv_w, conv_b, f_w1, f_b1, f_freq, f_w2, f_b2, f_w3, skip):
    l = u.shape[0]
    x0, z = hyena_gate(u, conv_w, conv_b)
    taps, asum = hyena_filter_taps(l, f_w1, f_b1, f_freq, f_w2, f_b2, f_w3)
    inv_norm = 1.0 / asum
    if 2 * l < DFT_N1 * SUBLANES * 2:
        return hyena_output_short(z, taps, x0, inv_norm, skip)
    n2 = 2 * l // DFT_N1
    d1, d1_inv, f2, f2_inv, twr, twi = _dft_tables(l)
    h = filter_spectrum(dft_stage1(taps, d1, n2), f2, twr, twi)
    b = spectral_mix(dft_stage1(z, d1, n2), h, f2, f2_inv, twr, twi)
    return hyena_output(b, d1_inv, x0, z, inv_norm, skip)


def _head_sumsq(x, bd):
    sq = x * x
    hi = sq.astype(BF16)
    lo = (sq - hi.astype(F32)).astype(BF16)
    return _dot(hi, bd) + _dot(lo, bd)


def _qk_prep_kernel(x_ref, gain_ref, bd_ref, *rest, rope, scale, transposed):
    x = x_ref[...]
    w = x.shape[1]
    ms = _head_sumsq(x, bd_ref[...]) * (1.0 / HEAD_DIM)
    xn = x * lax.rsqrt(ms + RMS_EPS) * gain_ref[...]
    if rope:
        cos_ref, sin_ref, o_ref = rest
        reps = w // cos_ref.shape[1]
        cos = jnp.tile(cos_ref[...], (1, reps)) if reps > 1 else cos_ref[...]
        sin = jnp.tile(sin_ref[...], (1, reps)) if reps > 1 else sin_ref[...]
        lane = lax.broadcasted_iota(I32, x.shape, 1)
        partner = jnp.where(lane % 2 == 0, pltpu.roll(xn, w - 1, 1), pltpu.roll(xn, 1, 1))
        xn = xn * cos + partner * sin
    else:
        (o_ref,) = rest
    if scale != 1.0:
        xn = xn * scale
    if transposed:
        xn = xn.T
    o_ref[...] = xn.astype(o_ref.dtype)


def _rope_tables(l):
    half = HEAD_DIM // 2
    inv_freq = ROPE_THETA ** (-jnp.arange(0, half, 2, dtype=F32) / half)
    t = jnp.arange(l)
    row = (t // GRID_W).astype(F32)
    col = (t % GRID_W).astype(F32)
    ang = jnp.concatenate([jnp.repeat(row[:, None] * inv_freq[None], 2, axis=1),
                           jnp.repeat(col[:, None] * inv_freq[None], 2, axis=1)], axis=1)
    sign = jnp.where(jnp.arange(HEAD_DIM) % 2 == 0, -1.0, 1.0).astype(F32)
    cos = jnp.tile(jnp.cos(ang), (1, 2))
    sin = jnp.tile(jnp.sin(ang) * sign[None], (1, 2))
    return cos, sin


def qk_prep(x, gain, rope_tabs, scale, transposed=False):
    l, w = x.shape
    tm = _row_tile(l, 512)
    head = jnp.arange(w) // HEAD_DIM
    bd = (head[:, None] == head[None, :]).astype(BF16)
    gain_t = jnp.tile(gain.reshape(1, HEAD_DIM), (1, w // HEAD_DIM))
    row = lambda i: (i, 0)
    fixed = lambda i: (0, 0)
    in_specs = [pl.BlockSpec((tm, w), row), pl.BlockSpec((1, w), fixed), pl.BlockSpec((w, w), fixed)]
    args = [x, gain_t, bd]
    if rope_tabs is not None:
        in_specs += [pl.BlockSpec((tm, 2 * HEAD_DIM), row)] * 2
        args += list(rope_tabs)
    kern = functools.partial(_qk_prep_kernel, rope=rope_tabs is not None, scale=scale, transposed=transposed)
    if transposed:
        out_spec, out_shape = pl.BlockSpec((w, tm), lambda i: (0, i)), (w, l)
    else:
        out_spec, out_shape = pl.BlockSpec((tm, w), row), (l, w)
    return pl.pallas_call(
        kern, grid=(l // tm,), in_specs=in_specs, out_specs=out_spec,
        out_shape=jax.ShapeDtypeStruct(out_shape, BF16),
        compiler_params=_cparams(("parallel",)), name="qk_prep",
    )(*args)


def _flash_kernel(qt_ref, k_ref, vt_ref, o_ref, qs_ref, s_ref, m_ref, l_ref, acc_ref, *, tk, nk):
    dh = HEAD_DIM
    rep = qt_ref.shape[0] // dh
    tq = qt_ref.shape[1]
    for r in range(rep):
        qs_ref[:, r * tq:(r + 1) * tq] = qt_ref[r * dh:(r + 1) * dh, :]
    m_ref[...] = jnp.full(m_ref.shape, NEG_BIG, F32)
    l_ref[...] = jnp.zeros(l_ref.shape, F32)
    acc_ref[...] = jnp.zeros(acc_ref.shape, F32)

    def scores(j):
        start = pl.multiple_of(jnp.minimum(j, nk - 1) * tk, tk)
        return _dot(k_ref[pl.ds(start, tk), :], qs_ref[...])

    def absorb(j, s):
        start = pl.multiple_of(j * tk, tk)
        m_old = m_ref[...]
        m_new = jnp.maximum(m_old, s.max(axis=0, keepdims=True))
        alpha = jnp.exp2(m_old - m_new)
        p = jnp.exp2(s - m_new)
        l_ref[...] = alpha * l_ref[...] + p.sum(axis=0, keepdims=True)
        acc_ref[...] = alpha * acc_ref[...] + _dot(vt_ref[:, pl.ds(start, tk)], p.astype(BF16))
        m_ref[...] = m_new

    s_ref[0] = scores(0)

    def body(i, carry):
        j = 2 * i
        s_ref[1] = scores(j + 1)
        absorb(j, s_ref[0])
        s_ref[0] = scores(j + 2)
        absorb(j + 1, s_ref[1])
        return carry

    lax.fori_loop(0, nk // 2, body, 0)
    if nk % 2:
        absorb(nk - 1, s_ref[0])
    out = acc_ref[...] / l_ref[...]
    for r in range(rep):
        o_ref[:, r * dh:(r + 1) * dh] = out[:, r * tq:(r + 1) * tq].T.astype(o_ref.dtype)


def _kv_chunk(lk):
    for tiles in (5, 4, 3, 2, 1):
        if lk % (tiles * 256) == 0:
            return tiles * 256
    raise ValueError(lk)


def gqa_attention(q_t, k_hm, v_t):
    wq, l = q_t.shape
    hkv, lk, dh = k_hm.shape
    wg = wq // hkv
    rep = wg // dh
    tq = _row_tile(l, 256)
    tk = _kv_chunk(lk)
    kern = functools.partial(_flash_kernel, tk=tk, nk=lk // tk)
    return pl.pallas_call(
        kern,
        grid=(hkv, l // tq),
        in_specs=[pl.BlockSpec((wg, tq), lambda g, i: (g, i)),
                  pl.BlockSpec((None, lk, dh), lambda g, i: (g, 0, 0)),
                  pl.BlockSpec((None, dh, lk), lambda g, i: (g, 0, 0))],
        out_specs=pl.BlockSpec((tq, wg), lambda g, i: (i, g)),
        out_shape=jax.ShapeDtypeStruct((l, wq), BF16),
        scratch_shapes=[pltpu.VMEM((dh, rep * tq), BF16), pltpu.VMEM((2, tk, rep * tq), F32),
                        pltpu.VMEM((1, rep * tq), F32), pltpu.VMEM((1, rep * tq), F32),
                        pltpu.VMEM((dh, rep * tq), F32)],
        compiler_params=_cparams(("parallel", "parallel")),
        name="gqa_attention",
    )(q_t, k_hm, v_t)


def _cmul(ar, ai, br, bi):
    return ar * br - ai * bi, ar * bi + ai * br


def _s5_operators(a_re, a_im, log_dt, b_re, b_im, c_re, c_im):
    t = S5_CHUNK
    gs = S5_GROUP
    hp = dict(precision=HIGHEST)
    dt = jnp.exp(log_dt)[..., None]
    zr, zi = a_re * dt, a_im * dt
    er = jnp.exp(zr)
    abr, abi = er * jnp.cos(zi), er * jnp.sin(zi)
    den = a_re * a_re + a_im * a_im
    fr = ((abr - 1.0) * a_re + abi * a_im) / den
    fi = (abi * a_re - (abr - 1.0) * a_im) / den
    bbr, bbi = _cmul(fr[..., None], fi[..., None], b_re, b_im)
    tau = jnp.arange(t + 1, dtype=F32)
    pr = jnp.exp(zr[..., None] * tau) * jnp.cos(zi[..., None] * tau)
    pi = jnp.exp(zr[..., None] * tau) * jnp.sin(zi[..., None] * tau)
    car, cai = _cmul(c_re[..., None], c_im[..., None], pr[:, :, None, :, :t], pi[:, :, None, :, :t])
    ktap = (jnp.einsum('dgqpt,dgpk->dgtqk', car, bbr, **hp) - jnp.einsum('dgqpt,dgpk->dgtqk', cai, bbi, **hp))
    ktp = jnp.concatenate([jnp.zeros_like(ktap), ktap], axis=2)
    win = jnp.stack([ktp[:, :, t - i:2 * t - i] for i in range(t)], axis=2)
    m_tot = (win[0] + win[1].transpose(0, 2, 1, 3, 4)).transpose(0, 1, 4, 2, 3)
    pw_r = jnp.stack([pr[0, ..., t - 1::-1], pr[1, ..., :t]])
    pw_i = jnp.stack([pi[0, ..., t - 1::-1], pi[1, ..., :t]])
    wr, wi = _cmul(pw_r[..., None], pw_i[..., None], bbr[:, :, :, None, :], bbi[:, :, :, None, :])
    w_in = jnp.stack([wr[0], wi[0], wr[1], wi[1]]).transpose(1, 3, 4, 0, 2)
    pv_r = jnp.stack([pr[0, ..., 1:], pr[1, ..., t:0:-1]])
    pv_i = jnp.stack([pi[0, ..., 1:], pi[1, ..., t:0:-1]])
    vr, vi = _cmul(c_re[..., None], c_im[..., None], pv_r[:, :, None], pv_i[:, :, None])
    v_out = jnp.stack([vr[0], -vi[0], vr[1], -vi[1]]).transpose(1, 0, 3, 4, 2)

    nb = S5_GROUPS // S5_LANE_GROUPS
    eye = jnp.eye(S5_LANE_GROUPS, dtype=F32)
    blk = lambda x: x.reshape((nb, S5_LANE_GROUPS) + x.shape[1:])
    m6, w6, v6 = blk(m_tot), blk(w_in), blk(v_out)
    m_op = (m6[:, :, :, :, :, None, :] * eye[None, :, None, None, None, :, None]).transpose(0, 2, 1, 3, 4, 5, 6)
    m_op = m_op.reshape(nb, t * LANES, t * LANES)
    w_op = (w6[:, :, :, :, :, None, :] * eye[None, :, None, None, None, :, None]).transpose(0, 2, 1, 3, 4, 5, 6)
    w_op = w_op.reshape(nb, t * LANES, 4 * S5_LANE_GROUPS * S5_STATE)
    v_op = (v6[:, :, :, :, :, None, :] * eye[None, :, None, None, None, :, None]).transpose(0, 2, 1, 3, 4, 5, 6)
    v_op = v_op.reshape(nb, 4 * S5_LANE_GROUPS * S5_STATE, t * LANES)
    return m_op.astype(BF16), w_op.astype(BF16), v_op.astype(BF16), pr[..., t], pi[..., t]


def _s5_layout_kernel(*refs):
    *u_refs, o_ref = refs
    rows = o_ref.shape[0]
    t = S5_CHUNK
    for b, u_ref in enumerate(u_refs):
        for i in range(t):
            o_ref[:, (b * t + i) * LANES:(b * t + i + 1) * LANES] = u_ref[pl.ds(i, rows, stride=t), :].astype(o_ref.dtype)


def _s5_state_in_kernel(u_ref, w_ref, *e_refs):
    e = _dot(u_ref[...], w_ref[...])
    q = e.shape[1] // len(e_refs)
    for part, e_ref in enumerate(e_refs):
        e_ref[...] = e[:, part * q:(part + 1) * q]


def _s5_scan_kernel(ar_ref, ai_ref, er_ref, ei_ref, sr_ref, si_ref, cr_ref, ci_ref, *, reverse):
    @pl.when(pl.program_id(0) == 0)
    def _():
        cr_ref[...] = jnp.zeros_like(cr_ref)
        ci_ref[...] = jnp.zeros_like(ci_ref)

    ar, ai = ar_ref[...], ai_ref[...]
    n = er_ref.shape[0]

    def body(k, carry):
        c = n - 1 - k if reverse else k
        sr, si = carry
        sr_ref[c] = sr
        si_ref[c] = si
        return ar * sr - ai * si + er_ref[c], ar * si + ai * sr + ei_ref[c]

    sr, si = lax.fori_loop(0, n, body, (cr_ref[...], ci_ref[...]))
    cr_ref[...] = sr
    ci_ref[...] = si


def _s5_out_kernel(u_ref, fr_ref, fi_ref, br_ref, bi_ref, m_ref, v_ref, y_ref):
    s = jnp.concatenate([fr_ref[...], fi_ref[...], br_ref[...], bi_ref[...]], axis=1).astype(BF16)
    y_ref[...] = _dot(u_ref[...], m_ref[...]) + _dot(s, v_ref[...])


def _s5_readout_kernel(y_ref, u_ref, d_ref, w_ref, b_ref, o_ref, ynat_ref):
    rows = y_ref.shape[0]
    t = S5_CHUNK
    nb = ynat_ref.shape[0]
    for b in range(nb):
        for i in range(t):
            ynat_ref[b, pl.ds(i, rows, stride=t), :] = y_ref[:, (b * t + i) * LANES:(b * t + i + 1) * LANES]
    y = jnp.concatenate([ynat_ref[b] for b in range(nb)], axis=1) + d_ref[...] * u_ref[...]
    y = 0.5 * y * (1.0 + jnp.tanh(math.sqrt(2.0 / math.pi) * (y + 0.044715 * (y * y * y))))
    gate = jax.nn.sigmoid(_dot(y.astype(BF16), w_ref[...]) + b_ref[...])
    o_ref[...] = (y * gate).astype(o_ref.dtype)


def s5_mix(u_ctx, u_lat, a_re, a_im, log_dt, b_re, b_im, c_re, c_im, d_skip, glu_w, glu_b):
    t = S5_CHUNK
    n_ctx, w = u_ctx.shape
    l = u_lat.shape[0]
    n_tok = n_ctx + l
    nch = n_tok // t
    tc = n_ctx // t
    n_tiles = nch // tc
    assert n_ctx == tc * t and tc % (2 * SUBLANES) == 0 and l % (tc * t) == 0
    nb = w // LANES
    cw = t * LANES
    sw = S5_LANE_GROUPS * S5_STATE
    n_state = S5_GROUPS * S5_STATE
    m_op, w_op, v_op, atr, ati = _s5_operators(a_re, a_im, log_dt, b_re, b_im, c_re, c_im)

    u_all = jnp.concatenate([u_ctx, u_lat], axis=0)
    u_ch = pl.pallas_call(
        _s5_layout_kernel,
        grid=(n_tiles,),
        in_specs=[pl.BlockSpec((tc * t, LANES), functools.partial(lambda b, i: (i, b), b)) for b in range(nb)],
        out_specs=pl.BlockSpec((tc, nb * cw), lambda i: (i, 0)),
        out_shape=jax.ShapeDtypeStruct((nch, nb * cw), BF16),
        compiler_params=_cparams(("parallel",)),
        name="s5_layout",
    )(*([u_all] * nb))

    u_blk = pl.BlockSpec((nch, cw), lambda b: (0, b))
    s_blk = pl.BlockSpec((nch, sw), lambda b: (0, b))
    states_in = pl.pallas_call(
        _s5_state_in_kernel,
        grid=(nb,),
        in_specs=[u_blk, pl.BlockSpec((None, cw, 4 * sw), lambda b: (b, 0, 0))],
        out_specs=[s_blk] * 4,
        out_shape=[jax.ShapeDtypeStruct((nch, n_state), F32)] * 4,
        compiler_params=_cparams(("parallel",)),
        name="s5_state_in",
    )(u_ch, w_op)

    slab = n_state // SUBLANES
    vec = pl.BlockSpec((SUBLANES, slab), lambda s: (0, 0))
    orders = (lambda s: (s, 0, 0),
              lambda s: (jnp.where(s == 0, 0, n_tiles - s), 0, 0))
    states = []
    for d in range(2):
        blk = pl.BlockSpec((tc, SUBLANES, slab), orders[d])
        s_re, s_im = pl.pallas_call(
            functools.partial(_s5_scan_kernel, reverse=bool(d)),
            grid=(n_tiles,),
            in_specs=[vec, vec, blk, blk],
            out_specs=[blk, blk],
            out_shape=[jax.ShapeDtypeStruct((nch, SUBLANES, slab), F32)] * 2,
            scratch_shapes=[pltpu.VMEM((SUBLANES, slab), F32)] * 2,
            compiler_params=_cparams(("arbitrary",)),
            name="s5_scan",
        )(atr[d].reshape(SUBLANES, slab), ati[d].reshape(SUBLANES, slab),
          states_in[2 * d].reshape(nch, SUBLANES, slab), states_in[2 * d + 1].reshape(nch, SUBLANES, slab))
        states += [s_re.reshape(nch, n_state), s_im.reshape(nch, n_state)]

    oc = cw // 4
    y_ch = pl.pallas_call(
        _s5_out_kernel,
        grid=(nb, cw // oc),
        in_specs=[pl.BlockSpec((nch, cw), lambda b, j: (0, b))] + [pl.BlockSpec((nch, sw), lambda b, j: (0, b))] * 4
                 + [pl.BlockSpec((None, cw, oc), lambda b, j: (b, 0, j)),
                    pl.BlockSpec((None, 4 * sw, oc), lambda b, j: (b, 0, j))],
        out_specs=pl.BlockSpec((nch, oc), lambda b, j: (0, b * (cw // oc) + j)),
        out_shape=jax.ShapeDtypeStruct((nch, nb * cw), F32),
        compiler_params=_cparams(("parallel", "parallel")),
        name="s5_out",
    )(u_ch, *states, m_op, v_op)

    row = lambda i: (i, 0)
    fixed = lambda i: (0, 0)
    return pl.pallas_call(
        _s5_readout_kernel,
        grid=(l // (tc * t),),
        in_specs=[pl.BlockSpec((tc, nb * cw), lambda i: (i + n_ctx // (tc * t), 0)), pl.BlockSpec((tc * t, w), row),
                  pl.BlockSpec((1, w), fixed), pl.BlockSpec((w, w), fixed), pl.BlockSpec((1, w), fixed)],
        out_specs=pl.BlockSpec((tc * t, w), row),
        out_shape=jax.ShapeDtypeStruct((l, w), BF16),
        scratch_shapes=[pltpu.VMEM((nb, tc * t, LANES), F32)],
        compiler_params=_cparams(("parallel",)),
        name="s5_readout",
    )(y_ch, u_lat, d_skip.reshape(1, w), glu_w.astype(BF16), glu_b.reshape(1, w))


def _first_max(vals, lane):
    m = vals.max(axis=1, keepdims=True)
    idx = jnp.where(vals == m, lane, jnp.int32(1 << 20)).min(axis=1, keepdims=True)
    return m, idx


def _stream_specs(n_ctx_tiles, tm, d):
    return (pl.BlockSpec((tm, d), lambda i: (jnp.clip(i, 0, max(n_ctx_tiles - 1, 0)), 0)),
            pl.BlockSpec((tm, d), lambda i: (jnp.maximum(i - n_ctx_tiles, 0), 0)))


def _stream_tile(xc_ref, xl_ref, n_ctx_tiles):
    if n_ctx_tiles == 0:
        return xl_ref[...]
    return jnp.where(pl.program_id(0) < n_ctx_tiles, xc_ref[...], xl_ref[...])


def _store_row_slabs(ref, x):
    rows = x.shape[0]
    for s in range(ROW_SLAB):
        ref[pl.ds(s, rows, stride=ROW_SLAB), :] = x[:, s * LANES:(s + 1) * LANES]


def _router_kernel(xc_ref, xl_ref, sc_ref, sh_ref, rw_ref, rb_ref, tri_ref,
                   hf_ref, te_ref, gt_ref, rk_ref, cnt_ref, run_ref, *, n_ctx_tiles):
    @pl.when(pl.program_id(0) == 0)
    def _():
        run_ref[...] = jnp.zeros_like(run_ref)

    hf = _stream_tile(xc_ref, xl_ref, n_ctx_tiles) * (1.0 + sc_ref[...]) + sh_ref[...]
    _store_row_slabs(hf_ref, hf)
    tm = hf.shape[0]
    scores = jax.nn.sigmoid(_dot(hf, rw_ref[...], precision=HIGHEST))
    biased = scores + rb_ref[...]
    lane = lax.broadcasted_iota(I32, (tm, N_EXPERTS), 1)
    grp = lane // (N_EXPERTS // N_EXPERT_GROUPS)
    lane_o = lax.broadcasted_iota(I32, (tm, LANES), 1)
    neg = jnp.float32(-jnp.inf)

    group_score = jnp.full((tm, LANES), neg, F32)
    for g in range(N_EXPERT_GROUPS):
        vals = jnp.where(grp == g, biased, neg)
        m1, i1 = _first_max(vals, lane)
        m2 = jnp.where(lane == i1, neg, vals).max(axis=1, keepdims=True)
        group_score = jnp.where(lane_o == g, m1 + m2, group_score)
    keep = jnp.zeros((tm, N_EXPERTS), F32)
    for _ in range(TOPK_GROUPS):
        _, gi = _first_max(group_score, lane_o)
        keep = jnp.where(grp == gi, 1.0, keep)
        group_score = jnp.where(lane_o == gi, neg, group_score)

    masked = jnp.where(keep > 0.0, biased, neg)
    member = jnp.zeros((tm, N_EXPERTS), F32)
    e_cols, g_cols = [], []
    for _ in range(TOP_K):
        _, ei = _first_max(masked, lane)
        hit = lane == ei
        g_cols.append(jnp.where(hit, scores, 0.0).sum(axis=1, keepdims=True))
        masked = jnp.where(hit, neg, masked)
        member = jnp.where(hit, 1.0, member)
        e_cols.append(ei)
    g_sum = g_cols[0]
    for gk in g_cols[1:]:
        g_sum = g_sum + gk

    before = _dot(tri_ref[...], member.astype(BF16)) + run_ref[...]
    te = jnp.zeros((tm, LANES), I32)
    rk = jnp.zeros((tm, LANES), I32)
    gt = jnp.zeros((tm, LANES), F32)
    for k in range(TOP_K):
        rank = jnp.where(lane == e_cols[k], before, 0.0).sum(axis=1, keepdims=True)
        te = jnp.where(lane_o == k, e_cols[k], te)
        rk = jnp.where(lane_o == k, rank.astype(I32), rk)
        gt = jnp.where(lane_o == k, ROUTED_SCALE * g_cols[k] / g_sum, gt)
    te_ref[...] = te
    rk_ref[...] = rk
    gt_ref[...] = gt
    run_ref[...] += member.sum(axis=0, keepdims=True)
    cnt_ref[...] = run_ref[...]


def moe_route(x_ctx, x_lat, sc2, sh2, n_ctx_tiles, router_w, router_bias):
    d = x_lat.shape[1]
    tm = MOE_TILE
    n = n_ctx_tiles * tm + x_lat.shape[0]
    tri = (jnp.arange(tm)[None, :] < jnp.arange(tm)[:, None]).astype(BF16)
    row = lambda i: (i, 0)
    fixed = lambda i: (0, 0)
    seg = lambda i: (jnp.where(i < n_ctx_tiles, 1, 0), 0, 0)
    kern = functools.partial(_router_kernel, n_ctx_tiles=n_ctx_tiles)
    return pl.pallas_call(
        kern,
        grid=(n // tm,),
        in_specs=[*_stream_specs(n_ctx_tiles, tm, d), pl.BlockSpec((None, 1, d), seg), pl.BlockSpec((None, 1, d), seg),
                  pl.BlockSpec((d, N_EXPERTS), fixed), pl.BlockSpec((1, N_EXPERTS), fixed),
                  pl.BlockSpec((tm, tm), fixed)],
        out_specs=[pl.BlockSpec((tm * ROW_SLAB, LANES), row), pl.BlockSpec((tm, LANES), row),
                   pl.BlockSpec((tm, LANES), row), pl.BlockSpec((tm, LANES), row),
                   pl.BlockSpec((1, N_EXPERTS), fixed)],
        out_shape=[jax.ShapeDtypeStruct((n * ROW_SLAB, LANES), F32), jax.ShapeDtypeStruct((n, LANES), I32),
                   jax.ShapeDtypeStruct((n, LANES), F32), jax.ShapeDtypeStruct((n, LANES), I32),
                   jax.ShapeDtypeStruct((1, N_EXPERTS), F32)],
        scratch_shapes=[pltpu.VMEM((1, N_EXPERTS), F32)],
        compiler_params=_cparams(("arbitrary",)),
        name="moe_route",
    )(x_ctx, x_lat, sc2, sh2, router_w, router_bias.reshape(1, N_EXPERTS), tri)


def _slots_kernel(te_ref, rk_ref, start_ref, o_ref):
    tm = te_ref.shape[0]
    lane = lax.broadcasted_iota(I32, (tm, N_EXPERTS), 1)
    lane_o = lax.broadcasted_iota(I32, (tm, LANES), 1)
    te = te_ref[...]
    out = rk_ref[...]
    for k in range(TOP_K):
        first = jnp.where(lane == te[:, k:k + 1], start_ref[...], 0.0).sum(axis=1, keepdims=True)
        out = jnp.where(lane_o == k, out + first.astype(I32), out)
    o_ref[...] = out


def moe_slots(te, rk, start):
    n = te.shape[0]
    tm = MOE_TILE
    row = lambda i: (i, 0)
    out = pl.pallas_call(
        _slots_kernel,
        grid=(n // tm,),
        in_specs=[pl.BlockSpec((tm, LANES), row), pl.BlockSpec((tm, LANES), row),
                  pl.BlockSpec((1, N_EXPERTS), lambda i: (0, 0))],
        out_specs=pl.BlockSpec((tm, LANES), row),
        out_shape=jax.ShapeDtypeStruct((n, LANES), I32),
        compiler_params=_cparams(("parallel",)),
        name="moe_slots",
    )(te, rk, start.astype(F32).reshape(1, N_EXPERTS))
    return out[:, :TOP_K].reshape(-1)


def _slab_rows(ref, row, n_rows):
    first = row * ROW_SLAB
    if not isinstance(first, int):
        first = pl.multiple_of(first, ROW_SLAB)
    return ref.at[pl.ds(first, n_rows * ROW_SLAB), :]


def _dispatch_kernel(dest_ref, hf_ref, xs_hbm, zbuf, sem, zsem, *, n_assign):
    n_rows = dest_ref.shape[0]

    @pl.when(pl.program_id(0) == 0)
    def _():
        zbuf[...] = jnp.zeros_like(zbuf)
        tail = pltpu.make_async_copy(zbuf, _slab_rows(xs_hbm, n_assign, EXPERT_BLOCK), zsem)
        tail.start()
        tail.wait()

    def body(r, carry):
        src = _slab_rows(hf_ref, r, 1)
        for k in range(TOP_K):
            pltpu.make_async_copy(src, _slab_rows(xs_hbm, dest_ref[r * TOP_K + k], 1), sem).start(priority=k % 2)
        return carry
    lax.fori_loop(0, n_rows // TOP_K, body, 0)
    for _ in range(TOP_K):
        pltpu.make_async_copy(hf_ref, hf_ref, sem).wait()


def moe_dispatch(dest, hf_slabs):
    n_assign = dest.shape[0]
    tm = MOE_TILE
    n_rows = tm * TOP_K
    kern = functools.partial(_dispatch_kernel, n_assign=n_assign)
    return pl.pallas_call(
        kern,
        grid=(n_assign // n_rows,),
        in_specs=[pl.BlockSpec((n_rows,), lambda i: (i,), memory_space=pltpu.SMEM),
                  pl.BlockSpec((tm * ROW_SLAB, LANES), lambda i: (i, 0))],
        out_specs=pl.BlockSpec(memory_space=pl.ANY),
        out_shape=jax.ShapeDtypeStruct(((n_assign + EXPERT_BLOCK) * ROW_SLAB, LANES), F32),
        scratch_shapes=[pltpu.VMEM((EXPERT_BLOCK * ROW_SLAB, LANES), F32), pltpu.SemaphoreType.DMA(()),
                        pltpu.SemaphoreType.DMA(())],
        compiler_params=_cparams(("arbitrary",)),
        name="moe_dispatch",
    )(dest, hf_slabs)


def _row_gather(idx_ref, base, count, src_hbm, dst, sem):
    group = 8

    def body(g, carry):
        for k in range(group):
            j = g * group + k
            cp = pltpu.make_async_copy(_slab_rows(src_hbm, idx_ref[base + j], 1), _slab_rows(dst, j, 1), sem)
            cp.start(priority=k % 2)
        return carry
    lax.fori_loop(0, count // group, body, 0)


def _wait_rows(dst, sem):
    pltpu.make_async_copy(dst, dst, sem).wait()


def _gathered_rows(buf, first, rows, stride):
    return jnp.concatenate(
        [buf[pl.ds(first * ROW_SLAB + s, rows, stride=stride * ROW_SLAB), :] for s in range(ROW_SLAB)], axis=1)


def _swiglu(x, wg, wu, wd):
    gate = _dot(x, wg)
    up = _dot(x, wu)
    return _dot((gate * jax.nn.sigmoid(gate) * up).astype(BF16), wd)


def _valid_row_copies(ybuf, ys_hbm, row0, valid, sem):
    out = [(valid == EXPERT_BLOCK,
            pltpu.make_async_copy(_slab_rows(ybuf, 0, EXPERT_BLOCK), _slab_rows(ys_hbm, row0, EXPERT_BLOCK), sem))]
    part = valid < EXPERT_BLOCK
    size = EXPERT_BLOCK // 2
    while size >= 1:
        off = valid & ~(2 * size - 1)
        out.append((part & ((valid & size) != 0),
                    pltpu.make_async_copy(_slab_rows(ybuf, off, size), _slab_rows(ys_hbm, row0 + off, size), sem)))
        size //= 2
    return out


def _expert_kernel(be_ref, r0_ref, nv_ref, ws_ref, nx_ref, xs_hbm, wg_hbm, wu_hbm, wd_hbm, ys_hbm,
                   xbuf, ybuf, wg_st, wu_st, wd_st, wg_bf, wu_bf, wd_bf, sem_in, sem_out, sem_w):
    b = pl.program_id(0)
    n_blocks = pl.num_programs(0)
    slot = b % 2

    def weight_copies(e, ws):
        return [pltpu.make_async_copy(hbm.at[e], st.at[ws], sem_w.at[ws])
                for hbm, st in ((wg_hbm, wg_st), (wu_hbm, wu_st), (wd_hbm, wd_st))]

    @pl.when(b == 0)
    def _():
        for cp in weight_copies(be_ref[0], 0):
            cp.start()

    @pl.when(ws_ref[b] >= 0)
    def _():
        ws = ws_ref[b]
        for cp in weight_copies(be_ref[b], ws):
            cp.wait()
        wg_bf[...] = wg_st[ws].astype(BF16)
        wu_bf[...] = wu_st[ws].astype(BF16)
        wd_bf[...] = wd_st[ws].astype(BF16)

        @pl.when(nx_ref[b] >= 0)
        def _():
            for cp in weight_copies(nx_ref[b], 1 - ws):
                cp.start()

    def fetch(blk, sl):
        return pltpu.make_async_copy(_slab_rows(xs_hbm, r0_ref[blk], EXPERT_BLOCK), xbuf.at[sl], sem_in.at[sl])

    def drain(blk, sl):
        for cond, cp in _valid_row_copies(ybuf.at[sl], ys_hbm, r0_ref[blk], nv_ref[blk], sem_out.at[sl]):
            pl.when(cond)(cp.wait)

    @pl.when(b == 0)
    def _():
        fetch(0, 0).start()

    nxt = jnp.minimum(b + 1, n_blocks - 1)

    @pl.when((b + 1 < n_blocks) & (nv_ref[nxt] > 0))
    def _():
        fetch(nxt, 1 - slot).start()

    @pl.when(b >= 2)
    def _():
        drain(jnp.maximum(b - 2, 0), slot)

    @pl.when(nv_ref[b] > 0)
    def _():
        fetch(b, slot).wait()
        x = _gathered_rows(xbuf.at[slot], 0, EXPERT_BLOCK, 1).astype(BF16)
        y = _swiglu(x, wg_bf[...], wu_bf[...], wd_bf[...])
        _store_row_slabs(ybuf.at[slot], y)
        for cond, cp in _valid_row_copies(ybuf.at[slot], ys_hbm, r0_ref[b], nv_ref[b], sem_out.at[slot]):
            pl.when(cond)(cp.start)

    @pl.when(b == n_blocks - 1)
    def _():
        drain(jnp.maximum(b - 1, 0), 1 - slot)
        drain(b, slot)


def moe_experts(xs_slabs, block_e, block_row0, block_valid, block_wslot, block_next_e, w_gate, w_up, w_down):
    n_blocks = block_e.shape[0]
    d, ff = w_gate.shape[1:]
    blk_rows = EXPERT_BLOCK * ROW_SLAB
    n_assign = xs_slabs.shape[0] // ROW_SLAB - EXPERT_BLOCK
    any_spec = pl.BlockSpec(memory_space=pl.ANY)
    grid_spec = pltpu.PrefetchScalarGridSpec(
        num_scalar_prefetch=5,
        grid=(n_blocks,),
        in_specs=[any_spec] * 4,
        out_specs=any_spec,
        scratch_shapes=[pltpu.VMEM((2, blk_rows, LANES), F32), pltpu.VMEM((2, blk_rows, LANES), F32),
                        pltpu.VMEM((2, d, ff), F32), pltpu.VMEM((2, d, ff), F32), pltpu.VMEM((2, ff, d), F32),
                        pltpu.VMEM((d, ff), BF16), pltpu.VMEM((d, ff), BF16), pltpu.VMEM((ff, d), BF16),
                        pltpu.SemaphoreType.DMA((2,)), pltpu.SemaphoreType.DMA((2,)), pltpu.SemaphoreType.DMA((2,))],
    )
    return pl.pallas_call(
        _expert_kernel,
        grid_spec=grid_spec,
        out_shape=jax.ShapeDtypeStruct((n_assign * ROW_SLAB, LANES), F32),
        compiler_params=_cparams(("arbitrary",)),
        name="moe_experts",
    )(block_e, block_row0, block_valid, block_wslot, block_next_e, xs_slabs, w_gate, w_up, w_down)


def _combine_kernel(cur_ref, nxt_ref, ys_hbm, xc_ref, xl_ref, hf_ref, gt_ref, g2_ref, sg_ref, su_ref, sd_ref,
                    lg_ref, lb_ref, *rest, n_ctx_tiles):
    *o_refs, ybuf, sem = rest
    i = pl.program_id(0)
    n_tiles = pl.num_programs(0)
    tm = xl_ref.shape[0]
    n_rows = tm * TOP_K
    slot = i % 2

    @pl.when(i == 0)
    def _():
        _row_gather(cur_ref, 0, n_rows, ys_hbm, ybuf.at[0], sem.at[0])

    @pl.when(i + 1 < n_tiles)
    def _():
        _row_gather(nxt_ref, 0, n_rows, ys_hbm, ybuf.at[1 - slot], sem.at[1 - slot])

    hf = _gathered_rows(hf_ref, 0, tm, 1).astype(BF16)
    y = _swiglu(hf, sg_ref[...], su_ref[...], sd_ref[...])

    _wait_rows(ybuf.at[slot], sem.at[slot])
    for k in range(TOP_K):
        y = y + gt_ref[:, k:k + 1] * _gathered_rows(ybuf.at[slot], k, tm, TOP_K)
    r = DEEPNORM_ALPHA * _stream_tile(xc_ref, xl_ref, n_ctx_tiles) + g2_ref[...] * y
    res = _layer_norm_rows(r, lg_ref[...], lb_ref[...])
    if n_ctx_tiles == 0:
        o_refs[0][...] = res
    else:
        oc_ref, ol_ref = o_refs

        @pl.when(i < n_ctx_tiles)
        def _():
            oc_ref[...] = res

        @pl.when(i >= n_ctx_tiles)
        def _():
            ol_ref[...] = res


def moe_combine(dest, ys_slabs, x_ctx, x_lat, hf_slabs, gate, g2, n_ctx_tiles, sh_gate, sh_up, sh_down, ln_g, ln_b):
    d = x_lat.shape[1]
    tm = MOE_TILE
    n_tiles = n_ctx_tiles + x_lat.shape[0] // tm
    n_rows = tm * TOP_K
    ff = sh_gate.shape[1]
    row = lambda i: (i, 0)
    fixed = lambda i: (0, 0)
    seg = lambda i: (jnp.where(i < n_ctx_tiles, 1, 0), 0, 0)
    ctx_spec, lat_spec = _stream_specs(n_ctx_tiles, tm, d)
    lat_out = jax.ShapeDtypeStruct(x_lat.shape, F32)
    if n_ctx_tiles == 0:
        out_specs, out_shape = [lat_spec], [lat_out]
    else:
        out_specs, out_shape = [ctx_spec, lat_spec], [jax.ShapeDtypeStruct(x_ctx.shape, F32), lat_out]
    kern = functools.partial(_combine_kernel, n_ctx_tiles=n_ctx_tiles)
    return pl.pallas_call(
        kern,
        grid=(n_tiles,),
        in_specs=[pl.BlockSpec((n_rows,), lambda i: (i,), memory_space=pltpu.SMEM),
                  pl.BlockSpec((n_rows,), lambda i: (jnp.minimum(i + 1, n_tiles - 1),), memory_space=pltpu.SMEM),
                  pl.BlockSpec(memory_space=pl.ANY),
                  ctx_spec, lat_spec, pl.BlockSpec((tm * ROW_SLAB, LANES), row), pl.BlockSpec((tm, LANES), row),
                  pl.BlockSpec((None, 1, d), seg),
                  pl.BlockSpec((d, ff), fixed), pl.BlockSpec((d, ff), fixed), pl.BlockSpec((ff, d), fixed),
                  pl.BlockSpec((1, d), fixed), pl.BlockSpec((1, d), fixed)],
        out_specs=out_specs,
        out_shape=out_shape,
        scratch_shapes=[pltpu.VMEM((2, n_rows * ROW_SLAB, LANES), F32), pltpu.SemaphoreType.DMA((2,))],
        compiler_params=_cparams(("arbitrary",)),
        name="moe_combine",
    )(dest, dest, ys_slabs, x_ctx, x_lat, hf_slabs, gate, g2,
      sh_gate.astype(BF16), sh_up.astype(BF16), sh_down.astype(BF16), ln_g.reshape(1, d), ln_b.reshape(1, d))


def moe_layer(x_ctx, x_lat, sc2, sh2, g2, n_ctx_tiles, router_w, router_bias, w_gate, w_up, w_down,
              sh_gate, sh_up, sh_down, ln_g, ln_b):
    n = n_ctx_tiles * MOE_TILE + x_lat.shape[0]
    hf, te, gt, rk, cnt = moe_route(x_ctx, x_lat, sc2, sh2, n_ctx_tiles, router_w, router_bias)
    counts = cnt[0].astype(I32)
    start = jnp.cumsum(counts) - counts
    experts = jnp.arange(N_EXPERTS, dtype=I32)
    dest = moe_slots(te, rk, start)
    nb = (counts + EXPERT_BLOCK - 1) // EXPERT_BLOCK
    blk_end = jnp.cumsum(nb)
    blk_start = blk_end - nb
    n_blocks = n * TOP_K // EXPERT_BLOCK + N_EXPERTS
    blk = jnp.arange(n_blocks, dtype=I32)
    bb = jnp.minimum(blk, blk_end[-1] - 1)[:, None]
    own = (blk_start[None, :] <= bb) & (bb < blk_end[None, :])
    sel = lambda v: jnp.sum(jnp.where(own, v[None, :], 0), axis=1)
    j = bb[:, 0] - sel(blk_start)
    block_e = sel(experts)
    block_row0 = sel(start) + j * EXPERT_BLOCK
    block_valid = jnp.where(blk < blk_end[-1], jnp.clip(sel(counts) - j * EXPERT_BLOCK, 0, EXPERT_BLOCK), 0)
    used = nb > 0
    ordinal = jnp.cumsum(used.astype(I32)) - 1
    later = used[None, :] & (experts[None, :] > experts[:, None])
    next_used = jnp.min(jnp.where(later, experts[None, :], N_EXPERTS), axis=1)
    next_used = jnp.where(next_used < N_EXPERTS, next_used, -1)
    first = (blk < blk_end[-1]) & (j == 0)
    block_wslot = jnp.where(first, sel(ordinal) % 2, -1)
    block_next_e = jnp.where(first, sel(next_used), -1)
    xs = moe_dispatch(dest, hf)
    ys = moe_experts(xs, block_e, block_row0, block_valid, block_wslot, block_next_e, w_gate, w_up, w_down)
    return moe_combine(dest, ys, x_ctx, x_lat, hf, gt, g2, n_ctx_tiles, sh_gate, sh_up, sh_down, ln_g, ln_b)


def kernel(x, c, ctx, c_ctx, w_mod, b_mod, ln_mix_g, ln_mix_b, ln_ffn_g, ln_ffn_b, ab_w_in, ab_w_out, na_rpb,
           hy_conv_w, hy_conv_b, hy_f_w1, hy_f_b1, hy_f_freq, hy_f_w2, hy_f_b2, hy_f_w3, hy_skip, cd_w_in, cd_w_out,
           q_norm_g, k_norm_g, s5_a_re, s5_a_im, s5_log_dt, s5_b_re, s5_b_im, s5_c_re, s5_c_im, s5_d, s5_glu_w,
           s5_glu_b, router_w, router_bias, exp_w_gate, exp_w_up, exp_w_down, sh_w_gate, sh_w_up, sh_w_down):
    b, l, d = x.shape
    assert b == 1
    n_ctx = ctx.shape[1]
    assert n_ctx == MOE_TILE
    xs = x[0]
    cs = ctx[0]
    cmat = jnp.zeros((SUBLANES, d), F32).at[0].set(c[0]).at[1].set(c_ctx)
    mods = modulation_all(cmat, w_mod, b_mod).reshape(DEPTH, SUBLANES, 6, d)
    qscale = HEAD_DIM ** -0.5

    for i in range(DEPTH):
        need_ctx = i < DEPTH - 1
        m = mods[i]
        sh1, sc1, g1, sh2, sc2, g2 = [m[0:1, t] for t in range(6)]
        csh1, csc1, cg1, csh2, csc2, cg2 = [m[1:2, t] for t in range(6)]
        j = i // 2
        if i % 2 == 0:
            filt = (hy_conv_w[j], hy_conv_b[j], hy_f_w1[j], hy_f_b1[j], hy_f_freq[j], hy_f_w2[j], hy_f_b2[j],
                    hy_f_w3[j], hy_skip[j])
            splits = (NA_WIDTH, NA_WIDTH, NA_WIDTH, 3 * HY_WIDTH)
            dts = (BF16, BF16, BF16, F32)
            scl = (qscale, 1.0, 1.0, 1.0)
            q_l, k_l, v_l, u_l = mod_project(xs, sc1, sh1, ab_w_in[j], splits, dts, scl)
            q_c, k_c, v_c, u_c = mod_project(cs, csc1, csh1, ab_w_in[j], splits, dts, scl)
            a_lat = neighbourhood_attention(q_l, k_l, v_l, k_c, v_c, na_rpb[j])
            y_hy = hyena_long(u_l, *filt)
            xs_new = outproj_ln(a_lat, y_hy, ab_w_out[j], xs, g1, ln_mix_g[i], ln_mix_b[i])
            if need_ctx:
                a_ctx = context_attention(q_c, k_c, v_c)
                yc_hy = hyena_long(u_c, *filt)
                cs = outproj_ln(a_ctx, yc_hy, ab_w_out[j], cs, cg1, ln_mix_g[i], ln_mix_b[i])
            xs = xs_new
        else:
            splits = (GQA_WIDTH, GQA_KV_WIDTH, GQA_KV_WIDTH, S5_WIDTH)
            q_l, k_l, v_l, u_l = mod_project(xs, sc1, sh1, cd_w_in[j], splits, (F32, F32, BF16, F32))
            k_c, v_c, u_c = mod_project(cs, csc1, csh1, cd_w_in[j][:, GQA_WIDTH:], splits[1:], (F32, BF16, F32))
            tabs = _rope_tables(l)
            qn_t = qk_prep(q_l, q_norm_g[j], tabs, qscale * math.log2(math.e), transposed=True)
            kn = qk_prep(k_l, k_norm_g[j], tabs, 1.0)
            kcn = qk_prep(k_c, k_norm_g[j], None, 1.0)
            k_all = jnp.concatenate([kn, kcn], axis=0)
            v_all = jnp.concatenate([v_l, v_c], axis=0)
            k_hm = k_all.reshape(-1, GQA_KV_HEADS, HEAD_DIM).transpose(1, 0, 2)
            v_t = v_all.T.reshape(GQA_KV_HEADS, HEAD_DIM, -1)
            att = gqa_attention(qn_t, k_hm, v_t)
            ssm = s5_mix(u_c, u_l, s5_a_re[j], s5_a_im[j], s5_log_dt[j], s5_b_re[j], s5_b_im[j],
                         s5_c_re[j], s5_c_im[j], s5_d[j], s5_glu_w[j], s5_glu_b[j])
            xs = outproj_ln(att, ssm, cd_w_out[j], xs, g1, ln_mix_g[i], ln_mix_b[i])
            assert not need_ctx

        moe_w = (router_w[i], router_bias[i], exp_w_gate[i], exp_w_up[i], exp_w_down[i],
                 sh_w_gate[i], sh_w_up[i], sh_w_down[i], ln_ffn_g[i], ln_ffn_b[i])
        stack2 = lambda lat, cx: jnp.stack([lat, cx])
        mod2 = (stack2(sc2, csc2), stack2(sh2, csh2), stack2(g2, cg2))
        if need_ctx:
            cs, xs = moe_layer(cs, xs, *mod2, n_ctx // MOE_TILE, *moe_w)
        else:
            (xs,) = moe_layer(xs, xs, *mod2, 0, *moe_w)
    return xs.reshape(b, l, d)
```

```python
import functools
import math

import jax
import jax.numpy as jnp
import numpy as np
from jax import lax
from jax.experimental import pallas as pl
from jax.experimental.pallas import tpu as pltpu

F32 = jnp.float32
BF16 = jnp.bfloat16
I32 = jnp.int32
HIGHEST = lax.Precision.HIGHEST

LANES = 128
SUBLANES = 8
VMEM_LIMIT = 56 * 1024 * 1024

D_MODEL = 1024
DEPTH = 2
GRID_W = 64
HEAD_DIM = 64
NA_HEADS = 8
NA_WIDTH = NA_HEADS * HEAD_DIM
NA_KH = 8
NA_KW = 16
HY_WIDTH = D_MODEL - NA_WIDTH
HY_BANDS = 16
HY_DECAY_PCT_MIN = 0.3
HY_DECAY_PCT_MAX = 1.5
HY_DECAY_TARGET = 1e-2
GQA_HEADS = 8
GQA_KV_HEADS = 2
GQA_WIDTH = GQA_HEADS * HEAD_DIM
GQA_KV_WIDTH = GQA_KV_HEADS * HEAD_DIM
ROPE_THETA = 10000.0
S5_WIDTH = D_MODEL - GQA_WIDTH
S5_GROUP = 16
S5_GROUPS = S5_WIDTH // S5_GROUP
S5_STATE = 64
N_EXPERTS = 256
TOP_K = 8
N_EXPERT_GROUPS = 8
TOPK_GROUPS = 4
EXPERT_FF = 256
ROUTED_SCALE = 2.5
EXPERT_BLOCK = 128
DEEPNORM_ALPHA = (2.0 * DEPTH) ** 0.25
LN_EPS = 1e-5
RMS_EPS = 1e-6

NEG_BIG = -1e30
NA_TILE_ROWS = 8
NA_KEY_ROWS = 16
NA_KEY_BLOCK_ROWS = 4
DFT_N1 = 128
S5_CHUNK = 16
S5_LANE_GROUPS = LANES // S5_GROUP
MOE_TILE = 256
EXPERT_LOOKAHEAD = 3
ROW_SLAB = D_MODEL // LANES

NT_DIMS = (((1,), (1,)), ((), ()))


def _cparams(sem, **kw):
    return pltpu.CompilerParams(dimension_semantics=sem, vmem_limit_bytes=VMEM_LIMIT, **kw)


def _dot(a, b, **kw):
    return jnp.dot(a, b, preferred_element_type=F32, **kw)


def _dot_nt(a, b):
    return lax.dot_general(a, b, NT_DIMS, preferred_element_type=F32)


def _row_tile(m, pref):
    return pref if m % pref == 0 else m


def _mod_kernel(c_ref, w_ref, b_ref, o_ref):
    cv = c_ref[...]
    s = cv * jax.nn.sigmoid(cv)
    o_ref[...] = _dot(s, w_ref[...], precision=HIGHEST) + b_ref[...]


def modulation_all(cmat, w_mod, b_mod):
    depth, d, n = w_mod.shape
    tn = 1536
    return pl.pallas_call(
        _mod_kernel,
        grid=(depth, n // tn),
        in_specs=[pl.BlockSpec((SUBLANES, d), lambda l, j: (0, 0)),
                  pl.BlockSpec((None, d, tn), lambda l, j: (l, 0, j)),
                  pl.BlockSpec((None, 1, tn), lambda l, j: (l, 0, j))],
        out_specs=pl.BlockSpec((None, SUBLANES, tn), lambda l, j: (l, 0, j)),
        out_shape=jax.ShapeDtypeStruct((depth, SUBLANES, n), F32),
        compiler_params=_cparams(("arbitrary", "arbitrary")),
        name="modulation",
    )(cmat, w_mod, b_mod.reshape(depth, 1, n))


def _proj_kernel(x_ref, sc_ref, sh_ref, w_ref, *o_refs, splits, scales):
    h = (x_ref[...] * (1.0 + sc_ref[...]) + sh_ref[...]).astype(BF16)
    off = 0
    for o_ref, wd, sc in zip(o_refs, splits, scales):
        y = _dot(h, w_ref[:, off:off + wd])
        if sc != 1.0:
            y = y * sc
        o_ref[...] = y.astype(o_ref.dtype)
        off += wd


def mod_project(x, sc, sh, w, splits, dtypes, scales=None):
    m, d = x.shape
    n = w.shape[1]
    assert sum(splits) == n
    scales = scales or (1.0,) * len(splits)
    tm = _row_tile(m, 512)
    kern = functools.partial(_proj_kernel, splits=tuple(splits), scales=tuple(scales))
    return pl.pallas_call(
        kern,
        grid=(m // tm,),
        in_specs=[pl.BlockSpec((tm, d), lambda i: (i, 0)),
                  pl.BlockSpec((1, d), lambda i: (0, 0)),
                  pl.BlockSpec((1, d), lambda i: (0, 0)),
                  pl.BlockSpec((d, n), lambda i: (0, 0))],
        out_specs=[pl.BlockSpec((tm, wd), lambda i: (i, 0)) for wd in splits],
        out_shape=[jax.ShapeDtypeStruct((m, wd), dt) for wd, dt in zip(splits, dtypes)],
        compiler_params=_cparams(("parallel",)),
        name="mod_project",
    )(x, sc, sh, w.astype(BF16))


def _layer_norm_rows(r, g, b):
    mu = jnp.mean(r, axis=-1, keepdims=True)
    c = r - mu
    var = jnp.mean(c * c, axis=-1, keepdims=True)
    return c * lax.rsqrt(var + LN_EPS) * g + b


def _outproj_ln_kernel(a_ref, b_ref, w_ref, x_ref, gate_ref, g_ref, beta_ref, o_ref):
    ka = a_ref.shape[1]
    y = _dot(a_ref[...], w_ref[:ka, :]) + _dot(b_ref[...], w_ref[ka:, :])
    r = DEEPNORM_ALPHA * x_ref[...] + gate_ref[...] * y
    o_ref[...] = _layer_norm_rows(r, g_ref[...], beta_ref[...])


def outproj_ln(a, b, w, x, gate, g, beta):
    m, d = x.shape
    ka, kb = a.shape[1], b.shape[1]
    tm = _row_tile(m, 512)
    row = lambda i: (i, 0)
    fixed = lambda i: (0, 0)
    return pl.pallas_call(
        _outproj_ln_kernel,
        grid=(m // tm,),
        in_specs=[pl.BlockSpec((tm, ka), row), pl.BlockSpec((tm, kb), row),
                  pl.BlockSpec((ka + kb, d), fixed), pl.BlockSpec((tm, d), row),
                  pl.BlockSpec((1, d), fixed), pl.BlockSpec((1, d), fixed), pl.BlockSpec((1, d), fixed)],
        out_specs=pl.BlockSpec((tm, d), row),
        out_shape=jax.ShapeDtypeStruct((m, d), F32),
        compiler_params=_cparams(("parallel",)),
        name="outproj_ln",
    )(a, b, w.astype(BF16), x, gate, g.reshape(1, d), beta.reshape(1, d))


def _na_bias_table(rpb, rows):
    h = rpb.shape[0]
    ri = np.arange(NA_TILE_ROWS)
    kr = np.arange(NA_KEY_ROWS)
    c = np.arange(GRID_W)
    cs = np.clip(c - NA_KW // 2, 0, GRID_W - NA_KW)
    vc = (c[None, :] >= cs[:, None]) & (c[None, :] < cs[:, None] + NA_KW)
    dc = np.clip(c[None, :] - c[:, None] + NA_KW - 1, 0, 2 * NA_KW - 2)
    pick = (dc.reshape(-1)[:, None] == np.arange(2 * NA_KW - 1)[None, :]).astype(np.float32)
    colb = jnp.einsum('qb,hab->haq', pick, rpb, precision=HIGHEST).reshape(h, 2 * NA_KH - 1, GRID_W, GRID_W)
    colb = jnp.where(vc[None, None], colb, NEG_BIG)

    def case(t):
        start = min(max(NA_TILE_ROWS * t - NA_KH // 2, 0), rows - NA_KEY_ROWS)
        r = NA_TILE_ROWS * t + ri
        rs = np.clip(r - NA_KH // 2, 0, rows - NA_KH)
        krow = start + kr
        vr = (krow[None, :] >= rs[:, None]) & (krow[None, :] < rs[:, None] + NA_KH)
        dr = np.clip(krow[None, :] - r[:, None] + NA_KH - 1, 0, 2 * NA_KH - 2)
        b = jnp.take(colb, dr.reshape(-1), axis=1).reshape(h, NA_TILE_ROWS, NA_KEY_ROWS, GRID_W, GRID_W)
        b = jnp.where(vr[None, :, :, None, None], b, NEG_BIG)
        b = b.transpose(0, 1, 3, 2, 4)
        return b.reshape(h // 2, 2, NA_TILE_ROWS * GRID_W, NA_KEY_ROWS * GRID_W)

    n_tiles = rows // NA_TILE_ROWS
    return jnp.stack([case(0), case(1), case(n_tiles - 1)])


def _pair_masks(shape):
    lane = lax.broadcasted_iota(I32, shape, 1)
    return lane < HEAD_DIM


def _na_kernel(q_ref, k0, k1, k2, k3, v0, v1, v2, v3, kc_ref, vc_ref, bias_ref, o_ref):
    q = q_ref[...]
    lo = _pair_masks(q.shape)
    ks = (k0, k1, k2, k3)
    vs = (v0, v1, v2, v3)
    kb = k0.shape[0]
    outs = []
    for hh in range(2):
        qh = jnp.where(lo if hh == 0 else jnp.logical_not(lo), q, jnp.zeros_like(q))
        s = [_dot_nt(qh, ks[i][...]) + bias_ref[hh, :, i * kb:(i + 1) * kb] for i in range(4)]
        s.append(_dot_nt(qh, kc_ref[...]))
        m = s[0].max(axis=1, keepdims=True)
        for si in s[1:]:
            m = jnp.maximum(m, si.max(axis=1, keepdims=True))
        p = [jnp.exp(si - m) for si in s]
        l = p[0].sum(axis=1, keepdims=True)
        for pi in p[1:]:
            l = l + pi.sum(axis=1, keepdims=True)
        acc = _dot(p[4].astype(BF16), vc_ref[...])
        for i in range(4):
            acc = acc + _dot(p[i].astype(BF16), vs[i][...])
        outs.append(acc / l)
    o_ref[...] = jnp.where(lo, outs[0], outs[1]).astype(o_ref.dtype)


def neighbourhood_attention(q, k, v, k_ctx, v_ctx, rpb):
    l, w = q.shape
    rows = l // GRID_W
    n_tiles = rows // NA_TILE_ROWS
    assert n_tiles >= 3 and rows % NA_TILE_ROWS == 0
    n_ctx = k_ctx.shape[0]
    tq = NA_TILE_ROWS * GRID_W
    kb = NA_KEY_BLOCK_ROWS * GRID_W
    n_kblk = rows // NA_KEY_BLOCK_ROWS
    bias = _na_bias_table(rpb, rows)
    pair_w = 2 * HEAD_DIM

    def kv_spec(i):
        def imap(p, t):
            start = jnp.clip(2 * t - 1, 0, n_kblk - 4)
            return (start + i, p)
        return pl.BlockSpec((kb, pair_w), imap)

    def bias_map(p, t):
        case = jnp.where(t == 0, 0, jnp.where(t == n_tiles - 1, 2, 1))
        return (case, p, 0, 0, 0)

    return pl.pallas_call(
        _na_kernel,
        grid=(w // pair_w, n_tiles),
        in_specs=[pl.BlockSpec((tq, pair_w), lambda p, t: (t, p))]
                 + [kv_spec(i) for i in range(4)] + [kv_spec(i) for i in range(4)]
                 + [pl.BlockSpec((n_ctx, pair_w), lambda p, t: (0, p)),
                    pl.BlockSpec((n_ctx, pair_w), lambda p, t: (0, p)),
                    pl.BlockSpec((None, None, 2, tq, NA_KEY_ROWS * GRID_W), bias_map)],
        out_specs=pl.BlockSpec((tq, pair_w), lambda p, t: (t, p)),
        out_shape=jax.ShapeDtypeStruct((l, w), BF16),
        compiler_params=_cparams(("parallel", "parallel")),
        name="neighbourhood_attention",
    )(q, k, k, k, k, v, v, v, v, k_ctx, v_ctx, bias)


def _ctx_attn_kernel(q_ref, k_ref, v_ref, o_ref):
    q = q_ref[...]
    lo = _pair_masks(q.shape)
    outs = []
    for hh in range(2):
        qh = jnp.where(lo if hh == 0 else jnp.logical_not(lo), q, jnp.zeros_like(q))
        s = _dot_nt(qh, k_ref[...])
        p = jnp.exp(s - s.max(axis=1, keepdims=True))
        outs.append(_dot(p.astype(BF16), v_ref[...]) / p.sum(axis=1, keepdims=True))
    o_ref[...] = jnp.where(lo, outs[0], outs[1]).astype(o_ref.dtype)


def context_attention(q, k, v):
    n, w = q.shape
    pair_w = 2 * HEAD_DIM
    spec = pl.BlockSpec((n, pair_w), lambda p: (0, p))
    return pl.pallas_call(
        _ctx_attn_kernel, grid=(w // pair_w,), in_specs=[spec, spec, spec], out_specs=spec,
        out_shape=jax.ShapeDtypeStruct((n, w), BF16),
        compiler_params=_cparams(("parallel",)), name="context_attention",
    )(q, k, v)


def _shortconv_kernel(u_ref, up_ref, un_ref, w_ref, b_ref, x0_ref, z_ref, *, n_tiles):
    i = pl.program_id(0)
    u = u_ref[...]
    tm = u.shape[0]
    prev_row = jnp.where(i > 0, up_ref[SUBLANES - 1:SUBLANES, :], 0.0)
    next_row = jnp.where(i < n_tiles - 1, un_ref[0:1, :], 0.0)
    row = lax.broadcasted_iota(I32, u.shape, 0)
    u_dn = jnp.where(row == 0, prev_row, pltpu.roll(u, 1, 0))
    u_up = jnp.where(row == tm - 1, next_row, pltpu.roll(u, tm - 1, 0))
    y = u_dn * w_ref[0:1, :] + u * w_ref[1:2, :] + u_up * w_ref[2:3, :] + b_ref[...]
    c = HY_WIDTH
    x0_ref[...] = y[:, :c]
    z_ref[...] = y[:, c:2 * c] * y[:, 2 * c:]


def hyena_gate(u, conv_w, conv_b):
    l, w3 = u.shape
    tm = _row_tile(l, 512)
    n_tiles = l // tm
    per = tm // SUBLANES
    last = l // SUBLANES - 1
    kern = functools.partial(_shortconv_kernel, n_tiles=n_tiles)
    return pl.pallas_call(
        kern,
        grid=(n_tiles,),
        in_specs=[pl.BlockSpec((tm, w3), lambda i: (i, 0)),
                  pl.BlockSpec((SUBLANES, w3), lambda i: (jnp.maximum(i * per - 1, 0), 0)),
                  pl.BlockSpec((SUBLANES, w3), lambda i: (jnp.minimum((i + 1) * per, last), 0)),
                  pl.BlockSpec((3, w3), lambda i: (0, 0)),
                  pl.BlockSpec((1, w3), lambda i: (0, 0))],
        out_specs=[pl.BlockSpec((tm, HY_WIDTH), lambda i: (i, 0))] * 2,
        out_shape=[jax.ShapeDtypeStruct((l, HY_WIDTH), F32)] * 2,
        compiler_params=_cparams(("parallel",)),
        name="hyena_gate",
    )(u, u, u, conv_w, conv_b.reshape(1, w3))


def _filter_kernel(bands_ref, w1t_ref, w1c_ref, w1s_ref, b1_ref, fr_ref, w2_ref, b2_ref, w3_ref, dl_ref,
                   taps_ref, asum_ref, *, l, tp):
    i = pl.program_id(0)
    hid_w = w2_ref.shape[0]
    c = HY_WIDTH
    denom = float(max(l - 1, 1))

    def pos(width):
        return (lax.broadcasted_iota(I32, (tp, width), 0) + i * tp).astype(F32)

    ang = (2.0 * math.pi / l) * pos(HY_BANDS) * bands_ref[...]
    pre = ((pos(hid_w) / denom) * w1t_ref[...]
           + _dot(jnp.cos(ang), w1c_ref[...], precision=HIGHEST)
           + _dot(-jnp.sin(ang), w1s_ref[...], precision=HIGHEST) + b1_ref[...])
    hid = jnp.sin(fr_ref[...] * pre)
    hid = jnp.sin(fr_ref[...] * (_dot(hid, w2_ref[...], precision=HIGHEST) + b2_ref[...]))
    taps = _dot(hid, w3_ref[...], precision=HIGHEST)
    pc = pos(c)
    window = jnp.exp(-(pc / denom) * dl_ref[...])
    fwd = taps[:, :c] * window
    bwd = jnp.where(pc == 0.0, 0.0, taps[:, c:] * window)
    taps_ref[:, :c] = fwd
    taps_ref[:, c:] = bwd

    @pl.when(i == 0)
    def _():
        asum_ref[...] = jnp.zeros_like(asum_ref)

    asum_ref[...] += jnp.sum(jnp.abs(fwd) + jnp.abs(bwd), axis=0, keepdims=True)


def hyena_filter_taps(l, f_w1, f_b1, f_freq, f_w2, f_b2, f_w3):
    c = HY_WIDTH
    hid = f_w2.shape[0]
    tp = _row_tile(l, 1024)
    bands = jnp.linspace(1e-4, HY_BANDS - 1, HY_BANDS, dtype=F32).reshape(1, HY_BANDS)
    deltas = jnp.abs(jnp.linspace(math.log(HY_DECAY_TARGET) / HY_DECAY_PCT_MAX,
                                  math.log(HY_DECAY_TARGET) / HY_DECAY_PCT_MIN, c, dtype=F32)).reshape(1, c)
    fixed = lambda i: (0, 0)
    full = lambda a: pl.BlockSpec(a.shape, fixed)
    args = (bands, f_w1[0:1], f_w1[1:1 + HY_BANDS], f_w1[1 + HY_BANDS:], f_b1.reshape(1, hid),
            f_freq.reshape(1, hid), f_w2, f_b2.reshape(1, hid), f_w3, deltas)
    kern = functools.partial(_filter_kernel, l=l, tp=tp)
    return pl.pallas_call(
        kern,
        grid=(l // tp,),
        in_specs=[full(a) for a in args],
        out_specs=[pl.BlockSpec((tp, 2 * c), lambda i: (i, 0)), pl.BlockSpec((1, c), fixed)],
        out_shape=[jax.ShapeDtypeStruct((l, 2 * c), F32), jax.ShapeDtypeStruct((1, c), F32)],
        compiler_params=_cparams(("arbitrary",)),
        name="hyena_filter",
    )(*args)


def _dft_tables(l):
    n = 2 * l
    n1 = DFT_N1
    n2 = n // n1
    k1 = jnp.arange(n1)[:, None]
    m1 = jnp.arange(n1 // 2)[None, :]
    ph1 = (2.0 * math.pi / n1) * ((k1 * m1) % n1).astype(F32)
    d1 = jnp.stack([jnp.cos(ph1), -jnp.sin(ph1)], axis=1).reshape(2 * n1, n1 // 2)
    d1_inv = d1.T
    j2 = jnp.arange(n2)
    ph2 = (2.0 * math.pi / n2) * ((j2[:, None] * j2[None, :]) % n2).astype(F32)
    cs, sn = jnp.cos(ph2), jnp.sin(ph2)
    f2 = jnp.concatenate([jnp.concatenate([cs, sn], axis=1), jnp.concatenate([-sn, cs], axis=1)], axis=0)
    pht = (2.0 * math.pi / n) * ((jnp.arange(n1)[:, None] * j2[None, :]) % n).astype(F32)[:, :, None]
    return d1.astype(BF16), d1_inv.astype(BF16), f2.astype(BF16), f2.T.astype(BF16), jnp.cos(pht), -jnp.sin(pht)


def _dft1_kernel(d_ref, x_ref, o_ref):
    n1h, m, c = x_ref.shape
    y = _dot(d_ref[...], x_ref[...].reshape(n1h * m, c).astype(BF16))
    o_ref[...] = y.reshape(o_ref.shape).astype(o_ref.dtype)


def dft_stage1(x, d1, n2):
    l, c = x.shape
    n1h = d1.shape[1]
    m = SUBLANES
    dk = jnp.kron(d1.astype(F32), jnp.eye(m, dtype=F32)).astype(BF16)
    out = pl.pallas_call(
        _dft1_kernel,
        grid=(n2 // m,),
        in_specs=[pl.BlockSpec(dk.shape, lambda j: (0, 0)), pl.BlockSpec((n1h, m, c), lambda j: (0, j, 0))],
        out_specs=pl.BlockSpec((d1.shape[0], m, c), lambda j: (0, j, 0)),
        out_shape=jax.ShapeDtypeStruct((d1.shape[0], n2, c), F32),
        compiler_params=_cparams(("parallel",)),
        name="dft_stage1",
    )(dk, x.reshape(n1h, n2, c))
    return out.reshape(d1.shape[0] // 2, 2, n2, c)


def _twiddled_stage2(f_ref, a_ref, twr_ref, twi_ref):
    shape = a_ref.shape[1:]
    twr = jnp.broadcast_to(twr_ref[...], shape)
    twi = jnp.broadcast_to(twi_ref[...], shape)
    ar, ai = a_ref[0].astype(F32), a_ref[1].astype(F32)
    a = jnp.concatenate([ar * twr - ai * twi, ar * twi + ai * twr], axis=0).astype(BF16)
    return _dot(f_ref[...], a), twr, twi


def _filter_spectrum_kernel(f_ref, a_ref, twr_ref, twi_ref, h_ref):
    n2 = a_ref.shape[1]
    c = h_ref.shape[2]
    x, _, _ = _twiddled_stage2(f_ref, a_ref, twr_ref, twi_ref)
    h_ref[0] = x[:n2, :c] + x[:n2, c:]
    h_ref[1] = x[n2:, :c] - x[n2:, c:]


def filter_spectrum(a_taps, f2, twr, twi):
    n1, _, n2, c2 = a_taps.shape
    c = c2 // 2
    tw = pl.BlockSpec((None, n2, 1), lambda i: (i, 0, 0))
    return pl.pallas_call(
        _filter_spectrum_kernel,
        grid=(n1,),
        in_specs=[pl.BlockSpec((2 * n2, 2 * n2), lambda i: (0, 0)),
                  pl.BlockSpec((None, 2, n2, c2), lambda i: (i, 0, 0, 0)), tw, tw],
        out_specs=pl.BlockSpec((None, 2, n2, c), lambda i: (i, 0, 0, 0)),
        out_shape=jax.ShapeDtypeStruct((n1, 2, n2, c), F32),
        compiler_params=_cparams(("parallel",)),
        name="filter_spectrum",
    )(f2, a_taps, twr, twi)


def _spectral_mix_kernel(f_ref, fi_ref, a_ref, h_ref, twr_ref, twi_ref, o_ref):
    n2 = a_ref.shape[1]
    x, twr, twi = _twiddled_stage2(f_ref, a_ref, twr_ref, twi_ref)
    xr, xi = x[:n2], x[n2:]
    hr, hi = h_ref[0], h_ref[1]
    y = jnp.concatenate([xr * hr - xi * hi, xr * hi + xi * hr], axis=0).astype(BF16)
    b = _dot(fi_ref[...], y)
    br, bi = b[:n2], b[n2:]
    o_ref[0] = (br * twr + bi * twi).astype(o_ref.dtype)
    o_ref[1] = (bi * twr - br * twi).astype(o_ref.dtype)


def spectral_mix(a_z, h, f2, f2_inv, twr, twi):
    n1, _, n2, c = a_z.shape
    blk = pl.BlockSpec((None, 2, n2, c), lambda i: (i, 0, 0, 0))
    mat = pl.BlockSpec((2 * n2, 2 * n2), lambda i: (0, 0))
    tw = pl.BlockSpec((None, n2, 1), lambda i: (i, 0, 0))
    return pl.pallas_call(
        _spectral_mix_kernel,
        grid=(n1,),
        in_specs=[mat, mat, blk, blk, tw, tw],
        out_specs=blk,
        out_shape=jax.ShapeDtypeStruct((n1, 2, n2, c), BF16),
        compiler_params=_cparams(("parallel",)),
        name="spectral_mix",
    )(f2, f2_inv, a_z, h, twr, twi)


def _hyena_out_kernel(di_ref, b_ref, x0_ref, z_ref, inv_ref, skip_ref, o_ref, *, inv_n):
    rows, m, c = b_ref.shape
    conv = (_dot(di_ref[...], b_ref[...].reshape(rows * m, c)) * inv_n).reshape(o_ref.shape)
    o_ref[...] = (x0_ref[...] * (conv * inv_ref[...] + z_ref[...] * skip_ref[...])).astype(o_ref.dtype)


def hyena_output(b, d1_inv, x0, z, inv_norm, skip):
    n1, _, n2, c = b.shape
    l = x0.shape[0]
    n1h = n1 // 2
    m = 2 * SUBLANES
    dk = jnp.kron(d1_inv.astype(F32), jnp.eye(m, dtype=F32)).astype(BF16)
    kern = functools.partial(_hyena_out_kernel, inv_n=1.0 / (2 * l))
    tile = pl.BlockSpec((n1h, m, c), lambda j: (0, j, 0))
    vec = pl.BlockSpec((1, 1, c), lambda j: (0, 0, 0))
    out = pl.pallas_call(
        kern,
        grid=(n2 // m,),
        in_specs=[pl.BlockSpec(dk.shape, lambda j: (0, 0)),
                  pl.BlockSpec((2 * n1, m, c), lambda j: (0, j, 0)),
                  tile, tile, vec, vec],
        out_specs=tile,
        out_shape=jax.ShapeDtypeStruct((n1h, n2, c), BF16),
        compiler_params=_cparams(("parallel",)),
        name="hyena_output",
    )(dk, b.reshape(2 * n1, n2, c), x0.reshape(n1h, n2, c), z.reshape(n1h, n2, c),
      inv_norm.reshape(1, 1, c), skip.reshape(1, 1, c))
    return out.reshape(l, c)


def _small_conv_kernel(d_ref, di_ref, z_ref, taps_ref, x0_ref, inv_ref, skip_ref, o_ref, *, inv_n):
    c = z_ref.shape[1]
    n = d_ref.shape[0] // 2
    zs = _dot(d_ref[...], z_ref[...], precision=HIGHEST)
    ts = _dot(d_ref[...], taps_ref[...], precision=HIGHEST)
    hr = ts[:n, :c] + ts[:n, c:]
    hi = ts[n:, :c] - ts[n:, c:]
    zr, zi = zs[:n], zs[n:]
    y = jnp.concatenate([zr * hr - zi * hi, zr * hi + zi * hr], axis=0)
    conv = _dot(di_ref[...], y, precision=HIGHEST) * inv_n
    o_ref[...] = (x0_ref[...] * (conv * inv_ref[...] + z_ref[...] * skip_ref[...])).astype(o_ref.dtype)


def hyena_output_short(z, taps, x0, inv_norm, skip):
    l, c = z.shape
    n = 2 * l
    ph = (2.0 * math.pi / n) * ((jnp.arange(n)[:, None] * jnp.arange(l)[None, :]) % n).astype(F32)
    d = jnp.concatenate([jnp.cos(ph), -jnp.sin(ph)], axis=0)
    di = jnp.concatenate([jnp.cos(ph), -jnp.sin(ph)], axis=0).T
    args = (d, di, z, taps, x0, inv_norm, skip.reshape(1, c))
    kern = functools.partial(_small_conv_kernel, inv_n=1.0 / n)
    return pl.pallas_call(
        kern,
        grid=(1,),
        in_specs=[pl.BlockSpec(a.shape, lambda i: (0, 0)) for a in args],
        out_specs=pl.BlockSpec((l, c), lambda i: (0, 0)),
        out_shape=jax.ShapeDtypeStruct((l, c), BF16),
        compiler_params=_cparams(("arbitrary",)),
        name="hyena_output_short",
    )(*args)


def hyena_long(u, conv_w, conv_b, f_w1, f_b1, f_freq, f_w2, f_b2, f_w3, skip):
    l = u.shape[0]
    x0, z = hyena_gate(u, conv_w, conv_b)
    taps, asum = hyena_filter_taps(l, f_w1, f_b1, f_freq, f_w2, f_b2, f_w3)
    inv_norm = 1.0 / asum
    if 2 * l < DFT_N1 * SUBLANES * 2:
        return hyena_output_short(z, taps, x0, inv_norm, skip)
    n2 = 2 * l // DFT_N1
    d1, d1_inv, f2, f2_inv, twr, twi = _dft_tables(l)
    h = filter_spectrum(dft_stage1(taps, d1, n2), f2, twr, twi)
    b = spectral_mix(dft_stage1(z, d1, n2), h, f2, f2_inv, twr, twi)
    return hyena_output(b, d1_inv, x0, z, inv_norm, skip)


def _head_sumsq(x, bd):
    sq = x * x
    hi = sq.astype(BF16)
    lo = (sq - hi.astype(F32)).astype(BF16)
    return _dot(hi, bd) + _dot(lo, bd)


def _qk_prep_kernel(x_ref, gain_ref, bd_ref, *rest, rope, scale, transposed):
    x = x_ref[...]
    w = x.shape[1]
    ms = _head_sumsq(x, bd_ref[...]) * (1.0 / HEAD_DIM)
    xn = x * lax.rsqrt(ms + RMS_EPS) * gain_ref[...]
    if rope:
        cos_ref, sin_ref, o_ref = rest
        reps = w // cos_ref.shape[1]
        cos = jnp.tile(cos_ref[...], (1, reps)) if reps > 1 else cos_ref[...]
        sin = jnp.tile(sin_ref[...], (1, reps)) if reps > 1 else sin_ref[...]
        lane = lax.broadcasted_iota(I32, x.shape, 1)
        partner = jnp.where(lane % 2 == 0, pltpu.roll(xn, w - 1, 1), pltpu.roll(xn, 1, 1))
        xn = xn * cos + partner * sin
    else:
        (o_ref,) = rest
    if scale != 1.0:
        xn = xn * scale
    if transposed:
        xn = xn.T
    o_ref[...] = xn.astype(o_ref.dtype)


def _rope_tables(l):
    half = HEAD_DIM // 2
    inv_freq = ROPE_THETA ** (-jnp.arange(0, half, 2, dtype=F32) / half)
    t = jnp.arange(l)
    row = (t // GRID_W).astype(F32)
    col = (t % GRID_W).astype(F32)
    ang = jnp.concatenate([jnp.repeat(row[:, None] * inv_freq[None], 2, axis=1),
                           jnp.repeat(col[:, None] * inv_freq[None], 2, axis=1)], axis=1)
    sign = jnp.where(jnp.arange(HEAD_DIM) % 2 == 0, -1.0, 1.0).astype(F32)
    cos = jnp.tile(jnp.cos(ang), (1, 2))
    sin = jnp.tile(jnp.sin(ang) * sign[None], (1, 2))
    return cos, sin


def qk_prep(x, gain, rope_tabs, scale, transposed=False):
    l, w = x.shape
    tm = _row_tile(l, 512)
    head = jnp.arange(w) // HEAD_DIM
    bd = (head[:, None] == head[None, :]).astype(BF16)
    gain_t = jnp.tile(gain.reshape(1, HEAD_DIM), (1, w // HEAD_DIM))
    row = lambda i: (i, 0)
    fixed = lambda i: (0, 0)
    in_specs = [pl.BlockSpec((tm, w), row), pl.BlockSpec((1, w), fixed), pl.BlockSpec((w, w), fixed)]
    args = [x, gain_t, bd]
    if rope_tabs is not None:
        in_specs += [pl.BlockSpec((tm, 2 * HEAD_DIM), row)] * 2
        args += list(rope_tabs)
    kern = functools.partial(_qk_prep_kernel, rope=rope_tabs is not None, scale=scale, transposed=transposed)
    if transposed:
        out_spec, out_shape = pl.BlockSpec((w, tm), lambda i: (0, i)), (w, l)
    else:
        out_spec, out_shape = pl.BlockSpec((tm, w), row), (l, w)
    return pl.pallas_call(
        kern, grid=(l // tm,), in_specs=in_specs, out_specs=out_spec,
        out_shape=jax.ShapeDtypeStruct(out_shape, BF16),
        compiler_params=_cparams(("parallel",)), name="qk_prep",
    )(*args)


def _flash_kernel(qt_ref, k_ref, vt_ref, o_ref, qs_ref, s_ref, m_ref, l_ref, acc_ref, *, tk, nk):
    dh = HEAD_DIM
    rep = qt_ref.shape[0] // dh
    tq = qt_ref.shape[1]
    for r in range(rep):
        qs_ref[:, r * tq:(r + 1) * tq] = qt_ref[r * dh:(r + 1) * dh, :]
    m_ref[...] = jnp.full(m_ref.shape, NEG_BIG, F32)
    l_ref[...] = jnp.zeros(l_ref.shape, F32)
    acc_ref[...] = jnp.zeros(acc_ref.shape, F32)

    def scores(j):
        start = pl.multiple_of(jnp.minimum(j, nk - 1) * tk, tk)
        return _dot(k_ref[pl.ds(start, tk), :], qs_ref[...])

    def absorb(j, s):
        start = pl.multiple_of(j * tk, tk)
        m_old = m_ref[...]
        m_new = jnp.maximum(m_old, s.max(axis=0, keepdims=True))
        alpha = jnp.exp2(m_old - m_new)
        p = jnp.exp2(s - m_new)
        l_ref[...] = alpha * l_ref[...] + p.sum(axis=0, keepdims=True)
        acc_ref[...] = alpha * acc_ref[...] + _dot(vt_ref[:, pl.ds(start, tk)], p.astype(BF16))
        m_ref[...] = m_new

    s_ref[0] = scores(0)

    def body(i, carry):
        j = 2 * i
        s_ref[1] = scores(j + 1)
        absorb(j, s_ref[0])
        s_ref[0] = scores(j + 2)
        absorb(j + 1, s_ref[1])
        return carry

    lax.fori_loop(0, nk // 2, body, 0)
    if nk % 2:
        absorb(nk - 1, s_ref[0])
    out = acc_ref[...] / l_ref[...]
    for r in range(rep):
        o_ref[:, r * dh:(r + 1) * dh] = out[:, r * tq:(r + 1) * tq].T.astype(o_ref.dtype)


def _kv_chunk(lk):
    for tiles in (5, 4, 3, 2, 1):
        if lk % (tiles * 256) == 0:
            return tiles * 256
    raise ValueError(lk)


def gqa_attention(q_t, k_hm, v_t):
    wq, l = q_t.shape
    hkv, lk, dh = k_hm.shape
    wg = wq // hkv
    rep = wg // dh
    tq = _row_tile(l, 256)
    tk = _kv_chunk(lk)
    kern = functools.partial(_flash_kernel, tk=tk, nk=lk // tk)
    return pl.pallas_call(
        kern,
        grid=(hkv, l // tq),
        in_specs=[pl.BlockSpec((wg, tq), lambda g, i: (g, i)),
                  pl.BlockSpec((None, lk, dh), lambda g, i: (g, 0, 0)),
                  pl.BlockSpec((None, dh, lk), lambda g, i: (g, 0, 0))],
        out_specs=pl.BlockSpec((tq, wg), lambda g, i: (i, g)),
        out_shape=jax.ShapeDtypeStruct((l, wq), BF16),
        scratch_shapes=[pltpu.VMEM((dh, rep * tq), BF16), pltpu.VMEM((2, tk, rep * tq), F32),
                        pltpu.VMEM((1, rep * tq), F32), pltpu.VMEM((1, rep * tq), F32),
                        pltpu.VMEM((dh, rep * tq), F32)],
        compiler_params=_cparams(("parallel", "parallel")),
        name="gqa_attention",
    )(q_t, k_hm, v_t)


def _cmul(ar, ai, br, bi):
    return ar * br - ai * bi, ar * bi + ai * br


def _s5_operators(a_re, a_im, log_dt, b_re, b_im, c_re, c_im):
    t = S5_CHUNK
    gs = S5_GROUP
    hp = dict(precision=HIGHEST)
    dt = jnp.exp(log_dt)[..., None]
    zr, zi = a_re * dt, a_im * dt
    er = jnp.exp(zr)
    abr, abi = er * jnp.cos(zi), er * jnp.sin(zi)
    den = a_re * a_re + a_im * a_im
    fr = ((abr - 1.0) * a_re + abi * a_im) / den
    fi = (abi * a_re - (abr - 1.0) * a_im) / den
    bbr, bbi = _cmul(fr[..., None], fi[..., None], b_re, b_im)
    tau = jnp.arange(t + 1, dtype=F32)
    pr = jnp.exp(zr[..., None] * tau) * jnp.cos(zi[..., None] * tau)
    pi = jnp.exp(zr[..., None] * tau) * jnp.sin(zi[..., None] * tau)
    car, cai = _cmul(c_re[..., None], c_im[..., None], pr[:, :, None, :, :t], pi[:, :, None, :, :t])
    ktap = (jnp.einsum('dgqpt,dgpk->dgtqk', car, bbr, **hp) - jnp.einsum('dgqpt,dgpk->dgtqk', cai, bbi, **hp))
    ktp = jnp.concatenate([jnp.zeros_like(ktap), ktap], axis=2)
    win = jnp.stack([ktp[:, :, t - i:2 * t - i] for i in range(t)], axis=2)
    m_tot = (win[0] + win[1].transpose(0, 2, 1, 3, 4)).transpose(0, 1, 4, 2, 3)
    pw_r = jnp.stack([pr[0, ..., t - 1::-1], pr[1, ..., :t]])
    pw_i = jnp.stack([pi[0, ..., t - 1::-1], pi[1, ..., :t]])
    wr, wi = _cmul(pw_r[..., None], pw_i[..., None], bbr[:, :, :, None, :], bbi[:, :, :, None, :])
    w_in = jnp.stack([wr[0], wi[0], wr[1], wi[1]]).transpose(1, 3, 4, 0, 2)
    pv_r = jnp.stack([pr[0, ..., 1:], pr[1, ..., t:0:-1]])
    pv_i = jnp.stack([pi[0, ..., 1:], pi[1, ..., t:0:-1]])
    vr, vi = _cmul(c_re[..., None], c_im[..., None], pv_r[:, :, None], pv_i[:, :, None])
    v_out = jnp.stack([vr[0], -vi[0], vr[1], -vi[1]]).transpose(1, 0, 3, 4, 2)

    nb = S5_GROUPS // S5_LANE_GROUPS
    eye = jnp.eye(S5_LANE_GROUPS, dtype=F32)
    blk = lambda x: x.reshape((nb, S5_LANE_GROUPS) + x.shape[1:])
    m6, w6, v6 = blk(m_tot), blk(w_in), blk(v_out)
    m_op = (m6[:, :, :, :, :, None, :] * eye[None, :, None, None, None, :, None]).transpose(0, 2, 1, 3, 4, 5, 6)
    m_op = m_op.reshape(nb, t * LANES, t * LANES)
    w_op = (w6[:, :, :, :, :, None, :] * eye[None, :, None, None, None, :, None]).transpose(0, 2, 1, 3, 4, 5, 6)
    w_op = w_op.reshape(nb, t * LANES, 4 * S5_LANE_GROUPS * S5_STATE)
    v_op = (v6[:, :, :, :, :, None, :] * eye[None, :, None, None, None, :, None]).transpose(0, 2, 1, 3, 4, 5, 6)
    v_op = v_op.reshape(nb, 4 * S5_LANE_GROUPS * S5_STATE, t * LANES)
    return m_op.astype(BF16), w_op.astype(BF16), v_op.astype(BF16), pr[..., t], pi[..., t]


def _s5_layout_kernel(*refs):
    *u_refs, o_ref = refs
    rows = o_ref.shape[0]
    t = S5_CHUNK
    for b, u_ref in enumerate(u_refs):
        for i in range(t):
            o_ref[:, (b * t + i) * LANES:(b * t + i + 1) * LANES] = u_ref[pl.ds(i, rows, stride=t), :].astype(o_ref.dtype)


def _s5_state_in_kernel(u_ref, w_ref, *e_refs):
    e = _dot(u_ref[...], w_ref[...])
    q = e.shape[1] // len(e_refs)
    for part, e_ref in enumerate(e_refs):
        e_ref[...] = e[:, part * q:(part + 1) * q]


def _s5_scan_kernel(ar_ref, ai_ref, er_ref, ei_ref, sr_ref, si_ref, cr_ref, ci_ref, *, reverse):
    @pl.when(pl.program_id(0) == 0)
    def _():
        cr_ref[...] = jnp.zeros_like(cr_ref)
        ci_ref[...] = jnp.zeros_like(ci_ref)

    ar, ai = ar_ref[...], ai_ref[...]
    n = er_ref.shape[0]

    def body(k, carry):
        c = n - 1 - k if reverse else k
        sr, si = carry
        sr_ref[c] = sr
        si_ref[c] = si
        return ar * sr - ai * si + er_ref[c], ar * si + ai * sr + ei_ref[c]

    sr, si = lax.fori_loop(0, n, body, (cr_ref[...], ci_ref[...]))
    cr_ref[...] = sr
    ci_ref[...] = si


def _s5_out_kernel(u_ref, fr_ref, fi_ref, br_ref, bi_ref, m_ref, v_ref, y_ref):
    s = jnp.concatenate([fr_ref[...], fi_ref[...], br_ref[...], bi_ref[...]], axis=1).astype(BF16)
    y_ref[...] = _dot(u_ref[...], m_ref[...]) + _dot(s, v_ref[...])


def _s5_readout_kernel(y_ref, u_ref, d_ref, w_ref, b_ref, o_ref, ynat_ref):
    rows = y_ref.shape[0]
    t = S5_CHUNK
    nb = ynat_ref.shape[0]
    for b in range(nb):
        for i in range(t):
            ynat_ref[b, pl.ds(i, rows, stride=t), :] = y_ref[:, (b * t + i) * LANES:(b * t + i + 1) * LANES]
    y = jnp.concatenate([ynat_ref[b] for b in range(nb)], axis=1) + d_ref[...] * u_ref[...]
    y = 0.5 * y * (1.0 + jnp.tanh(math.sqrt(2.0 / math.pi) * (y + 0.044715 * (y * y * y))))
    gate = jax.nn.sigmoid(_dot(y.astype(BF16), w_ref[...]) + b_ref[...])
    o_ref[...] = (y * gate).astype(o_ref.dtype)


def s5_mix(u_ctx, u_lat, a_re, a_im, log_dt, b_re, b_im, c_re, c_im, d_skip, glu_w, glu_b):
    t = S5_CHUNK
    n_ctx, w = u_ctx.shape
    l = u_lat.shape[0]
    n_tok = n_ctx + l
    nch = n_tok // t
    tc = n_ctx // t
    n_tiles = nch // tc
    assert n_ctx == tc * t and tc % (2 * SUBLANES) == 0 and l % (tc * t) == 0
    nb = w // LANES
    cw = t * LANES
    sw = S5_LANE_GROUPS * S5_STATE
    n_state = S5_GROUPS * S5_STATE
    m_op, w_op, v_op, atr, ati = _s5_operators(a_re, a_im, log_dt, b_re, b_im, c_re, c_im)

    u_all = jnp.concatenate([u_ctx, u_lat], axis=0)
    u_ch = pl.pallas_call(
        _s5_layout_kernel,
        grid=(n_tiles,),
        in_specs=[pl.BlockSpec((tc * t, LANES), functools.partial(lambda b, i: (i, b), b)) for b in range(nb)],
        out_specs=pl.BlockSpec((tc, nb * cw), lambda i: (i, 0)),
        out_shape=jax.ShapeDtypeStruct((nch, nb * cw), BF16),
        compiler_params=_cparams(("parallel",)),
        name="s5_layout",
    )(*([u_all] * nb))

    u_blk = pl.BlockSpec((nch, cw), lambda b: (0, b))
    s_blk = pl.BlockSpec((nch, sw), lambda b: (0, b))
    states_in = pl.pallas_call(
        _s5_state_in_kernel,
        grid=(nb,),
        in_specs=[u_blk, pl.BlockSpec((None, cw, 4 * sw), lambda b: (b, 0, 0))],
        out_specs=[s_blk] * 4,
        out_shape=[jax.ShapeDtypeStruct((nch, n_state), F32)] * 4,
        compiler_params=_cparams(("parallel",)),
        name="s5_state_in",
    )(u_ch, w_op)

    slab = n_state // SUBLANES
    vec = pl.BlockSpec((SUBLANES, slab), lambda s: (0, 0))
    orders = (lambda s: (s, 0, 0),
              lambda s: (jnp.where(s == 0, 0, n_tiles - s), 0, 0))
    states = []
    for d in range(2):
        blk = pl.BlockSpec((tc, SUBLANES, slab), orders[d])
        s_re, s_im = pl.pallas_call(
            functools.partial(_s5_scan_kernel, reverse=bool(d)),
            grid=(n_tiles,),
            in_specs=[vec, vec, blk, blk],
            out_specs=[blk, blk],
            out_shape=[jax.ShapeDtypeStruct((nch, SUBLANES, slab), F32)] * 2,
            scratch_shapes=[pltpu.VMEM((SUBLANES, slab), F32)] * 2,
            compiler_params=_cparams(("arbitrary",)),
            name="s5_scan",
        )(atr[d].reshape(SUBLANES, slab), ati[d].reshape(SUBLANES, slab),
          states_in[2 * d].reshape(nch, SUBLANES, slab), states_in[2 * d + 1].reshape(nch, SUBLANES, slab))
        states += [s_re.reshape(nch, n_state), s_im.reshape(nch, n_state)]

    oc = cw // 4
    y_ch = pl.pallas_call(
        _s5_out_kernel,
        grid=(nb, cw // oc),
        in_specs=[pl.BlockSpec((nch, cw), lambda b, j: (0, b))] + [pl.BlockSpec((nch, sw), lambda b, j: (0, b))] * 4
                 + [pl.BlockSpec((None, cw, oc), lambda b, j: (b, 0, j)),
                    pl.BlockSpec((None, 4 * sw, oc), lambda b, j: (b, 0, j))],
        out_specs=pl.BlockSpec((nch, oc), lambda b, j: (0, b * (cw // oc) + j)),
        out_shape=jax.ShapeDtypeStruct((nch, nb * cw), F32),
        compiler_params=_cparams(("parallel", "parallel")),
        name="s5_out",
    )(u_ch, *states, m_op, v_op)

    row = lambda i: (i, 0)
    fixed = lambda i: (0, 0)
    return pl.pallas_call(
        _s5_readout_kernel,
        grid=(l // (tc * t),),
        in_specs=[pl.BlockSpec((tc, nb * cw), lambda i: (i + n_ctx // (tc * t), 0)), pl.BlockSpec((tc * t, w), row),
                  pl.BlockSpec((1, w), fixed), pl.BlockSpec((w, w), fixed), pl.BlockSpec((1, w), fixed)],
        out_specs=pl.BlockSpec((tc * t, w), row),
        out_shape=jax.ShapeDtypeStruct((l, w), BF16),
        scratch_shapes=[pltpu.VMEM((nb, tc * t, LANES), F32)],
        compiler_params=_cparams(("parallel",)),
        name="s5_readout",
    )(y_ch, u_lat, d_skip.reshape(1, w), glu_w.astype(BF16), glu_b.reshape(1, w))


def _first_max(vals, lane):
    m = vals.max(axis=1, keepdims=True)
    idx = jnp.where(vals == m, lane, jnp.int32(1 << 20)).min(axis=1, keepdims=True)
    return m, idx


def _stream_specs(n_ctx_tiles, tm, d):
    return (pl.BlockSpec((tm, d), lambda i: (jnp.clip(i, 0, max(n_ctx_tiles - 1, 0)), 0)),
            pl.BlockSpec((tm, d), lambda i: (jnp.maximum(i - n_ctx_tiles, 0), 0)))


def _stream_tile(xc_ref, xl_ref, n_ctx_tiles):
    if n_ctx_tiles == 0:
        return xl_ref[...]
    return jnp.where(pl.program_id(0) < n_ctx_tiles, xc_ref[...], xl_ref[...])


def _store_row_slabs(ref, x):
    rows = x.shape[0]
    for s in range(ROW_SLAB):
        ref[pl.ds(s, rows, stride=ROW_SLAB), :] = x[:, s * LANES:(s + 1) * LANES]


def _router_kernel(xc_ref, xl_ref, sc_ref, sh_ref, rw_ref, rb_ref, tri_ref,
                   hf_ref, te_ref, gt_ref, rk_ref, cnt_ref, run_ref, *, n_ctx_tiles):
    @pl.when(pl.program_id(0) == 0)
    def _():
        run_ref[...] = jnp.zeros_like(run_ref)

    hf = _stream_tile(xc_ref, xl_ref, n_ctx_tiles) * (1.0 + sc_ref[...]) + sh_ref[...]
    _store_row_slabs(hf_ref, hf)
    tm = hf.shape[0]
    scores = jax.nn.sigmoid(_dot(hf, rw_ref[...], precision=HIGHEST))
    biased = scores + rb_ref[...]
    lane = lax.broadcasted_iota(I32, (tm, N_EXPERTS), 1)
    grp = lane // (N_EXPERTS // N_EXPERT_GROUPS)
    lane_o = lax.broadcasted_iota(I32, (tm, LANES), 1)
    neg = jnp.float32(-jnp.inf)

    group_score = jnp.full((tm, LANES), neg, F32)
    for g in range(N_EXPERT_GROUPS):
        vals = jnp.where(grp == g, biased, neg)
        m1, i1 = _first_max(vals, lane)
        m2 = jnp.where(lane == i1, neg, vals).max(axis=1, keepdims=True)
        group_score = jnp.where(lane_o == g, m1 + m2, group_score)
    keep = jnp.zeros((tm, N_EXPERTS), F32)
    for _ in range(TOPK_GROUPS):
        _, gi = _first_max(group_score, lane_o)
        keep = jnp.where(grp == gi, 1.0, keep)
        group_score = jnp.where(lane_o == gi, neg, group_score)

    masked = jnp.where(keep > 0.0, biased, neg)
    member = jnp.zeros((tm, N_EXPERTS), F32)
    e_cols, g_cols = [], []
    for _ in range(TOP_K):
        _, ei = _first_max(masked, lane)
        hit = lane == ei
        g_cols.append(jnp.where(hit, scores, 0.0).sum(axis=1, keepdims=True))
        masked = jnp.where(hit, neg, masked)
        member = jnp.where(hit, 1.0, member)
        e_cols.append(ei)
    g_sum = g_cols[0]
    for gk in g_cols[1:]:
        g_sum = g_sum + gk

    before = _dot(tri_ref[...], member.astype(BF16)) + run_ref[...]
    te = jnp.zeros((tm, LANES), I32)
    rk = jnp.zeros((tm, LANES), I32)
    gt = jnp.zeros((tm, LANES), F32)
    for k in range(TOP_K):
        rank = jnp.where(lane == e_cols[k], before, 0.0).sum(axis=1, keepdims=True)
        te = jnp.where(lane_o == k, e_cols[k], te)
        rk = jnp.where(lane_o == k, rank.astype(I32), rk)
        gt = jnp.where(lane_o == k, ROUTED_SCALE * g_cols[k] / g_sum, gt)
    te_ref[...] = te
    rk_ref[...] = rk
    gt_ref[...] = gt
    run_ref[...] += member.sum(axis=0, keepdims=True)
    cnt_ref[...] = run_ref[...]


def moe_route(x_ctx, x_lat, sc2, sh2, n_ctx_tiles, router_w, router_bias):
    d = x_lat.shape[1]
    tm = MOE_TILE
    n = n_ctx_tiles * tm + x_lat.shape[0]
    tri = (jnp.arange(tm)[None, :] < jnp.arange(tm)[:, None]).astype(BF16)
    row = lambda i: (i, 0)
    fixed = lambda i: (0, 0)
    seg = lambda i: (jnp.where(i < n_ctx_tiles, 1, 0), 0, 0)
    kern = functools.partial(_router_kernel, n_ctx_tiles=n_ctx_tiles)
    return pl.pallas_call(
        kern,
        grid=(n // tm,),
        in_specs=[*_stream_specs(n_ctx_tiles, tm, d), pl.BlockSpec((None, 1, d), seg), pl.BlockSpec((None, 1, d), seg),
                  pl.BlockSpec((d, N_EXPERTS), fixed), pl.BlockSpec((1, N_EXPERTS), fixed),
                  pl.BlockSpec((tm, tm), fixed)],
        out_specs=[pl.BlockSpec((tm * ROW_SLAB, LANES), row), pl.BlockSpec((tm, LANES), row),
                   pl.BlockSpec((tm, LANES), row), pl.BlockSpec((tm, LANES), row),
                   pl.BlockSpec((1, N_EXPERTS), fixed)],
        out_shape=[jax.ShapeDtypeStruct((n * ROW_SLAB, LANES), F32), jax.ShapeDtypeStruct((n, LANES), I32),
                   jax.ShapeDtypeStruct((n, LANES), F32), jax.ShapeDtypeStruct((n, LANES), I32),
                   jax.ShapeDtypeStruct((1, N_EXPERTS), F32)],
        scratch_shapes=[pltpu.VMEM((1, N_EXPERTS), F32)],
        compiler_params=_cparams(("arbitrary",)),
        name="moe_route",
    )(x_ctx, x_lat, sc2, sh2, router_w, router_bias.reshape(1, N_EXPERTS), tri)


def _slots_kernel(te_ref, rk_ref, start_ref, o_ref):
    tm = te_ref.shape[0]
    lane = lax.broadcasted_iota(I32, (tm, N_EXPERTS), 1)
    lane_o = lax.broadcasted_iota(I32, (tm, LANES), 1)
    te = te_ref[...]
    out = rk_ref[...]
    for k in range(TOP_K):
        first = jnp.where(lane == te[:, k:k + 1], start_ref[...], 0.0).sum(axis=1, keepdims=True)
        out = jnp.where(lane_o == k, out + first.astype(I32), out)
    o_ref[...] = out


def moe_slots(te, rk, start):
    n = te.shape[0]
    tm = MOE_TILE
    row = lambda i: (i, 0)
    out = pl.pallas_call(
        _slots_kernel,
        grid=(n // tm,),
        in_specs=[pl.BlockSpec((tm, LANES), row), pl.BlockSpec((tm, LANES), row),
                  pl.BlockSpec((1, N_EXPERTS), lambda i: (0, 0))],
        out_specs=pl.BlockSpec((tm, LANES), row),
        out_shape=jax.ShapeDtypeStruct((n, LANES), I32),
        compiler_params=_cparams(("parallel",)),
        name="moe_slots",
    )(te, rk, start.astype(F32).reshape(1, N_EXPERTS))
    return out[:, :TOP_K].reshape(-1)


def _slab_rows(ref, row, n_rows):
    first = row * ROW_SLAB
    if not isinstance(first, int):
        first = pl.multiple_of(first, ROW_SLAB)
    return ref.at[pl.ds(first, n_rows * ROW_SLAB), :]


def _dispatch_kernel(dest_ref, hf_ref, xs_hbm, zbuf, sem, zsem, *, n_assign):
    n_rows = dest_ref.shape[0]

    @pl.when(pl.program_id(0) == 0)
    def _():
        zbuf[...] = jnp.zeros_like(zbuf)
        tail = pltpu.make_async_copy(zbuf, _slab_rows(xs_hbm, n_assign, EXPERT_BLOCK), zsem)
        tail.start()
        tail.wait()

    def body(r, carry):
        src = _slab_rows(hf_ref, r, 1)
        for k in range(TOP_K):
            pltpu.make_async_copy(src, _slab_rows(xs_hbm, dest_ref[r * TOP_K + k], 1), sem).start(priority=k % 2)
        return carry
    lax.fori_loop(0, n_rows // TOP_K, body, 0)
    for _ in range(TOP_K):
        pltpu.make_async_copy(hf_ref, hf_ref, sem).wait()


def moe_dispatch(dest, hf_slabs):
    n_assign = dest.shape[0]
    tm = MOE_TILE
    n_rows = tm * TOP_K
    kern = functools.partial(_dispatch_kernel, n_assign=n_assign)
    return pl.pallas_call(
        kern,
        grid=(n_assign // n_rows,),
        in_specs=[pl.BlockSpec((n_rows,), lambda i: (i,), memory_space=pltpu.SMEM),
                  pl.BlockSpec((tm * ROW_SLAB, LANES), lambda i: (i, 0))],
        out_specs=pl.BlockSpec(memory_space=pl.ANY),
        out_shape=jax.ShapeDtypeStruct(((n_assign + EXPERT_BLOCK) * ROW_SLAB, LANES), F32),
        scratch_shapes=[pltpu.VMEM((EXPERT_BLOCK * ROW_SLAB, LANES), F32), pltpu.SemaphoreType.DMA(()),
                        pltpu.SemaphoreType.DMA(())],
        compiler_params=_cparams(("arbitrary",)),
        name="moe_dispatch",
    )(dest, hf_slabs)


def _row_gather(idx_ref, base, count, src_hbm, dst, sem):
    group = 8

    def body(g, carry):
        for k in range(group):
            j = g * group + k
            cp = pltpu.make_async_copy(_slab_rows(src_hbm, idx_ref[base + j], 1), _slab_rows(dst, j, 1), sem)
            cp.start(priority=k % 2)
        return carry
    lax.fori_loop(0, count // group, body, 0)


def _wait_rows(dst, sem):
    pltpu.make_async_copy(dst, dst, sem).wait()


def _gathered_rows(buf, first, rows, stride):
    return jnp.concatenate(
        [buf[pl.ds(first * ROW_SLAB + s, rows, stride=stride * ROW_SLAB), :] for s in range(ROW_SLAB)], axis=1)


def _swiglu(x, wg, wu, wd):
    gate = _dot(x, wg)
    up = _dot(x, wu)
    return _dot((gate * jax.nn.sigmoid(gate) * up).astype(BF16), wd)


def _valid_row_copies(ybuf, ys_hbm, row0, valid, sem):
    out = [(valid == EXPERT_BLOCK,
            pltpu.make_async_copy(_slab_rows(ybuf, 0, EXPERT_BLOCK), _slab_rows(ys_hbm, row0, EXPERT_BLOCK), sem))]
    part = valid < EXPERT_BLOCK
    size = EXPERT_BLOCK // 2
    while size >= 1:
        off = valid & ~(2 * size - 1)
        out.append((part & ((valid & size) != 0),
                    pltpu.make_async_copy(_slab_rows(ybuf, off, size), _slab_rows(ys_hbm, row0 + off, size), sem)))
        size //= 2
    return out


def _expert_kernel(be_ref, r0_ref, nv_ref, ws_ref, nx_ref, xs_hbm, wg_hbm, wu_hbm, wd_hbm, ys_hbm,
                   xbuf, ybuf, wg_st, wu_st, wd_st, wg_bf, wu_bf, wd_bf, sem_in, sem_out, sem_w):
    b = pl.program_id(0)
    n_blocks = pl.num_programs(0)
    slot = b % 2

    def weight_copies(e, ws):
        return [pltpu.make_async_copy(hbm.at[e], st.at[ws], sem_w.at[ws])
                for hbm, st in ((wg_hbm, wg_st), (wu_hbm, wu_st), (wd_hbm, wd_st))]

    @pl.when(b == 0)
    def _():
        for cp in weight_copies(be_ref[0], 0):
            cp.start()

    @pl.when(ws_ref[b] >= 0)
    def _():
        ws = ws_ref[b]
        for cp in weight_copies(be_ref[b], ws):
            cp.wait()
        wg_bf[...] = wg_st[ws].astype(BF16)
        wu_bf[...] = wu_st[ws].astype(BF16)
        wd_bf[...] = wd_st[ws].astype(BF16)

        @pl.when(nx_ref[b] >= 0)
        def _():
            for cp in weight_copies(nx_ref[b], 1 - ws):
                cp.start()

    def fetch(blk, sl):
        return pltpu.make_async_copy(_slab_rows(xs_hbm, r0_ref[blk], EXPERT_BLOCK), xbuf.at[sl], sem_in.at[sl])

    def drain(blk, sl):
        for cond, cp in _valid_row_copies(ybuf.at[sl], ys_hbm, r0_ref[blk], nv_ref[blk], sem_out.at[sl]):
            pl.when(cond)(cp.wait)

    @pl.when(b == 0)
    def _():
        for k in range(EXPERT_LOOKAHEAD):
            pl.when(nv_ref[k] > 0)(fetch(k, k).start)

    nxt = jnp.minimum(b + EXPERT_LOOKAHEAD, n_blocks - 1)

    @pl.when((b + EXPERT_LOOKAHEAD < n_blocks) & (nv_ref[nxt] > 0))
    def _():
        fetch(nxt, nxt % (EXPERT_LOOKAHEAD + 1)).start()

    @pl.when(b >= 2)
    def _():
        drain(jnp.maximum(b - 2, 0), slot)

    @pl.when(nv_ref[b] > 0)
    def _():
        xslot = b % (EXPERT_LOOKAHEAD + 1)
        fetch(b, xslot).wait()
        x = _gathered_rows(xbuf.at[xslot], 0, EXPERT_BLOCK, 1).astype(BF16)
        y = _swiglu(x, wg_bf[...], wu_bf[...], wd_bf[...])
        _store_row_slabs(ybuf.at[slot], y)
        for cond, cp in _valid_row_copies(ybuf.at[slot], ys_hbm, r0_ref[b], nv_ref[b], sem_out.at[slot]):
            pl.when(cond)(cp.start)

    @pl.when(b == n_blocks - 1)
    def _():
        drain(jnp.maximum(b - 1, 0), 1 - slot)
        drain(b, slot)


def moe_experts(xs_slabs, block_e, block_row0, block_valid, block_wslot, block_next_e, w_gate, w_up, w_down):
    n_blocks = block_e.shape[0]
    d, ff = w_gate.shape[1:]
    blk_rows = EXPERT_BLOCK * ROW_SLAB
    n_assign = xs_slabs.shape[0] // ROW_SLAB - EXPERT_BLOCK
    any_spec = pl.BlockSpec(memory_space=pl.ANY)
    grid_spec = pltpu.PrefetchScalarGridSpec(
        num_scalar_prefetch=5,
        grid=(n_blocks,),
        in_specs=[any_spec] * 4,
        out_specs=any_spec,
        scratch_shapes=[pltpu.VMEM((EXPERT_LOOKAHEAD + 1, blk_rows, LANES), F32), pltpu.VMEM((2, blk_rows, LANES), F32),
                        pltpu.VMEM((2, d, ff), F32), pltpu.VMEM((2, d, ff), F32), pltpu.VMEM((2, ff, d), F32),
                        pltpu.VMEM((d, ff), BF16), pltpu.VMEM((d, ff), BF16), pltpu.VMEM((ff, d), BF16),
                        pltpu.SemaphoreType.DMA((EXPERT_LOOKAHEAD + 1,)), pltpu.SemaphoreType.DMA((2,)),
                        pltpu.SemaphoreType.DMA((2,))],
    )
    return pl.pallas_call(
        _expert_kernel,
        grid_spec=grid_spec,
        out_shape=jax.ShapeDtypeStruct((n_assign * ROW_SLAB, LANES), F32),
        compiler_params=_cparams(("arbitrary",)),
        name="moe_experts",
    )(block_e, block_row0, block_valid, block_wslot, block_next_e, xs_slabs, w_gate, w_up, w_down)


def _combine_kernel(cur_ref, nxt_ref, ys_hbm, xc_ref, xl_ref, hf_ref, gt_ref, g2_ref, sg_ref, su_ref, sd_ref,
                    lg_ref, lb_ref, *rest, n_ctx_tiles):
    *o_refs, ybuf, sem = rest
    i = pl.program_id(0)
    n_tiles = pl.num_programs(0)
    tm = xl_ref.shape[0]
    n_rows = tm * TOP_K
    slot = i % 2

    @pl.when(i == 0)
    def _():
        _row_gather(cur_ref, 0, n_rows, ys_hbm, ybuf.at[0], sem.at[0])

    @pl.when(i + 1 < n_tiles)
    def _():
        _row_gather(nxt_ref, 0, n_rows, ys_hbm, ybuf.at[1 - slot], sem.at[1 - slot])

    hf = _gathered_rows(hf_ref, 0, tm, 1).astype(BF16)
    y = _swiglu(hf, sg_ref[...], su_ref[...], sd_ref[...])

    _wait_rows(ybuf.at[slot], sem.at[slot])
    for k in range(TOP_K):
        y = y + gt_ref[:, k:k + 1] * _gathered_rows(ybuf.at[slot], k, tm, TOP_K)
    r = DEEPNORM_ALPHA * _stream_tile(xc_ref, xl_ref, n_ctx_tiles) + g2_ref[...] * y
    res = _layer_norm_rows(r, lg_ref[...], lb_ref[...])
    if n_ctx_tiles == 0:
        o_refs[0][...] = res
    else:
        oc_ref, ol_ref = o_refs

        @pl.when(i < n_ctx_tiles)
        def _():
            oc_ref[...] = res

        @pl.when(i >= n_ctx_tiles)
        def _():
            ol_ref[...] = res


def moe_combine(dest, ys_slabs, x_ctx, x_lat, hf_slabs, gate, g2, n_ctx_tiles, sh_gate, sh_up, sh_down, ln_g, ln_b):
    d = x_lat.shape[1]
    tm = MOE_TILE
    n_tiles = n_ctx_tiles + x_lat.shape[0] // tm
    n_rows = tm * TOP_K
    ff = sh_gate.shape[1]
    row = lambda i: (i, 0)
    fixed = lambda i: (0, 0)
    seg = lambda i: (jnp.where(i < n_ctx_tiles, 1, 0), 0, 0)
    ctx_spec, lat_spec = _stream_specs(n_ctx_tiles, tm, d)
    lat_out = jax.ShapeDtypeStruct(x_lat.shape, F32)
    if n_ctx_tiles == 0:
        out_specs, out_shape = [lat_spec], [lat_out]
    else:
        out_specs, out_shape = [ctx_spec, lat_spec], [jax.ShapeDtypeStruct(x_ctx.shape, F32), lat_out]
    kern = functools.partial(_combine_kernel, n_ctx_tiles=n_ctx_tiles)
    return pl.pallas_call(
        kern,
        grid=(n_tiles,),
        in_specs=[pl.BlockSpec((n_rows,), lambda i: (i,), memory_space=pltpu.SMEM),
                  pl.BlockSpec((n_rows,), lambda i: (jnp.minimum(i + 1, n_tiles - 1),), memory_space=pltpu.SMEM),
                  pl.BlockSpec(memory_space=pl.ANY),
                  ctx_spec, lat_spec, pl.BlockSpec((tm * ROW_SLAB, LANES), row), pl.BlockSpec((tm, LANES), row),
                  pl.BlockSpec((None, 1, d), seg),
                  pl.BlockSpec((d, ff), fixed), pl.BlockSpec((d, ff), fixed), pl.BlockSpec((ff, d), fixed),
                  pl.BlockSpec((1, d), fixed), pl.BlockSpec((1, d), fixed)],
        out_specs=out_specs,
        out_shape=out_shape,
        scratch_shapes=[pltpu.VMEM((2, n_rows * ROW_SLAB, LANES), F32), pltpu.SemaphoreType.DMA((2,))],
        compiler_params=_cparams(("arbitrary",)),
        name="moe_combine",
    )(dest, dest, ys_slabs, x_ctx, x_lat, hf_slabs, gate, g2,
      sh_gate.astype(BF16), sh_up.astype(BF16), sh_down.astype(BF16), ln_g.reshape(1, d), ln_b.reshape(1, d))


def moe_layer(x_ctx, x_lat, sc2, sh2, g2, n_ctx_tiles, router_w, router_bias, w_gate, w_up, w_down,
              sh_gate, sh_up, sh_down, ln_g, ln_b):
    n = n_ctx_tiles * MOE_TILE + x_lat.shape[0]
    hf, te, gt, rk, cnt = moe_route(x_ctx, x_lat, sc2, sh2, n_ctx_tiles, router_w, router_bias)
    counts = cnt[0].astype(I32)
    start = jnp.cumsum(counts) - counts
    experts = jnp.arange(N_EXPERTS, dtype=I32)
    dest = moe_slots(te, rk, start)
    nb = (counts + EXPERT_BLOCK - 1) // EXPERT_BLOCK
    blk_end = jnp.cumsum(nb)
    blk_start = blk_end - nb
    n_blocks = n * TOP_K // EXPERT_BLOCK + N_EXPERTS
    blk = jnp.arange(n_blocks, dtype=I32)
    bb = jnp.minimum(blk, blk_end[-1] - 1)[:, None]
    own = (blk_start[None, :] <= bb) & (bb < blk_end[None, :])
    sel = lambda v: jnp.sum(jnp.where(own, v[None, :], 0), axis=1)
    j = bb[:, 0] - sel(blk_start)
    block_e = sel(experts)
    block_row0 = sel(start) + j * EXPERT_BLOCK
    block_valid = jnp.where(blk < blk_end[-1], jnp.clip(sel(counts) - j * EXPERT_BLOCK, 0, EXPERT_BLOCK), 0)
    used = nb > 0
    ordinal = jnp.cumsum(used.astype(I32)) - 1
    later = used[None, :] & (experts[None, :] > experts[:, None])
    next_used = jnp.min(jnp.where(later, experts[None, :], N_EXPERTS), axis=1)
    next_used = jnp.where(next_used < N_EXPERTS, next_used, -1)
    first = (blk < blk_end[-1]) & (j == 0)
    block_wslot = jnp.where(first, sel(ordinal) % 2, -1)
    block_next_e = jnp.where(first, sel(next_used), -1)
    xs = moe_dispatch(dest, hf)
    ys = moe_experts(xs, block_e, block_row0, block_valid, block_wslot, block_next_e, w_gate, w_up, w_down)
    return moe_combine(dest, ys, x_ctx, x_lat, hf, gt, g2, n_ctx_tiles, sh_gate, sh_up, sh_down, ln_g, ln_b)


def kernel(x, c, ctx, c_ctx, w_mod, b_mod, ln_mix_g, ln_mix_b, ln_ffn_g, ln_ffn_b, ab_w_in, ab_w_out, na_rpb,
           hy_conv_w, hy_conv_b, hy_f_w1, hy_f_b1, hy_f_freq, hy_f_w2, hy_f_b2, hy_f_w3, hy_skip, cd_w_in, cd_w_out,
           q_norm_g, k_norm_g, s5_a_re, s5_a_im, s5_log_dt, s5_b_re, s5_b_im, s5_c_re, s5_c_im, s5_d, s5_glu_w,
           s5_glu_b, router_w, router_bias, exp_w_gate, exp_w_up, exp_w_down, sh_w_gate, sh_w_up, sh_w_down):
    b, l, d = x.shape
    assert b == 1
    n_ctx = ctx.shape[1]
    assert n_ctx == MOE_TILE
    xs = x[0]
    cs = ctx[0]
    cmat = jnp.zeros((SUBLANES, d), F32).at[0].set(c[0]).at[1].set(c_ctx)
    mods = modulation_all(cmat, w_mod, b_mod).reshape(DEPTH, SUBLANES, 6, d)
    qscale = HEAD_DIM ** -0.5

    for i in range(DEPTH):
        need_ctx = i < DEPTH - 1
        m = mods[i]
        sh1, sc1, g1, sh2, sc2, g2 = [m[0:1, t] for t in range(6)]
        csh1, csc1, cg1, csh2, csc2, cg2 = [m[1:2, t] for t in range(6)]
        j = i // 2
        if i % 2 == 0:
            filt = (hy_conv_w[j], hy_conv_b[j], hy_f_w1[j], hy_f_b1[j], hy_f_freq[j], hy_f_w2[j], hy_f_b2[j],
                    hy_f_w3[j], hy_skip[j])
            splits = (NA_WIDTH, NA_WIDTH, NA_WIDTH, 3 * HY_WIDTH)
            dts = (BF16, BF16, BF16, F32)
            scl = (qscale, 1.0, 1.0, 1.0)
            q_l, k_l, v_l, u_l = mod_project(xs, sc1, sh1, ab_w_in[j], splits, dts, scl)
            q_c, k_c, v_c, u_c = mod_project(cs, csc1, csh1, ab_w_in[j], splits, dts, scl)
            a_lat = neighbourhood_attention(q_l, k_l, v_l, k_c, v_c, na_rpb[j])
            y_hy = hyena_long(u_l, *filt)
            xs_new = outproj_ln(a_lat, y_hy, ab_w_out[j], xs, g1, ln_mix_g[i], ln_mix_b[i])
            if need_ctx:
                a_ctx = context_attention(q_c, k_c, v_c)
                yc_hy = hyena_long(u_c, *filt)
                cs = outproj_ln(a_ctx, yc_hy, ab_w_out[j], cs, cg1, ln_mix_g[i], ln_mix_b[i])
            xs = xs_new
        else:
            splits = (GQA_WIDTH, GQA_KV_WIDTH, GQA_KV_WIDTH, S5_WIDTH)
            q_l, k_l, v_l, u_l = mod_project(xs, sc1, sh1, cd_w_in[j], splits, (F32, F32, BF16, F32))
            k_c, v_c, u_c = mod_project(cs, csc1, csh1, cd_w_in[j][:, GQA_WIDTH:], splits[1:], (F32, BF16, F32))
            tabs = _rope_tables(l)
            qn_t = qk_prep(q_l, q_norm_g[j], tabs, qscale * math.log2(math.e), transposed=True)
            kn = qk_prep(k_l, k_norm_g[j], tabs, 1.0)
            kcn = qk_prep(k_c, k_norm_g[j], None, 1.0)
            k_all = jnp.concatenate([kn, kcn], axis=0)
            v_all = jnp.concatenate([v_l, v_c], axis=0)
            k_hm = k_all.reshape(-1, GQA_KV_HEADS, HEAD_DIM).transpose(1, 0, 2)
            v_t = v_all.T.reshape(GQA_KV_HEADS, HEAD_DIM, -1)
            att = gqa_attention(qn_t, k_hm, v_t)
            ssm = s5_mix(u_c, u_l, s5_a_re[j], s5_a_im[j], s5_log_dt[j], s5_b_re[j], s5_b_im[j],
                         s5_c_re[j], s5_c_im[j], s5_d[j], s5_glu_w[j], s5_glu_b[j])
            xs = outproj_ln(att, ssm, cd_w_out[j], xs, g1, ln_mix_g[i], ln_mix_b[i])
            assert not need_ctx

        moe_w = (router_w[i], router_bias[i], exp_w_gate[i], exp_w_up[i], exp_w_down[i],
                 sh_w_gate[i], sh_w_up[i], sh_w_down[i], ln_ffn_g[i], ln_ffn_b[i])
        stack2 = lambda lat, cx: jnp.stack([lat, cx])
        mod2 = (stack2(sc2, csc2), stack2(sh2, csh2), stack2(g2, cg2))
        if need_ctx:
            cs, xs = moe_layer(cs, xs, *mod2, n_ctx // MOE_TILE, *moe_w)
        else:
            (xs,) = moe_layer(xs, xs, *mod2, 0, *moe_w)
    return xs.reshape(b, l, d)
```

```python
import functools
import math

import jax
import jax.numpy as jnp
import numpy as np
from jax import lax
from jax.experimental import pallas as pl
from jax.experimental.pallas import tpu as pltpu

F32 = jnp.float32
BF16 = jnp.bfloat16
I32 = jnp.int32
HIGHEST = lax.Precision.HIGHEST

LANES = 128
SUBLANES = 8
VMEM_LIMIT = 56 * 1024 * 1024

D_MODEL = 1024
DEPTH = 2
GRID_W = 64
HEAD_DIM = 64
NA_HEADS = 8
NA_WIDTH = NA_HEADS * HEAD_DIM
NA_KH = 8
NA_KW = 16
HY_WIDTH = D_MODEL - NA_WIDTH
HY_BANDS = 16
HY_DECAY_PCT_MIN = 0.3
HY_DECAY_PCT_MAX = 1.5
HY_DECAY_TARGET = 1e-2
GQA_HEADS = 8
GQA_KV_HEADS = 2
GQA_WIDTH = GQA_HEADS * HEAD_DIM
GQA_KV_WIDTH = GQA_KV_HEADS * HEAD_DIM
ROPE_THETA = 10000.0
S5_WIDTH = D_MODEL - GQA_WIDTH
S5_GROUP = 16
S5_GROUPS = S5_WIDTH // S5_GROUP
S5_STATE = 64
N_EXPERTS = 256
TOP_K = 8
N_EXPERT_GROUPS = 8
TOPK_GROUPS = 4
EXPERT_FF = 256
ROUTED_SCALE = 2.5
EXPERT_BLOCK = 128
DEEPNORM_ALPHA = (2.0 * DEPTH) ** 0.25
LN_EPS = 1e-5
RMS_EPS = 1e-6

NEG_BIG = -1e30
NA_TILE_ROWS = 8
NA_KEY_ROWS = 16
NA_KEY_BLOCK_ROWS = 4
DFT_N1 = 128
S5_CHUNK = 16
S5_LANE_GROUPS = LANES // S5_GROUP
MOE_TILE = 256
EXPERT_LOOKAHEAD = 3
ROW_SLAB = D_MODEL // LANES

NT_DIMS = (((1,), (1,)), ((), ()))


def _cparams(sem, **kw):
    return pltpu.CompilerParams(dimension_semantics=sem, vmem_limit_bytes=VMEM_LIMIT, **kw)


def _dot(a, b, **kw):
    return jnp.dot(a, b, preferred_element_type=F32, **kw)


def _dot_nt(a, b):
    return lax.dot_general(a, b, NT_DIMS, preferred_element_type=F32)


def _row_tile(m, pref):
    return pref if m % pref == 0 else m


def _mod_kernel(c_ref, w_ref, b_ref, o_ref):
    cv = c_ref[...]
    s = cv * jax.nn.sigmoid(cv)
    o_ref[...] = _dot(s, w_ref[...], precision=HIGHEST) + b_ref[...]


def modulation_all(cmat, w_mod, b_mod):
    depth, d, n = w_mod.shape
    tn = 1536
    return pl.pallas_call(
        _mod_kernel,
        grid=(depth, n // tn),
        in_specs=[pl.BlockSpec((SUBLANES, d), lambda l, j: (0, 0)),
                  pl.BlockSpec((None, d, tn), lambda l, j: (l, 0, j)),
                  pl.BlockSpec((None, 1, tn), lambda l, j: (l, 0, j))],
        out_specs=pl.BlockSpec((None, SUBLANES, tn), lambda l, j: (l, 0, j)),
        out_shape=jax.ShapeDtypeStruct((depth, SUBLANES, n), F32),
        compiler_params=_cparams(("arbitrary", "arbitrary")),
        name="modulation",
    )(cmat, w_mod, b_mod.reshape(depth, 1, n))


def _proj_kernel(x_ref, sc_ref, sh_ref, w_ref, *o_refs, splits, scales):
    h = (x_ref[...] * (1.0 + sc_ref[...]) + sh_ref[...]).astype(BF16)
    off = 0
    for o_ref, wd, sc in zip(o_refs, splits, scales):
        y = _dot(h, w_ref[:, off:off + wd])
        if sc != 1.0:
            y = y * sc
        o_ref[...] = y.astype(o_ref.dtype)
        off += wd


def mod_project(x, sc, sh, w, splits, dtypes, scales=None):
    m, d = x.shape
    n = w.shape[1]
    assert sum(splits) == n
    scales = scales or (1.0,) * len(splits)
    tm = _row_tile(m, 512)
    kern = functools.partial(_proj_kernel, splits=tuple(splits), scales=tuple(scales))
    return pl.pallas_call(
        kern,
        grid=(m // tm,),
        in_specs=[pl.BlockSpec((tm, d), lambda i: (i, 0)),
                  pl.BlockSpec((1, d), lambda i: (0, 0)),
                  pl.BlockSpec((1, d), lambda i: (0, 0)),
                  pl.BlockSpec((d, n), lambda i: (0, 0))],
        out_specs=[pl.BlockSpec((tm, wd), lambda i: (i, 0)) for wd in splits],
        out_shape=[jax.ShapeDtypeStruct((m, wd), dt) for wd, dt in zip(splits, dtypes)],
        compiler_params=_cparams(("parallel",)),
        name="mod_project",
    )(x, sc, sh, w.astype(BF16))


def _layer_norm_rows(r, g, b):
    mu = jnp.mean(r, axis=-1, keepdims=True)
    c = r - mu
    var = jnp.mean(c * c, axis=-1, keepdims=True)
    return c * lax.rsqrt(var + LN_EPS) * g + b


def _outproj_ln_kernel(a_ref, b_ref, w_ref, x_ref, gate_ref, g_ref, beta_ref, o_ref):
    ka = a_ref.shape[1]
    y = _dot(a_ref[...], w_ref[:ka, :]) + _dot(b_ref[...], w_ref[ka:, :])
    r = DEEPNORM_ALPHA * x_ref[...] + gate_ref[...] * y
    o_ref[...] = _layer_norm_rows(r, g_ref[...], beta_ref[...])


def outproj_ln(a, b, w, x, gate, g, beta):
    m, d = x.shape
    ka, kb = a.shape[1], b.shape[1]
    tm = _row_tile(m, 512)
    row = lambda i: (i, 0)
    fixed = lambda i: (0, 0)
    return pl.pallas_call(
        _outproj_ln_kernel,
        grid=(m // tm,),
        in_specs=[pl.BlockSpec((tm, ka), row), pl.BlockSpec((tm, kb), row),
                  pl.BlockSpec((ka + kb, d), fixed), pl.BlockSpec((tm, d), row),
                  pl.BlockSpec((1, d), fixed), pl.BlockSpec((1, d), fixed), pl.BlockSpec((1, d), fixed)],
        out_specs=pl.BlockSpec((tm, d), row),
        out_shape=jax.ShapeDtypeStruct((m, d), F32),
        compiler_params=_cparams(("parallel",)),
        name="outproj_ln",
    )(a, b, w.astype(BF16), x, gate, g.reshape(1, d), beta.reshape(1, d))


NA_BIAS_PAD = NA_TILE_ROWS


def _na_key_start(t, rows):
    lo, hi = NA_TILE_ROWS * t - NA_KH // 2, rows - NA_KEY_ROWS
    return min(max(lo, 0), hi) if isinstance(t, int) else jnp.clip(lo, 0, hi)


def _na_bias_tables(rpb, rows):
    h = rpb.shape[0]
    ri = np.arange(NA_TILE_ROWS)
    kr = np.arange(NA_KEY_ROWS)
    c = np.arange(GRID_W)
    cs = np.clip(c - NA_KW // 2, 0, GRID_W - NA_KW)
    vc = (c[None, :] >= cs[:, None]) & (c[None, :] < cs[:, None] + NA_KW)
    dc = np.clip(c[None, :] - c[:, None] + NA_KW - 1, 0, 2 * NA_KW - 2)
    pick = (dc.reshape(-1)[:, None] == np.arange(2 * NA_KW - 1)[None, :]).astype(np.float32)
    colb = jnp.einsum('qb,hab->haq', pick, rpb, precision=HIGHEST).reshape(h, 2 * NA_KH - 1, GRID_W, GRID_W)
    colb = jnp.where(vc[None, None], colb, NEG_BIG)
    pad = jnp.zeros((h, NA_BIAS_PAD, GRID_W, GRID_W), F32)
    colb = jnp.concatenate([pad, colb, pad], axis=1)
    colb2 = jnp.concatenate([colb[:, :-1], colb[:, 1:]], axis=-1)
    colb2 = colb2.reshape((h // 2, 2) + colb2.shape[1:])

    def case(t):
        r = NA_TILE_ROWS * t + ri
        rs = np.clip(r - NA_KH // 2, 0, rows - NA_KH)
        krow = _na_key_start(t, rows) + kr
        vr = (krow[None, :] >= rs[:, None]) & (krow[None, :] < rs[:, None] + NA_KH)
        m = np.where(vr, 0.0, NEG_BIG).astype(np.float32)
        return np.repeat(np.repeat(m, GRID_W, axis=0), GRID_W, axis=1)

    n_tiles = rows // NA_TILE_ROWS
    return colb2, jnp.asarray(np.stack([case(0), case(1), case(n_tiles - 1)]))


def _pair_masks(shape):
    lane = lax.broadcasted_iota(I32, shape, 1)
    return lane < HEAD_DIM


def _na_kernel(q_ref, k0, k1, k2, k3, v0, v1, v2, v3, kc_ref, vc_ref, colb_ref, mask_ref, o_ref, *, rows):
    q = q_ref[...]
    lo = _pair_masks(q.shape)
    ks = (k0, k1, k2, k3)
    vs = (v0, v1, v2, v3)
    kb = k0.shape[0]
    t = pl.program_id(1)
    off = _na_key_start(t, rows) - NA_TILE_ROWS * t + (NA_KH - 1) + NA_BIAS_PAD
    key_rows_per_block = kb // GRID_W

    def bias(hh, i):
        first = i * key_rows_per_block
        return jnp.concatenate(
            [jnp.concatenate([colb_ref[hh, first + 2 * m - ri + off] for m in range(key_rows_per_block // 2)], axis=1)
             for ri in range(NA_TILE_ROWS)], axis=0)

    outs = []
    for hh in range(2):
        qh = jnp.where(lo if hh == 0 else jnp.logical_not(lo), q, jnp.zeros_like(q))
        s = [_dot_nt(qh, ks[i][...]) + bias(hh, i) + mask_ref[:, i * kb:(i + 1) * kb] for i in range(4)]
        s.append(_dot_nt(qh, kc_ref[...]))
        m = s[0].max(axis=1, keepdims=True)
        for si in s[1:]:
            m = jnp.maximum(m, si.max(axis=1, keepdims=True))
        p = [jnp.exp(si - m) for si in s]
        l = p[0].sum(axis=1, keepdims=True)
        for pi in p[1:]:
            l = l + pi.sum(axis=1, keepdims=True)
        acc = _dot(p[4].astype(BF16), vc_ref[...])
        for i in range(4):
            acc = acc + _dot(p[i].astype(BF16), vs[i][...])
        outs.append(acc / l)
    o_ref[...] = jnp.where(lo, outs[0], outs[1]).astype(o_ref.dtype)


def neighbourhood_attention(q, k, v, k_ctx, v_ctx, rpb):
    l, w = q.shape
    rows = l // GRID_W
    n_tiles = rows // NA_TILE_ROWS
    assert n_tiles >= 3 and rows % NA_TILE_ROWS == 0
    n_ctx = k_ctx.shape[0]
    tq = NA_TILE_ROWS * GRID_W
    kb = NA_KEY_BLOCK_ROWS * GRID_W
    colb2, rowmask = _na_bias_tables(rpb, rows)
    pair_w = 2 * HEAD_DIM

    def kv_spec(i):
        def imap(p, t):
            return (_na_key_start(t, rows) // NA_KEY_BLOCK_ROWS + i, p)
        return pl.BlockSpec((kb, pair_w), imap)

    def mask_map(p, t):
        return (jnp.where(t == 0, 0, jnp.where(t == n_tiles - 1, 2, 1)), 0, 0)

    return pl.pallas_call(
        functools.partial(_na_kernel, rows=rows),
        grid=(w // pair_w, n_tiles),
        in_specs=[pl.BlockSpec((tq, pair_w), lambda p, t: (t, p))]
                 + [kv_spec(i) for i in range(4)] + [kv_spec(i) for i in range(4)]
                 + [pl.BlockSpec((n_ctx, pair_w), lambda p, t: (0, p)),
                    pl.BlockSpec((n_ctx, pair_w), lambda p, t: (0, p)),
                    pl.BlockSpec((None,) + colb2.shape[1:], lambda p, t: (p, 0, 0, 0, 0)),
                    pl.BlockSpec((None, tq, NA_KEY_ROWS * GRID_W), mask_map)],
        out_specs=pl.BlockSpec((tq, pair_w), lambda p, t: (t, p)),
        out_shape=jax.ShapeDtypeStruct((l, w), BF16),
        compiler_params=_cparams(("parallel", "parallel")),
        name="neighbourhood_attention",
    )(q, k, k, k, k, v, v, v, v, k_ctx, v_ctx, colb2, rowmask)


def _ctx_attn_kernel(q_ref, k_ref, v_ref, o_ref):
    q = q_ref[...]
    lo = _pair_masks(q.shape)
    outs = []
    for hh in range(2):
        qh = jnp.where(lo if hh == 0 else jnp.logical_not(lo), q, jnp.zeros_like(q))
        s = _dot_nt(qh, k_ref[...])
        p = jnp.exp(s - s.max(axis=1, keepdims=True))
        outs.append(_dot(p.astype(BF16), v_ref[...]) / p.sum(axis=1, keepdims=True))
    o_ref[...] = jnp.where(lo, outs[0], outs[1]).astype(o_ref.dtype)


def context_attention(q, k, v):
    n, w = q.shape
    pair_w = 2 * HEAD_DIM
    spec = pl.BlockSpec((n, pair_w), lambda p: (0, p))
    return pl.pallas_call(
        _ctx_attn_kernel, grid=(w // pair_w,), in_specs=[spec, spec, spec], out_specs=spec,
        out_shape=jax.ShapeDtypeStruct((n, w), BF16),
        compiler_params=_cparams(("parallel",)), name="context_attention",
    )(q, k, v)


def _shortconv_kernel(u_ref, up_ref, un_ref, w_ref, b_ref, x0_ref, z_ref, *, n_tiles):
    i = pl.program_id(0)
    u = u_ref[...]
    tm = u.shape[0]
    prev_row = jnp.where(i > 0, up_ref[SUBLANES - 1:SUBLANES, :], 0.0)
    next_row = jnp.where(i < n_tiles - 1, un_ref[0:1, :], 0.0)
    row = lax.broadcasted_iota(I32, u.shape, 0)
    u_dn = jnp.where(row == 0, prev_row, pltpu.roll(u, 1, 0))
    u_up = jnp.where(row == tm - 1, next_row, pltpu.roll(u, tm - 1, 0))
    y = u_dn * w_ref[0:1, :] + u * w_ref[1:2, :] + u_up * w_ref[2:3, :] + b_ref[...]
    c = HY_WIDTH
    x0_ref[...] = y[:, :c]
    z_ref[...] = y[:, c:2 * c] * y[:, 2 * c:]


def hyena_gate(u, conv_w, conv_b):
    l, w3 = u.shape
    tm = _row_tile(l, 512)
    n_tiles = l // tm
    per = tm // SUBLANES
    last = l // SUBLANES - 1
    kern = functools.partial(_shortconv_kernel, n_tiles=n_tiles)
    return pl.pallas_call(
        kern,
        grid=(n_tiles,),
        in_specs=[pl.BlockSpec((tm, w3), lambda i: (i, 0)),
                  pl.BlockSpec((SUBLANES, w3), lambda i: (jnp.maximum(i * per - 1, 0), 0)),
                  pl.BlockSpec((SUBLANES, w3), lambda i: (jnp.minimum((i + 1) * per, last), 0)),
                  pl.BlockSpec((3, w3), lambda i: (0, 0)),
                  pl.BlockSpec((1, w3), lambda i: (0, 0))],
        out_specs=[pl.BlockSpec((tm, HY_WIDTH), lambda i: (i, 0))] * 2,
        out_shape=[jax.ShapeDtypeStruct((l, HY_WIDTH), F32)] * 2,
        compiler_params=_cparams(("parallel",)),
        name="hyena_gate",
    )(u, u, u, conv_w, conv_b.reshape(1, w3))


def _filter_kernel(bands_ref, w1t_ref, w1c_ref, w1s_ref, b1_ref, fr_ref, w2_ref, b2_ref, w3_ref, dl_ref,
                   taps_ref, asum_ref, *, l, tp):
    i = pl.program_id(0)
    hid_w = w2_ref.shape[0]
    c = HY_WIDTH
    denom = float(max(l - 1, 1))

    def pos(width):
        return (lax.broadcasted_iota(I32, (tp, width), 0) + i * tp).astype(F32)

    ang = (2.0 * math.pi / l) * pos(HY_BANDS) * bands_ref[...]
    pre = ((pos(hid_w) / denom) * w1t_ref[...]
           + _dot(jnp.cos(ang), w1c_ref[...], precision=HIGHEST)
           + _dot(-jnp.sin(ang), w1s_ref[...], precision=HIGHEST) + b1_ref[...])
    hid = jnp.sin(fr_ref[...] * pre)
    hid = jnp.sin(fr_ref[...] * (_dot(hid, w2_ref[...], precision=HIGHEST) + b2_ref[...]))
    taps = _dot(hid, w3_ref[...], precision=HIGHEST)
    pc = pos(c)
    window = jnp.exp(-(pc / denom) * dl_ref[...])
    fwd = taps[:, :c] * window
    bwd = jnp.where(pc == 0.0, 0.0, taps[:, c:] * window)
    taps_ref[:, :c] = fwd
    taps_ref[:, c:] = bwd

    @pl.when(i == 0)
    def _():
        asum_ref[...] = jnp.zeros_like(asum_ref)

    asum_ref[...] += jnp.sum(jnp.abs(fwd) + jnp.abs(bwd), axis=0, keepdims=True)


def hyena_filter_taps(l, f_w1, f_b1, f_freq, f_w2, f_b2, f_w3):
    c = HY_WIDTH
    hid = f_w2.shape[0]
    tp = _row_tile(l, 1024)
    bands = jnp.linspace(1e-4, HY_BANDS - 1, HY_BANDS, dtype=F32).reshape(1, HY_BANDS)
    deltas = jnp.abs(jnp.linspace(math.log(HY_DECAY_TARGET) / HY_DECAY_PCT_MAX,
                                  math.log(HY_DECAY_TARGET) / HY_DECAY_PCT_MIN, c, dtype=F32)).reshape(1, c)
    fixed = lambda i: (0, 0)
    full = lambda a: pl.BlockSpec(a.shape, fixed)
    args = (bands, f_w1[0:1], f_w1[1:1 + HY_BANDS], f_w1[1 + HY_BANDS:], f_b1.reshape(1, hid),
            f_freq.reshape(1, hid), f_w2, f_b2.reshape(1, hid), f_w3, deltas)
    kern = functools.partial(_filter_kernel, l=l, tp=tp)
    return pl.pallas_call(
        kern,
        grid=(l // tp,),
        in_specs=[full(a) for a in args],
        out_specs=[pl.BlockSpec((tp, 2 * c), lambda i: (i, 0)), pl.BlockSpec((1, c), fixed)],
        out_shape=[jax.ShapeDtypeStruct((l, 2 * c), F32), jax.ShapeDtypeStruct((1, c), F32)],
        compiler_params=_cparams(("arbitrary",)),
        name="hyena_filter",
    )(*args)


def _dft_tables(l):
    n = 2 * l
    n1 = DFT_N1
    n2 = n // n1
    k1 = jnp.arange(n1)[:, None]
    m1 = jnp.arange(n1 // 2)[None, :]
    ph1 = (2.0 * math.pi / n1) * ((k1 * m1) % n1).astype(F32)
    d1 = jnp.stack([jnp.cos(ph1), -jnp.sin(ph1)], axis=1).reshape(2 * n1, n1 // 2)
    d1_inv = d1.T
    j2 = jnp.arange(n2)
    ph2 = (2.0 * math.pi / n2) * ((j2[:, None] * j2[None, :]) % n2).astype(F32)
    cs, sn = jnp.cos(ph2), jnp.sin(ph2)
    f2 = jnp.concatenate([jnp.concatenate([cs, sn], axis=1), jnp.concatenate([-sn, cs], axis=1)], axis=0)
    pht = (2.0 * math.pi / n) * ((jnp.arange(n1)[:, None] * j2[None, :]) % n).astype(F32)[:, :, None]
    return d1.astype(BF16), d1_inv.astype(BF16), f2.astype(BF16), f2.T.astype(BF16), jnp.cos(pht), -jnp.sin(pht)


def _dft1_kernel(d_ref, x_ref, o_ref):
    n1h, m, c = x_ref.shape
    y = _dot(d_ref[...], x_ref[...].reshape(n1h * m, c).astype(BF16))
    o_ref[...] = y.reshape(o_ref.shape).astype(o_ref.dtype)


def dft_stage1(x, d1, n2):
    l, c = x.shape
    n1h = d1.shape[1]
    m = SUBLANES
    dk = jnp.kron(d1.astype(F32), jnp.eye(m, dtype=F32)).astype(BF16)
    out = pl.pallas_call(
        _dft1_kernel,
        grid=(n2 // m,),
        in_specs=[pl.BlockSpec(dk.shape, lambda j: (0, 0)), pl.BlockSpec((n1h, m, c), lambda j: (0, j, 0))],
        out_specs=pl.BlockSpec((d1.shape[0], m, c), lambda j: (0, j, 0)),
        out_shape=jax.ShapeDtypeStruct((d1.shape[0], n2, c), F32),
        compiler_params=_cparams(("parallel",)),
        name="dft_stage1",
    )(dk, x.reshape(n1h, n2, c))
    return out.reshape(d1.shape[0] // 2, 2, n2, c)


def _twiddled_stage2(f_ref, a_ref, twr_ref, twi_ref):
    shape = a_ref.shape[1:]
    twr = jnp.broadcast_to(twr_ref[...], shape)
    twi = jnp.broadcast_to(twi_ref[...], shape)
    ar, ai = a_ref[0].astype(F32), a_ref[1].astype(F32)
    a = jnp.concatenate([ar * twr - ai * twi, ar * twi + ai * twr], axis=0).astype(BF16)
    return _dot(f_ref[...], a), twr, twi


def _filter_spectrum_kernel(f_ref, a_ref, twr_ref, twi_ref, h_ref):
    n2 = a_ref.shape[1]
    c = h_ref.shape[2]
    x, _, _ = _twiddled_stage2(f_ref, a_ref, twr_ref, twi_ref)
    h_ref[0] = x[:n2, :c] + x[:n2, c:]
    h_ref[1] = x[n2:, :c] - x[n2:, c:]


def filter_spectrum(a_taps, f2, twr, twi):
    n1, _, n2, c2 = a_taps.shape
    c = c2 // 2
    tw = pl.BlockSpec((None, n2, 1), lambda i: (i, 0, 0))
    return pl.pallas_call(
        _filter_spectrum_kernel,
        grid=(n1,),
        in_specs=[pl.BlockSpec((2 * n2, 2 * n2), lambda i: (0, 0)),
                  pl.BlockSpec((None, 2, n2, c2), lambda i: (i, 0, 0, 0)), tw, tw],
        out_specs=pl.BlockSpec((None, 2, n2, c), lambda i: (i, 0, 0, 0)),
        out_shape=jax.ShapeDtypeStruct((n1, 2, n2, c), F32),
        compiler_params=_cparams(("parallel",)),
        name="filter_spectrum",
    )(f2, a_taps, twr, twi)


def _spectral_mix_kernel(f_ref, fi_ref, a_ref, h_ref, twr_ref, twi_ref, o_ref):
    n2 = a_ref.shape[1]
    x, twr, twi = _twiddled_stage2(f_ref, a_ref, twr_ref, twi_ref)
    xr, xi = x[:n2], x[n2:]
    hr, hi = h_ref[0], h_ref[1]
    y = jnp.concatenate([xr * hr - xi * hi, xr * hi + xi * hr], axis=0).astype(BF16)
    b = _dot(fi_ref[...], y)
    br, bi = b[:n2], b[n2:]
    o_ref[0] = (br * twr + bi * twi).astype(o_ref.dtype)
    o_ref[1] = (bi * twr - br * twi).astype(o_ref.dtype)


def spectral_mix(a_z, h, f2, f2_inv, twr, twi):
    n1, _, n2, c = a_z.shape
    blk = pl.BlockSpec((None, 2, n2, c), lambda i: (i, 0, 0, 0))
    mat = pl.BlockSpec((2 * n2, 2 * n2), lambda i: (0, 0))
    tw = pl.BlockSpec((None, n2, 1), lambda i: (i, 0, 0))
    return pl.pallas_call(
        _spectral_mix_kernel,
        grid=(n1,),
        in_specs=[mat, mat, blk, blk, tw, tw],
        out_specs=blk,
        out_shape=jax.ShapeDtypeStruct((n1, 2, n2, c), BF16),
        compiler_params=_cparams(("parallel",)),
        name="spectral_mix",
    )(f2, f2_inv, a_z, h, twr, twi)


def _hyena_out_kernel(di_ref, b_ref, x0_ref, z_ref, inv_ref, skip_ref, o_ref, *, inv_n):
    rows, m, c = b_ref.shape
    conv = (_dot(di_ref[...], b_ref[...].reshape(rows * m, c)) * inv_n).reshape(o_ref.shape)
    o_ref[...] = (x0_ref[...] * (conv * inv_ref[...] + z_ref[...] * skip_ref[...])).astype(o_ref.dtype)


def hyena_output(b, d1_inv, x0, z, inv_norm, skip):
    n1, _, n2, c = b.shape
    l = x0.shape[0]
    n1h = n1 // 2
    m = 2 * SUBLANES
    dk = jnp.kron(d1_inv.astype(F32), jnp.eye(m, dtype=F32)).astype(BF16)
    kern = functools.partial(_hyena_out_kernel, inv_n=1.0 / (2 * l))
    tile = pl.BlockSpec((n1h, m, c), lambda j: (0, j, 0))
    vec = pl.BlockSpec((1, 1, c), lambda j: (0, 0, 0))
    out = pl.pallas_call(
        kern,
        grid=(n2 // m,),
        in_specs=[pl.BlockSpec(dk.shape, lambda j: (0, 0)),
                  pl.BlockSpec((2 * n1, m, c), lambda j: (0, j, 0)),
                  tile, tile, vec, vec],
        out_specs=tile,
        out_shape=jax.ShapeDtypeStruct((n1h, n2, c), BF16),
        compiler_params=_cparams(("parallel",)),
        name="hyena_output",
    )(dk, b.reshape(2 * n1, n2, c), x0.reshape(n1h, n2, c), z.reshape(n1h, n2, c),
      inv_norm.reshape(1, 1, c), skip.reshape(1, 1, c))
    return out.reshape(l, c)


def _small_conv_kernel(d_ref, di_ref, z_ref, taps_ref, x0_ref, inv_ref, skip_ref, o_ref, *, inv_n):
    c = z_ref.shape[1]
    n = d_ref.shape[0] // 2
    zs = _dot(d_ref[...], z_ref[...], precision=HIGHEST)
    ts = _dot(d_ref[...], taps_ref[...], precision=HIGHEST)
    hr = ts[:n, :c] + ts[:n, c:]
    hi = ts[n:, :c] - ts[n:, c:]
    zr, zi = zs[:n], zs[n:]
    y = jnp.concatenate([zr * hr - zi * hi, zr * hi + zi * hr], axis=0)
    conv = _dot(di_ref[...], y, precision=HIGHEST) * inv_n
    o_ref[...] = (x0_ref[...] * (conv * inv_ref[...] + z_ref[...] * skip_ref[...])).astype(o_ref.dtype)


def hyena_output_short(z, taps, x0, inv_norm, skip):
    l, c = z.shape
    n = 2 * l
    ph = (2.0 * math.pi / n) * ((jnp.arange(n)[:, None] * jnp.arange(l)[None, :]) % n).astype(F32)
    d = jnp.concatenate([jnp.cos(ph), -jnp.sin(ph)], axis=0)
    di = jnp.concatenate([jnp.cos(ph), -jnp.sin(ph)], axis=0).T
    args = (d, di, z, taps, x0, inv_norm, skip.reshape(1, c))
    kern = functools.partial(_small_conv_kernel, inv_n=1.0 / n)
    return pl.pallas_call(
        kern,
        grid=(1,),
        in_specs=[pl.BlockSpec(a.shape, lambda i: (0, 0)) for a in args],
        out_specs=pl.BlockSpec((l, c), lambda i: (0, 0)),
        out_shape=jax.ShapeDtypeStruct((l, c), BF16),
        compiler_params=_cparams(("arbitrary",)),
        name="hyena_output_short",
    )(*args)


def hyena_long(u, conv_w, conv_b, f_w1, f_b1, f_freq, f_w2, f_b2, f_w3, skip):
    l = u.shape[0]
    x0, z = hyena_gate(u, conv_w, conv_b)
    taps, asum = hyena_filter_taps(l, f_w1, f_b1, f_freq, f_w2, f_b2, f_w3)
    inv_norm = 1.0 / asum
    if 2 * l < DFT_N1 * SUBLANES * 2:
        return hyena_output_short(z, taps, x0, inv_norm, skip)
    n2 = 2 * l // DFT_N1
    d1, d1_inv, f2, f2_inv, twr, twi = _dft_tables(l)
    h = filter_spectrum(dft_stage1(taps, d1, n2), f2, twr, twi)
    b = spectral_mix(dft_stage1(z, d1, n2), h, f2, f2_inv, twr, twi)
    return hyena_output(b, d1_inv, x0, z, inv_norm, skip)


def _head_sumsq(x, bd):
    sq = x * x
    hi = sq.astype(BF16)
    lo = (sq - hi.astype(F32)).astype(BF16)
    return _dot(hi, bd) + _dot(lo, bd)


def _qk_prep_kernel(x_ref, gain_ref, bd_ref, *rest, rope, scale, transposed):
    x = x_ref[...]
    w = x.shape[1]
    ms = _head_sumsq(x, bd_ref[...]) * (1.0 / HEAD_DIM)
    xn = x * lax.rsqrt(ms + RMS_EPS) * gain_ref[...]
    if rope:
        cos_ref, sin_ref, o_ref = rest
        reps = w // cos_ref.shape[1]
        cos = jnp.tile(cos_ref[...], (1, reps)) if reps > 1 else cos_ref[...]
        sin = jnp.tile(sin_ref[...], (1, reps)) if reps > 1 else sin_ref[...]
        lane = lax.broadcasted_iota(I32, x.shape, 1)
        partner = jnp.where(lane % 2 == 0, pltpu.roll(xn, w - 1, 1), pltpu.roll(xn, 1, 1))
        xn = xn * cos + partner * sin
    else:
        (o_ref,) = rest
    if scale != 1.0:
        xn = xn * scale
    if transposed:
        xn = xn.T
    o_ref[...] = xn.astype(o_ref.dtype)


def _rope_tables(l):
    half = HEAD_DIM // 2
    inv_freq = ROPE_THETA ** (-jnp.arange(0, half, 2, dtype=F32) / half)
    t = jnp.arange(l)
    row = (t // GRID_W).astype(F32)
    col = (t % GRID_W).astype(F32)
    ang = jnp.concatenate([jnp.repeat(row[:, None] * inv_freq[None], 2, axis=1),
                           jnp.repeat(col[:, None] * inv_freq[None], 2, axis=1)], axis=1)
    sign = jnp.where(jnp.arange(HEAD_DIM) % 2 == 0, -1.0, 1.0).astype(F32)
    cos = jnp.tile(jnp.cos(ang), (1, 2))
    sin = jnp.tile(jnp.sin(ang) * sign[None], (1, 2))
    return cos, sin


def qk_prep(x, gain, rope_tabs, scale, transposed=False):
    l, w = x.shape
    tm = _row_tile(l, 512)
    head = jnp.arange(w) // HEAD_DIM
    bd = (head[:, None] == head[None, :]).astype(BF16)
    gain_t = jnp.tile(gain.reshape(1, HEAD_DIM), (1, w // HEAD_DIM))
    row = lambda i: (i, 0)
    fixed = lambda i: (0, 0)
    in_specs = [pl.BlockSpec((tm, w), row), pl.BlockSpec((1, w), fixed), pl.BlockSpec((w, w), fixed)]
    args = [x, gain_t, bd]
    if rope_tabs is not None:
        in_specs += [pl.BlockSpec((tm, 2 * HEAD_DIM), row)] * 2
        args += list(rope_tabs)
    kern = functools.partial(_qk_prep_kernel, rope=rope_tabs is not None, scale=scale, transposed=transposed)
    if transposed:
        out_spec, out_shape = pl.BlockSpec((w, tm), lambda i: (0, i)), (w, l)
    else:
        out_spec, out_shape = pl.BlockSpec((tm, w), row), (l, w)
    return pl.pallas_call(
        kern, grid=(l // tm,), in_specs=in_specs, out_specs=out_spec,
        out_shape=jax.ShapeDtypeStruct(out_shape, BF16),
        compiler_params=_cparams(("parallel",)), name="qk_prep",
    )(*args)


def _flash_kernel(qt_ref, k_ref, vt_ref, o_ref, qs_ref, s_ref, m_ref, l_ref, acc_ref, *, tk, nk):
    dh = HEAD_DIM
    rep = qt_ref.shape[0] // dh
    tq = qt_ref.shape[1]
    for r in range(rep):
        qs_ref[:, r * tq:(r + 1) * tq] = qt_ref[r * dh:(r + 1) * dh, :]
    m_ref[...] = jnp.full(m_ref.shape, NEG_BIG, F32)
    l_ref[...] = jnp.zeros(l_ref.shape, F32)
    acc_ref[...] = jnp.zeros(acc_ref.shape, F32)

    def scores(j):
        start = pl.multiple_of(jnp.minimum(j, nk - 1) * tk, tk)
        return _dot(k_ref[pl.ds(start, tk), :], qs_ref[...])

    def absorb(j, s):
        start = pl.multiple_of(j * tk, tk)
        m_old = m_ref[...]
        m_new = jnp.maximum(m_old, s.max(axis=0, keepdims=True))
        alpha = jnp.exp2(m_old - m_new)
        p = jnp.exp2(s - m_new)
        l_ref[...] = alpha * l_ref[...] + p.sum(axis=0, keepdims=True)
        acc_ref[...] = alpha * acc_ref[...] + _dot(vt_ref[:, pl.ds(start, tk)], p.astype(BF16))
        m_ref[...] = m_new

    s_ref[0] = scores(0)

    def body(i, carry):
        j = 2 * i
        s_ref[1] = scores(j + 1)
        absorb(j, s_ref[0])
        s_ref[0] = scores(j + 2)
        absorb(j + 1, s_ref[1])
        return carry

    lax.fori_loop(0, nk // 2, body, 0)
    if nk % 2:
        absorb(nk - 1, s_ref[0])
    out = acc_ref[...] / l_ref[...]
    for r in range(rep):
        o_ref[:, r * dh:(r + 1) * dh] = out[:, r * tq:(r + 1) * tq].T.astype(o_ref.dtype)


def _kv_chunk(lk):
    for tiles in (5, 4, 3, 2, 1):
        if lk % (tiles * 256) == 0:
            return tiles * 256
    raise ValueError(lk)


def gqa_attention(q_t, k_hm, v_t):
    wq, l = q_t.shape
    hkv, lk, dh = k_hm.shape
    wg = wq // hkv
    rep = wg // dh
    tq = _row_tile(l, 256)
    tk = _kv_chunk(lk)
    kern = functools.partial(_flash_kernel, tk=tk, nk=lk // tk)
    return pl.pallas_call(
        kern,
        grid=(hkv, l // tq),
        in_specs=[pl.BlockSpec((wg, tq), lambda g, i: (g, i)),
                  pl.BlockSpec((None, lk, dh), lambda g, i: (g, 0, 0)),
                  pl.BlockSpec((None, dh, lk), lambda g, i: (g, 0, 0))],
        out_specs=pl.BlockSpec((tq, wg), lambda g, i: (i, g)),
        out_shape=jax.ShapeDtypeStruct((l, wq), BF16),
        scratch_shapes=[pltpu.VMEM((dh, rep * tq), BF16), pltpu.VMEM((2, tk, rep * tq), F32),
                        pltpu.VMEM((1, rep * tq), F32), pltpu.VMEM((1, rep * tq), F32),
                        pltpu.VMEM((dh, rep * tq), F32)],
        compiler_params=_cparams(("parallel", "parallel")),
        name="gqa_attention",
    )(q_t, k_hm, v_t)


def _cmul(ar, ai, br, bi):
    return ar * br - ai * bi, ar * bi + ai * br


def _s5_operators(a_re, a_im, log_dt, b_re, b_im, c_re, c_im):
    t = S5_CHUNK
    gs = S5_GROUP
    hp = dict(precision=HIGHEST)
    dt = jnp.exp(log_dt)[..., None]
    zr, zi = a_re * dt, a_im * dt
    er = jnp.exp(zr)
    abr, abi = er * jnp.cos(zi), er * jnp.sin(zi)
    den = a_re * a_re + a_im * a_im
    fr = ((abr - 1.0) * a_re + abi * a_im) / den
    fi = (abi * a_re - (abr - 1.0) * a_im) / den
    bbr, bbi = _cmul(fr[..., None], fi[..., None], b_re, b_im)
    tau = jnp.arange(t + 1, dtype=F32)
    pr = jnp.exp(zr[..., None] * tau) * jnp.cos(zi[..., None] * tau)
    pi = jnp.exp(zr[..., None] * tau) * jnp.sin(zi[..., None] * tau)
    car, cai = _cmul(c_re[..., None], c_im[..., None], pr[:, :, None, :, :t], pi[:, :, None, :, :t])
    ktap = (jnp.einsum('dgqpt,dgpk->dgtqk', car, bbr, **hp) - jnp.einsum('dgqpt,dgpk->dgtqk', cai, bbi, **hp))
    ktp = jnp.concatenate([jnp.zeros_like(ktap), ktap], axis=2)
    win = jnp.stack([ktp[:, :, t - i:2 * t - i] for i in range(t)], axis=2)
    m_tot = (win[0] + win[1].transpose(0, 2, 1, 3, 4)).transpose(0, 1, 4, 2, 3)
    pw_r = jnp.stack([pr[0, ..., t - 1::-1], pr[1, ..., :t]])
    pw_i = jnp.stack([pi[0, ..., t - 1::-1], pi[1, ..., :t]])
    wr, wi = _cmul(pw_r[..., None], pw_i[..., None], bbr[:, :, :, None, :], bbi[:, :, :, None, :])
    w_in = jnp.stack([wr[0], wi[0], wr[1], wi[1]]).transpose(1, 3, 4, 0, 2)
    pv_r = jnp.stack([pr[0, ..., 1:], pr[1, ..., t:0:-1]])
    pv_i = jnp.stack([pi[0, ..., 1:], pi[1, ..., t:0:-1]])
    vr, vi = _cmul(c_re[..., None], c_im[..., None], pv_r[:, :, None], pv_i[:, :, None])
    v_out = jnp.stack([vr[0], -vi[0], vr[1], -vi[1]]).transpose(1, 0, 3, 4, 2)

    nb = S5_GROUPS // S5_LANE_GROUPS
    eye = jnp.eye(S5_LANE_GROUPS, dtype=F32)[None, None, :, None, None, :, None]
    blk = lambda x: x.reshape((nb, S5_LANE_GROUPS) + x.shape[1:]).transpose(0, 2, 1, 3, 4, 5)[:, :, :, :, :, None, :]
    m_op = (blk(m_tot) * eye).reshape(nb, t * LANES, t * LANES)
    w_op = (blk(w_in) * eye).reshape(nb, t * LANES, 4 * S5_LANE_GROUPS * S5_STATE)
    v_op = (blk(v_out) * eye).reshape(nb, 4 * S5_LANE_GROUPS * S5_STATE, t * LANES)
    return m_op.astype(BF16), w_op.astype(BF16), v_op.astype(BF16), pr[..., t], pi[..., t]


def _s5_layout_kernel(*refs):
    *u_refs, o_ref = refs
    rows = o_ref.shape[0]
    t = S5_CHUNK
    for b, u_ref in enumerate(u_refs):
        for i in range(t):
            o_ref[:, (b * t + i) * LANES:(b * t + i + 1) * LANES] = u_ref[pl.ds(i, rows, stride=t), :].astype(o_ref.dtype)


def _s5_state_in_kernel(u_ref, w_ref, *e_refs):
    e = _dot(u_ref[...], w_ref[...])
    q = e.shape[1] // len(e_refs)
    for part, e_ref in enumerate(e_refs):
        e_ref[...] = e[:, part * q:(part + 1) * q]


def _s5_scan_kernel(ar_ref, ai_ref, er_ref, ei_ref, sr_ref, si_ref, cr_ref, ci_ref, *, reverse):
    @pl.when(pl.program_id(0) == 0)
    def _():
        cr_ref[...] = jnp.zeros_like(cr_ref)
        ci_ref[...] = jnp.zeros_like(ci_ref)

    ar, ai = ar_ref[...], ai_ref[...]
    n = er_ref.shape[0]

    def body(k, carry):
        c = n - 1 - k if reverse else k
        sr, si = carry
        sr_ref[c] = sr
        si_ref[c] = si
        return ar * sr - ai * si + er_ref[c], ar * si + ai * sr + ei_ref[c]

    sr, si = lax.fori_loop(0, n, body, (cr_ref[...], ci_ref[...]))
    cr_ref[...] = sr
    ci_ref[...] = si


def _s5_out_kernel(u_ref, fr_ref, fi_ref, br_ref, bi_ref, m_ref, v_ref, y_ref):
    s = jnp.concatenate([fr_ref[...], fi_ref[...], br_ref[...], bi_ref[...]], axis=1).astype(BF16)
    y_ref[...] = _dot(u_ref[...], m_ref[...]) + _dot(s, v_ref[...])


def _s5_readout_kernel(y_ref, u_ref, d_ref, w_ref, b_ref, o_ref, ynat_ref):
    rows = y_ref.shape[0]
    t = S5_CHUNK
    nb = ynat_ref.shape[0]
    for b in range(nb):
        for i in range(t):
            ynat_ref[b, pl.ds(i, rows, stride=t), :] = y_ref[:, (b * t + i) * LANES:(b * t + i + 1) * LANES]
    y = jnp.concatenate([ynat_ref[b] for b in range(nb)], axis=1) + d_ref[...] * u_ref[...]
    y = 0.5 * y * (1.0 + jnp.tanh(math.sqrt(2.0 / math.pi) * (y + 0.044715 * (y * y * y))))
    gate = jax.nn.sigmoid(_dot(y.astype(BF16), w_ref[...]) + b_ref[...])
    o_ref[...] = (y * gate).astype(o_ref.dtype)


def s5_mix(u_ctx, u_lat, a_re, a_im, log_dt, b_re, b_im, c_re, c_im, d_skip, glu_w, glu_b):
    t = S5_CHUNK
    n_ctx, w = u_ctx.shape
    l = u_lat.shape[0]
    n_tok = n_ctx + l
    nch = n_tok // t
    tc = n_ctx // t
    n_tiles = nch // tc
    assert n_ctx == tc * t and tc % (2 * SUBLANES) == 0 and l % (tc * t) == 0
    nb = w // LANES
    cw = t * LANES
    sw = S5_LANE_GROUPS * S5_STATE
    n_state = S5_GROUPS * S5_STATE
    m_op, w_op, v_op, atr, ati = _s5_operators(a_re, a_im, log_dt, b_re, b_im, c_re, c_im)

    u_all = jnp.concatenate([u_ctx, u_lat], axis=0)
    u_ch = pl.pallas_call(
        _s5_layout_kernel,
        grid=(n_tiles,),
        in_specs=[pl.BlockSpec((tc * t, LANES), functools.partial(lambda b, i: (i, b), b)) for b in range(nb)],
        out_specs=pl.BlockSpec((tc, nb * cw), lambda i: (i, 0)),
        out_shape=jax.ShapeDtypeStruct((nch, nb * cw), BF16),
        compiler_params=_cparams(("parallel",)),
        name="s5_layout",
    )(*([u_all] * nb))

    u_blk = pl.BlockSpec((nch, cw), lambda b: (0, b))
    s_blk = pl.BlockSpec((nch, sw), lambda b: (0, b))
    states_in = pl.pallas_call(
        _s5_state_in_kernel,
        grid=(nb,),
        in_specs=[u_blk, pl.BlockSpec((None, cw, 4 * sw), lambda b: (b, 0, 0))],
        out_specs=[s_blk] * 4,
        out_shape=[jax.ShapeDtypeStruct((nch, n_state), F32)] * 4,
        compiler_params=_cparams(("parallel",)),
        name="s5_state_in",
    )(u_ch, w_op)

    slab = n_state // SUBLANES
    vec = pl.BlockSpec((SUBLANES, slab), lambda s: (0, 0))
    orders = (lambda s: (s, 0, 0),
              lambda s: (jnp.where(s == 0, 0, n_tiles - s), 0, 0))
    states = []
    for d in range(2):
        blk = pl.BlockSpec((tc, SUBLANES, slab), orders[d])
        s_re, s_im = pl.pallas_call(
            functools.partial(_s5_scan_kernel, reverse=bool(d)),
            grid=(n_tiles,),
            in_specs=[vec, vec, blk, blk],
            out_specs=[blk, blk],
            out_shape=[jax.ShapeDtypeStruct((nch, SUBLANES, slab), F32)] * 2,
            scratch_shapes=[pltpu.VMEM((SUBLANES, slab), F32)] * 2,
            compiler_params=_cparams(("arbitrary",)),
            name="s5_scan",
        )(atr[d].reshape(SUBLANES, slab), ati[d].reshape(SUBLANES, slab),
          states_in[2 * d].reshape(nch, SUBLANES, slab), states_in[2 * d + 1].reshape(nch, SUBLANES, slab))
        states += [s_re.reshape(nch, n_state), s_im.reshape(nch, n_state)]

    oc = cw // 4
    y_ch = pl.pallas_call(
        _s5_out_kernel,
        grid=(nb, cw // oc),
        in_specs=[pl.BlockSpec((nch, cw), lambda b, j: (0, b))] + [pl.BlockSpec((nch, sw), lambda b, j: (0, b))] * 4
                 + [pl.BlockSpec((None, cw, oc), lambda b, j: (b, 0, j)),
                    pl.BlockSpec((None, 4 * sw, oc), lambda b, j: (b, 0, j))],
        out_specs=pl.BlockSpec((nch, oc), lambda b, j: (0, b * (cw // oc) + j)),
        out_shape=jax.ShapeDtypeStruct((nch, nb * cw), F32),
        compiler_params=_cparams(("parallel", "parallel")),
        name="s5_out",
    )(u_ch, *states, m_op, v_op)

    row = lambda i: (i, 0)
    fixed = lambda i: (0, 0)
    return pl.pallas_call(
        _s5_readout_kernel,
        grid=(l // (tc * t),),
        in_specs=[pl.BlockSpec((tc, nb * cw), lambda i: (i + n_ctx // (tc * t), 0)), pl.BlockSpec((tc * t, w), row),
                  pl.BlockSpec((1, w), fixed), pl.BlockSpec((w, w), fixed), pl.BlockSpec((1, w), fixed)],
        out_specs=pl.BlockSpec((tc * t, w), row),
        out_shape=jax.ShapeDtypeStruct((l, w), BF16),
        scratch_shapes=[pltpu.VMEM((nb, tc * t, LANES), F32)],
        compiler_params=_cparams(("parallel",)),
        name="s5_readout",
    )(y_ch, u_lat, d_skip.reshape(1, w), glu_w.astype(BF16), glu_b.reshape(1, w))


def _first_max(vals, lane):
    m = vals.max(axis=1, keepdims=True)
    idx = jnp.where(vals == m, lane, jnp.int32(1 << 20)).min(axis=1, keepdims=True)
    return m, idx


def _stream_specs(n_ctx_tiles, tm, d):
    return (pl.BlockSpec((tm, d), lambda i: (jnp.clip(i, 0, max(n_ctx_tiles - 1, 0)), 0)),
            pl.BlockSpec((tm, d), lambda i: (jnp.maximum(i - n_ctx_tiles, 0), 0)))


def _stream_tile(xc_ref, xl_ref, n_ctx_tiles):
    if n_ctx_tiles == 0:
        return xl_ref[...]
    return jnp.where(pl.program_id(0) < n_ctx_tiles, xc_ref[...], xl_ref[...])


def _store_row_slabs(ref, x):
    rows = x.shape[0]
    for s in range(ROW_SLAB):
        ref[pl.ds(s, rows, stride=ROW_SLAB), :] = x[:, s * LANES:(s + 1) * LANES]


def _router_kernel(xc_ref, xl_ref, sc_ref, sh_ref, rw_ref, rb_ref, tri_ref,
                   hf_ref, te_ref, gt_ref, rk_ref, cnt_ref, run_ref, *, n_ctx_tiles):
    @pl.when(pl.program_id(0) == 0)
    def _():
        run_ref[...] = jnp.zeros_like(run_ref)

    hf = _stream_tile(xc_ref, xl_ref, n_ctx_tiles) * (1.0 + sc_ref[...]) + sh_ref[...]
    _store_row_slabs(hf_ref, hf)
    tm = hf.shape[0]
    scores = jax.nn.sigmoid(_dot(hf, rw_ref[...], precision=HIGHEST))
    biased = scores + rb_ref[...]
    lane = lax.broadcasted_iota(I32, (tm, N_EXPERTS), 1)
    grp = lane // (N_EXPERTS // N_EXPERT_GROUPS)
    lane_o = lax.broadcasted_iota(I32, (tm, LANES), 1)
    neg = jnp.float32(-jnp.inf)

    group_score = jnp.full((tm, LANES), neg, F32)
    for g in range(N_EXPERT_GROUPS):
        vals = jnp.where(grp == g, biased, neg)
        m1, i1 = _first_max(vals, lane)
        m2 = jnp.where(lane == i1, neg, vals).max(axis=1, keepdims=True)
        group_score = jnp.where(lane_o == g, m1 + m2, group_score)
    keep = jnp.zeros((tm, N_EXPERTS), F32)
    for _ in range(TOPK_GROUPS):
        _, gi = _first_max(group_score, lane_o)
        keep = jnp.where(grp == gi, 1.0, keep)
        group_score = jnp.where(lane_o == gi, neg, group_score)

    masked = jnp.where(keep > 0.0, biased, neg)
    member = jnp.zeros((tm, N_EXPERTS), F32)
    e_cols, g_cols = [], []
    for _ in range(TOP_K):
        _, ei = _first_max(masked, lane)
        hit = lane == ei
        g_cols.append(jnp.where(hit, scores, 0.0).sum(axis=1, keepdims=True))
        masked = jnp.where(hit, neg, masked)
        member = jnp.where(hit, 1.0, member)
        e_cols.append(ei)
    g_sum = g_cols[0]
    for gk in g_cols[1:]:
        g_sum = g_sum + gk

    before = _dot(tri_ref[...], member.astype(BF16)) + run_ref[...]
    te = jnp.zeros((tm, LANES), I32)
    rk = jnp.zeros((tm, LANES), I32)
    gt = jnp.zeros((tm, LANES), F32)
    for k in range(TOP_K):
        rank = jnp.where(lane == e_cols[k], before, 0.0).sum(axis=1, keepdims=True)
        te = jnp.where(lane_o == k, e_cols[k], te)
        rk = jnp.where(lane_o == k, rank.astype(I32), rk)
        gt = jnp.where(lane_o == k, ROUTED_SCALE * g_cols[k] / g_sum, gt)
    te_ref[...] = te
    rk_ref[...] = rk
    gt_ref[...] = gt
    run_ref[...] += member.sum(axis=0, keepdims=True)
    cnt_ref[...] = run_ref[...]


def moe_route(x_ctx, x_lat, sc2, sh2, n_ctx_tiles, router_w, router_bias):
    d = x_lat.shape[1]
    tm = MOE_TILE
    n = n_ctx_tiles * tm + x_lat.shape[0]
    tri = (jnp.arange(tm)[None, :] < jnp.arange(tm)[:, None]).astype(BF16)
    row = lambda i: (i, 0)
    fixed = lambda i: (0, 0)
    seg = lambda i: (jnp.where(i < n_ctx_tiles, 1, 0), 0, 0)
    kern = functools.partial(_router_kernel, n_ctx_tiles=n_ctx_tiles)
    return pl.pallas_call(
        kern,
        grid=(n // tm,),
        in_specs=[*_stream_specs(n_ctx_tiles, tm, d), pl.BlockSpec((None, 1, d), seg), pl.BlockSpec((None, 1, d), seg),
                  pl.BlockSpec((d, N_EXPERTS), fixed), pl.BlockSpec((1, N_EXPERTS), fixed),
                  pl.BlockSpec((tm, tm), fixed)],
        out_specs=[pl.BlockSpec((tm * ROW_SLAB, LANES), row), pl.BlockSpec((tm, LANES), row),
                   pl.BlockSpec((tm, LANES), row), pl.BlockSpec((tm, LANES), row),
                   pl.BlockSpec((1, N_EXPERTS), fixed)],
        out_shape=[jax.ShapeDtypeStruct((n * ROW_SLAB, LANES), F32), jax.ShapeDtypeStruct((n, LANES), I32),
                   jax.ShapeDtypeStruct((n, LANES), F32), jax.ShapeDtypeStruct((n, LANES), I32),
                   jax.ShapeDtypeStruct((1, N_EXPERTS), F32)],
        scratch_shapes=[pltpu.VMEM((1, N_EXPERTS), F32)],
        compiler_params=_cparams(("arbitrary",)),
        name="moe_route",
    )(x_ctx, x_lat, sc2, sh2, router_w, router_bias.reshape(1, N_EXPERTS), tri)


def _slots_kernel(te_ref, rk_ref, start_ref, o_ref):
    tm = te_ref.shape[0]
    lane = lax.broadcasted_iota(I32, (tm, N_EXPERTS), 1)
    lane_o = lax.broadcasted_iota(I32, (tm, LANES), 1)
    te = te_ref[...]
    out = rk_ref[...]
    for k in range(TOP_K):
        first = jnp.where(lane == te[:, k:k + 1], start_ref[...], 0.0).sum(axis=1, keepdims=True)
        out = jnp.where(lane_o == k, out + first.astype(I32), out)
    o_ref[...] = out


def moe_slots(te, rk, start):
    n = te.shape[0]
    tm = MOE_TILE
    row = lambda i: (i, 0)
    out = pl.pallas_call(
        _slots_kernel,
        grid=(n // tm,),
        in_specs=[pl.BlockSpec((tm, LANES), row), pl.BlockSpec((tm, LANES), row),
                  pl.BlockSpec((1, N_EXPERTS), lambda i: (0, 0))],
        out_specs=pl.BlockSpec((tm, LANES), row),
        out_shape=jax.ShapeDtypeStruct((n, LANES), I32),
        compiler_params=_cparams(("parallel",)),
        name="moe_slots",
    )(te, rk, start.astype(F32).reshape(1, N_EXPERTS))
    return out[:, :TOP_K].reshape(-1)


def _slab_rows(ref, row, n_rows):
    first = row * ROW_SLAB
    if not isinstance(first, int):
        first = pl.multiple_of(first, ROW_SLAB)
    return ref.at[pl.ds(first, n_rows * ROW_SLAB), :]


def _dispatch_kernel(dest_ref, hf_ref, xs_hbm, zbuf, sem, zsem, *, n_assign):
    n_rows = dest_ref.shape[0]

    @pl.when(pl.program_id(0) == 0)
    def _():
        zbuf[...] = jnp.zeros_like(zbuf)
        tail = pltpu.make_async_copy(zbuf, _slab_rows(xs_hbm, n_assign, EXPERT_BLOCK), zsem)
        tail.start()
        tail.wait()

    def body(r, carry):
        src = _slab_rows(hf_ref, r, 1)
        for k in range(TOP_K):
            pltpu.make_async_copy(src, _slab_rows(xs_hbm, dest_ref[r * TOP_K + k], 1), sem).start(priority=k % 2)
        return carry
    lax.fori_loop(0, n_rows // TOP_K, body, 0)
    for _ in range(TOP_K):
        pltpu.make_async_copy(hf_ref, hf_ref, sem).wait()


def moe_dispatch(dest, hf_slabs):
    n_assign = dest.shape[0]
    tm = MOE_TILE
    n_rows = tm * TOP_K
    kern = functools.partial(_dispatch_kernel, n_assign=n_assign)
    return pl.pallas_call(
        kern,
        grid=(n_assign // n_rows,),
        in_specs=[pl.BlockSpec((n_rows,), lambda i: (i,), memory_space=pltpu.SMEM),
                  pl.BlockSpec((tm * ROW_SLAB, LANES), lambda i: (i, 0))],
        out_specs=pl.BlockSpec(memory_space=pl.ANY),
        out_shape=jax.ShapeDtypeStruct(((n_assign + EXPERT_BLOCK) * ROW_SLAB, LANES), F32),
        scratch_shapes=[pltpu.VMEM((EXPERT_BLOCK * ROW_SLAB, LANES), F32), pltpu.SemaphoreType.DMA(()),
                        pltpu.SemaphoreType.DMA(())],
        compiler_params=_cparams(("arbitrary",)),
        name="moe_dispatch",
    )(dest, hf_slabs)


def _row_gather(idx_ref, base, count, src_hbm, dst, sem):
    group = 8

    def body(g, carry):
        for k in range(group):
            j = g * group + k
            cp = pltpu.make_async_copy(_slab_rows(src_hbm, idx_ref[base + j], 1), _slab_rows(dst, j, 1), sem)
            cp.start(priority=k % 2)
        return carry
    lax.fori_loop(0, count // group, body, 0)


def _wait_rows(dst, sem):
    pltpu.make_async_copy(dst, dst, sem).wait()


def _gathered_rows(buf, first, rows, stride):
    return jnp.concatenate(
        [buf[pl.ds(first * ROW_SLAB + s, rows, stride=stride * ROW_SLAB), :] for s in range(ROW_SLAB)], axis=1)


def _swiglu(x, wg, wu, wd):
    gate = _dot(x, wg)
    up = _dot(x, wu)
    return _dot((gate * jax.nn.sigmoid(gate) * up).astype(BF16), wd)


def _valid_row_copies(ybuf, ys_hbm, row0, valid, sem):
    out = [(valid == EXPERT_BLOCK,
            pltpu.make_async_copy(_slab_rows(ybuf, 0, EXPERT_BLOCK), _slab_rows(ys_hbm, row0, EXPERT_BLOCK), sem))]
    part = valid < EXPERT_BLOCK
    size = EXPERT_BLOCK // 2
    while size >= 1:
        off = valid & ~(2 * size - 1)
        out.append((part & ((valid & size) != 0),
                    pltpu.make_async_copy(_slab_rows(ybuf, off, size), _slab_rows(ys_hbm, row0 + off, size), sem)))
        size //= 2
    return out


def _expert_kernel(be_ref, r0_ref, nv_ref, ws_ref, nx_ref, xs_hbm, wg_hbm, wu_hbm, wd_hbm, ys_hbm,
                   xbuf, ybuf, wg_st, wu_st, wd_st, wg_bf, wu_bf, wd_bf, sem_in, sem_out, sem_w, *, layer):
    b = pl.program_id(0)
    n_blocks = pl.num_programs(0)
    slot = b % 2

    def weight_copies(e, ws):
        return [pltpu.make_async_copy(hbm.at[layer, e], st.at[ws], sem_w.at[ws])
                for hbm, st in ((wg_hbm, wg_st), (wu_hbm, wu_st), (wd_hbm, wd_st))]

    @pl.when(b == 0)
    def _():
        for cp in weight_copies(be_ref[0], 0):
            cp.start()

    @pl.when(ws_ref[b] >= 0)
    def _():
        ws = ws_ref[b]
        for cp in weight_copies(be_ref[b], ws):
            cp.wait()
        wg_bf[...] = wg_st[ws].astype(BF16)
        wu_bf[...] = wu_st[ws].astype(BF16)
        wd_bf[...] = wd_st[ws].astype(BF16)

        @pl.when(nx_ref[b] >= 0)
        def _():
            for cp in weight_copies(nx_ref[b], 1 - ws):
                cp.start()

    def fetch(blk, sl):
        return pltpu.make_async_copy(_slab_rows(xs_hbm, r0_ref[blk], EXPERT_BLOCK), xbuf.at[sl], sem_in.at[sl])

    def drain(blk, sl):
        for cond, cp in _valid_row_copies(ybuf.at[sl], ys_hbm, r0_ref[blk], nv_ref[blk], sem_out.at[sl]):
            pl.when(cond)(cp.wait)

    @pl.when(b == 0)
    def _():
        for k in range(EXPERT_LOOKAHEAD):
            pl.when(nv_ref[k] > 0)(fetch(k, k).start)

    nxt = jnp.minimum(b + EXPERT_LOOKAHEAD, n_blocks - 1)

    @pl.when((b + EXPERT_LOOKAHEAD < n_blocks) & (nv_ref[nxt] > 0))
    def _():
        fetch(nxt, nxt % (EXPERT_LOOKAHEAD + 1)).start()

    @pl.when(b >= 2)
    def _():
        drain(jnp.maximum(b - 2, 0), slot)

    @pl.when(nv_ref[b] > 0)
    def _():
        xslot = b % (EXPERT_LOOKAHEAD + 1)
        fetch(b, xslot).wait()
        x = _gathered_rows(xbuf.at[xslot], 0, EXPERT_BLOCK, 1).astype(BF16)
        y = _swiglu(x, wg_bf[...], wu_bf[...], wd_bf[...])
        _store_row_slabs(ybuf.at[slot], y)
        for cond, cp in _valid_row_copies(ybuf.at[slot], ys_hbm, r0_ref[b], nv_ref[b], sem_out.at[slot]):
            pl.when(cond)(cp.start)

    @pl.when(b == n_blocks - 1)
    def _():
        drain(jnp.maximum(b - 1, 0), 1 - slot)
        drain(b, slot)


def moe_experts(xs_slabs, block_e, block_row0, block_valid, block_wslot, block_next_e, w_gate, w_up, w_down, layer):
    n_blocks = block_e.shape[0]
    d, ff = w_gate.shape[2:]
    blk_rows = EXPERT_BLOCK * ROW_SLAB
    n_assign = xs_slabs.shape[0] // ROW_SLAB - EXPERT_BLOCK
    any_spec = pl.BlockSpec(memory_space=pl.ANY)
    grid_spec = pltpu.PrefetchScalarGridSpec(
        num_scalar_prefetch=5,
        grid=(n_blocks,),
        in_specs=[any_spec] * 4,
        out_specs=any_spec,
        scratch_shapes=[pltpu.VMEM((EXPERT_LOOKAHEAD + 1, blk_rows, LANES), F32), pltpu.VMEM((2, blk_rows, LANES), F32),
                        pltpu.VMEM((2, d, ff), F32), pltpu.VMEM((2, d, ff), F32), pltpu.VMEM((2, ff, d), F32),
                        pltpu.VMEM((d, ff), BF16), pltpu.VMEM((d, ff), BF16), pltpu.VMEM((ff, d), BF16),
                        pltpu.SemaphoreType.DMA((EXPERT_LOOKAHEAD + 1,)), pltpu.SemaphoreType.DMA((2,)),
                        pltpu.SemaphoreType.DMA((2,))],
    )
    return pl.pallas_call(
        functools.partial(_expert_kernel, layer=layer),
        grid_spec=grid_spec,
        out_shape=jax.ShapeDtypeStruct((n_assign * ROW_SLAB, LANES), F32),
        compiler_params=_cparams(("arbitrary",)),
        name="moe_experts",
    )(block_e, block_row0, block_valid, block_wslot, block_next_e, xs_slabs, w_gate, w_up, w_down)


def _combine_kernel(cur_ref, nxt_ref, ys_hbm, xc_ref, xl_ref, hf_ref, gt_ref, g2_ref, sg_ref, su_ref, sd_ref,
                    lg_ref, lb_ref, *rest, n_ctx_tiles):
    *o_refs, ybuf, sem = rest
    i = pl.program_id(0)
    n_tiles = pl.num_programs(0)
    tm = xl_ref.shape[0]
    n_rows = tm * TOP_K
    slot = i % 2

    @pl.when(i == 0)
    def _():
        _row_gather(cur_ref, 0, n_rows, ys_hbm, ybuf.at[0], sem.at[0])

    @pl.when(i + 1 < n_tiles)
    def _():
        _row_gather(nxt_ref, 0, n_rows, ys_hbm, ybuf.at[1 - slot], sem.at[1 - slot])

    hf = _gathered_rows(hf_ref, 0, tm, 1).astype(BF16)
    y = _swiglu(hf, sg_ref[...], su_ref[...], sd_ref[...])

    _wait_rows(ybuf.at[slot], sem.at[slot])
    for k in range(TOP_K):
        y = y + gt_ref[:, k:k + 1] * _gathered_rows(ybuf.at[slot], k, tm, TOP_K)
    r = DEEPNORM_ALPHA * _stream_tile(xc_ref, xl_ref, n_ctx_tiles) + g2_ref[...] * y
    res = _layer_norm_rows(r, lg_ref[...], lb_ref[...])
    if n_ctx_tiles == 0:
        o_refs[0][...] = res
    else:
        oc_ref, ol_ref = o_refs

        @pl.when(i < n_ctx_tiles)
        def _():
            oc_ref[...] = res

        @pl.when(i >= n_ctx_tiles)
        def _():
            ol_ref[...] = res


def moe_combine(dest, ys_slabs, x_ctx, x_lat, hf_slabs, gate, g2, n_ctx_tiles, sh_gate, sh_up, sh_down, ln_g, ln_b):
    d = x_lat.shape[1]
    tm = MOE_TILE
    n_tiles = n_ctx_tiles + x_lat.shape[0] // tm
    n_rows = tm * TOP_K
    ff = sh_gate.shape[1]
    row = lambda i: (i, 0)
    fixed = lambda i: (0, 0)
    seg = lambda i: (jnp.where(i < n_ctx_tiles, 1, 0), 0, 0)
    ctx_spec, lat_spec = _stream_specs(n_ctx_tiles, tm, d)
    lat_out = jax.ShapeDtypeStruct(x_lat.shape, F32)
    if n_ctx_tiles == 0:
        out_specs, out_shape = [lat_spec], [lat_out]
    else:
        out_specs, out_shape = [ctx_spec, lat_spec], [jax.ShapeDtypeStruct(x_ctx.shape, F32), lat_out]
    kern = functools.partial(_combine_kernel, n_ctx_tiles=n_ctx_tiles)
    return pl.pallas_call(
        kern,
        grid=(n_tiles,),
        in_specs=[pl.BlockSpec((n_rows,), lambda i: (i,), memory_space=pltpu.SMEM),
                  pl.BlockSpec((n_rows,), lambda i: (jnp.minimum(i + 1, n_tiles - 1),), memory_space=pltpu.SMEM),
                  pl.BlockSpec(memory_space=pl.ANY),
                  ctx_spec, lat_spec, pl.BlockSpec((tm * ROW_SLAB, LANES), row), pl.BlockSpec((tm, LANES), row),
                  pl.BlockSpec((None, 1, d), seg),
                  pl.BlockSpec((d, ff), fixed), pl.BlockSpec((d, ff), fixed), pl.BlockSpec((ff, d), fixed),
                  pl.BlockSpec((1, d), fixed), pl.BlockSpec((1, d), fixed)],
        out_specs=out_specs,
        out_shape=out_shape,
        scratch_shapes=[pltpu.VMEM((2, n_rows * ROW_SLAB, LANES), F32), pltpu.SemaphoreType.DMA((2,))],
        compiler_params=_cparams(("arbitrary",)),
        name="moe_combine",
    )(dest, dest, ys_slabs, x_ctx, x_lat, hf_slabs, gate, g2,
      sh_gate.astype(BF16), sh_up.astype(BF16), sh_down.astype(BF16), ln_g.reshape(1, d), ln_b.reshape(1, d))


def moe_layer(x_ctx, x_lat, sc2, sh2, g2, n_ctx_tiles, layer, router_w, router_bias, w_gate, w_up, w_down,
              sh_gate, sh_up, sh_down, ln_g, ln_b):
    n = n_ctx_tiles * MOE_TILE + x_lat.shape[0]
    hf, te, gt, rk, cnt = moe_route(x_ctx, x_lat, sc2, sh2, n_ctx_tiles, router_w, router_bias)
    counts = cnt[0].astype(I32)
    start = jnp.cumsum(counts) - counts
    experts = jnp.arange(N_EXPERTS, dtype=I32)
    dest = moe_slots(te, rk, start)
    nb = (counts + EXPERT_BLOCK - 1) // EXPERT_BLOCK
    blk_end = jnp.cumsum(nb)
    blk_start = blk_end - nb
    n_blocks = n * TOP_K // EXPERT_BLOCK + N_EXPERTS
    blk = jnp.arange(n_blocks, dtype=I32)
    bb = jnp.minimum(blk, blk_end[-1] - 1)[:, None]
    own = (blk_start[None, :] <= bb) & (bb < blk_end[None, :])
    sel = lambda v: jnp.sum(jnp.where(own, v[None, :], 0), axis=1)
    j = bb[:, 0] - sel(blk_start)
    block_e = sel(experts)
    block_row0 = sel(start) + j * EXPERT_BLOCK
    block_valid = jnp.where(blk < blk_end[-1], jnp.clip(sel(counts) - j * EXPERT_BLOCK, 0, EXPERT_BLOCK), 0)
    used = nb > 0
    ordinal = jnp.cumsum(used.astype(I32)) - 1
    later = used[None, :] & (experts[None, :] > experts[:, None])
    next_used = jnp.min(jnp.where(later, experts[None, :], N_EXPERTS), axis=1)
    next_used = jnp.where(next_used < N_EXPERTS, next_used, -1)
    first = (blk < blk_end[-1]) & (j == 0)
    block_wslot = jnp.where(first, sel(ordinal) % 2, -1)
    block_next_e = jnp.where(first, sel(next_used), -1)
    xs = moe_dispatch(dest, hf)
    ys = moe_experts(xs, block_e, block_row0, block_valid, block_wslot, block_next_e, w_gate, w_up, w_down, layer)
    return moe_combine(dest, ys, x_ctx, x_lat, hf, gt, g2, n_ctx_tiles, sh_gate, sh_up, sh_down, ln_g, ln_b)


def kernel(x, c, ctx, c_ctx, w_mod, b_mod, ln_mix_g, ln_mix_b, ln_ffn_g, ln_ffn_b, ab_w_in, ab_w_out, na_rpb,
           hy_conv_w, hy_conv_b, hy_f_w1, hy_f_b1, hy_f_freq, hy_f_w2, hy_f_b2, hy_f_w3, hy_skip, cd_w_in, cd_w_out,
           q_norm_g, k_norm_g, s5_a_re, s5_a_im, s5_log_dt, s5_b_re, s5_b_im, s5_c_re, s5_c_im, s5_d, s5_glu_w,
           s5_glu_b, router_w, router_bias, exp_w_gate, exp_w_up, exp_w_down, sh_w_gate, sh_w_up, sh_w_down):
    b, l, d = x.shape
    assert b == 1
    n_ctx = ctx.shape[1]
    assert n_ctx == MOE_TILE
    xs = x[0]
    cs = ctx[0]
    cmat = jnp.zeros((SUBLANES, d), F32).at[0].set(c[0]).at[1].set(c_ctx)
    mods = modulation_all(cmat, w_mod, b_mod).reshape(DEPTH, SUBLANES, 6, d)
    qscale = HEAD_DIM ** -0.5

    for i in range(DEPTH):
        need_ctx = i < DEPTH - 1
        m = mods[i]
        sh1, sc1, g1, sh2, sc2, g2 = [m[0:1, t] for t in range(6)]
        csh1, csc1, cg1, csh2, csc2, cg2 = [m[1:2, t] for t in range(6)]
        j = i // 2
        if i % 2 == 0:
            filt = (hy_conv_w[j], hy_conv_b[j], hy_f_w1[j], hy_f_b1[j], hy_f_freq[j], hy_f_w2[j], hy_f_b2[j],
                    hy_f_w3[j], hy_skip[j])
            splits = (NA_WIDTH, NA_WIDTH, NA_WIDTH, 3 * HY_WIDTH)
            dts = (BF16, BF16, BF16, F32)
            scl = (qscale, 1.0, 1.0, 1.0)
            q_l, k_l, v_l, u_l = mod_project(xs, sc1, sh1, ab_w_in[j], splits, dts, scl)
            q_c, k_c, v_c, u_c = mod_project(cs, csc1, csh1, ab_w_in[j], splits, dts, scl)
            a_lat = neighbourhood_attention(q_l, k_l, v_l, k_c, v_c, na_rpb[j])
            y_hy = hyena_long(u_l, *filt)
            xs_new = outproj_ln(a_lat, y_hy, ab_w_out[j], xs, g1, ln_mix_g[i], ln_mix_b[i])
            if need_ctx:
                a_ctx = context_attention(q_c, k_c, v_c)
                yc_hy = hyena_long(u_c, *filt)
                cs = outproj_ln(a_ctx, yc_hy, ab_w_out[j], cs, cg1, ln_mix_g[i], ln_mix_b[i])
            xs = xs_new
        else:
            splits = (GQA_WIDTH, GQA_KV_WIDTH, GQA_KV_WIDTH, S5_WIDTH)
            q_l, k_l, v_l, u_l = mod_project(xs, sc1, sh1, cd_w_in[j], splits, (F32, F32, BF16, F32))
            k_c, v_c, u_c = mod_project(cs, csc1, csh1, cd_w_in[j][:, GQA_WIDTH:], splits[1:], (F32, BF16, F32))
            tabs = _rope_tables(l)
            qn_t = qk_prep(q_l, q_norm_g[j], tabs, qscale * math.log2(math.e), transposed=True)
            kn = qk_prep(k_l, k_norm_g[j], tabs, 1.0)
            kcn = qk_prep(k_c, k_norm_g[j], None, 1.0)
            k_all = jnp.concatenate([kn, kcn], axis=0)
            v_all = jnp.concatenate([v_l, v_c], axis=0)
            k_hm = k_all.reshape(-1, GQA_KV_HEADS, HEAD_DIM).transpose(1, 0, 2)
            v_t = v_all.T.reshape(GQA_KV_HEADS, HEAD_DIM, -1)
            att = gqa_attention(qn_t, k_hm, v_t)
            ssm = s5_mix(u_c, u_l, s5_a_re[j], s5_a_im[j], s5_log_dt[j], s5_b_re[j], s5_b_im[j],
                         s5_c_re[j], s5_c_im[j], s5_d[j], s5_glu_w[j], s5_glu_b[j])
            xs = outproj_ln(att, ssm, cd_w_out[j], xs, g1, ln_mix_g[i], ln_mix_b[i])
            assert not need_ctx

        moe_w = (i, router_w[i], router_bias[i], exp_w_gate, exp_w_up, exp_w_down,
                 sh_w_gate[i], sh_w_up[i], sh_w_down[i], ln_ffn_g[i], ln_ffn_b[i])
        stack2 = lambda lat, cx: jnp.stack([lat, cx])
        mod2 = (stack2(sc2, csc2), stack2(sh2, csh2), stack2(g2, cg2))
        if need_ctx:
            cs, xs = moe_layer(cs, xs, *mod2, n_ctx // MOE_TILE, *moe_w)
        else:
            (xs,) = moe_layer(xs, xs, *mod2, 0, *moe_w)
    return xs.reshape(b, l, d)
```

```python
import functools
import math

import jax
import jax.numpy as jnp
import numpy as np
from jax import lax
from jax.experimental import pallas as pl
from jax.experimental.pallas import tpu as pltpu

F32 = jnp.float32
BF16 = jnp.bfloat16
I32 = jnp.int32
HIGHEST = lax.Precision.HIGHEST

LANES = 128
SUBLANES = 8
VMEM_LIMIT = 56 * 1024 * 1024

D_MODEL = 1024
DEPTH = 2
GRID_W = 64
HEAD_DIM = 64
NA_HEADS = 8
NA_WIDTH = NA_HEADS * HEAD_DIM
NA_KH = 8
NA_KW = 16
HY_WIDTH = D_MODEL - NA_WIDTH
HY_BANDS = 16
HY_DECAY_PCT_MIN = 0.3
HY_DECAY_PCT_MAX = 1.5
HY_DECAY_TARGET = 1e-2
GQA_HEADS = 8
GQA_KV_HEADS = 2
GQA_WIDTH = GQA_HEADS * HEAD_DIM
GQA_KV_WIDTH = GQA_KV_HEADS * HEAD_DIM
ROPE_THETA = 10000.0
S5_WIDTH = D_MODEL - GQA_WIDTH
S5_GROUP = 16
S5_GROUPS = S5_WIDTH // S5_GROUP
S5_STATE = 64
N_EXPERTS = 256
TOP_K = 8
N_EXPERT_GROUPS = 8
TOPK_GROUPS = 4
EXPERT_FF = 256
ROUTED_SCALE = 2.5
EXPERT_BLOCK = 256
DEEPNORM_ALPHA = (2.0 * DEPTH) ** 0.25
LN_EPS = 1e-5
RMS_EPS = 1e-6

NEG_BIG = -1e30
NA_TILE_ROWS = 8
NA_KEY_ROWS = 16
NA_KEY_BLOCK_ROWS = 4
DFT_N1 = 128
S5_CHUNK = 16
S5_LANE_GROUPS = LANES // S5_GROUP
MOE_TILE = 256
EXPERT_LOOKAHEAD = 3
ROW_SLAB = D_MODEL // LANES

NT_DIMS = (((1,), (1,)), ((), ()))


def _cparams(sem, **kw):
    return pltpu.CompilerParams(dimension_semantics=sem, vmem_limit_bytes=VMEM_LIMIT, **kw)


def _dot(a, b, **kw):
    return jnp.dot(a, b, preferred_element_type=F32, **kw)


def _dot_nt(a, b):
    return lax.dot_general(a, b, NT_DIMS, preferred_element_type=F32)


def _row_tile(m, pref):
    return pref if m % pref == 0 else m


def _mod_kernel(c_ref, w_ref, b_ref, o_ref):
    cv = c_ref[...]
    s = cv * jax.nn.sigmoid(cv)
    o_ref[...] = _dot(s, w_ref[...], precision=HIGHEST) + b_ref[...]


def modulation_all(cmat, w_mod, b_mod):
    depth, d, n = w_mod.shape
    tn = 1536
    return pl.pallas_call(
        _mod_kernel,
        grid=(depth, n // tn),
        in_specs=[pl.BlockSpec((SUBLANES, d), lambda l, j: (0, 0)),
                  pl.BlockSpec((None, d, tn), lambda l, j: (l, 0, j)),
                  pl.BlockSpec((None, 1, tn), lambda l, j: (l, 0, j))],
        out_specs=pl.BlockSpec((None, SUBLANES, tn), lambda l, j: (l, 0, j)),
        out_shape=jax.ShapeDtypeStruct((depth, SUBLANES, n), F32),
        compiler_params=_cparams(("arbitrary", "arbitrary")),
        name="modulation",
    )(cmat, w_mod, b_mod.reshape(depth, 1, n))


def _proj_kernel(x_ref, sc_ref, sh_ref, w_ref, *o_refs, splits, scales):
    h = (x_ref[...] * (1.0 + sc_ref[...]) + sh_ref[...]).astype(BF16)
    off = 0
    for o_ref, wd, sc in zip(o_refs, splits, scales):
        y = _dot(h, w_ref[:, off:off + wd])
        if sc != 1.0:
            y = y * sc
        o_ref[...] = y.astype(o_ref.dtype)
        off += wd


def mod_project(x, sc, sh, w, splits, dtypes, scales=None):
    m, d = x.shape
    n = w.shape[1]
    assert sum(splits) == n
    scales = scales or (1.0,) * len(splits)
    tm = _row_tile(m, 512)
    kern = functools.partial(_proj_kernel, splits=tuple(splits), scales=tuple(scales))
    return pl.pallas_call(
        kern,
        grid=(m // tm,),
        in_specs=[pl.BlockSpec((tm, d), lambda i: (i, 0)),
                  pl.BlockSpec((1, d), lambda i: (0, 0)),
                  pl.BlockSpec((1, d), lambda i: (0, 0)),
                  pl.BlockSpec((d, n), lambda i: (0, 0))],
        out_specs=[pl.BlockSpec((tm, wd), lambda i: (i, 0)) for wd in splits],
        out_shape=[jax.ShapeDtypeStruct((m, wd), dt) for wd, dt in zip(splits, dtypes)],
        compiler_params=_cparams(("parallel",)),
        name="mod_project",
    )(x, sc, sh, w.astype(BF16))


def _layer_norm_rows(r, g, b):
    mu = jnp.mean(r, axis=-1, keepdims=True)
    c = r - mu
    var = jnp.mean(c * c, axis=-1, keepdims=True)
    return c * lax.rsqrt(var + LN_EPS) * g + b


def _outproj_ln_kernel(a_ref, b_ref, w_ref, x_ref, gate_ref, g_ref, beta_ref, o_ref):
    ka = a_ref.shape[1]
    y = _dot(a_ref[...], w_ref[:ka, :]) + _dot(b_ref[...], w_ref[ka:, :])
    r = DEEPNORM_ALPHA * x_ref[...] + gate_ref[...] * y
    o_ref[...] = _layer_norm_rows(r, g_ref[...], beta_ref[...])


def outproj_ln(a, b, w, x, gate, g, beta):
    m, d = x.shape
    ka, kb = a.shape[1], b.shape[1]
    tm = _row_tile(m, 512)
    row = lambda i: (i, 0)
    fixed = lambda i: (0, 0)
    return pl.pallas_call(
        _outproj_ln_kernel,
        grid=(m // tm,),
        in_specs=[pl.BlockSpec((tm, ka), row), pl.BlockSpec((tm, kb), row),
                  pl.BlockSpec((ka + kb, d), fixed), pl.BlockSpec((tm, d), row),
                  pl.BlockSpec((1, d), fixed), pl.BlockSpec((1, d), fixed), pl.BlockSpec((1, d), fixed)],
        out_specs=pl.BlockSpec((tm, d), row),
        out_shape=jax.ShapeDtypeStruct((m, d), F32),
        compiler_params=_cparams(("parallel",)),
        name="outproj_ln",
    )(a, b, w.astype(BF16), x, gate, g.reshape(1, d), beta.reshape(1, d))


NA_BIAS_PAD = NA_TILE_ROWS


def _na_key_start(t, rows):
    lo, hi = NA_TILE_ROWS * t - NA_KH // 2, rows - NA_KEY_ROWS
    return min(max(lo, 0), hi) if isinstance(t, int) else jnp.clip(lo, 0, hi)


def _na_bias_tables(rpb, rows):
    h = rpb.shape[0]
    ri = np.arange(NA_TILE_ROWS)
    kr = np.arange(NA_KEY_ROWS)
    c = np.arange(GRID_W)
    cs = np.clip(c - NA_KW // 2, 0, GRID_W - NA_KW)
    vc = (c[None, :] >= cs[:, None]) & (c[None, :] < cs[:, None] + NA_KW)
    dc = np.clip(c[None, :] - c[:, None] + NA_KW - 1, 0, 2 * NA_KW - 2)
    pick = (dc.reshape(-1)[:, None] == np.arange(2 * NA_KW - 1)[None, :]).astype(np.float32)
    colb = jnp.einsum('qb,hab->haq', pick, rpb, precision=HIGHEST).reshape(h, 2 * NA_KH - 1, GRID_W, GRID_W)
    colb = jnp.where(vc[None, None], colb, NEG_BIG)
    pad = jnp.zeros((h, NA_BIAS_PAD, GRID_W, GRID_W), F32)
    colb = jnp.concatenate([pad, colb, pad], axis=1)
    colb2 = jnp.concatenate([colb[:, :-1], colb[:, 1:]], axis=-1)
    colb2 = colb2.reshape((h // 2, 2) + colb2.shape[1:])

    def case(t):
        r = NA_TILE_ROWS * t + ri
        rs = np.clip(r - NA_KH // 2, 0, rows - NA_KH)
        krow = _na_key_start(t, rows) + kr
        vr = (krow[None, :] >= rs[:, None]) & (krow[None, :] < rs[:, None] + NA_KH)
        m = np.where(vr, 0.0, NEG_BIG).astype(np.float32)
        return np.repeat(np.repeat(m, GRID_W, axis=0), GRID_W, axis=1)

    n_tiles = rows // NA_TILE_ROWS
    return colb2, jnp.asarray(np.stack([case(0), case(1), case(n_tiles - 1)]))


def _pair_masks(shape):
    lane = lax.broadcasted_iota(I32, shape, 1)
    return lane < HEAD_DIM


def _na_kernel(q_ref, k0, k1, k2, k3, v0, v1, v2, v3, kc_ref, vc_ref, colb_ref, mask_ref, o_ref, *, rows):
    q = q_ref[...]
    lo = _pair_masks(q.shape)
    ks = (k0, k1, k2, k3)
    vs = (v0, v1, v2, v3)
    kb = k0.shape[0]
    t = pl.program_id(1)
    off = _na_key_start(t, rows) - NA_TILE_ROWS * t + (NA_KH - 1) + NA_BIAS_PAD
    key_rows_per_block = kb // GRID_W

    def bias(hh, i):
        first = i * key_rows_per_block
        return jnp.concatenate(
            [jnp.concatenate([colb_ref[hh, first + 2 * m - ri + off] for m in range(key_rows_per_block // 2)], axis=1)
             for ri in range(NA_TILE_ROWS)], axis=0)

    outs = []
    for hh in range(2):
        qh = jnp.where(lo if hh == 0 else jnp.logical_not(lo), q, jnp.zeros_like(q))
        s = [_dot_nt(qh, ks[i][...]) + bias(hh, i) + mask_ref[:, i * kb:(i + 1) * kb] for i in range(4)]
        s.append(_dot_nt(qh, kc_ref[...]))
        m = s[0].max(axis=1, keepdims=True)
        for si in s[1:]:
            m = jnp.maximum(m, si.max(axis=1, keepdims=True))
        p = [jnp.exp(si - m) for si in s]
        l = p[0].sum(axis=1, keepdims=True)
        for pi in p[1:]:
            l = l + pi.sum(axis=1, keepdims=True)
        acc = _dot(p[4].astype(BF16), vc_ref[...])
        for i in range(4):
            acc = acc + _dot(p[i].astype(BF16), vs[i][...])
        outs.append(acc / l)
    o_ref[...] = jnp.where(lo, outs[0], outs[1]).astype(o_ref.dtype)


def neighbourhood_attention(q, k, v, k_ctx, v_ctx, rpb):
    l, w = q.shape
    rows = l // GRID_W
    n_tiles = rows // NA_TILE_ROWS
    assert n_tiles >= 3 and rows % NA_TILE_ROWS == 0
    n_ctx = k_ctx.shape[0]
    tq = NA_TILE_ROWS * GRID_W
    kb = NA_KEY_BLOCK_ROWS * GRID_W
    colb2, rowmask = _na_bias_tables(rpb, rows)
    pair_w = 2 * HEAD_DIM

    def kv_spec(i):
        def imap(p, t):
            return (_na_key_start(t, rows) // NA_KEY_BLOCK_ROWS + i, p)
        return pl.BlockSpec((kb, pair_w), imap)

    def mask_map(p, t):
        return (jnp.where(t == 0, 0, jnp.where(t == n_tiles - 1, 2, 1)), 0, 0)

    return pl.pallas_call(
        functools.partial(_na_kernel, rows=rows),
        grid=(w // pair_w, n_tiles),
        in_specs=[pl.BlockSpec((tq, pair_w), lambda p, t: (t, p))]
                 + [kv_spec(i) for i in range(4)] + [kv_spec(i) for i in range(4)]
                 + [pl.BlockSpec((n_ctx, pair_w), lambda p, t: (0, p)),
                    pl.BlockSpec((n_ctx, pair_w), lambda p, t: (0, p)),
                    pl.BlockSpec((None,) + colb2.shape[1:], lambda p, t: (p, 0, 0, 0, 0)),
                    pl.BlockSpec((None, tq, NA_KEY_ROWS * GRID_W), mask_map)],
        out_specs=pl.BlockSpec((tq, pair_w), lambda p, t: (t, p)),
        out_shape=jax.ShapeDtypeStruct((l, w), BF16),
        compiler_params=_cparams(("parallel", "parallel")),
        name="neighbourhood_attention",
    )(q, k, k, k, k, v, v, v, v, k_ctx, v_ctx, colb2, rowmask)


def _ctx_attn_kernel(q_ref, k_ref, v_ref, o_ref):
    q = q_ref[...]
    lo = _pair_masks(q.shape)
    outs = []
    for hh in range(2):
        qh = jnp.where(lo if hh == 0 else jnp.logical_not(lo), q, jnp.zeros_like(q))
        s = _dot_nt(qh, k_ref[...])
        p = jnp.exp(s - s.max(axis=1, keepdims=True))
        outs.append(_dot(p.astype(BF16), v_ref[...]) / p.sum(axis=1, keepdims=True))
    o_ref[...] = jnp.where(lo, outs[0], outs[1]).astype(o_ref.dtype)


def context_attention(q, k, v):
    n, w = q.shape
    pair_w = 2 * HEAD_DIM
    spec = pl.BlockSpec((n, pair_w), lambda p: (0, p))
    return pl.pallas_call(
        _ctx_attn_kernel, grid=(w // pair_w,), in_specs=[spec, spec, spec], out_specs=spec,
        out_shape=jax.ShapeDtypeStruct((n, w), BF16),
        compiler_params=_cparams(("parallel",)), name="context_attention",
    )(q, k, v)


def _shortconv_kernel(u_ref, up_ref, un_ref, w_ref, b_ref, x0_ref, z_ref, *, n_tiles):
    i = pl.program_id(0)
    u = u_ref[...]
    tm = u.shape[0]
    prev_row = jnp.where(i > 0, up_ref[SUBLANES - 1:SUBLANES, :], 0.0)
    next_row = jnp.where(i < n_tiles - 1, un_ref[0:1, :], 0.0)
    row = lax.broadcasted_iota(I32, u.shape, 0)
    u_dn = jnp.where(row == 0, prev_row, pltpu.roll(u, 1, 0))
    u_up = jnp.where(row == tm - 1, next_row, pltpu.roll(u, tm - 1, 0))
    y = u_dn * w_ref[0:1, :] + u * w_ref[1:2, :] + u_up * w_ref[2:3, :] + b_ref[...]
    c = HY_WIDTH
    x0_ref[...] = y[:, :c]
    z_ref[...] = y[:, c:2 * c] * y[:, 2 * c:]


def hyena_gate(u, conv_w, conv_b):
    l, w3 = u.shape
    tm = _row_tile(l, 512)
    n_tiles = l // tm
    per = tm // SUBLANES
    last = l // SUBLANES - 1
    kern = functools.partial(_shortconv_kernel, n_tiles=n_tiles)
    return pl.pallas_call(
        kern,
        grid=(n_tiles,),
        in_specs=[pl.BlockSpec((tm, w3), lambda i: (i, 0)),
                  pl.BlockSpec((SUBLANES, w3), lambda i: (jnp.maximum(i * per - 1, 0), 0)),
                  pl.BlockSpec((SUBLANES, w3), lambda i: (jnp.minimum((i + 1) * per, last), 0)),
                  pl.BlockSpec((3, w3), lambda i: (0, 0)),
                  pl.BlockSpec((1, w3), lambda i: (0, 0))],
        out_specs=[pl.BlockSpec((tm, HY_WIDTH), lambda i: (i, 0))] * 2,
        out_shape=[jax.ShapeDtypeStruct((l, HY_WIDTH), F32)] * 2,
        compiler_params=_cparams(("parallel",)),
        name="hyena_gate",
    )(u, u, u, conv_w, conv_b.reshape(1, w3))


def _filter_kernel(bands_ref, w1t_ref, w1c_ref, w1s_ref, b1_ref, fr_ref, w2_ref, b2_ref, w3_ref, dl_ref,
                   taps_ref, asum_ref, *, l, tp):
    i = pl.program_id(0)
    hid_w = w2_ref.shape[0]
    c = HY_WIDTH
    denom = float(max(l - 1, 1))

    def pos(width):
        return (lax.broadcasted_iota(I32, (tp, width), 0) + i * tp).astype(F32)

    ang = (2.0 * math.pi / l) * pos(HY_BANDS) * bands_ref[...]
    pre = ((pos(hid_w) / denom) * w1t_ref[...]
           + _dot(jnp.cos(ang), w1c_ref[...], precision=HIGHEST)
           + _dot(-jnp.sin(ang), w1s_ref[...], precision=HIGHEST) + b1_ref[...])
    hid = jnp.sin(fr_ref[...] * pre)
    hid = jnp.sin(fr_ref[...] * (_dot(hid, w2_ref[...], precision=HIGHEST) + b2_ref[...]))
    taps = _dot(hid, w3_ref[...], precision=HIGHEST)
    pc = pos(c)
    window = jnp.exp(-(pc / denom) * dl_ref[...])
    fwd = taps[:, :c] * window
    bwd = jnp.where(pc == 0.0, 0.0, taps[:, c:] * window)
    taps_ref[:, :c] = fwd
    taps_ref[:, c:] = bwd

    @pl.when(i == 0)
    def _():
        asum_ref[...] = jnp.zeros_like(asum_ref)

    asum_ref[...] += jnp.sum(jnp.abs(fwd) + jnp.abs(bwd), axis=0, keepdims=True)


def hyena_filter_taps(l, f_w1, f_b1, f_freq, f_w2, f_b2, f_w3):
    c = HY_WIDTH
    hid = f_w2.shape[0]
    tp = _row_tile(l, 1024)
    bands = jnp.linspace(1e-4, HY_BANDS - 1, HY_BANDS, dtype=F32).reshape(1, HY_BANDS)
    deltas = jnp.abs(jnp.linspace(math.log(HY_DECAY_TARGET) / HY_DECAY_PCT_MAX,
                                  math.log(HY_DECAY_TARGET) / HY_DECAY_PCT_MIN, c, dtype=F32)).reshape(1, c)
    fixed = lambda i: (0, 0)
    full = lambda a: pl.BlockSpec(a.shape, fixed)
    args = (bands, f_w1[0:1], f_w1[1:1 + HY_BANDS], f_w1[1 + HY_BANDS:], f_b1.reshape(1, hid),
            f_freq.reshape(1, hid), f_w2, f_b2.reshape(1, hid), f_w3, deltas)
    kern = functools.partial(_filter_kernel, l=l, tp=tp)
    return pl.pallas_call(
        kern,
        grid=(l // tp,),
        in_specs=[full(a) for a in args],
        out_specs=[pl.BlockSpec((tp, 2 * c), lambda i: (i, 0)), pl.BlockSpec((1, c), fixed)],
        out_shape=[jax.ShapeDtypeStruct((l, 2 * c), F32), jax.ShapeDtypeStruct((1, c), F32)],
        compiler_params=_cparams(("arbitrary",)),
        name="hyena_filter",
    )(*args)


def _dft_tables(l):
    n = 2 * l
    n1 = DFT_N1
    n2 = n // n1
    k1 = jnp.arange(n1)[:, None]
    m1 = jnp.arange(n1 // 2)[None, :]
    ph1 = (2.0 * math.pi / n1) * ((k1 * m1) % n1).astype(F32)
    d1 = jnp.stack([jnp.cos(ph1), -jnp.sin(ph1)], axis=1).reshape(2 * n1, n1 // 2)
    d1_inv = d1.T
    j2 = jnp.arange(n2)
    ph2 = (2.0 * math.pi / n2) * ((j2[:, None] * j2[None, :]) % n2).astype(F32)
    cs, sn = jnp.cos(ph2), jnp.sin(ph2)
    f2 = jnp.concatenate([jnp.concatenate([cs, sn], axis=1), jnp.concatenate([-sn, cs], axis=1)], axis=0)
    pht = (2.0 * math.pi / n) * ((jnp.arange(n1)[:, None] * j2[None, :]) % n).astype(F32)[:, :, None]
    return d1.astype(BF16), d1_inv.astype(BF16), f2.astype(BF16), f2.T.astype(BF16), jnp.cos(pht), -jnp.sin(pht)


def _dft1_kernel(d_ref, x_ref, o_ref):
    n1h, m, c = x_ref.shape
    y = _dot(d_ref[...], x_ref[...].reshape(n1h * m, c).astype(BF16))
    o_ref[...] = y.reshape(o_ref.shape).astype(o_ref.dtype)


def dft_stage1(x, d1, n2):
    l, c = x.shape
    n1h = d1.shape[1]
    m = SUBLANES
    dk = jnp.kron(d1.astype(F32), jnp.eye(m, dtype=F32)).astype(BF16)
    out = pl.pallas_call(
        _dft1_kernel,
        grid=(n2 // m,),
        in_specs=[pl.BlockSpec(dk.shape, lambda j: (0, 0)), pl.BlockSpec((n1h, m, c), lambda j: (0, j, 0))],
        out_specs=pl.BlockSpec((d1.shape[0], m, c), lambda j: (0, j, 0)),
        out_shape=jax.ShapeDtypeStruct((d1.shape[0], n2, c), F32),
        compiler_params=_cparams(("parallel",)),
        name="dft_stage1",
    )(dk, x.reshape(n1h, n2, c))
    return out.reshape(d1.shape[0] // 2, 2, n2, c)


def _twiddled_stage2(f_ref, a_ref, twr_ref, twi_ref):
    shape = a_ref.shape[1:]
    twr = jnp.broadcast_to(twr_ref[...], shape)
    twi = jnp.broadcast_to(twi_ref[...], shape)
    ar, ai = a_ref[0].astype(F32), a_ref[1].astype(F32)
    a = jnp.concatenate([ar * twr - ai * twi, ar * twi + ai * twr], axis=0).astype(BF16)
    return _dot(f_ref[...], a), twr, twi


def _filter_spectrum_kernel(f_ref, a_ref, twr_ref, twi_ref, h_ref):
    n2 = a_ref.shape[1]
    c = h_ref.shape[2]
    x, _, _ = _twiddled_stage2(f_ref, a_ref, twr_ref, twi_ref)
    h_ref[0] = x[:n2, :c] + x[:n2, c:]
    h_ref[1] = x[n2:, :c] - x[n2:, c:]


def filter_spectrum(a_taps, f2, twr, twi):
    n1, _, n2, c2 = a_taps.shape
    c = c2 // 2
    tw = pl.BlockSpec((None, n2, 1), lambda i: (i, 0, 0))
    return pl.pallas_call(
        _filter_spectrum_kernel,
        grid=(n1,),
        in_specs=[pl.BlockSpec((2 * n2, 2 * n2), lambda i: (0, 0)),
                  pl.BlockSpec((None, 2, n2, c2), lambda i: (i, 0, 0, 0)), tw, tw],
        out_specs=pl.BlockSpec((None, 2, n2, c), lambda i: (i, 0, 0, 0)),
        out_shape=jax.ShapeDtypeStruct((n1, 2, n2, c), F32),
        compiler_params=_cparams(("parallel",)),
        name="filter_spectrum",
    )(f2, a_taps, twr, twi)


def _spectral_mix_kernel(f_ref, fi_ref, a_ref, h_ref, twr_ref, twi_ref, o_ref):
    n2 = a_ref.shape[1]
    x, twr, twi = _twiddled_stage2(f_ref, a_ref, twr_ref, twi_ref)
    xr, xi = x[:n2], x[n2:]
    hr, hi = h_ref[0], h_ref[1]
    y = jnp.concatenate([xr * hr - xi * hi, xr * hi + xi * hr], axis=0).astype(BF16)
    b = _dot(fi_ref[...], y)
    br, bi = b[:n2], b[n2:]
    o_ref[0] = (br * twr + bi * twi).astype(o_ref.dtype)
    o_ref[1] = (bi * twr - br * twi).astype(o_ref.dtype)


def spectral_mix(a_z, h, f2, f2_inv, twr, twi):
    n1, _, n2, c = a_z.shape
    blk = pl.BlockSpec((None, 2, n2, c), lambda i: (i, 0, 0, 0))
    mat = pl.BlockSpec((2 * n2, 2 * n2), lambda i: (0, 0))
    tw = pl.BlockSpec((None, n2, 1), lambda i: (i, 0, 0))
    return pl.pallas_call(
        _spectral_mix_kernel,
        grid=(n1,),
        in_specs=[mat, mat, blk, blk, tw, tw],
        out_specs=blk,
        out_shape=jax.ShapeDtypeStruct((n1, 2, n2, c), BF16),
        compiler_params=_cparams(("parallel",)),
        name="spectral_mix",
    )(f2, f2_inv, a_z, h, twr, twi)


def _hyena_out_kernel(di_ref, b_ref, x0_ref, z_ref, inv_ref, skip_ref, o_ref, *, inv_n):
    rows, m, c = b_ref.shape
    conv = (_dot(di_ref[...], b_ref[...].reshape(rows * m, c)) * inv_n).reshape(o_ref.shape)
    o_ref[...] = (x0_ref[...] * (conv * inv_ref[...] + z_ref[...] * skip_ref[...])).astype(o_ref.dtype)


def hyena_output(b, d1_inv, x0, z, inv_norm, skip):
    n1, _, n2, c = b.shape
    l = x0.shape[0]
    n1h = n1 // 2
    m = 2 * SUBLANES
    dk = jnp.kron(d1_inv.astype(F32), jnp.eye(m, dtype=F32)).astype(BF16)
    kern = functools.partial(_hyena_out_kernel, inv_n=1.0 / (2 * l))
    tile = pl.BlockSpec((n1h, m, c), lambda j: (0, j, 0))
    vec = pl.BlockSpec((1, 1, c), lambda j: (0, 0, 0))
    out = pl.pallas_call(
        kern,
        grid=(n2 // m,),
        in_specs=[pl.BlockSpec(dk.shape, lambda j: (0, 0)),
                  pl.BlockSpec((2 * n1, m, c), lambda j: (0, j, 0)),
                  tile, tile, vec, vec],
        out_specs=tile,
        out_shape=jax.ShapeDtypeStruct((n1h, n2, c), BF16),
        compiler_params=_cparams(("parallel",)),
        name="hyena_output",
    )(dk, b.reshape(2 * n1, n2, c), x0.reshape(n1h, n2, c), z.reshape(n1h, n2, c),
      inv_norm.reshape(1, 1, c), skip.reshape(1, 1, c))
    return out.reshape(l, c)


def _small_conv_kernel(d_ref, di_ref, z_ref, taps_ref, x0_ref, inv_ref, skip_ref, o_ref, *, inv_n):
    c = z_ref.shape[1]
    n = d_ref.shape[0] // 2
    zs = _dot(d_ref[...], z_ref[...], precision=HIGHEST)
    ts = _dot(d_ref[...], taps_ref[...], precision=HIGHEST)
    hr = ts[:n, :c] + ts[:n, c:]
    hi = ts[n:, :c] - ts[n:, c:]
    zr, zi = zs[:n], zs[n:]
    y = jnp.concatenate([zr * hr - zi * hi, zr * hi + zi * hr], axis=0)
    conv = _dot(di_ref[...], y, precision=HIGHEST) * inv_n
    o_ref[...] = (x0_ref[...] * (conv * inv_ref[...] + z_ref[...] * skip_ref[...])).astype(o_ref.dtype)


def hyena_output_short(z, taps, x0, inv_norm, skip):
    l, c = z.shape
    n = 2 * l
    ph = (2.0 * math.pi / n) * ((jnp.arange(n)[:, None] * jnp.arange(l)[None, :]) % n).astype(F32)
    d = jnp.concatenate([jnp.cos(ph), -jnp.sin(ph)], axis=0)
    di = jnp.concatenate([jnp.cos(ph), -jnp.sin(ph)], axis=0).T
    args = (d, di, z, taps, x0, inv_norm, skip.reshape(1, c))
    kern = functools.partial(_small_conv_kernel, inv_n=1.0 / n)
    return pl.pallas_call(
        kern,
        grid=(1,),
        in_specs=[pl.BlockSpec(a.shape, lambda i: (0, 0)) for a in args],
        out_specs=pl.BlockSpec((l, c), lambda i: (0, 0)),
        out_shape=jax.ShapeDtypeStruct((l, c), BF16),
        compiler_params=_cparams(("arbitrary",)),
        name="hyena_output_short",
    )(*args)


def hyena_long(u, conv_w, conv_b, f_w1, f_b1, f_freq, f_w2, f_b2, f_w3, skip):
    l = u.shape[0]
    x0, z = hyena_gate(u, conv_w, conv_b)
    taps, asum = hyena_filter_taps(l, f_w1, f_b1, f_freq, f_w2, f_b2, f_w3)
    inv_norm = 1.0 / asum
    if 2 * l < DFT_N1 * SUBLANES * 2:
        return hyena_output_short(z, taps, x0, inv_norm, skip)
    n2 = 2 * l // DFT_N1
    d1, d1_inv, f2, f2_inv, twr, twi = _dft_tables(l)
    h = filter_spectrum(dft_stage1(taps, d1, n2), f2, twr, twi)
    b = spectral_mix(dft_stage1(z, d1, n2), h, f2, f2_inv, twr, twi)
    return hyena_output(b, d1_inv, x0, z, inv_norm, skip)


def _head_sumsq(x, bd):
    sq = x * x
    hi = sq.astype(BF16)
    lo = (sq - hi.astype(F32)).astype(BF16)
    return _dot(hi, bd) + _dot(lo, bd)


def _qk_prep_kernel(x_ref, gain_ref, bd_ref, *rest, rope, scale, transposed):
    x = x_ref[...]
    w = x.shape[1]
    ms = _head_sumsq(x, bd_ref[...]) * (1.0 / HEAD_DIM)
    xn = x * lax.rsqrt(ms + RMS_EPS) * gain_ref[...]
    if rope:
        cos_ref, sin_ref, o_ref = rest
        reps = w // cos_ref.shape[1]
        cos = jnp.tile(cos_ref[...], (1, reps)) if reps > 1 else cos_ref[...]
        sin = jnp.tile(sin_ref[...], (1, reps)) if reps > 1 else sin_ref[...]
        lane = lax.broadcasted_iota(I32, x.shape, 1)
        partner = jnp.where(lane % 2 == 0, pltpu.roll(xn, w - 1, 1), pltpu.roll(xn, 1, 1))
        xn = xn * cos + partner * sin
    else:
        (o_ref,) = rest
    if scale != 1.0:
        xn = xn * scale
    if transposed:
        xn = xn.T
    o_ref[...] = xn.astype(o_ref.dtype)


def _rope_tables(l):
    half = HEAD_DIM // 2
    inv_freq = ROPE_THETA ** (-jnp.arange(0, half, 2, dtype=F32) / half)
    t = jnp.arange(l)
    row = (t // GRID_W).astype(F32)
    col = (t % GRID_W).astype(F32)
    ang = jnp.concatenate([jnp.repeat(row[:, None] * inv_freq[None], 2, axis=1),
                           jnp.repeat(col[:, None] * inv_freq[None], 2, axis=1)], axis=1)
    sign = jnp.where(jnp.arange(HEAD_DIM) % 2 == 0, -1.0, 1.0).astype(F32)
    cos = jnp.tile(jnp.cos(ang), (1, 2))
    sin = jnp.tile(jnp.sin(ang) * sign[None], (1, 2))
    return cos, sin


def qk_prep(x, gain, rope_tabs, scale, transposed=False):
    l, w = x.shape
    tm = _row_tile(l, 512)
    head = jnp.arange(w) // HEAD_DIM
    bd = (head[:, None] == head[None, :]).astype(BF16)
    gain_t = jnp.tile(gain.reshape(1, HEAD_DIM), (1, w // HEAD_DIM))
    row = lambda i: (i, 0)
    fixed = lambda i: (0, 0)
    in_specs = [pl.BlockSpec((tm, w), row), pl.BlockSpec((1, w), fixed), pl.BlockSpec((w, w), fixed)]
    args = [x, gain_t, bd]
    if rope_tabs is not None:
        in_specs += [pl.BlockSpec((tm, 2 * HEAD_DIM), row)] * 2
        args += list(rope_tabs)
    kern = functools.partial(_qk_prep_kernel, rope=rope_tabs is not None, scale=scale, transposed=transposed)
    if transposed:
        out_spec, out_shape = pl.BlockSpec((w, tm), lambda i: (0, i)), (w, l)
    else:
        out_spec, out_shape = pl.BlockSpec((tm, w), row), (l, w)
    return pl.pallas_call(
        kern, grid=(l // tm,), in_specs=in_specs, out_specs=out_spec,
        out_shape=jax.ShapeDtypeStruct(out_shape, BF16),
        compiler_params=_cparams(("parallel",)), name="qk_prep",
    )(*args)


def _flash_kernel(qt_ref, k_ref, vt_ref, o_ref, qs_ref, s_ref, m_ref, l_ref, acc_ref, *, tk, nk):
    dh = HEAD_DIM
    rep = qt_ref.shape[0] // dh
    tq = qt_ref.shape[1]
    for r in range(rep):
        qs_ref[:, r * tq:(r + 1) * tq] = qt_ref[r * dh:(r + 1) * dh, :]
    m_ref[...] = jnp.full(m_ref.shape, NEG_BIG, F32)
    l_ref[...] = jnp.zeros(l_ref.shape, F32)
    acc_ref[...] = jnp.zeros(acc_ref.shape, F32)

    def scores(j):
        start = pl.multiple_of(jnp.minimum(j, nk - 1) * tk, tk)
        return _dot(k_ref[pl.ds(start, tk), :], qs_ref[...])

    def absorb(j, s):
        start = pl.multiple_of(j * tk, tk)
        m_old = m_ref[...]
        m_new = jnp.maximum(m_old, s.max(axis=0, keepdims=True))
        alpha = jnp.exp2(m_old - m_new)
        p = jnp.exp2(s - m_new)
        l_ref[...] = alpha * l_ref[...] + p.sum(axis=0, keepdims=True)
        acc_ref[...] = alpha * acc_ref[...] + _dot(vt_ref[:, pl.ds(start, tk)], p.astype(BF16))
        m_ref[...] = m_new

    s_ref[0] = scores(0)

    def body(i, carry):
        j = 2 * i
        s_ref[1] = scores(j + 1)
        absorb(j, s_ref[0])
        s_ref[0] = scores(j + 2)
        absorb(j + 1, s_ref[1])
        return carry

    lax.fori_loop(0, nk // 2, body, 0)
    if nk % 2:
        absorb(nk - 1, s_ref[0])
    out = acc_ref[...] / l_ref[...]
    for r in range(rep):
        o_ref[:, r * dh:(r + 1) * dh] = out[:, r * tq:(r + 1) * tq].T.astype(o_ref.dtype)


def _kv_chunk(lk):
    for tiles in (5, 4, 3, 2, 1):
        if lk % (tiles * 256) == 0:
            return tiles * 256
    raise ValueError(lk)


def gqa_attention(q_t, k_hm, v_t):
    wq, l = q_t.shape
    hkv, lk, dh = k_hm.shape
    wg = wq // hkv
    rep = wg // dh
    tq = _row_tile(l, 256)
    tk = _kv_chunk(lk)
    kern = functools.partial(_flash_kernel, tk=tk, nk=lk // tk)
    return pl.pallas_call(
        kern,
        grid=(hkv, l // tq),
        in_specs=[pl.BlockSpec((wg, tq), lambda g, i: (g, i)),
                  pl.BlockSpec((None, lk, dh), lambda g, i: (g, 0, 0)),
                  pl.BlockSpec((None, dh, lk), lambda g, i: (g, 0, 0))],
        out_specs=pl.BlockSpec((tq, wg), lambda g, i: (i, g)),
        out_shape=jax.ShapeDtypeStruct((l, wq), BF16),
        scratch_shapes=[pltpu.VMEM((dh, rep * tq), BF16), pltpu.VMEM((2, tk, rep * tq), F32),
                        pltpu.VMEM((1, rep * tq), F32), pltpu.VMEM((1, rep * tq), F32),
                        pltpu.VMEM((dh, rep * tq), F32)],
        compiler_params=_cparams(("parallel", "parallel")),
        name="gqa_attention",
    )(q_t, k_hm, v_t)


def _cmul(ar, ai, br, bi):
    return ar * br - ai * bi, ar * bi + ai * br


def _s5_operators(a_re, a_im, log_dt, b_re, b_im, c_re, c_im):
    t = S5_CHUNK
    gs = S5_GROUP
    hp = dict(precision=HIGHEST)
    dt = jnp.exp(log_dt)[..., None]
    zr, zi = a_re * dt, a_im * dt
    er = jnp.exp(zr)
    abr, abi = er * jnp.cos(zi), er * jnp.sin(zi)
    den = a_re * a_re + a_im * a_im
    fr = ((abr - 1.0) * a_re + abi * a_im) / den
    fi = (abi * a_re - (abr - 1.0) * a_im) / den
    bbr, bbi = _cmul(fr[..., None], fi[..., None], b_re, b_im)
    tau = jnp.arange(t + 1, dtype=F32)
    pr = jnp.exp(zr[..., None] * tau) * jnp.cos(zi[..., None] * tau)
    pi = jnp.exp(zr[..., None] * tau) * jnp.sin(zi[..., None] * tau)
    car, cai = _cmul(c_re[..., None], c_im[..., None], pr[:, :, None, :, :t], pi[:, :, None, :, :t])
    ktap = (jnp.einsum('dgqpt,dgpk->dgtqk', car, bbr, **hp) - jnp.einsum('dgqpt,dgpk->dgtqk', cai, bbi, **hp))
    ktp = jnp.concatenate([jnp.zeros_like(ktap), ktap], axis=2)
    win = jnp.stack([ktp[:, :, t - i:2 * t - i] for i in range(t)], axis=2)
    m_tot = (win[0] + win[1].transpose(0, 2, 1, 3, 4)).transpose(0, 1, 4, 2, 3)
    pw_r = jnp.stack([pr[0, ..., t - 1::-1], pr[1, ..., :t]])
    pw_i = jnp.stack([pi[0, ..., t - 1::-1], pi[1, ..., :t]])
    wr, wi = _cmul(pw_r[..., None], pw_i[..., None], bbr[:, :, :, None, :], bbi[:, :, :, None, :])
    w_in = jnp.stack([wr[0], wi[0], wr[1], wi[1]]).transpose(1, 3, 4, 0, 2)
    pv_r = jnp.stack([pr[0, ..., 1:], pr[1, ..., t:0:-1]])
    pv_i = jnp.stack([pi[0, ..., 1:], pi[1, ..., t:0:-1]])
    vr, vi = _cmul(c_re[..., None], c_im[..., None], pv_r[:, :, None], pv_i[:, :, None])
    v_out = jnp.stack([vr[0], -vi[0], vr[1], -vi[1]]).transpose(1, 0, 3, 4, 2)

    nb = S5_GROUPS // S5_LANE_GROUPS
    lg = S5_LANE_GROUPS

    def spread(width, n_outer):
        c = jnp.arange(n_outer * width)
        lane = (c // width)[None, :] * (lg * width) + jnp.arange(lg)[:, None] * width + (c % width)[None, :]
        return (lane[:, :, None] == jnp.arange(n_outer * lg * width)[None, None, :]).astype(F32)

    def expand(x, pattern, width, n_outer):
        x = x.reshape((nb, lg) + x.shape[1:-2] + (n_outer * width,)).astype(BF16).astype(F32)
        return jnp.einsum(pattern, x, spread(width, n_outer)).astype(BF16)

    m_op = expand(m_tot, 'bgikc,gcl->bigkl', gs, t).reshape(nb, t * LANES, t * LANES)
    w_op = expand(w_in, 'bgikc,gcl->bigkl', S5_STATE, 4).reshape(nb, t * LANES, 4 * lg * S5_STATE)
    v_op = expand(v_out, 'bgrpc,gcl->brgpl', gs, t).reshape(nb, 4 * lg * S5_STATE, t * LANES)
    return m_op, w_op, v_op, pr[..., t], pi[..., t]


def _s5_layout_kernel(*refs):
    *u_refs, o_ref = refs
    rows = o_ref.shape[0]
    t = S5_CHUNK
    for b, u_ref in enumerate(u_refs):
        for i in range(t):
            o_ref[:, (b * t + i) * LANES:(b * t + i + 1) * LANES] = u_ref[pl.ds(i, rows, stride=t), :].astype(o_ref.dtype)


def _s5_state_in_kernel(u_ref, w_ref, *e_refs):
    e = _dot(u_ref[...], w_ref[...])
    q = e.shape[1] // len(e_refs)
    for part, e_ref in enumerate(e_refs):
        e_ref[...] = e[:, part * q:(part + 1) * q]


def _s5_scan_kernel(ar_ref, ai_ref, er_ref, ei_ref, sr_ref, si_ref, cr_ref, ci_ref, *, reverse):
    @pl.when(pl.program_id(0) == 0)
    def _():
        cr_ref[...] = jnp.zeros_like(cr_ref)
        ci_ref[...] = jnp.zeros_like(ci_ref)

    ar, ai = ar_ref[...], ai_ref[...]
    n = er_ref.shape[0]

    def body(k, carry):
        c = n - 1 - k if reverse else k
        sr, si = carry
        sr_ref[c] = sr
        si_ref[c] = si
        return ar * sr - ai * si + er_ref[c], ar * si + ai * sr + ei_ref[c]

    sr, si = lax.fori_loop(0, n, body, (cr_ref[...], ci_ref[...]))
    cr_ref[...] = sr
    ci_ref[...] = si


def _s5_out_kernel(u_ref, fr_ref, fi_ref, br_ref, bi_ref, m_ref, v_ref, y_ref):
    s = jnp.concatenate([fr_ref[...], fi_ref[...], br_ref[...], bi_ref[...]], axis=1).astype(BF16)
    y_ref[...] = _dot(u_ref[...], m_ref[...]) + _dot(s, v_ref[...])


def _s5_readout_kernel(y_ref, u_ref, d_ref, w_ref, b_ref, o_ref, ynat_ref):
    rows = y_ref.shape[0]
    t = S5_CHUNK
    nb = ynat_ref.shape[0]
    for b in range(nb):
        for i in range(t):
            ynat_ref[b, pl.ds(i, rows, stride=t), :] = y_ref[:, (b * t + i) * LANES:(b * t + i + 1) * LANES]
    y = jnp.concatenate([ynat_ref[b] for b in range(nb)], axis=1) + d_ref[...] * u_ref[...]
    y = 0.5 * y * (1.0 + jnp.tanh(math.sqrt(2.0 / math.pi) * (y + 0.044715 * (y * y * y))))
    gate = jax.nn.sigmoid(_dot(y.astype(BF16), w_ref[...]) + b_ref[...])
    o_ref[...] = (y * gate).astype(o_ref.dtype)


def s5_mix(u_ctx, u_lat, a_re, a_im, log_dt, b_re, b_im, c_re, c_im, d_skip, glu_w, glu_b):
    t = S5_CHUNK
    n_ctx, w = u_ctx.shape
    l = u_lat.shape[0]
    n_tok = n_ctx + l
    nch = n_tok // t
    tc = n_ctx // t
    n_tiles = nch // tc
    assert n_ctx == tc * t and tc % (2 * SUBLANES) == 0 and l % (tc * t) == 0
    nb = w // LANES
    cw = t * LANES
    sw = S5_LANE_GROUPS * S5_STATE
    n_state = S5_GROUPS * S5_STATE
    m_op, w_op, v_op, atr, ati = _s5_operators(a_re, a_im, log_dt, b_re, b_im, c_re, c_im)

    u_all = jnp.concatenate([u_ctx, u_lat], axis=0)
    u_ch = pl.pallas_call(
        _s5_layout_kernel,
        grid=(n_tiles,),
        in_specs=[pl.BlockSpec((tc * t, LANES), functools.partial(lambda b, i: (i, b), b)) for b in range(nb)],
        out_specs=pl.BlockSpec((tc, nb * cw), lambda i: (i, 0)),
        out_shape=jax.ShapeDtypeStruct((nch, nb * cw), BF16),
        compiler_params=_cparams(("parallel",)),
        name="s5_layout",
    )(*([u_all] * nb))

    u_blk = pl.BlockSpec((nch, cw), lambda b: (0, b))
    s_blk = pl.BlockSpec((nch, sw), lambda b: (0, b))
    states_in = pl.pallas_call(
        _s5_state_in_kernel,
        grid=(nb,),
        in_specs=[u_blk, pl.BlockSpec((None, cw, 4 * sw), lambda b: (b, 0, 0))],
        out_specs=[s_blk] * 4,
        out_shape=[jax.ShapeDtypeStruct((nch, n_state), F32)] * 4,
        compiler_params=_cparams(("parallel",)),
        name="s5_state_in",
    )(u_ch, w_op)

    slab = n_state // SUBLANES
    vec = pl.BlockSpec((SUBLANES, slab), lambda s: (0, 0))
    orders = (lambda s: (s, 0, 0),
              lambda s: (jnp.where(s == 0, 0, n_tiles - s), 0, 0))
    states = []
    for d in range(2):
        blk = pl.BlockSpec((tc, SUBLANES, slab), orders[d])
        s_re, s_im = pl.pallas_call(
            functools.partial(_s5_scan_kernel, reverse=bool(d)),
            grid=(n_tiles,),
            in_specs=[vec, vec, blk, blk],
            out_specs=[blk, blk],
            out_shape=[jax.ShapeDtypeStruct((nch, SUBLANES, slab), F32)] * 2,
            scratch_shapes=[pltpu.VMEM((SUBLANES, slab), F32)] * 2,
            compiler_params=_cparams(("arbitrary",)),
            name="s5_scan",
        )(atr[d].reshape(SUBLANES, slab), ati[d].reshape(SUBLANES, slab),
          states_in[2 * d].reshape(nch, SUBLANES, slab), states_in[2 * d + 1].reshape(nch, SUBLANES, slab))
        states += [s_re.reshape(nch, n_state), s_im.reshape(nch, n_state)]

    oc = cw // 4
    y_ch = pl.pallas_call(
        _s5_out_kernel,
        grid=(nb, cw // oc),
        in_specs=[pl.BlockSpec((nch, cw), lambda b, j: (0, b))] + [pl.BlockSpec((nch, sw), lambda b, j: (0, b))] * 4
                 + [pl.BlockSpec((None, cw, oc), lambda b, j: (b, 0, j)),
                    pl.BlockSpec((None, 4 * sw, oc), lambda b, j: (b, 0, j))],
        out_specs=pl.BlockSpec((nch, oc), lambda b, j: (0, b * (cw // oc) + j)),
        out_shape=jax.ShapeDtypeStruct((nch, nb * cw), F32),
        compiler_params=_cparams(("parallel", "parallel")),
        name="s5_out",
    )(u_ch, *states, m_op, v_op)

    row = lambda i: (i, 0)
    fixed = lambda i: (0, 0)
    return pl.pallas_call(
        _s5_readout_kernel,
        grid=(l // (tc * t),),
        in_specs=[pl.BlockSpec((tc, nb * cw), lambda i: (i + n_ctx // (tc * t), 0)), pl.BlockSpec((tc * t, w), row),
                  pl.BlockSpec((1, w), fixed), pl.BlockSpec((w, w), fixed), pl.BlockSpec((1, w), fixed)],
        out_specs=pl.BlockSpec((tc * t, w), row),
        out_shape=jax.ShapeDtypeStruct((l, w), BF16),
        scratch_shapes=[pltpu.VMEM((nb, tc * t, LANES), F32)],
        compiler_params=_cparams(("parallel",)),
        name="s5_readout",
    )(y_ch, u_lat, d_skip.reshape(1, w), glu_w.astype(BF16), glu_b.reshape(1, w))


def _first_max(vals, lane):
    m = vals.max(axis=1, keepdims=True)
    idx = jnp.where(vals == m, lane, jnp.int32(1 << 20)).min(axis=1, keepdims=True)
    return m, idx


def _stream_specs(n_ctx_tiles, tm, d):
    return (pl.BlockSpec((tm, d), lambda i: (jnp.clip(i, 0, max(n_ctx_tiles - 1, 0)), 0)),
            pl.BlockSpec((tm, d), lambda i: (jnp.maximum(i - n_ctx_tiles, 0), 0)))


def _stream_tile(xc_ref, xl_ref, n_ctx_tiles):
    if n_ctx_tiles == 0:
        return xl_ref[...]
    return jnp.where(pl.program_id(0) < n_ctx_tiles, xc_ref[...], xl_ref[...])


def _store_row_slabs(ref, x):
    rows = x.shape[0]
    for s in range(ROW_SLAB):
        ref[pl.ds(s, rows, stride=ROW_SLAB), :] = x[:, s * LANES:(s + 1) * LANES]


def _router_kernel(xc_ref, xl_ref, sc_ref, sh_ref, rw_ref, rb_ref, tri_ref,
                   hf_ref, te_ref, gt_ref, rk_ref, cnt_ref, run_ref, *, n_ctx_tiles):
    @pl.when(pl.program_id(0) == 0)
    def _():
        run_ref[...] = jnp.zeros_like(run_ref)

    hf = _stream_tile(xc_ref, xl_ref, n_ctx_tiles) * (1.0 + sc_ref[...]) + sh_ref[...]
    _store_row_slabs(hf_ref, hf)
    tm = hf.shape[0]
    scores = jax.nn.sigmoid(_dot(hf, rw_ref[...], precision=HIGHEST))
    biased = scores + rb_ref[...]
    lane = lax.broadcasted_iota(I32, (tm, N_EXPERTS), 1)
    grp = lane // (N_EXPERTS // N_EXPERT_GROUPS)
    lane_o = lax.broadcasted_iota(I32, (tm, LANES), 1)
    neg = jnp.float32(-jnp.inf)

    group_score = jnp.full((tm, LANES), neg, F32)
    for g in range(N_EXPERT_GROUPS):
        vals = jnp.where(grp == g, biased, neg)
        m1, i1 = _first_max(vals, lane)
        m2 = jnp.where(lane == i1, neg, vals).max(axis=1, keepdims=True)
        group_score = jnp.where(lane_o == g, m1 + m2, group_score)
    keep = jnp.zeros((tm, N_EXPERTS), F32)
    for _ in range(TOPK_GROUPS):
        _, gi = _first_max(group_score, lane_o)
        keep = jnp.where(grp == gi, 1.0, keep)
        group_score = jnp.where(lane_o == gi, neg, group_score)

    masked = jnp.where(keep > 0.0, biased, neg)
    member = jnp.zeros((tm, N_EXPERTS), F32)
    e_cols, g_cols = [], []
    for _ in range(TOP_K):
        _, ei = _first_max(masked, lane)
        hit = lane == ei
        g_cols.append(jnp.where(hit, scores, 0.0).sum(axis=1, keepdims=True))
        masked = jnp.where(hit, neg, masked)
        member = jnp.where(hit, 1.0, member)
        e_cols.append(ei)
    g_sum = g_cols[0]
    for gk in g_cols[1:]:
        g_sum = g_sum + gk

    before = _dot(tri_ref[...], member.astype(BF16)) + run_ref[...]
    te = jnp.zeros((tm, LANES), I32)
    rk = jnp.zeros((tm, LANES), I32)
    gt = jnp.zeros((tm, LANES), F32)
    for k in range(TOP_K):
        rank = jnp.where(lane == e_cols[k], before, 0.0).sum(axis=1, keepdims=True)
        te = jnp.where(lane_o == k, e_cols[k], te)
        rk = jnp.where(lane_o == k, rank.astype(I32), rk)
        gt = jnp.where(lane_o == k, ROUTED_SCALE * g_cols[k] / g_sum, gt)
    te_ref[...] = te
    rk_ref[...] = rk
    gt_ref[...] = gt
    run_ref[...] += member.sum(axis=0, keepdims=True)
    cnt_ref[...] = run_ref[...]


def moe_route(x_ctx, x_lat, sc2, sh2, n_ctx_tiles, router_w, router_bias):
    d = x_lat.shape[1]
    tm = MOE_TILE
    n = n_ctx_tiles * tm + x_lat.shape[0]
    tri = (jnp.arange(tm)[None, :] < jnp.arange(tm)[:, None]).astype(BF16)
    row = lambda i: (i, 0)
    fixed = lambda i: (0, 0)
    seg = lambda i: (jnp.where(i < n_ctx_tiles, 1, 0), 0, 0)
    kern = functools.partial(_router_kernel, n_ctx_tiles=n_ctx_tiles)
    return pl.pallas_call(
        kern,
        grid=(n // tm,),
        in_specs=[*_stream_specs(n_ctx_tiles, tm, d), pl.BlockSpec((None, 1, d), seg), pl.BlockSpec((None, 1, d), seg),
                  pl.BlockSpec((d, N_EXPERTS), fixed), pl.BlockSpec((1, N_EXPERTS), fixed),
                  pl.BlockSpec((tm, tm), fixed)],
        out_specs=[pl.BlockSpec((tm * ROW_SLAB, LANES), row), pl.BlockSpec((tm, LANES), row),
                   pl.BlockSpec((tm, LANES), row), pl.BlockSpec((tm, LANES), row),
                   pl.BlockSpec((1, N_EXPERTS), fixed)],
        out_shape=[jax.ShapeDtypeStruct((n * ROW_SLAB, LANES), F32), jax.ShapeDtypeStruct((n, LANES), I32),
                   jax.ShapeDtypeStruct((n, LANES), F32), jax.ShapeDtypeStruct((n, LANES), I32),
                   jax.ShapeDtypeStruct((1, N_EXPERTS), F32)],
        scratch_shapes=[pltpu.VMEM((1, N_EXPERTS), F32)],
        compiler_params=_cparams(("arbitrary",)),
        name="moe_route",
    )(x_ctx, x_lat, sc2, sh2, router_w, router_bias.reshape(1, N_EXPERTS), tri)


def _slots_kernel(te_ref, rk_ref, start_ref, o_ref):
    tm = te_ref.shape[0]
    lane = lax.broadcasted_iota(I32, (tm, N_EXPERTS), 1)
    lane_o = lax.broadcasted_iota(I32, (tm, LANES), 1)
    te = te_ref[...]
    out = rk_ref[...]
    for k in range(TOP_K):
        first = jnp.where(lane == te[:, k:k + 1], start_ref[...], 0.0).sum(axis=1, keepdims=True)
        out = jnp.where(lane_o == k, out + first.astype(I32), out)
    o_ref[...] = out


def moe_slots(te, rk, start):
    n = te.shape[0]
    tm = MOE_TILE
    row = lambda i: (i, 0)
    out = pl.pallas_call(
        _slots_kernel,
        grid=(n // tm,),
        in_specs=[pl.BlockSpec((tm, LANES), row), pl.BlockSpec((tm, LANES), row),
                  pl.BlockSpec((1, N_EXPERTS), lambda i: (0, 0))],
        out_specs=pl.BlockSpec((tm, LANES), row),
        out_shape=jax.ShapeDtypeStruct((n, LANES), I32),
        compiler_params=_cparams(("parallel",)),
        name="moe_slots",
    )(te, rk, start.astype(F32).reshape(1, N_EXPERTS))
    return out[:, :TOP_K].reshape(-1)


def _slab_rows(ref, row, n_rows):
    first = row * ROW_SLAB
    if not isinstance(first, int):
        first = pl.multiple_of(first, ROW_SLAB)
    return ref.at[pl.ds(first, n_rows * ROW_SLAB), :]


def _dispatch_kernel(dest_ref, hf_ref, xs_hbm, zbuf, sem, zsem, *, n_assign):
    n_rows = dest_ref.shape[0]

    @pl.when(pl.program_id(0) == 0)
    def _():
        zbuf[...] = jnp.zeros_like(zbuf)
        tail = pltpu.make_async_copy(zbuf, _slab_rows(xs_hbm, n_assign, EXPERT_BLOCK), zsem)
        tail.start()
        tail.wait()

    def body(r, carry):
        src = _slab_rows(hf_ref, r, 1)
        for k in range(TOP_K):
            pltpu.make_async_copy(src, _slab_rows(xs_hbm, dest_ref[r * TOP_K + k], 1), sem).start(priority=k % 2)
        return carry
    lax.fori_loop(0, n_rows // TOP_K, body, 0)
    for _ in range(TOP_K):
        pltpu.make_async_copy(hf_ref, hf_ref, sem).wait()


def moe_dispatch(dest, hf_slabs):
    n_assign = dest.shape[0]
    tm = MOE_TILE
    n_rows = tm * TOP_K
    kern = functools.partial(_dispatch_kernel, n_assign=n_assign)
    return pl.pallas_call(
        kern,
        grid=(n_assign // n_rows,),
        in_specs=[pl.BlockSpec((n_rows,), lambda i: (i,), memory_space=pltpu.SMEM),
                  pl.BlockSpec((tm * ROW_SLAB, LANES), lambda i: (i, 0))],
        out_specs=pl.BlockSpec(memory_space=pl.ANY),
        out_shape=jax.ShapeDtypeStruct(((n_assign + EXPERT_BLOCK) * ROW_SLAB, LANES), F32),
        scratch_shapes=[pltpu.VMEM((EXPERT_BLOCK * ROW_SLAB, LANES), F32), pltpu.SemaphoreType.DMA(()),
                        pltpu.SemaphoreType.DMA(())],
        compiler_params=_cparams(("arbitrary",)),
        name="moe_dispatch",
    )(dest, hf_slabs)


def _row_gather(idx_ref, base, count, src_hbm, dst, sem):
    group = 8

    def body(g, carry):
        for k in range(group):
            j = g * group + k
            cp = pltpu.make_async_copy(_slab_rows(src_hbm, idx_ref[base + j], 1), _slab_rows(dst, j, 1), sem)
            cp.start(priority=k % 2)
        return carry
    lax.fori_loop(0, count // group, body, 0)


def _wait_rows(dst, sem):
    pltpu.make_async_copy(dst, dst, sem).wait()


def _gathered_rows(buf, first, rows, stride):
    return jnp.concatenate(
        [buf[pl.ds(first * ROW_SLAB + s, rows, stride=stride * ROW_SLAB), :] for s in range(ROW_SLAB)], axis=1)


def _swiglu(x, wg, wu, wd):
    gate = _dot(x, wg)
    up = _dot(x, wu)
    return _dot((gate * jax.nn.sigmoid(gate) * up).astype(BF16), wd)


def _valid_row_copies(ybuf, ys_hbm, row0, valid, sem):
    out = [(valid == EXPERT_BLOCK,
            pltpu.make_async_copy(_slab_rows(ybuf, 0, EXPERT_BLOCK), _slab_rows(ys_hbm, row0, EXPERT_BLOCK), sem))]
    part = valid < EXPERT_BLOCK
    size = EXPERT_BLOCK // 2
    while size >= 1:
        off = valid & ~(2 * size - 1)
        out.append((part & ((valid & size) != 0),
                    pltpu.make_async_copy(_slab_rows(ybuf, off, size), _slab_rows(ys_hbm, row0 + off, size), sem)))
        size //= 2
    return out


def _expert_kernel(be_ref, r0_ref, nv_ref, ws_ref, nx_ref, xs_hbm, wg_hbm, wu_hbm, wd_hbm, ys_hbm,
                   xbuf, ybuf, wg_st, wu_st, wd_st, wg_bf, wu_bf, wd_bf, sem_in, sem_out, sem_w, *, layer):
    b = pl.program_id(0)
    n_blocks = pl.num_programs(0)
    slot = b % 2

    def weight_copies(e, ws):
        return [pltpu.make_async_copy(hbm.at[layer, e], st.at[ws], sem_w.at[ws])
                for hbm, st in ((wg_hbm, wg_st), (wu_hbm, wu_st), (wd_hbm, wd_st))]

    @pl.when(b == 0)
    def _():
        for cp in weight_copies(be_ref[0], 0):
            cp.start()

    @pl.when(ws_ref[b] >= 0)
    def _():
        ws = ws_ref[b]
        for cp in weight_copies(be_ref[b], ws):
            cp.wait()
        wg_bf[...] = wg_st[ws].astype(BF16)
        wu_bf[...] = wu_st[ws].astype(BF16)
        wd_bf[...] = wd_st[ws].astype(BF16)

        @pl.when(nx_ref[b] >= 0)
        def _():
            for cp in weight_copies(nx_ref[b], 1 - ws):
                cp.start()

    def fetch(blk, sl):
        return pltpu.make_async_copy(_slab_rows(xs_hbm, r0_ref[blk], EXPERT_BLOCK), xbuf.at[sl], sem_in.at[sl])

    def drain(blk, sl):
        for cond, cp in _valid_row_copies(ybuf.at[sl], ys_hbm, r0_ref[blk], nv_ref[blk], sem_out.at[sl]):
            pl.when(cond)(cp.wait)

    @pl.when(b == 0)
    def _():
        for k in range(EXPERT_LOOKAHEAD):
            pl.when(nv_ref[k] > 0)(fetch(k, k).start)

    nxt = jnp.minimum(b + EXPERT_LOOKAHEAD, n_blocks - 1)

    @pl.when((b + EXPERT_LOOKAHEAD < n_blocks) & (nv_ref[nxt] > 0))
    def _():
        fetch(nxt, nxt % (EXPERT_LOOKAHEAD + 1)).start()

    @pl.when(b >= 2)
    def _():
        drain(jnp.maximum(b - 2, 0), slot)

    @pl.when(nv_ref[b] > 0)
    def _():
        xslot = b % (EXPERT_LOOKAHEAD + 1)
        fetch(b, xslot).wait()
        x = _gathered_rows(xbuf.at[xslot], 0, EXPERT_BLOCK, 1).astype(BF16)
        y = _swiglu(x, wg_bf[...], wu_bf[...], wd_bf[...])
        _store_row_slabs(ybuf.at[slot], y)
        for cond, cp in _valid_row_copies(ybuf.at[slot], ys_hbm, r0_ref[b], nv_ref[b], sem_out.at[slot]):
            pl.when(cond)(cp.start)

    @pl.when(b == n_blocks - 1)
    def _():
        drain(jnp.maximum(b - 1, 0), 1 - slot)
        drain(b, slot)


def moe_experts(xs_slabs, block_e, block_row0, block_valid, block_wslot, block_next_e, w_gate, w_up, w_down, layer):
    n_blocks = block_e.shape[0]
    d, ff = w_gate.shape[2:]
    blk_rows = EXPERT_BLOCK * ROW_SLAB
    n_assign = xs_slabs.shape[0] // ROW_SLAB - EXPERT_BLOCK
    any_spec = pl.BlockSpec(memory_space=pl.ANY)
    grid_spec = pltpu.PrefetchScalarGridSpec(
        num_scalar_prefetch=5,
        grid=(n_blocks,),
        in_specs=[any_spec] * 4,
        out_specs=any_spec,
        scratch_shapes=[pltpu.VMEM((EXPERT_LOOKAHEAD + 1, blk_rows, LANES), F32), pltpu.VMEM((2, blk_rows, LANES), F32),
                        pltpu.VMEM((2, d, ff), F32), pltpu.VMEM((2, d, ff), F32), pltpu.VMEM((2, ff, d), F32),
                        pltpu.VMEM((d, ff), BF16), pltpu.VMEM((d, ff), BF16), pltpu.VMEM((ff, d), BF16),
                        pltpu.SemaphoreType.DMA((EXPERT_LOOKAHEAD + 1,)), pltpu.SemaphoreType.DMA((2,)),
                        pltpu.SemaphoreType.DMA((2,))],
    )
    return pl.pallas_call(
        functools.partial(_expert_kernel, layer=layer),
        grid_spec=grid_spec,
        out_shape=jax.ShapeDtypeStruct((n_assign * ROW_SLAB, LANES), F32),
        compiler_params=_cparams(("arbitrary",)),
        name="moe_experts",
    )(block_e, block_row0, block_valid, block_wslot, block_next_e, xs_slabs, w_gate, w_up, w_down)


def _combine_kernel(cur_ref, nxt_ref, ys_hbm, xc_ref, xl_ref, hf_ref, gt_ref, g2_ref, sg_ref, su_ref, sd_ref,
                    lg_ref, lb_ref, *rest, n_ctx_tiles):
    *o_refs, ybuf, sem = rest
    i = pl.program_id(0)
    n_tiles = pl.num_programs(0)
    tm = xl_ref.shape[0]
    n_rows = tm * TOP_K
    slot = i % 2

    @pl.when(i == 0)
    def _():
        _row_gather(cur_ref, 0, n_rows, ys_hbm, ybuf.at[0], sem.at[0])

    @pl.when(i + 1 < n_tiles)
    def _():
        _row_gather(nxt_ref, 0, n_rows, ys_hbm, ybuf.at[1 - slot], sem.at[1 - slot])

    hf = _gathered_rows(hf_ref, 0, tm, 1).astype(BF16)
    y = _swiglu(hf, sg_ref[...], su_ref[...], sd_ref[...])

    _wait_rows(ybuf.at[slot], sem.at[slot])
    for k in range(TOP_K):
        y = y + gt_ref[:, k:k + 1] * _gathered_rows(ybuf.at[slot], k, tm, TOP_K)
    r = DEEPNORM_ALPHA * _stream_tile(xc_ref, xl_ref, n_ctx_tiles) + g2_ref[...] * y
    res = _layer_norm_rows(r, lg_ref[...], lb_ref[...])
    if n_ctx_tiles == 0:
        o_refs[0][...] = res
    else:
        oc_ref, ol_ref = o_refs

        @pl.when(i < n_ctx_tiles)
        def _():
            oc_ref[...] = res

        @pl.when(i >= n_ctx_tiles)
        def _():
            ol_ref[...] = res


def moe_combine(dest, ys_slabs, x_ctx, x_lat, hf_slabs, gate, g2, n_ctx_tiles, sh_gate, sh_up, sh_down, ln_g, ln_b):
    d = x_lat.shape[1]
    tm = MOE_TILE
    n_tiles = n_ctx_tiles + x_lat.shape[0] // tm
    n_rows = tm * TOP_K
    ff = sh_gate.shape[1]
    row = lambda i: (i, 0)
    fixed = lambda i: (0, 0)
    seg = lambda i: (jnp.where(i < n_ctx_tiles, 1, 0), 0, 0)
    ctx_spec, lat_spec = _stream_specs(n_ctx_tiles, tm, d)
    lat_out = jax.ShapeDtypeStruct(x_lat.shape, F32)
    if n_ctx_tiles == 0:
        out_specs, out_shape = [lat_spec], [lat_out]
    else:
        out_specs, out_shape = [ctx_spec, lat_spec], [jax.ShapeDtypeStruct(x_ctx.shape, F32), lat_out]
    kern = functools.partial(_combine_kernel, n_ctx_tiles=n_ctx_tiles)
    return pl.pallas_call(
        kern,
        grid=(n_tiles,),
        in_specs=[pl.BlockSpec((n_rows,), lambda i: (i,), memory_space=pltpu.SMEM),
                  pl.BlockSpec((n_rows,), lambda i: (jnp.minimum(i + 1, n_tiles - 1),), memory_space=pltpu.SMEM),
                  pl.BlockSpec(memory_space=pl.ANY),
                  ctx_spec, lat_spec, pl.BlockSpec((tm * ROW_SLAB, LANES), row), pl.BlockSpec((tm, LANES), row),
                  pl.BlockSpec((None, 1, d), seg),
                  pl.BlockSpec((d, ff), fixed), pl.BlockSpec((d, ff), fixed), pl.BlockSpec((ff, d), fixed),
                  pl.BlockSpec((1, d), fixed), pl.BlockSpec((1, d), fixed)],
        out_specs=out_specs,
        out_shape=out_shape,
        scratch_shapes=[pltpu.VMEM((2, n_rows * ROW_SLAB, LANES), F32), pltpu.SemaphoreType.DMA((2,))],
        compiler_params=_cparams(("arbitrary",)),
        name="moe_combine",
    )(dest, dest, ys_slabs, x_ctx, x_lat, hf_slabs, gate, g2,
      sh_gate.astype(BF16), sh_up.astype(BF16), sh_down.astype(BF16), ln_g.reshape(1, d), ln_b.reshape(1, d))


def moe_layer(x_ctx, x_lat, sc2, sh2, g2, n_ctx_tiles, layer, router_w, router_bias, w_gate, w_up, w_down,
              sh_gate, sh_up, sh_down, ln_g, ln_b):
    n = n_ctx_tiles * MOE_TILE + x_lat.shape[0]
    hf, te, gt, rk, cnt = moe_route(x_ctx, x_lat, sc2, sh2, n_ctx_tiles, router_w, router_bias)
    counts = cnt[0].astype(I32)
    start = jnp.cumsum(counts) - counts
    experts = jnp.arange(N_EXPERTS, dtype=I32)
    dest = moe_slots(te, rk, start)
    nb = (counts + EXPERT_BLOCK - 1) // EXPERT_BLOCK
    blk_end = jnp.cumsum(nb)
    blk_start = blk_end - nb
    n_blocks = n * TOP_K // EXPERT_BLOCK + N_EXPERTS
    blk = jnp.arange(n_blocks, dtype=I32)
    bb = jnp.minimum(blk, blk_end[-1] - 1)[:, None]
    own = (blk_start[None, :] <= bb) & (bb < blk_end[None, :])
    sel = lambda v: jnp.sum(jnp.where(own, v[None, :], 0), axis=1)
    j = bb[:, 0] - sel(blk_start)
    block_e = sel(experts)
    block_row0 = sel(start) + j * EXPERT_BLOCK
    block_valid = jnp.where(blk < blk_end[-1], jnp.clip(sel(counts) - j * EXPERT_BLOCK, 0, EXPERT_BLOCK), 0)
    used = nb > 0
    ordinal = jnp.cumsum(used.astype(I32)) - 1
    later = used[None, :] & (experts[None, :] > experts[:, None])
    next_used = jnp.min(jnp.where(later, experts[None, :], N_EXPERTS), axis=1)
    next_used = jnp.where(next_used < N_EXPERTS, next_used, -1)
    first = (blk < blk_end[-1]) & (j == 0)
    block_wslot = jnp.where(first, sel(ordinal) % 2, -1)
    block_next_e = jnp.where(first, sel(next_used), -1)
    xs = moe_dispatch(dest, hf)
    ys = moe_experts(xs, block_e, block_row0, block_valid, block_wslot, block_next_e, w_gate, w_up, w_down, layer)
    return moe_combine(dest, ys, x_ctx, x_lat, hf, gt, g2, n_ctx_tiles, sh_gate, sh_up, sh_down, ln_g, ln_b)


def kernel(x, c, ctx, c_ctx, w_mod, b_mod, ln_mix_g, ln_mix_b, ln_ffn_g, ln_ffn_b, ab_w_in, ab_w_out, na_rpb,
           hy_conv_w, hy_conv_b, hy_f_w1, hy_f_b1, hy_f_freq, hy_f_w2, hy_f_b2, hy_f_w3, hy_skip, cd_w_in, cd_w_out,
           q_norm_g, k_norm_g, s5_a_re, s5_a_im, s5_log_dt, s5_b_re, s5_b_im, s5_c_re, s5_c_im, s5_d, s5_glu_w,
           s5_glu_b, router_w, router_bias, exp_w_gate, exp_w_up, exp_w_down, sh_w_gate, sh_w_up, sh_w_down):
    b, l, d = x.shape
    assert b == 1
    n_ctx = ctx.shape[1]
    assert n_ctx == MOE_TILE
    xs = x[0]
    cs = ctx[0]
    cmat = jnp.zeros((SUBLANES, d), F32).at[0].set(c[0]).at[1].set(c_ctx)
    mods = modulation_all(cmat, w_mod, b_mod).reshape(DEPTH, SUBLANES, 6, d)
    qscale = HEAD_DIM ** -0.5

    for i in range(DEPTH):
        need_ctx = i < DEPTH - 1
        m = mods[i]
        sh1, sc1, g1, sh2, sc2, g2 = [m[0:1, t] for t in range(6)]
        csh1, csc1, cg1, csh2, csc2, cg2 = [m[1:2, t] for t in range(6)]
        j = i // 2
        if i % 2 == 0:
            filt = (hy_conv_w[j], hy_conv_b[j], hy_f_w1[j], hy_f_b1[j], hy_f_freq[j], hy_f_w2[j], hy_f_b2[j],
                    hy_f_w3[j], hy_skip[j])
            splits = (NA_WIDTH, NA_WIDTH, NA_WIDTH, 3 * HY_WIDTH)
            dts = (BF16, BF16, BF16, F32)
            scl = (qscale, 1.0, 1.0, 1.0)
            q_l, k_l, v_l, u_l = mod_project(xs, sc1, sh1, ab_w_in[j], splits, dts, scl)
            q_c, k_c, v_c, u_c = mod_project(cs, csc1, csh1, ab_w_in[j], splits, dts, scl)
            a_lat = neighbourhood_attention(q_l, k_l, v_l, k_c, v_c, na_rpb[j])
            y_hy = hyena_long(u_l, *filt)
            xs_new = outproj_ln(a_lat, y_hy, ab_w_out[j], xs, g1, ln_mix_g[i], ln_mix_b[i])
            if need_ctx:
                a_ctx = context_attention(q_c, k_c, v_c)
                yc_hy = hyena_long(u_c, *filt)
                cs = outproj_ln(a_ctx, yc_hy, ab_w_out[j], cs, cg1, ln_mix_g[i], ln_mix_b[i])
            xs = xs_new
        else:
            splits = (GQA_WIDTH, GQA_KV_WIDTH, GQA_KV_WIDTH, S5_WIDTH)
            q_l, k_l, v_l, u_l = mod_project(xs, sc1, sh1, cd_w_in[j], splits, (F32, F32, BF16, F32))
            k_c, v_c, u_c = mod_project(cs, csc1, csh1, cd_w_in[j][:, GQA_WIDTH:], splits[1:], (F32, BF16, F32))
            tabs = _rope_tables(l)
            qn_t = qk_prep(q_l, q_norm_g[j], tabs, qscale * math.log2(math.e), transposed=True)
            kn = qk_prep(k_l, k_norm_g[j], tabs, 1.0)
            kcn = qk_prep(k_c, k_norm_g[j], None, 1.0)
            k_all = jnp.concatenate([kn, kcn], axis=0)
            v_all = jnp.concatenate([v_l, v_c], axis=0)
            k_hm = k_all.reshape(-1, GQA_KV_HEADS, HEAD_DIM).transpose(1, 0, 2)
            v_t = v_all.T.reshape(GQA_KV_HEADS, HEAD_DIM, -1)
            att = gqa_attention(qn_t, k_hm, v_t)
            ssm = s5_mix(u_c, u_l, s5_a_re[j], s5_a_im[j], s5_log_dt[j], s5_b_re[j], s5_b_im[j],
                         s5_c_re[j], s5_c_im[j], s5_d[j], s5_glu_w[j], s5_glu_b[j])
            xs = outproj_ln(att, ssm, cd_w_out[j], xs, g1, ln_mix_g[i], ln_mix_b[i])
            assert not need_ctx

        moe_w = (i, router_w[i], router_bias[i], exp_w_gate, exp_w_up, exp_w_down,
                 sh_w_gate[i], sh_w_up[i], sh_w_down[i], ln_ffn_g[i], ln_ffn_b[i])
        stack2 = lambda lat, cx: jnp.stack([lat, cx])
        mod2 = (stack2(sc2, csc2), stack2(sh2, csh2), stack2(g2, cg2))
        if need_ctx:
            cs, xs = moe_layer(cs, xs, *mod2, n_ctx // MOE_TILE, *moe_w)
        else:
            (xs,) = moe_layer(xs, xs, *mod2, 0, *moe_w)
    return xs.reshape(b, l, d)
```

```python
import functools
import math

import jax
import jax.numpy as jnp
import numpy as np
from jax import lax
from jax.experimental import pallas as pl
from jax.experimental.pallas import tpu as pltpu

F32 = jnp.float32
BF16 = jnp.bfloat16
I32 = jnp.int32
HIGHEST = lax.Precision.HIGHEST

LANES = 128
SUBLANES = 8
VMEM_LIMIT = 56 * 1024 * 1024

D_MODEL = 1024
DEPTH = 2
GRID_W = 64
HEAD_DIM = 64
NA_HEADS = 8
NA_WIDTH = NA_HEADS * HEAD_DIM
NA_KH = 8
NA_KW = 16
HY_WIDTH = D_MODEL - NA_WIDTH
HY_BANDS = 16
HY_DECAY_PCT_MIN = 0.3
HY_DECAY_PCT_MAX = 1.5
HY_DECAY_TARGET = 1e-2
GQA_HEADS = 8
GQA_KV_HEADS = 2
GQA_WIDTH = GQA_HEADS * HEAD_DIM
GQA_KV_WIDTH = GQA_KV_HEADS * HEAD_DIM
ROPE_THETA = 10000.0
S5_WIDTH = D_MODEL - GQA_WIDTH
S5_GROUP = 16
S5_GROUPS = S5_WIDTH // S5_GROUP
S5_STATE = 64
N_EXPERTS = 256
TOP_K = 8
N_EXPERT_GROUPS = 8
TOPK_GROUPS = 4
EXPERT_FF = 256
ROUTED_SCALE = 2.5
EXPERT_BLOCK = 256
DEEPNORM_ALPHA = (2.0 * DEPTH) ** 0.25
LN_EPS = 1e-5
RMS_EPS = 1e-6

NEG_BIG = -1e30
NA_TILE_ROWS = 8
NA_KEY_ROWS = 16
NA_KEY_BLOCK_ROWS = 4
DFT_N1 = 128
S5_CHUNK = 16
S5_LANE_GROUPS = LANES // S5_GROUP
MOE_TILE = 256
EXPERT_LOOKAHEAD = 3
EXPERT_WEIGHT_SLOTS = 3
ROW_SLAB = D_MODEL // LANES

NT_DIMS = (((1,), (1,)), ((), ()))


def _cparams(sem, **kw):
    return pltpu.CompilerParams(dimension_semantics=sem, vmem_limit_bytes=VMEM_LIMIT, **kw)


def _dot(a, b, **kw):
    return jnp.dot(a, b, preferred_element_type=F32, **kw)


def _dot_nt(a, b):
    return lax.dot_general(a, b, NT_DIMS, preferred_element_type=F32)


def _row_tile(m, pref):
    return pref if m % pref == 0 else m


def _mod_kernel(c_ref, w_ref, b_ref, o_ref):
    cv = c_ref[...]
    s = cv * jax.nn.sigmoid(cv)
    o_ref[...] = _dot(s, w_ref[...], precision=HIGHEST) + b_ref[...]


def modulation_all(cmat, w_mod, b_mod):
    depth, d, n = w_mod.shape
    tn = 1536
    return pl.pallas_call(
        _mod_kernel,
        grid=(depth, n // tn),
        in_specs=[pl.BlockSpec((SUBLANES, d), lambda l, j: (0, 0)),
                  pl.BlockSpec((None, d, tn), lambda l, j: (l, 0, j)),
                  pl.BlockSpec((None, 1, tn), lambda l, j: (l, 0, j))],
        out_specs=pl.BlockSpec((None, SUBLANES, tn), lambda l, j: (l, 0, j)),
        out_shape=jax.ShapeDtypeStruct((depth, SUBLANES, n), F32),
        compiler_params=_cparams(("arbitrary", "arbitrary")),
        name="modulation",
    )(cmat, w_mod, b_mod.reshape(depth, 1, n))


def _proj_kernel(x_ref, sc_ref, sh_ref, w_ref, *o_refs, splits, scales):
    h = (x_ref[...] * (1.0 + sc_ref[...]) + sh_ref[...]).astype(BF16)
    off = 0
    for o_ref, wd, sc in zip(o_refs, splits, scales):
        y = _dot(h, w_ref[:, off:off + wd])
        if sc != 1.0:
            y = y * sc
        o_ref[...] = y.astype(o_ref.dtype)
        off += wd


def mod_project(x, sc, sh, w, splits, dtypes, scales=None):
    m, d = x.shape
    n = w.shape[1]
    assert sum(splits) == n
    scales = scales or (1.0,) * len(splits)
    tm = _row_tile(m, 512)
    kern = functools.partial(_proj_kernel, splits=tuple(splits), scales=tuple(scales))
    return pl.pallas_call(
        kern,
        grid=(m // tm,),
        in_specs=[pl.BlockSpec((tm, d), lambda i: (i, 0)),
                  pl.BlockSpec((1, d), lambda i: (0, 0)),
                  pl.BlockSpec((1, d), lambda i: (0, 0)),
                  pl.BlockSpec((d, n), lambda i: (0, 0))],
        out_specs=[pl.BlockSpec((tm, wd), lambda i: (i, 0)) for wd in splits],
        out_shape=[jax.ShapeDtypeStruct((m, wd), dt) for wd, dt in zip(splits, dtypes)],
        compiler_params=_cparams(("parallel",)),
        name="mod_project",
    )(x, sc, sh, w.astype(BF16))


def _layer_norm_rows(r, g, b):
    mu = jnp.mean(r, axis=-1, keepdims=True)
    c = r - mu
    var = jnp.mean(c * c, axis=-1, keepdims=True)
    return c * lax.rsqrt(var + LN_EPS) * g + b


def _outproj_ln_kernel(a_ref, b_ref, w_ref, x_ref, gate_ref, g_ref, beta_ref, o_ref):
    ka = a_ref.shape[1]
    y = _dot(a_ref[...], w_ref[:ka, :]) + _dot(b_ref[...], w_ref[ka:, :])
    r = DEEPNORM_ALPHA * x_ref[...] + gate_ref[...] * y
    o_ref[...] = _layer_norm_rows(r, g_ref[...], beta_ref[...])


def outproj_ln(a, b, w, x, gate, g, beta):
    m, d = x.shape
    ka, kb = a.shape[1], b.shape[1]
    tm = _row_tile(m, 512)
    row = lambda i: (i, 0)
    fixed = lambda i: (0, 0)
    return pl.pallas_call(
        _outproj_ln_kernel,
        grid=(m // tm,),
        in_specs=[pl.BlockSpec((tm, ka), row), pl.BlockSpec((tm, kb), row),
                  pl.BlockSpec((ka + kb, d), fixed), pl.BlockSpec((tm, d), row),
                  pl.BlockSpec((1, d), fixed), pl.BlockSpec((1, d), fixed), pl.BlockSpec((1, d), fixed)],
        out_specs=pl.BlockSpec((tm, d), row),
        out_shape=jax.ShapeDtypeStruct((m, d), F32),
        compiler_params=_cparams(("parallel",)),
        name="outproj_ln",
    )(a, b, w.astype(BF16), x, gate, g.reshape(1, d), beta.reshape(1, d))


NA_BIAS_PAD = NA_TILE_ROWS


def _na_key_start(t, rows):
    lo, hi = NA_TILE_ROWS * t - NA_KH // 2, rows - NA_KEY_ROWS
    return min(max(lo, 0), hi) if isinstance(t, int) else jnp.clip(lo, 0, hi)


def _na_bias_tables(rpb, rows):
    h = rpb.shape[0]
    ri = np.arange(NA_TILE_ROWS)
    kr = np.arange(NA_KEY_ROWS)
    c = np.arange(GRID_W)
    cs = np.clip(c - NA_KW // 2, 0, GRID_W - NA_KW)
    vc = (c[None, :] >= cs[:, None]) & (c[None, :] < cs[:, None] + NA_KW)
    dc = np.clip(c[None, :] - c[:, None] + NA_KW - 1, 0, 2 * NA_KW - 2)
    pick = (dc.reshape(-1)[:, None] == np.arange(2 * NA_KW - 1)[None, :]).astype(np.float32)
    colb = jnp.einsum('qb,hab->haq', pick, rpb, precision=HIGHEST).reshape(h, 2 * NA_KH - 1, GRID_W, GRID_W)
    colb = jnp.where(vc[None, None], colb, NEG_BIG)
    pad = jnp.zeros((h, NA_BIAS_PAD, GRID_W, GRID_W), F32)
    colb = jnp.concatenate([pad, colb, pad], axis=1)
    colb2 = jnp.concatenate([colb[:, :-1], colb[:, 1:]], axis=-1)
    colb2 = colb2.reshape((h // 2, 2) + colb2.shape[1:])

    def case(t):
        r = NA_TILE_ROWS * t + ri
        rs = np.clip(r - NA_KH // 2, 0, rows - NA_KH)
        krow = _na_key_start(t, rows) + kr
        vr = (krow[None, :] >= rs[:, None]) & (krow[None, :] < rs[:, None] + NA_KH)
        m = np.where(vr, 0.0, NEG_BIG).astype(np.float32)
        return np.repeat(np.repeat(m, GRID_W, axis=0), GRID_W, axis=1)

    n_tiles = rows // NA_TILE_ROWS
    return colb2, jnp.asarray(np.stack([case(0), case(1), case(n_tiles - 1)]))


def _pair_masks(shape):
    lane = lax.broadcasted_iota(I32, shape, 1)
    return lane < HEAD_DIM


def _na_kernel(q_ref, k0, k1, k2, k3, v0, v1, v2, v3, kc_ref, vc_ref, colb_ref, mask_ref, o_ref, *, rows):
    q = q_ref[...]
    lo = _pair_masks(q.shape)
    ks = (k0, k1, k2, k3)
    vs = (v0, v1, v2, v3)
    kb = k0.shape[0]
    t = pl.program_id(1)
    off = _na_key_start(t, rows) - NA_TILE_ROWS * t + (NA_KH - 1) + NA_BIAS_PAD
    key_rows_per_block = kb // GRID_W

    def bias(hh, i):
        first = i * key_rows_per_block
        return jnp.concatenate(
            [jnp.concatenate([colb_ref[hh, first + 2 * m - ri + off] for m in range(key_rows_per_block // 2)], axis=1)
             for ri in range(NA_TILE_ROWS)], axis=0)

    outs = []
    for hh in range(2):
        qh = jnp.where(lo if hh == 0 else jnp.logical_not(lo), q, jnp.zeros_like(q))
        s = [_dot_nt(qh, ks[i][...]) + bias(hh, i) + mask_ref[:, i * kb:(i + 1) * kb] for i in range(4)]
        s.append(_dot_nt(qh, kc_ref[...]))
        m = s[0].max(axis=1, keepdims=True)
        for si in s[1:]:
            m = jnp.maximum(m, si.max(axis=1, keepdims=True))
        p = [jnp.exp(si - m) for si in s]
        l = p[0].sum(axis=1, keepdims=True)
        for pi in p[1:]:
            l = l + pi.sum(axis=1, keepdims=True)
        acc = _dot(p[4].astype(BF16), vc_ref[...])
        for i in range(4):
            acc = acc + _dot(p[i].astype(BF16), vs[i][...])
        outs.append(acc / l)
    o_ref[...] = jnp.where(lo, outs[0], outs[1]).astype(o_ref.dtype)


def neighbourhood_attention(q, k, v, k_ctx, v_ctx, rpb):
    l, w = q.shape
    rows = l // GRID_W
    n_tiles = rows // NA_TILE_ROWS
    assert n_tiles >= 3 and rows % NA_TILE_ROWS == 0
    n_ctx = k_ctx.shape[0]
    tq = NA_TILE_ROWS * GRID_W
    kb = NA_KEY_BLOCK_ROWS * GRID_W
    colb2, rowmask = _na_bias_tables(rpb, rows)
    pair_w = 2 * HEAD_DIM

    def kv_spec(i):
        def imap(p, t):
            return (_na_key_start(t, rows) // NA_KEY_BLOCK_ROWS + i, p)
        return pl.BlockSpec((kb, pair_w), imap)

    def mask_map(p, t):
        return (jnp.where(t == 0, 0, jnp.where(t == n_tiles - 1, 2, 1)), 0, 0)

    return pl.pallas_call(
        functools.partial(_na_kernel, rows=rows),
        grid=(w // pair_w, n_tiles),
        in_specs=[pl.BlockSpec((tq, pair_w), lambda p, t: (t, p))]
                 + [kv_spec(i) for i in range(4)] + [kv_spec(i) for i in range(4)]
                 + [pl.BlockSpec((n_ctx, pair_w), lambda p, t: (0, p)),
                    pl.BlockSpec((n_ctx, pair_w), lambda p, t: (0, p)),
                    pl.BlockSpec((None,) + colb2.shape[1:], lambda p, t: (p, 0, 0, 0, 0)),
                    pl.BlockSpec((None, tq, NA_KEY_ROWS * GRID_W), mask_map)],
        out_specs=pl.BlockSpec((tq, pair_w), lambda p, t: (t, p)),
        out_shape=jax.ShapeDtypeStruct((l, w), BF16),
        compiler_params=_cparams(("parallel", "parallel")),
        name="neighbourhood_attention",
    )(q, k, k, k, k, v, v, v, v, k_ctx, v_ctx, colb2, rowmask)


def _ctx_attn_kernel(q_ref, k_ref, v_ref, o_ref):
    q = q_ref[...]
    lo = _pair_masks(q.shape)
    outs = []
    for hh in range(2):
        qh = jnp.where(lo if hh == 0 else jnp.logical_not(lo), q, jnp.zeros_like(q))
        s = _dot_nt(qh, k_ref[...])
        p = jnp.exp(s - s.max(axis=1, keepdims=True))
        outs.append(_dot(p.astype(BF16), v_ref[...]) / p.sum(axis=1, keepdims=True))
    o_ref[...] = jnp.where(lo, outs[0], outs[1]).astype(o_ref.dtype)


def context_attention(q, k, v):
    n, w = q.shape
    pair_w = 2 * HEAD_DIM
    spec = pl.BlockSpec((n, pair_w), lambda p: (0, p))
    return pl.pallas_call(
        _ctx_attn_kernel, grid=(w // pair_w,), in_specs=[spec, spec, spec], out_specs=spec,
        out_shape=jax.ShapeDtypeStruct((n, w), BF16),
        compiler_params=_cparams(("parallel",)), name="context_attention",
    )(q, k, v)


def _shortconv_kernel(u_ref, up_ref, un_ref, w_ref, b_ref, x0_ref, z_ref, *, n_tiles):
    i = pl.program_id(0)
    u = u_ref[...]
    tm = u.shape[0]
    prev_row = jnp.where(i > 0, up_ref[SUBLANES - 1:SUBLANES, :], 0.0)
    next_row = jnp.where(i < n_tiles - 1, un_ref[0:1, :], 0.0)
    row = lax.broadcasted_iota(I32, u.shape, 0)
    u_dn = jnp.where(row == 0, prev_row, pltpu.roll(u, 1, 0))
    u_up = jnp.where(row == tm - 1, next_row, pltpu.roll(u, tm - 1, 0))
    y = u_dn * w_ref[0:1, :] + u * w_ref[1:2, :] + u_up * w_ref[2:3, :] + b_ref[...]
    c = HY_WIDTH
    x0_ref[...] = y[:, :c]
    z_ref[...] = y[:, c:2 * c] * y[:, 2 * c:]


def hyena_gate(u, conv_w, conv_b):
    l, w3 = u.shape
    tm = _row_tile(l, 512)
    n_tiles = l // tm
    per = tm // SUBLANES
    last = l // SUBLANES - 1
    kern = functools.partial(_shortconv_kernel, n_tiles=n_tiles)
    return pl.pallas_call(
        kern,
        grid=(n_tiles,),
        in_specs=[pl.BlockSpec((tm, w3), lambda i: (i, 0)),
                  pl.BlockSpec((SUBLANES, w3), lambda i: (jnp.maximum(i * per - 1, 0), 0)),
                  pl.BlockSpec((SUBLANES, w3), lambda i: (jnp.minimum((i + 1) * per, last), 0)),
                  pl.BlockSpec((3, w3), lambda i: (0, 0)),
                  pl.BlockSpec((1, w3), lambda i: (0, 0))],
        out_specs=[pl.BlockSpec((tm, HY_WIDTH), lambda i: (i, 0))] * 2,
        out_shape=[jax.ShapeDtypeStruct((l, HY_WIDTH), F32)] * 2,
        compiler_params=_cparams(("parallel",)),
        name="hyena_gate",
    )(u, u, u, conv_w, conv_b.reshape(1, w3))


def _filter_kernel(bands_ref, w1t_ref, w1c_ref, w1s_ref, b1_ref, fr_ref, w2_ref, b2_ref, w3_ref, dl_ref,
                   taps_ref, asum_ref, *, l, tp):
    i = pl.program_id(0)
    hid_w = w2_ref.shape[0]
    c = HY_WIDTH
    denom = float(max(l - 1, 1))

    def pos(width):
        return (lax.broadcasted_iota(I32, (tp, width), 0) + i * tp).astype(F32)

    ang = (2.0 * math.pi / l) * pos(HY_BANDS) * bands_ref[...]
    pre = ((pos(hid_w) / denom) * w1t_ref[...]
           + _dot(jnp.cos(ang), w1c_ref[...], precision=HIGHEST)
           + _dot(-jnp.sin(ang), w1s_ref[...], precision=HIGHEST) + b1_ref[...])
    hid = jnp.sin(fr_ref[...] * pre)
    hid = jnp.sin(fr_ref[...] * (_dot(hid, w2_ref[...], precision=HIGHEST) + b2_ref[...]))
    taps = _dot(hid, w3_ref[...], precision=HIGHEST)
    pc = pos(c)
    window = jnp.exp(-(pc / denom) * dl_ref[...])
    fwd = taps[:, :c] * window
    bwd = jnp.where(pc == 0.0, 0.0, taps[:, c:] * window)
    taps_ref[:, :c] = fwd
    taps_ref[:, c:] = bwd

    @pl.when(i == 0)
    def _():
        asum_ref[...] = jnp.zeros_like(asum_ref)

    asum_ref[...] += jnp.sum(jnp.abs(fwd) + jnp.abs(bwd), axis=0, keepdims=True)


def hyena_filter_taps(l, f_w1, f_b1, f_freq, f_w2, f_b2, f_w3):
    c = HY_WIDTH
    hid = f_w2.shape[0]
    tp = _row_tile(l, 1024)
    bands = jnp.linspace(1e-4, HY_BANDS - 1, HY_BANDS, dtype=F32).reshape(1, HY_BANDS)
    deltas = jnp.abs(jnp.linspace(math.log(HY_DECAY_TARGET) / HY_DECAY_PCT_MAX,
                                  math.log(HY_DECAY_TARGET) / HY_DECAY_PCT_MIN, c, dtype=F32)).reshape(1, c)
    fixed = lambda i: (0, 0)
    full = lambda a: pl.BlockSpec(a.shape, fixed)
    args = (bands, f_w1[0:1], f_w1[1:1 + HY_BANDS], f_w1[1 + HY_BANDS:], f_b1.reshape(1, hid),
            f_freq.reshape(1, hid), f_w2, f_b2.reshape(1, hid), f_w3, deltas)
    kern = functools.partial(_filter_kernel, l=l, tp=tp)
    return pl.pallas_call(
        kern,
        grid=(l // tp,),
        in_specs=[full(a) for a in args],
        out_specs=[pl.BlockSpec((tp, 2 * c), lambda i: (i, 0)), pl.BlockSpec((1, c), fixed)],
        out_shape=[jax.ShapeDtypeStruct((l, 2 * c), F32), jax.ShapeDtypeStruct((1, c), F32)],
        compiler_params=_cparams(("arbitrary",)),
        name="hyena_filter",
    )(*args)


def _dft_tables(l):
    n = 2 * l
    n1 = DFT_N1
    n2 = n // n1
    k1 = jnp.arange(n1)[:, None]
    m1 = jnp.arange(n1 // 2)[None, :]
    ph1 = (2.0 * math.pi / n1) * ((k1 * m1) % n1).astype(F32)
    d1 = jnp.stack([jnp.cos(ph1), -jnp.sin(ph1)], axis=1).reshape(2 * n1, n1 // 2)
    d1_inv = d1.T
    j2 = jnp.arange(n2)
    ph2 = (2.0 * math.pi / n2) * ((j2[:, None] * j2[None, :]) % n2).astype(F32)
    cs, sn = jnp.cos(ph2), jnp.sin(ph2)
    f2 = jnp.concatenate([jnp.concatenate([cs, sn], axis=1), jnp.concatenate([-sn, cs], axis=1)], axis=0)
    pht = (2.0 * math.pi / n) * ((jnp.arange(n1)[:, None] * j2[None, :]) % n).astype(F32)[:, :, None]
    return d1.astype(BF16), d1_inv.astype(BF16), f2.astype(BF16), f2.T.astype(BF16), jnp.cos(pht), -jnp.sin(pht)


def _dft1_kernel(d_ref, x_ref, o_ref):
    n1h, m, c = x_ref.shape
    y = _dot(d_ref[...], x_ref[...].reshape(n1h * m, c).astype(BF16))
    o_ref[...] = y.reshape(o_ref.shape).astype(o_ref.dtype)


def dft_stage1(x, d1, n2):
    l, c = x.shape
    n1h = d1.shape[1]
    m = SUBLANES
    dk = jnp.kron(d1.astype(F32), jnp.eye(m, dtype=F32)).astype(BF16)
    out = pl.pallas_call(
        _dft1_kernel,
        grid=(n2 // m,),
        in_specs=[pl.BlockSpec(dk.shape, lambda j: (0, 0)), pl.BlockSpec((n1h, m, c), lambda j: (0, j, 0))],
        out_specs=pl.BlockSpec((d1.shape[0], m, c), lambda j: (0, j, 0)),
        out_shape=jax.ShapeDtypeStruct((d1.shape[0], n2, c), F32),
        compiler_params=_cparams(("parallel",)),
        name="dft_stage1",
    )(dk, x.reshape(n1h, n2, c))
    return out.reshape(d1.shape[0] // 2, 2, n2, c)


def _twiddled_stage2(f_ref, a_ref, twr_ref, twi_ref):
    shape = a_ref.shape[1:]
    twr = jnp.broadcast_to(twr_ref[...], shape)
    twi = jnp.broadcast_to(twi_ref[...], shape)
    ar, ai = a_ref[0].astype(F32), a_ref[1].astype(F32)
    a = jnp.concatenate([ar * twr - ai * twi, ar * twi + ai * twr], axis=0).astype(BF16)
    return _dot(f_ref[...], a), twr, twi


def _filter_spectrum_kernel(f_ref, a_ref, twr_ref, twi_ref, h_ref):
    n2 = a_ref.shape[1]
    c = h_ref.shape[2]
    x, _, _ = _twiddled_stage2(f_ref, a_ref, twr_ref, twi_ref)
    h_ref[0] = x[:n2, :c] + x[:n2, c:]
    h_ref[1] = x[n2:, :c] - x[n2:, c:]


def filter_spectrum(a_taps, f2, twr, twi):
    n1, _, n2, c2 = a_taps.shape
    c = c2 // 2
    tw = pl.BlockSpec((None, n2, 1), lambda i: (i, 0, 0))
    return pl.pallas_call(
        _filter_spectrum_kernel,
        grid=(n1,),
        in_specs=[pl.BlockSpec((2 * n2, 2 * n2), lambda i: (0, 0)),
                  pl.BlockSpec((None, 2, n2, c2), lambda i: (i, 0, 0, 0)), tw, tw],
        out_specs=pl.BlockSpec((None, 2, n2, c), lambda i: (i, 0, 0, 0)),
        out_shape=jax.ShapeDtypeStruct((n1, 2, n2, c), F32),
        compiler_params=_cparams(("parallel",)),
        name="filter_spectrum",
    )(f2, a_taps, twr, twi)


def _spectral_mix_kernel(f_ref, fi_ref, a_ref, h_ref, twr_ref, twi_ref, o_ref):
    n2 = a_ref.shape[1]
    x, twr, twi = _twiddled_stage2(f_ref, a_ref, twr_ref, twi_ref)
    xr, xi = x[:n2], x[n2:]
    hr, hi = h_ref[0], h_ref[1]
    y = jnp.concatenate([xr * hr - xi * hi, xr * hi + xi * hr], axis=0).astype(BF16)
    b = _dot(fi_ref[...], y)
    br, bi = b[:n2], b[n2:]
    o_ref[0] = (br * twr + bi * twi).astype(o_ref.dtype)
    o_ref[1] = (bi * twr - br * twi).astype(o_ref.dtype)


def spectral_mix(a_z, h, f2, f2_inv, twr, twi):
    n1, _, n2, c = a_z.shape
    blk = pl.BlockSpec((None, 2, n2, c), lambda i: (i, 0, 0, 0))
    mat = pl.BlockSpec((2 * n2, 2 * n2), lambda i: (0, 0))
    tw = pl.BlockSpec((None, n2, 1), lambda i: (i, 0, 0))
    return pl.pallas_call(
        _spectral_mix_kernel,
        grid=(n1,),
        in_specs=[mat, mat, blk, blk, tw, tw],
        out_specs=blk,
        out_shape=jax.ShapeDtypeStruct((n1, 2, n2, c), BF16),
        compiler_params=_cparams(("parallel",)),
        name="spectral_mix",
    )(f2, f2_inv, a_z, h, twr, twi)


def _hyena_out_kernel(di_ref, b_ref, x0_ref, z_ref, inv_ref, skip_ref, o_ref, *, inv_n):
    rows, m, c = b_ref.shape
    conv = (_dot(di_ref[...], b_ref[...].reshape(rows * m, c)) * inv_n).reshape(o_ref.shape)
    o_ref[...] = (x0_ref[...] * (conv * inv_ref[...] + z_ref[...] * skip_ref[...])).astype(o_ref.dtype)


def hyena_output(b, d1_inv, x0, z, inv_norm, skip):
    n1, _, n2, c = b.shape
    l = x0.shape[0]
    n1h = n1 // 2
    m = 2 * SUBLANES
    dk = jnp.kron(d1_inv.astype(F32), jnp.eye(m, dtype=F32)).astype(BF16)
    kern = functools.partial(_hyena_out_kernel, inv_n=1.0 / (2 * l))
    tile = pl.BlockSpec((n1h, m, c), lambda j: (0, j, 0))
    vec = pl.BlockSpec((1, 1, c), lambda j: (0, 0, 0))
    out = pl.pallas_call(
        kern,
        grid=(n2 // m,),
        in_specs=[pl.BlockSpec(dk.shape, lambda j: (0, 0)),
                  pl.BlockSpec((2 * n1, m, c), lambda j: (0, j, 0)),
                  tile, tile, vec, vec],
        out_specs=tile,
        out_shape=jax.ShapeDtypeStruct((n1h, n2, c), BF16),
        compiler_params=_cparams(("parallel",)),
        name="hyena_output",
    )(dk, b.reshape(2 * n1, n2, c), x0.reshape(n1h, n2, c), z.reshape(n1h, n2, c),
      inv_norm.reshape(1, 1, c), skip.reshape(1, 1, c))
    return out.reshape(l, c)


def _small_conv_kernel(d_ref, di_ref, z_ref, taps_ref, x0_ref, inv_ref, skip_ref, o_ref, *, inv_n):
    c = z_ref.shape[1]
    n = d_ref.shape[0] // 2
    zs = _dot(d_ref[...], z_ref[...], precision=HIGHEST)
    ts = _dot(d_ref[...], taps_ref[...], precision=HIGHEST)
    hr = ts[:n, :c] + ts[:n, c:]
    hi = ts[n:, :c] - ts[n:, c:]
    zr, zi = zs[:n], zs[n:]
    y = jnp.concatenate([zr * hr - zi * hi, zr * hi + zi * hr], axis=0)
    conv = _dot(di_ref[...], y, precision=HIGHEST) * inv_n
    o_ref[...] = (x0_ref[...] * (conv * inv_ref[...] + z_ref[...] * skip_ref[...])).astype(o_ref.dtype)


def hyena_output_short(z, taps, x0, inv_norm, skip):
    l, c = z.shape
    n = 2 * l
    ph = (2.0 * math.pi / n) * ((jnp.arange(n)[:, None] * jnp.arange(l)[None, :]) % n).astype(F32)
    d = jnp.concatenate([jnp.cos(ph), -jnp.sin(ph)], axis=0)
    di = jnp.concatenate([jnp.cos(ph), -jnp.sin(ph)], axis=0).T
    args = (d, di, z, taps, x0, inv_norm, skip.reshape(1, c))
    kern = functools.partial(_small_conv_kernel, inv_n=1.0 / n)
    return pl.pallas_call(
        kern,
        grid=(1,),
        in_specs=[pl.BlockSpec(a.shape, lambda i: (0, 0)) for a in args],
        out_specs=pl.BlockSpec((l, c), lambda i: (0, 0)),
        out_shape=jax.ShapeDtypeStruct((l, c), BF16),
        compiler_params=_cparams(("arbitrary",)),
        name="hyena_output_short",
    )(*args)


def hyena_long(u, conv_w, conv_b, f_w1, f_b1, f_freq, f_w2, f_b2, f_w3, skip):
    l = u.shape[0]
    x0, z = hyena_gate(u, conv_w, conv_b)
    taps, asum = hyena_filter_taps(l, f_w1, f_b1, f_freq, f_w2, f_b2, f_w3)
    inv_norm = 1.0 / asum
    if 2 * l < DFT_N1 * SUBLANES * 2:
        return hyena_output_short(z, taps, x0, inv_norm, skip)
    n2 = 2 * l // DFT_N1
    d1, d1_inv, f2, f2_inv, twr, twi = _dft_tables(l)
    h = filter_spectrum(dft_stage1(taps, d1, n2), f2, twr, twi)
    b = spectral_mix(dft_stage1(z, d1, n2), h, f2, f2_inv, twr, twi)
    return hyena_output(b, d1_inv, x0, z, inv_norm, skip)


def _head_sumsq(x, bd):
    sq = x * x
    hi = sq.astype(BF16)
    lo = (sq - hi.astype(F32)).astype(BF16)
    return _dot(hi, bd) + _dot(lo, bd)


def _qk_prep_kernel(x_ref, gain_ref, bd_ref, *rest, rope, scale, transposed):
    x = x_ref[...]
    w = x.shape[1]
    ms = _head_sumsq(x, bd_ref[...]) * (1.0 / HEAD_DIM)
    xn = x * lax.rsqrt(ms + RMS_EPS) * gain_ref[...]
    if rope:
        cos_ref, sin_ref, o_ref = rest
        reps = w // cos_ref.shape[1]
        cos = jnp.tile(cos_ref[...], (1, reps)) if reps > 1 else cos_ref[...]
        sin = jnp.tile(sin_ref[...], (1, reps)) if reps > 1 else sin_ref[...]
        lane = lax.broadcasted_iota(I32, x.shape, 1)
        partner = jnp.where(lane % 2 == 0, pltpu.roll(xn, w - 1, 1), pltpu.roll(xn, 1, 1))
        xn = xn * cos + partner * sin
    else:
        (o_ref,) = rest
    if scale != 1.0:
        xn = xn * scale
    if transposed:
        xn = xn.T
    o_ref[...] = xn.astype(o_ref.dtype)


def _rope_tables(l):
    half = HEAD_DIM // 2
    inv_freq = ROPE_THETA ** (-jnp.arange(0, half, 2, dtype=F32) / half)
    t = jnp.arange(l)
    row = (t // GRID_W).astype(F32)
    col = (t % GRID_W).astype(F32)
    ang = jnp.concatenate([jnp.repeat(row[:, None] * inv_freq[None], 2, axis=1),
                           jnp.repeat(col[:, None] * inv_freq[None], 2, axis=1)], axis=1)
    sign = jnp.where(jnp.arange(HEAD_DIM) % 2 == 0, -1.0, 1.0).astype(F32)
    cos = jnp.tile(jnp.cos(ang), (1, 2))
    sin = jnp.tile(jnp.sin(ang) * sign[None], (1, 2))
    return cos, sin


def qk_prep(x, gain, rope_tabs, scale, transposed=False):
    l, w = x.shape
    tm = _row_tile(l, 512)
    head = jnp.arange(w) // HEAD_DIM
    bd = (head[:, None] == head[None, :]).astype(BF16)
    gain_t = jnp.tile(gain.reshape(1, HEAD_DIM), (1, w // HEAD_DIM))
    row = lambda i: (i, 0)
    fixed = lambda i: (0, 0)
    in_specs = [pl.BlockSpec((tm, w), row), pl.BlockSpec((1, w), fixed), pl.BlockSpec((w, w), fixed)]
    args = [x, gain_t, bd]
    if rope_tabs is not None:
        in_specs += [pl.BlockSpec((tm, 2 * HEAD_DIM), row)] * 2
        args += list(rope_tabs)
    kern = functools.partial(_qk_prep_kernel, rope=rope_tabs is not None, scale=scale, transposed=transposed)
    if transposed:
        out_spec, out_shape = pl.BlockSpec((w, tm), lambda i: (0, i)), (w, l)
    else:
        out_spec, out_shape = pl.BlockSpec((tm, w), row), (l, w)
    return pl.pallas_call(
        kern, grid=(l // tm,), in_specs=in_specs, out_specs=out_spec,
        out_shape=jax.ShapeDtypeStruct(out_shape, BF16),
        compiler_params=_cparams(("parallel",)), name="qk_prep",
    )(*args)


def _flash_kernel(qt_ref, k_ref, vt_ref, o_ref, qs_ref, s_ref, m_ref, l_ref, acc_ref, *, tk, nk):
    dh = HEAD_DIM
    rep = qt_ref.shape[0] // dh
    tq = qt_ref.shape[1]
    for r in range(rep):
        qs_ref[:, r * tq:(r + 1) * tq] = qt_ref[r * dh:(r + 1) * dh, :]
    m_ref[...] = jnp.full(m_ref.shape, NEG_BIG, F32)
    l_ref[...] = jnp.zeros(l_ref.shape, F32)
    acc_ref[...] = jnp.zeros(acc_ref.shape, F32)

    def scores(j):
        start = pl.multiple_of(jnp.minimum(j, nk - 1) * tk, tk)
        return _dot(k_ref[pl.ds(start, tk), :], qs_ref[...])

    def absorb(j, s):
        start = pl.multiple_of(j * tk, tk)
        m_old = m_ref[...]
        m_new = jnp.maximum(m_old, s.max(axis=0, keepdims=True))
        alpha = jnp.exp2(m_old - m_new)
        p = jnp.exp2(s - m_new)
        l_ref[...] = alpha * l_ref[...] + p.sum(axis=0, keepdims=True)
        acc_ref[...] = alpha * acc_ref[...] + _dot(vt_ref[:, pl.ds(start, tk)], p.astype(BF16))
        m_ref[...] = m_new

    s_ref[0] = scores(0)

    def body(i, carry):
        j = 2 * i
        s_ref[1] = scores(j + 1)
        absorb(j, s_ref[0])
        s_ref[0] = scores(j + 2)
        absorb(j + 1, s_ref[1])
        return carry

    lax.fori_loop(0, nk // 2, body, 0)
    if nk % 2:
        absorb(nk - 1, s_ref[0])
    out = acc_ref[...] / l_ref[...]
    for r in range(rep):
        o_ref[:, r * dh:(r + 1) * dh] = out[:, r * tq:(r + 1) * tq].T.astype(o_ref.dtype)


def _kv_chunk(lk):
    for tiles in (5, 4, 3, 2, 1):
        if lk % (tiles * 256) == 0:
            return tiles * 256
    raise ValueError(lk)


def gqa_attention(q_t, k_hm, v_t):
    wq, l = q_t.shape
    hkv, lk, dh = k_hm.shape
    wg = wq // hkv
    rep = wg // dh
    tq = _row_tile(l, 256)
    tk = _kv_chunk(lk)
    kern = functools.partial(_flash_kernel, tk=tk, nk=lk // tk)
    return pl.pallas_call(
        kern,
        grid=(hkv, l // tq),
        in_specs=[pl.BlockSpec((wg, tq), lambda g, i: (g, i)),
                  pl.BlockSpec((None, lk, dh), lambda g, i: (g, 0, 0)),
                  pl.BlockSpec((None, dh, lk), lambda g, i: (g, 0, 0))],
        out_specs=pl.BlockSpec((tq, wg), lambda g, i: (i, g)),
        out_shape=jax.ShapeDtypeStruct((l, wq), BF16),
        scratch_shapes=[pltpu.VMEM((dh, rep * tq), BF16), pltpu.VMEM((2, tk, rep * tq), F32),
                        pltpu.VMEM((1, rep * tq), F32), pltpu.VMEM((1, rep * tq), F32),
                        pltpu.VMEM((dh, rep * tq), F32)],
        compiler_params=_cparams(("parallel", "parallel")),
        name="gqa_attention",
    )(q_t, k_hm, v_t)


def _cmul(ar, ai, br, bi):
    return ar * br - ai * bi, ar * bi + ai * br


def _s5_operators(a_re, a_im, log_dt, b_re, b_im, c_re, c_im):
    t = S5_CHUNK
    gs = S5_GROUP
    hp = dict(precision=HIGHEST)
    dt = jnp.exp(log_dt)[..., None]
    zr, zi = a_re * dt, a_im * dt
    er = jnp.exp(zr)
    abr, abi = er * jnp.cos(zi), er * jnp.sin(zi)
    den = a_re * a_re + a_im * a_im
    fr = ((abr - 1.0) * a_re + abi * a_im) / den
    fi = (abi * a_re - (abr - 1.0) * a_im) / den
    bbr, bbi = _cmul(fr[..., None], fi[..., None], b_re, b_im)
    tau = jnp.arange(t + 1, dtype=F32)
    pr = jnp.exp(zr[..., None] * tau) * jnp.cos(zi[..., None] * tau)
    pi = jnp.exp(zr[..., None] * tau) * jnp.sin(zi[..., None] * tau)
    car, cai = _cmul(c_re[..., None], c_im[..., None], pr[:, :, None, :, :t], pi[:, :, None, :, :t])
    ktap = (jnp.einsum('dgqpt,dgpk->dgtqk', car, bbr, **hp) - jnp.einsum('dgqpt,dgpk->dgtqk', cai, bbi, **hp))
    ktp = jnp.concatenate([jnp.zeros_like(ktap), ktap], axis=2)
    win = jnp.stack([ktp[:, :, t - i:2 * t - i] for i in range(t)], axis=2)
    m_tot = (win[0] + win[1].transpose(0, 2, 1, 3, 4)).transpose(0, 1, 4, 2, 3)
    pw_r = jnp.stack([pr[0, ..., t - 1::-1], pr[1, ..., :t]])
    pw_i = jnp.stack([pi[0, ..., t - 1::-1], pi[1, ..., :t]])
    wr, wi = _cmul(pw_r[..., None], pw_i[..., None], bbr[:, :, :, None, :], bbi[:, :, :, None, :])
    w_in = jnp.stack([wr[0], wi[0], wr[1], wi[1]]).transpose(1, 3, 4, 0, 2)
    pv_r = jnp.stack([pr[0, ..., 1:], pr[1, ..., t:0:-1]])
    pv_i = jnp.stack([pi[0, ..., 1:], pi[1, ..., t:0:-1]])
    vr, vi = _cmul(c_re[..., None], c_im[..., None], pv_r[:, :, None], pv_i[:, :, None])
    v_out = jnp.stack([vr[0], -vi[0], vr[1], -vi[1]]).transpose(1, 0, 3, 4, 2)

    nb = S5_GROUPS // S5_LANE_GROUPS
    lg = S5_LANE_GROUPS

    def spread(width, n_outer):
        c = jnp.arange(n_outer * width)
        lane = (c // width)[None, :] * (lg * width) + jnp.arange(lg)[:, None] * width + (c % width)[None, :]
        return (lane[:, :, None] == jnp.arange(n_outer * lg * width)[None, None, :]).astype(F32)

    def expand(x, pattern, width, n_outer):
        x = x.reshape((nb, lg) + x.shape[1:-2] + (n_outer * width,)).astype(BF16).astype(F32)
        return jnp.einsum(pattern, x, spread(width, n_outer)).astype(BF16)

    m_op = expand(m_tot, 'bgikc,gcl->bigkl', gs, t).reshape(nb, t * LANES, t * LANES)
    w_op = expand(w_in, 'bgikc,gcl->bigkl', S5_STATE, 4).reshape(nb, t * LANES, 4 * lg * S5_STATE)
    v_op = expand(v_out, 'bgrpc,gcl->brgpl', gs, t).reshape(nb, 4 * lg * S5_STATE, t * LANES)
    return m_op, w_op, v_op, pr[..., t], pi[..., t]


def _s5_layout_kernel(*refs):
    *u_refs, o_ref = refs
    rows = o_ref.shape[0]
    t = S5_CHUNK
    for b, u_ref in enumerate(u_refs):
        for i in range(t):
            o_ref[:, (b * t + i) * LANES:(b * t + i + 1) * LANES] = u_ref[pl.ds(i, rows, stride=t), :].astype(o_ref.dtype)


def _s5_state_in_kernel(u_ref, w_ref, *e_refs):
    e = _dot(u_ref[...], w_ref[...])
    q = e.shape[1] // len(e_refs)
    for part, e_ref in enumerate(e_refs):
        e_ref[...] = e[:, part * q:(part + 1) * q]


def _s5_scan_kernel(ar_ref, ai_ref, er_ref, ei_ref, sr_ref, si_ref, cr_ref, ci_ref, *, reverse):
    @pl.when(pl.program_id(0) == 0)
    def _():
        cr_ref[...] = jnp.zeros_like(cr_ref)
        ci_ref[...] = jnp.zeros_like(ci_ref)

    ar, ai = ar_ref[...], ai_ref[...]
    n = er_ref.shape[0]

    def body(k, carry):
        c = n - 1 - k if reverse else k
        sr, si = carry
        sr_ref[c] = sr
        si_ref[c] = si
        return ar * sr - ai * si + er_ref[c], ar * si + ai * sr + ei_ref[c]

    sr, si = lax.fori_loop(0, n, body, (cr_ref[...], ci_ref[...]))
    cr_ref[...] = sr
    ci_ref[...] = si


def _s5_out_kernel(u_ref, fr_ref, fi_ref, br_ref, bi_ref, m_ref, v_ref, y_ref):
    s = jnp.concatenate([fr_ref[...], fi_ref[...], br_ref[...], bi_ref[...]], axis=1).astype(BF16)
    y_ref[...] = _dot(u_ref[...], m_ref[...]) + _dot(s, v_ref[...])


def _s5_readout_kernel(y_ref, u_ref, d_ref, w_ref, b_ref, o_ref, ynat_ref):
    rows = y_ref.shape[0]
    t = S5_CHUNK
    nb = ynat_ref.shape[0]
    for b in range(nb):
        for i in range(t):
            ynat_ref[b, pl.ds(i, rows, stride=t), :] = y_ref[:, (b * t + i) * LANES:(b * t + i + 1) * LANES]
    y = jnp.concatenate([ynat_ref[b] for b in range(nb)], axis=1) + d_ref[...] * u_ref[...]
    y = 0.5 * y * (1.0 + jnp.tanh(math.sqrt(2.0 / math.pi) * (y + 0.044715 * (y * y * y))))
    gate = jax.nn.sigmoid(_dot(y.astype(BF16), w_ref[...]) + b_ref[...])
    o_ref[...] = (y * gate).astype(o_ref.dtype)


def s5_mix(u_ctx, u_lat, a_re, a_im, log_dt, b_re, b_im, c_re, c_im, d_skip, glu_w, glu_b):
    t = S5_CHUNK
    n_ctx, w = u_ctx.shape
    l = u_lat.shape[0]
    n_tok = n_ctx + l
    nch = n_tok // t
    tc = n_ctx // t
    n_tiles = nch // tc
    assert n_ctx == tc * t and tc % (2 * SUBLANES) == 0 and l % (tc * t) == 0
    nb = w // LANES
    cw = t * LANES
    sw = S5_LANE_GROUPS * S5_STATE
    n_state = S5_GROUPS * S5_STATE
    m_op, w_op, v_op, atr, ati = _s5_operators(a_re, a_im, log_dt, b_re, b_im, c_re, c_im)

    u_all = jnp.concatenate([u_ctx, u_lat], axis=0)
    u_ch = pl.pallas_call(
        _s5_layout_kernel,
        grid=(n_tiles,),
        in_specs=[pl.BlockSpec((tc * t, LANES), functools.partial(lambda b, i: (i, b), b)) for b in range(nb)],
        out_specs=pl.BlockSpec((tc, nb * cw), lambda i: (i, 0)),
        out_shape=jax.ShapeDtypeStruct((nch, nb * cw), BF16),
        compiler_params=_cparams(("parallel",)),
        name="s5_layout",
    )(*([u_all] * nb))

    u_blk = pl.BlockSpec((nch, cw), lambda b: (0, b))
    s_blk = pl.BlockSpec((nch, sw), lambda b: (0, b))
    states_in = pl.pallas_call(
        _s5_state_in_kernel,
        grid=(nb,),
        in_specs=[u_blk, pl.BlockSpec((None, cw, 4 * sw), lambda b: (b, 0, 0))],
        out_specs=[s_blk] * 4,
        out_shape=[jax.ShapeDtypeStruct((nch, n_state), F32)] * 4,
        compiler_params=_cparams(("parallel",)),
        name="s5_state_in",
    )(u_ch, w_op)

    slab = n_state // SUBLANES
    vec = pl.BlockSpec((SUBLANES, slab), lambda s: (0, 0))
    orders = (lambda s: (s, 0, 0),
              lambda s: (jnp.where(s == 0, 0, n_tiles - s), 0, 0))
    states = []
    for d in range(2):
        blk = pl.BlockSpec((tc, SUBLANES, slab), orders[d])
        s_re, s_im = pl.pallas_call(
            functools.partial(_s5_scan_kernel, reverse=bool(d)),
            grid=(n_tiles,),
            in_specs=[vec, vec, blk, blk],
            out_specs=[blk, blk],
            out_shape=[jax.ShapeDtypeStruct((nch, SUBLANES, slab), F32)] * 2,
            scratch_shapes=[pltpu.VMEM((SUBLANES, slab), F32)] * 2,
            compiler_params=_cparams(("arbitrary",)),
            name="s5_scan",
        )(atr[d].reshape(SUBLANES, slab), ati[d].reshape(SUBLANES, slab),
          states_in[2 * d].reshape(nch, SUBLANES, slab), states_in[2 * d + 1].reshape(nch, SUBLANES, slab))
        states += [s_re.reshape(nch, n_state), s_im.reshape(nch, n_state)]

    oc = cw // 4
    y_ch = pl.pallas_call(
        _s5_out_kernel,
        grid=(nb, cw // oc),
        in_specs=[pl.BlockSpec((nch, cw), lambda b, j: (0, b))] + [pl.BlockSpec((nch, sw), lambda b, j: (0, b))] * 4
                 + [pl.BlockSpec((None, cw, oc), lambda b, j: (b, 0, j)),
                    pl.BlockSpec((None, 4 * sw, oc), lambda b, j: (b, 0, j))],
        out_specs=pl.BlockSpec((nch, oc), lambda b, j: (0, b * (cw // oc) + j)),
        out_shape=jax.ShapeDtypeStruct((nch, nb * cw), F32),
        compiler_params=_cparams(("parallel", "parallel")),
        name="s5_out",
    )(u_ch, *states, m_op, v_op)

    row = lambda i: (i, 0)
    fixed = lambda i: (0, 0)
    return pl.pallas_call(
        _s5_readout_kernel,
        grid=(l // (tc * t),),
        in_specs=[pl.BlockSpec((tc, nb * cw), lambda i: (i + n_ctx // (tc * t), 0)), pl.BlockSpec((tc * t, w), row),
                  pl.BlockSpec((1, w), fixed), pl.BlockSpec((w, w), fixed), pl.BlockSpec((1, w), fixed)],
        out_specs=pl.BlockSpec((tc * t, w), row),
        out_shape=jax.ShapeDtypeStruct((l, w), BF16),
        scratch_shapes=[pltpu.VMEM((nb, tc * t, LANES), F32)],
        compiler_params=_cparams(("parallel",)),
        name="s5_readout",
    )(y_ch, u_lat, d_skip.reshape(1, w), glu_w.astype(BF16), glu_b.reshape(1, w))


def _first_max(vals, lane):
    m = vals.max(axis=1, keepdims=True)
    idx = jnp.where(vals == m, lane, jnp.int32(1 << 20)).min(axis=1, keepdims=True)
    return m, idx


def _stream_specs(n_ctx_tiles, tm, d):
    return (pl.BlockSpec((tm, d), lambda i: (jnp.clip(i, 0, max(n_ctx_tiles - 1, 0)), 0)),
            pl.BlockSpec((tm, d), lambda i: (jnp.maximum(i - n_ctx_tiles, 0), 0)))


def _stream_tile(xc_ref, xl_ref, n_ctx_tiles):
    if n_ctx_tiles == 0:
        return xl_ref[...]
    return jnp.where(pl.program_id(0) < n_ctx_tiles, xc_ref[...], xl_ref[...])


def _store_row_slabs(ref, x):
    rows = x.shape[0]
    for s in range(ROW_SLAB):
        ref[pl.ds(s, rows, stride=ROW_SLAB), :] = x[:, s * LANES:(s + 1) * LANES]


def _router_kernel(xc_ref, xl_ref, sc_ref, sh_ref, rw_ref, rb_ref, tri_ref,
                   hf_ref, te_ref, gt_ref, rk_ref, cnt_ref, run_ref, *, n_ctx_tiles):
    @pl.when(pl.program_id(0) == 0)
    def _():
        run_ref[...] = jnp.zeros_like(run_ref)

    hf = _stream_tile(xc_ref, xl_ref, n_ctx_tiles) * (1.0 + sc_ref[...]) + sh_ref[...]
    _store_row_slabs(hf_ref, hf)
    tm = hf.shape[0]
    scores = jax.nn.sigmoid(_dot(hf, rw_ref[...], precision=HIGHEST))
    biased = scores + rb_ref[...]
    lane = lax.broadcasted_iota(I32, (tm, N_EXPERTS), 1)
    grp = lane // (N_EXPERTS // N_EXPERT_GROUPS)
    lane_o = lax.broadcasted_iota(I32, (tm, LANES), 1)
    neg = jnp.float32(-jnp.inf)

    group_score = jnp.full((tm, LANES), neg, F32)
    for g in range(N_EXPERT_GROUPS):
        vals = jnp.where(grp == g, biased, neg)
        m1, i1 = _first_max(vals, lane)
        m2 = jnp.where(lane == i1, neg, vals).max(axis=1, keepdims=True)
        group_score = jnp.where(lane_o == g, m1 + m2, group_score)
    keep = jnp.zeros((tm, N_EXPERTS), F32)
    for _ in range(TOPK_GROUPS):
        _, gi = _first_max(group_score, lane_o)
        keep = jnp.where(grp == gi, 1.0, keep)
        group_score = jnp.where(lane_o == gi, neg, group_score)

    masked = jnp.where(keep > 0.0, biased, neg)
    member = jnp.zeros((tm, N_EXPERTS), F32)
    e_cols, g_cols = [], []
    for _ in range(TOP_K):
        _, ei = _first_max(masked, lane)
        hit = lane == ei
        g_cols.append(jnp.where(hit, scores, 0.0).sum(axis=1, keepdims=True))
        masked = jnp.where(hit, neg, masked)
        member = jnp.where(hit, 1.0, member)
        e_cols.append(ei)
    g_sum = g_cols[0]
    for gk in g_cols[1:]:
        g_sum = g_sum + gk

    before = _dot(tri_ref[...], member.astype(BF16)) + run_ref[...]
    te = jnp.zeros((tm, LANES), I32)
    rk = jnp.zeros((tm, LANES), I32)
    gt = jnp.zeros((tm, LANES), F32)
    for k in range(TOP_K):
        rank = jnp.where(lane == e_cols[k], before, 0.0).sum(axis=1, keepdims=True)
        te = jnp.where(lane_o == k, e_cols[k], te)
        rk = jnp.where(lane_o == k, rank.astype(I32), rk)
        gt = jnp.where(lane_o == k, ROUTED_SCALE * g_cols[k] / g_sum, gt)
    te_ref[...] = te
    rk_ref[...] = rk
    gt_ref[...] = gt
    run_ref[...] += member.sum(axis=0, keepdims=True)
    cnt_ref[...] = run_ref[...]


def moe_route(x_ctx, x_lat, sc2, sh2, n_ctx_tiles, router_w, router_bias):
    d = x_lat.shape[1]
    tm = MOE_TILE
    n = n_ctx_tiles * tm + x_lat.shape[0]
    tri = (jnp.arange(tm)[None, :] < jnp.arange(tm)[:, None]).astype(BF16)
    row = lambda i: (i, 0)
    fixed = lambda i: (0, 0)
    seg = lambda i: (jnp.where(i < n_ctx_tiles, 1, 0), 0, 0)
    kern = functools.partial(_router_kernel, n_ctx_tiles=n_ctx_tiles)
    return pl.pallas_call(
        kern,
        grid=(n // tm,),
        in_specs=[*_stream_specs(n_ctx_tiles, tm, d), pl.BlockSpec((None, 1, d), seg), pl.BlockSpec((None, 1, d), seg),
                  pl.BlockSpec((d, N_EXPERTS), fixed), pl.BlockSpec((1, N_EXPERTS), fixed),
                  pl.BlockSpec((tm, tm), fixed)],
        out_specs=[pl.BlockSpec((tm * ROW_SLAB, LANES), row), pl.BlockSpec((tm, LANES), row),
                   pl.BlockSpec((tm, LANES), row), pl.BlockSpec((tm, LANES), row),
                   pl.BlockSpec((1, N_EXPERTS), fixed)],
        out_shape=[jax.ShapeDtypeStruct((n * ROW_SLAB, LANES), F32), jax.ShapeDtypeStruct((n, LANES), I32),
                   jax.ShapeDtypeStruct((n, LANES), F32), jax.ShapeDtypeStruct((n, LANES), I32),
                   jax.ShapeDtypeStruct((1, N_EXPERTS), F32)],
        scratch_shapes=[pltpu.VMEM((1, N_EXPERTS), F32)],
        compiler_params=_cparams(("arbitrary",)),
        name="moe_route",
    )(x_ctx, x_lat, sc2, sh2, router_w, router_bias.reshape(1, N_EXPERTS), tri)


def _slots_kernel(te_ref, rk_ref, start_ref, o_ref):
    tm = te_ref.shape[0]
    lane = lax.broadcasted_iota(I32, (tm, N_EXPERTS), 1)
    lane_o = lax.broadcasted_iota(I32, (tm, LANES), 1)
    te = te_ref[...]
    out = rk_ref[...]
    for k in range(TOP_K):
        first = jnp.where(lane == te[:, k:k + 1], start_ref[...], 0.0).sum(axis=1, keepdims=True)
        out = jnp.where(lane_o == k, out + first.astype(I32), out)
    o_ref[...] = out


def moe_slots(te, rk, start):
    n = te.shape[0]
    tm = MOE_TILE
    row = lambda i: (i, 0)
    out = pl.pallas_call(
        _slots_kernel,
        grid=(n // tm,),
        in_specs=[pl.BlockSpec((tm, LANES), row), pl.BlockSpec((tm, LANES), row),
                  pl.BlockSpec((1, N_EXPERTS), lambda i: (0, 0))],
        out_specs=pl.BlockSpec((tm, LANES), row),
        out_shape=jax.ShapeDtypeStruct((n, LANES), I32),
        compiler_params=_cparams(("parallel",)),
        name="moe_slots",
    )(te, rk, start.astype(F32).reshape(1, N_EXPERTS))
    return out[:, :TOP_K].reshape(-1)


def _slab_rows(ref, row, n_rows):
    first = row * ROW_SLAB
    if not isinstance(first, int):
        first = pl.multiple_of(first, ROW_SLAB)
    return ref.at[pl.ds(first, n_rows * ROW_SLAB), :]


def _dispatch_kernel(dest_ref, hf_ref, xs_hbm, zbuf, sem, zsem, *, n_assign):
    n_rows = dest_ref.shape[0]

    @pl.when(pl.program_id(0) == 0)
    def _():
        zbuf[...] = jnp.zeros_like(zbuf)
        tail = pltpu.make_async_copy(zbuf, _slab_rows(xs_hbm, n_assign, EXPERT_BLOCK), zsem)
        tail.start()
        tail.wait()

    def body(r, carry):
        src = _slab_rows(hf_ref, r, 1)
        for k in range(TOP_K):
            pltpu.make_async_copy(src, _slab_rows(xs_hbm, dest_ref[r * TOP_K + k], 1), sem).start(priority=k % 2)
        return carry
    lax.fori_loop(0, n_rows // TOP_K, body, 0)
    for _ in range(TOP_K):
        pltpu.make_async_copy(hf_ref, hf_ref, sem).wait()


def moe_dispatch(dest, hf_slabs):
    n_assign = dest.shape[0]
    tm = MOE_TILE
    n_rows = tm * TOP_K
    kern = functools.partial(_dispatch_kernel, n_assign=n_assign)
    return pl.pallas_call(
        kern,
        grid=(n_assign // n_rows,),
        in_specs=[pl.BlockSpec((n_rows,), lambda i: (i,), memory_space=pltpu.SMEM),
                  pl.BlockSpec((tm * ROW_SLAB, LANES), lambda i: (i, 0))],
        out_specs=pl.BlockSpec(memory_space=pl.ANY),
        out_shape=jax.ShapeDtypeStruct(((n_assign + EXPERT_BLOCK) * ROW_SLAB, LANES), F32),
        scratch_shapes=[pltpu.VMEM((EXPERT_BLOCK * ROW_SLAB, LANES), F32), pltpu.SemaphoreType.DMA(()),
                        pltpu.SemaphoreType.DMA(())],
        compiler_params=_cparams(("arbitrary",)),
        name="moe_dispatch",
    )(dest, hf_slabs)


def _row_gather(idx_ref, base, count, src_hbm, dst, sem):
    group = 8

    def body(g, carry):
        for k in range(group):
            j = g * group + k
            cp = pltpu.make_async_copy(_slab_rows(src_hbm, idx_ref[base + j], 1), _slab_rows(dst, j, 1), sem)
            cp.start(priority=k % 2)
        return carry
    lax.fori_loop(0, count // group, body, 0)


def _wait_rows(dst, sem):
    pltpu.make_async_copy(dst, dst, sem).wait()


def _gathered_rows(buf, first, rows, stride):
    return jnp.concatenate(
        [buf[pl.ds(first * ROW_SLAB + s, rows, stride=stride * ROW_SLAB), :] for s in range(ROW_SLAB)], axis=1)


def _swiglu(x, wg, wu, wd):
    gate = _dot(x, wg)
    up = _dot(x, wu)
    return _dot((gate * jax.nn.sigmoid(gate) * up).astype(BF16), wd)


def _valid_row_copies(ybuf, ys_hbm, row0, valid, sem):
    out = [(valid == EXPERT_BLOCK,
            pltpu.make_async_copy(_slab_rows(ybuf, 0, EXPERT_BLOCK), _slab_rows(ys_hbm, row0, EXPERT_BLOCK), sem))]
    part = valid < EXPERT_BLOCK
    size = EXPERT_BLOCK // 2
    while size >= 1:
        off = valid & ~(2 * size - 1)
        out.append((part & ((valid & size) != 0),
                    pltpu.make_async_copy(_slab_rows(ybuf, off, size), _slab_rows(ys_hbm, row0 + off, size), sem)))
        size //= 2
    return out


def _expert_kernel(be_ref, r0_ref, nv_ref, ws_ref, nx_ref, e1_ref, xs_hbm, wg_hbm, wu_hbm, wd_hbm, ys_hbm,
                   xbuf, ybuf, wg_st, wu_st, wd_st, wg_bf, wu_bf, wd_bf, sem_in, sem_out, sem_w, *, layer):
    b = pl.program_id(0)
    n_blocks = pl.num_programs(0)
    slot = b % 2

    def weight_copies(e, ws):
        return [pltpu.make_async_copy(hbm.at[layer, e], st.at[ws], sem_w.at[ws])
                for hbm, st in ((wg_hbm, wg_st), (wu_hbm, wu_st), (wd_hbm, wd_st))]

    @pl.when(b == 0)
    def _():
        for cp in weight_copies(be_ref[0], 0):
            cp.start()

        @pl.when(e1_ref[0] >= 0)
        def _():
            for cp in weight_copies(e1_ref[0], 1):
                cp.start()

    @pl.when(ws_ref[b] >= 0)
    def _():
        ws = ws_ref[b]
        for cp in weight_copies(be_ref[b], ws):
            cp.wait()
        wg_bf[...] = wg_st[ws].astype(BF16)
        wu_bf[...] = wu_st[ws].astype(BF16)
        wd_bf[...] = wd_st[ws].astype(BF16)

        @pl.when(nx_ref[b] >= 0)
        def _():
            for cp in weight_copies(nx_ref[b], (ws + EXPERT_WEIGHT_SLOTS - 1) % EXPERT_WEIGHT_SLOTS):
                cp.start()

    def fetch(blk, sl):
        return pltpu.make_async_copy(_slab_rows(xs_hbm, r0_ref[blk], EXPERT_BLOCK), xbuf.at[sl], sem_in.at[sl])

    def drain(blk, sl):
        for cond, cp in _valid_row_copies(ybuf.at[sl], ys_hbm, r0_ref[blk], nv_ref[blk], sem_out.at[sl]):
            pl.when(cond)(cp.wait)

    @pl.when(b == 0)
    def _():
        for k in range(EXPERT_LOOKAHEAD):
            pl.when(nv_ref[k] > 0)(fetch(k, k).start)

    nxt = jnp.minimum(b + EXPERT_LOOKAHEAD, n_blocks - 1)

    @pl.when((b + EXPERT_LOOKAHEAD < n_blocks) & (nv_ref[nxt] > 0))
    def _():
        fetch(nxt, nxt % (EXPERT_LOOKAHEAD + 1)).start()

    @pl.when(b >= 2)
    def _():
        drain(jnp.maximum(b - 2, 0), slot)

    @pl.when(nv_ref[b] > 0)
    def _():
        xslot = b % (EXPERT_LOOKAHEAD + 1)
        fetch(b, xslot).wait()
        x = _gathered_rows(xbuf.at[xslot], 0, EXPERT_BLOCK, 1).astype(BF16)
        y = _swiglu(x, wg_bf[...], wu_bf[...], wd_bf[...])
        _store_row_slabs(ybuf.at[slot], y)
        for cond, cp in _valid_row_copies(ybuf.at[slot], ys_hbm, r0_ref[b], nv_ref[b], sem_out.at[slot]):
            pl.when(cond)(cp.start)

    @pl.when(b == n_blocks - 1)
    def _():
        drain(jnp.maximum(b - 1, 0), 1 - slot)
        drain(b, slot)


def moe_experts(xs_slabs, block_e, block_row0, block_valid, block_wslot, block_next_e, second_e,
                w_gate, w_up, w_down, layer):
    n_blocks = block_e.shape[0]
    d, ff = w_gate.shape[2:]
    blk_rows = EXPERT_BLOCK * ROW_SLAB
    n_assign = xs_slabs.shape[0] // ROW_SLAB - EXPERT_BLOCK
    any_spec = pl.BlockSpec(memory_space=pl.ANY)
    grid_spec = pltpu.PrefetchScalarGridSpec(
        num_scalar_prefetch=6,
        grid=(n_blocks,),
        in_specs=[any_spec] * 4,
        out_specs=any_spec,
        scratch_shapes=[pltpu.VMEM((EXPERT_LOOKAHEAD + 1, blk_rows, LANES), F32), pltpu.VMEM((2, blk_rows, LANES), F32),
                        pltpu.VMEM((EXPERT_WEIGHT_SLOTS, d, ff), F32), pltpu.VMEM((EXPERT_WEIGHT_SLOTS, d, ff), F32),
                        pltpu.VMEM((EXPERT_WEIGHT_SLOTS, ff, d), F32),
                        pltpu.VMEM((d, ff), BF16), pltpu.VMEM((d, ff), BF16), pltpu.VMEM((ff, d), BF16),
                        pltpu.SemaphoreType.DMA((EXPERT_LOOKAHEAD + 1,)), pltpu.SemaphoreType.DMA((2,)),
                        pltpu.SemaphoreType.DMA((EXPERT_WEIGHT_SLOTS,))],
    )
    return pl.pallas_call(
        functools.partial(_expert_kernel, layer=layer),
        grid_spec=grid_spec,
        out_shape=jax.ShapeDtypeStruct((n_assign * ROW_SLAB, LANES), F32),
        compiler_params=_cparams(("arbitrary",)),
        name="moe_experts",
    )(block_e, block_row0, block_valid, block_wslot, block_next_e, second_e, xs_slabs, w_gate, w_up, w_down)


def _combine_kernel(cur_ref, nxt_ref, ys_hbm, xc_ref, xl_ref, hf_ref, gt_ref, g2_ref, sg_ref, su_ref, sd_ref,
                    lg_ref, lb_ref, *rest, n_ctx_tiles):
    *o_refs, ybuf, sem = rest
    i = pl.program_id(0)
    n_tiles = pl.num_programs(0)
    tm = xl_ref.shape[0]
    n_rows = tm * TOP_K
    slot = i % 2

    @pl.when(i == 0)
    def _():
        _row_gather(cur_ref, 0, n_rows, ys_hbm, ybuf.at[0], sem.at[0])

    @pl.when(i + 1 < n_tiles)
    def _():
        _row_gather(nxt_ref, 0, n_rows, ys_hbm, ybuf.at[1 - slot], sem.at[1 - slot])

    hf = _gathered_rows(hf_ref, 0, tm, 1).astype(BF16)
    y = _swiglu(hf, sg_ref[...], su_ref[...], sd_ref[...])

    _wait_rows(ybuf.at[slot], sem.at[slot])
    for k in range(TOP_K):
        y = y + gt_ref[:, k:k + 1] * _gathered_rows(ybuf.at[slot], k, tm, TOP_K)
    r = DEEPNORM_ALPHA * _stream_tile(xc_ref, xl_ref, n_ctx_tiles) + g2_ref[...] * y
    res = _layer_norm_rows(r, lg_ref[...], lb_ref[...])
    if n_ctx_tiles == 0:
        o_refs[0][...] = res
    else:
        oc_ref, ol_ref = o_refs

        @pl.when(i < n_ctx_tiles)
        def _():
            oc_ref[...] = res

        @pl.when(i >= n_ctx_tiles)
        def _():
            ol_ref[...] = res


def moe_combine(dest, ys_slabs, x_ctx, x_lat, hf_slabs, gate, g2, n_ctx_tiles, sh_gate, sh_up, sh_down, ln_g, ln_b):
    d = x_lat.shape[1]
    tm = MOE_TILE
    n_tiles = n_ctx_tiles + x_lat.shape[0] // tm
    n_rows = tm * TOP_K
    ff = sh_gate.shape[1]
    row = lambda i: (i, 0)
    fixed = lambda i: (0, 0)
    seg = lambda i: (jnp.where(i < n_ctx_tiles, 1, 0), 0, 0)
    ctx_spec, lat_spec = _stream_specs(n_ctx_tiles, tm, d)
    lat_out = jax.ShapeDtypeStruct(x_lat.shape, F32)
    if n_ctx_tiles == 0:
        out_specs, out_shape = [lat_spec], [lat_out]
    else:
        out_specs, out_shape = [ctx_spec, lat_spec], [jax.ShapeDtypeStruct(x_ctx.shape, F32), lat_out]
    kern = functools.partial(_combine_kernel, n_ctx_tiles=n_ctx_tiles)
    return pl.pallas_call(
        kern,
        grid=(n_tiles,),
        in_specs=[pl.BlockSpec((n_rows,), lambda i: (i,), memory_space=pltpu.SMEM),
                  pl.BlockSpec((n_rows,), lambda i: (jnp.minimum(i + 1, n_tiles - 1),), memory_space=pltpu.SMEM),
                  pl.BlockSpec(memory_space=pl.ANY),
                  ctx_spec, lat_spec, pl.BlockSpec((tm * ROW_SLAB, LANES), row), pl.BlockSpec((tm, LANES), row),
                  pl.BlockSpec((None, 1, d), seg),
                  pl.BlockSpec((d, ff), fixed), pl.BlockSpec((d, ff), fixed), pl.BlockSpec((ff, d), fixed),
                  pl.BlockSpec((1, d), fixed), pl.BlockSpec((1, d), fixed)],
        out_specs=out_specs,
        out_shape=out_shape,
        scratch_shapes=[pltpu.VMEM((2, n_rows * ROW_SLAB, LANES), F32), pltpu.SemaphoreType.DMA((2,))],
        compiler_params=_cparams(("arbitrary",)),
        name="moe_combine",
    )(dest, dest, ys_slabs, x_ctx, x_lat, hf_slabs, gate, g2,
      sh_gate.astype(BF16), sh_up.astype(BF16), sh_down.astype(BF16), ln_g.reshape(1, d), ln_b.reshape(1, d))


def moe_layer(x_ctx, x_lat, sc2, sh2, g2, n_ctx_tiles, layer, router_w, router_bias, w_gate, w_up, w_down,
              sh_gate, sh_up, sh_down, ln_g, ln_b):
    n = n_ctx_tiles * MOE_TILE + x_lat.shape[0]
    hf, te, gt, rk, cnt = moe_route(x_ctx, x_lat, sc2, sh2, n_ctx_tiles, router_w, router_bias)
    counts = cnt[0].astype(I32)
    start = jnp.cumsum(counts) - counts
    experts = jnp.arange(N_EXPERTS, dtype=I32)
    dest = moe_slots(te, rk, start)
    nb = (counts + EXPERT_BLOCK - 1) // EXPERT_BLOCK
    blk_end = jnp.cumsum(nb)
    blk_start = blk_end - nb
    n_blocks = n * TOP_K // EXPERT_BLOCK + N_EXPERTS
    blk = jnp.arange(n_blocks, dtype=I32)
    bb = jnp.minimum(blk, blk_end[-1] - 1)[:, None]
    own = (blk_start[None, :] <= bb) & (bb < blk_end[None, :])
    sel = lambda v: jnp.sum(jnp.where(own, v[None, :], 0), axis=1)
    j = bb[:, 0] - sel(blk_start)
    block_e = sel(experts)
    block_row0 = sel(start) + j * EXPERT_BLOCK
    block_valid = jnp.where(blk < blk_end[-1], jnp.clip(sel(counts) - j * EXPERT_BLOCK, 0, EXPERT_BLOCK), 0)
    used = nb > 0
    ordinal = jnp.cumsum(used.astype(I32)) - 1
    later = jnp.where(used[None, :] & (experts[None, :] > experts[:, None]), experts[None, :], N_EXPERTS)
    next1 = jnp.min(later, axis=1)
    next2 = jnp.min(jnp.where(later > next1[:, None], later, N_EXPERTS), axis=1)
    none = lambda e: jnp.where(e < N_EXPERTS, e, -1)
    first = (blk < blk_end[-1]) & (j == 0)
    block_wslot = jnp.where(first, sel(ordinal) % EXPERT_WEIGHT_SLOTS, -1)
    block_next_e = jnp.where(first, sel(none(next2)), -1)
    second_e = jnp.sum(jnp.where(experts == block_e[0], none(next1), 0)).reshape(1)
    xs = moe_dispatch(dest, hf)
    ys = moe_experts(xs, block_e, block_row0, block_valid, block_wslot, block_next_e, second_e,
                     w_gate, w_up, w_down, layer)
    return moe_combine(dest, ys, x_ctx, x_lat, hf, gt, g2, n_ctx_tiles, sh_gate, sh_up, sh_down, ln_g, ln_b)


def kernel(x, c, ctx, c_ctx, w_mod, b_mod, ln_mix_g, ln_mix_b, ln_ffn_g, ln_ffn_b, ab_w_in, ab_w_out, na_rpb,
           hy_conv_w, hy_conv_b, hy_f_w1, hy_f_b1, hy_f_freq, hy_f_w2, hy_f_b2, hy_f_w3, hy_skip, cd_w_in, cd_w_out,
           q_norm_g, k_norm_g, s5_a_re, s5_a_im, s5_log_dt, s5_b_re, s5_b_im, s5_c_re, s5_c_im, s5_d, s5_glu_w,
           s5_glu_b, router_w, router_bias, exp_w_gate, exp_w_up, exp_w_down, sh_w_gate, sh_w_up, sh_w_down):
    b, l, d = x.shape
    assert b == 1
    n_ctx = ctx.shape[1]
    assert n_ctx == MOE_TILE
    xs = x[0]
    cs = ctx[0]
    cmat = jnp.zeros((SUBLANES, d), F32).at[0].set(c[0]).at[1].set(c_ctx)
    mods = modulation_all(cmat, w_mod, b_mod).reshape(DEPTH, SUBLANES, 6, d)
    qscale = HEAD_DIM ** -0.5

    for i in range(DEPTH):
        need_ctx = i < DEPTH - 1
        m = mods[i]
        sh1, sc1, g1, sh2, sc2, g2 = [m[0:1, t] for t in range(6)]
        csh1, csc1, cg1, csh2, csc2, cg2 = [m[1:2, t] for t in range(6)]
        j = i // 2
        if i % 2 == 0:
            filt = (hy_conv_w[j], hy_conv_b[j], hy_f_w1[j], hy_f_b1[j], hy_f_freq[j], hy_f_w2[j], hy_f_b2[j],
                    hy_f_w3[j], hy_skip[j])
            splits = (NA_WIDTH, NA_WIDTH, NA_WIDTH, 3 * HY_WIDTH)
            dts = (BF16, BF16, BF16, F32)
            scl = (qscale, 1.0, 1.0, 1.0)
            q_l, k_l, v_l, u_l = mod_project(xs, sc1, sh1, ab_w_in[j], splits, dts, scl)
            q_c, k_c, v_c, u_c = mod_project(cs, csc1, csh1, ab_w_in[j], splits, dts, scl)
            a_lat = neighbourhood_attention(q_l, k_l, v_l, k_c, v_c, na_rpb[j])
            y_hy = hyena_long(u_l, *filt)
            xs_new = outproj_ln(a_lat, y_hy, ab_w_out[j], xs, g1, ln_mix_g[i], ln_mix_b[i])
            if need_ctx:
                a_ctx = context_attention(q_c, k_c, v_c)
                yc_hy = hyena_long(u_c, *filt)
                cs = outproj_ln(a_ctx, yc_hy, ab_w_out[j], cs, cg1, ln_mix_g[i], ln_mix_b[i])
            xs = xs_new
        else:
            splits = (GQA_WIDTH, GQA_KV_WIDTH, GQA_KV_WIDTH, S5_WIDTH)
            q_l, k_l, v_l, u_l = mod_project(xs, sc1, sh1, cd_w_in[j], splits, (F32, F32, BF16, F32))
            k_c, v_c, u_c = mod_project(cs, csc1, csh1, cd_w_in[j][:, GQA_WIDTH:], splits[1:], (F32, BF16, F32))
            tabs = _rope_tables(l)
            qn_t = qk_prep(q_l, q_norm_g[j], tabs, qscale * math.log2(math.e), transposed=True)
            kn = qk_prep(k_l, k_norm_g[j], tabs, 1.0)
            kcn = qk_prep(k_c, k_norm_g[j], None, 1.0)
            k_all = jnp.concatenate([kn, kcn], axis=0)
            v_all = jnp.concatenate([v_l, v_c], axis=0)
            k_hm = k_all.reshape(-1, GQA_KV_HEADS, HEAD_DIM).transpose(1, 0, 2)
            v_t = v_all.T.reshape(GQA_KV_HEADS, HEAD_DIM, -1)
            att = gqa_attention(qn_t, k_hm, v_t)
            ssm = s5_mix(u_c, u_l, s5_a_re[j], s5_a_im[j], s5_log_dt[j], s5_b_re[j], s5_b_im[j],
                         s5_c_re[j], s5_c_im[j], s5_d[j], s5_glu_w[j], s5_glu_b[j])
            xs = outproj_ln(att, ssm, cd_w_out[j], xs, g1, ln_mix_g[i], ln_mix_b[i])
            assert not need_ctx

        moe_w = (i, router_w[i], router_bias[i], exp_w_gate, exp_w_up, exp_w_down,
                 sh_w_gate[i], sh_w_up[i], sh_w_down[i], ln_ffn_g[i], ln_ffn_b[i])
        stack2 = lambda lat, cx: jnp.stack([lat, cx])
        mod2 = (stack2(sc2, csc2), stack2(sh2, csh2), stack2(g2, cg2))
        if need_ctx:
            cs, xs = moe_layer(cs, xs, *mod2, n_ctx // MOE_TILE, *moe_w)
        else:
            (xs,) = moe_layer(xs, xs, *mod2, 0, *moe_w)
    return xs.reshape(b, l, d)
```

```python
import functools
import math

import jax
import jax.numpy as jnp
import numpy as np
from jax import lax
from jax.experimental import pallas as pl
from jax.experimental.pallas import tpu as pltpu

F32 = jnp.float32
BF16 = jnp.bfloat16
I32 = jnp.int32
HIGHEST = lax.Precision.HIGHEST

LANES = 128
SUBLANES = 8
VMEM_LIMIT = 56 * 1024 * 1024

D_MODEL = 1024
DEPTH = 2
GRID_W = 64
HEAD_DIM = 64
NA_HEADS = 8
NA_WIDTH = NA_HEADS * HEAD_DIM
NA_KH = 8
NA_KW = 16
HY_WIDTH = D_MODEL - NA_WIDTH
HY_BANDS = 16
HY_DECAY_PCT_MIN = 0.3
HY_DECAY_PCT_MAX = 1.5
HY_DECAY_TARGET = 1e-2
GQA_HEADS = 8
GQA_KV_HEADS = 2
GQA_WIDTH = GQA_HEADS * HEAD_DIM
GQA_KV_WIDTH = GQA_KV_HEADS * HEAD_DIM
ROPE_THETA = 10000.0
S5_WIDTH = D_MODEL - GQA_WIDTH
S5_GROUP = 16
S5_GROUPS = S5_WIDTH // S5_GROUP
S5_STATE = 64
N_EXPERTS = 256
TOP_K = 8
N_EXPERT_GROUPS = 8
TOPK_GROUPS = 4
EXPERT_FF = 256
ROUTED_SCALE = 2.5
EXPERT_BLOCK = 256
DEEPNORM_ALPHA = (2.0 * DEPTH) ** 0.25
LN_EPS = 1e-5
RMS_EPS = 1e-6

NEG_BIG = -1e30
NA_TILE_ROWS = 8
NA_KEY_ROWS = 16
NA_KEY_BLOCK_ROWS = 4
DFT_N1 = 128
S5_CHUNK = 16
S5_LANE_GROUPS = LANES // S5_GROUP
MOE_TILE = 256
EXPERT_LOOKAHEAD = 3
EXPERT_WEIGHT_SLOTS = 3
ROW_SLAB = D_MODEL // LANES

NT_DIMS = (((1,), (1,)), ((), ()))


def _cparams(sem, **kw):
    return pltpu.CompilerParams(dimension_semantics=sem, vmem_limit_bytes=VMEM_LIMIT, **kw)


def _dot(a, b, **kw):
    return jnp.dot(a, b, preferred_element_type=F32, **kw)


def _dot_nt(a, b):
    return lax.dot_general(a, b, NT_DIMS, preferred_element_type=F32)


def _row_tile(m, pref):
    return pref if m % pref == 0 else m


def _mod_kernel(c_ref, w_ref, b_ref, o_ref):
    cv = c_ref[...]
    s = cv * jax.nn.sigmoid(cv)
    o_ref[...] = _dot(s, w_ref[...], precision=HIGHEST) + b_ref[...]


def modulation_all(cmat, w_mod, b_mod):
    depth, d, n = w_mod.shape
    tn = 1536
    return pl.pallas_call(
        _mod_kernel,
        grid=(depth, n // tn),
        in_specs=[pl.BlockSpec((SUBLANES, d), lambda l, j: (0, 0)),
                  pl.BlockSpec((None, d, tn), lambda l, j: (l, 0, j)),
                  pl.BlockSpec((None, 1, tn), lambda l, j: (l, 0, j))],
        out_specs=pl.BlockSpec((None, SUBLANES, tn), lambda l, j: (l, 0, j)),
        out_shape=jax.ShapeDtypeStruct((depth, SUBLANES, n), F32),
        compiler_params=_cparams(("arbitrary", "arbitrary")),
        name="modulation",
    )(cmat, w_mod, b_mod.reshape(depth, 1, n))


def _proj_kernel(x_ref, sc_ref, sh_ref, w_ref, *o_refs, splits, scales):
    h = (x_ref[...] * (1.0 + sc_ref[...]) + sh_ref[...]).astype(BF16)
    off = 0
    for o_ref, wd, sc in zip(o_refs, splits, scales):
        y = _dot(h, w_ref[:, off:off + wd])
        if sc != 1.0:
            y = y * sc
        o_ref[...] = y.astype(o_ref.dtype)
        off += wd


def mod_project(x, sc, sh, w, splits, dtypes, scales=None):
    m, d = x.shape
    n = w.shape[1]
    assert sum(splits) == n
    scales = scales or (1.0,) * len(splits)
    tm = _row_tile(m, 512)
    kern = functools.partial(_proj_kernel, splits=tuple(splits), scales=tuple(scales))
    return pl.pallas_call(
        kern,
        grid=(m // tm,),
        in_specs=[pl.BlockSpec((tm, d), lambda i: (i, 0)),
                  pl.BlockSpec((1, d), lambda i: (0, 0)),
                  pl.BlockSpec((1, d), lambda i: (0, 0)),
                  pl.BlockSpec((d, n), lambda i: (0, 0))],
        out_specs=[pl.BlockSpec((tm, wd), lambda i: (i, 0)) for wd in splits],
        out_shape=[jax.ShapeDtypeStruct((m, wd), dt) for wd, dt in zip(splits, dtypes)],
        compiler_params=_cparams(("parallel",)),
        name="mod_project",
    )(x, sc, sh, w.astype(BF16))


def _layer_norm_rows(r, g, b):
    mu = jnp.mean(r, axis=-1, keepdims=True)
    c = r - mu
    var = jnp.mean(c * c, axis=-1, keepdims=True)
    return c * lax.rsqrt(var + LN_EPS) * g + b


def _outproj_ln_kernel(a_ref, b_ref, w_ref, x_ref, gate_ref, g_ref, beta_ref, o_ref):
    ka = a_ref.shape[1]
    y = _dot(a_ref[...], w_ref[:ka, :]) + _dot(b_ref[...], w_ref[ka:, :])
    r = DEEPNORM_ALPHA * x_ref[...] + gate_ref[...] * y
    o_ref[...] = _layer_norm_rows(r, g_ref[...], beta_ref[...])


def outproj_ln(a, b, w, x, gate, g, beta):
    m, d = x.shape
    ka, kb = a.shape[1], b.shape[1]
    tm = _row_tile(m, 512)
    row = lambda i: (i, 0)
    fixed = lambda i: (0, 0)
    return pl.pallas_call(
        _outproj_ln_kernel,
        grid=(m // tm,),
        in_specs=[pl.BlockSpec((tm, ka), row), pl.BlockSpec((tm, kb), row),
                  pl.BlockSpec((ka + kb, d), fixed), pl.BlockSpec((tm, d), row),
                  pl.BlockSpec((1, d), fixed), pl.BlockSpec((1, d), fixed), pl.BlockSpec((1, d), fixed)],
        out_specs=pl.BlockSpec((tm, d), row),
        out_shape=jax.ShapeDtypeStruct((m, d), F32),
        compiler_params=_cparams(("parallel",)),
        name="outproj_ln",
    )(a, b, w.astype(BF16), x, gate, g.reshape(1, d), beta.reshape(1, d))


NA_BIAS_PAD = NA_TILE_ROWS


def _na_key_start(t, rows):
    lo, hi = NA_TILE_ROWS * t - NA_KH // 2, rows - NA_KEY_ROWS
    return min(max(lo, 0), hi) if isinstance(t, int) else jnp.clip(lo, 0, hi)


def _na_bias_tables(rpb, rows):
    h = rpb.shape[0]
    ri = np.arange(NA_TILE_ROWS)
    kr = np.arange(NA_KEY_ROWS)
    c = np.arange(GRID_W)
    cs = np.clip(c - NA_KW // 2, 0, GRID_W - NA_KW)
    vc = (c[None, :] >= cs[:, None]) & (c[None, :] < cs[:, None] + NA_KW)
    dc = np.clip(c[None, :] - c[:, None] + NA_KW - 1, 0, 2 * NA_KW - 2)
    pick = (dc.reshape(-1)[:, None] == np.arange(2 * NA_KW - 1)[None, :]).astype(np.float32)
    colb = jnp.einsum('qb,hab->haq', pick, rpb, precision=HIGHEST).reshape(h, 2 * NA_KH - 1, GRID_W, GRID_W)
    colb = jnp.where(vc[None, None], colb, NEG_BIG)
    pad = jnp.zeros((h, NA_BIAS_PAD, GRID_W, GRID_W), F32)
    colb = jnp.concatenate([pad, colb, pad], axis=1)
    colb2 = jnp.concatenate([colb[:, :-1], colb[:, 1:]], axis=-1)
    colb2 = colb2.reshape((h // 2, 2) + colb2.shape[1:])

    def case(t):
        r = NA_TILE_ROWS * t + ri
        rs = np.clip(r - NA_KH // 2, 0, rows - NA_KH)
        krow = _na_key_start(t, rows) + kr
        vr = (krow[None, :] >= rs[:, None]) & (krow[None, :] < rs[:, None] + NA_KH)
        m = np.where(vr, 0.0, NEG_BIG).astype(np.float32)
        return np.repeat(np.repeat(m, GRID_W, axis=0), GRID_W, axis=1)

    n_tiles = rows // NA_TILE_ROWS
    return colb2, jnp.asarray(np.stack([case(0), case(1), case(n_tiles - 1)]))


def _pair_masks(shape):
    lane = lax.broadcasted_iota(I32, shape, 1)
    return lane < HEAD_DIM


def _na_kernel(q_ref, k0, k1, k2, k3, v0, v1, v2, v3, kc_ref, vc_ref, colb_ref, mask_ref, o_ref, *, rows):
    q = q_ref[...]
    lo = _pair_masks(q.shape)
    ks = (k0, k1, k2, k3)
    vs = (v0, v1, v2, v3)
    kb = k0.shape[0]
    t = pl.program_id(1)
    off = _na_key_start(t, rows) - NA_TILE_ROWS * t + (NA_KH - 1) + NA_BIAS_PAD
    key_rows_per_block = kb // GRID_W

    def bias(hh, i):
        first = i * key_rows_per_block
        return jnp.concatenate(
            [jnp.concatenate([colb_ref[hh, first + 2 * m - ri + off] for m in range(key_rows_per_block // 2)], axis=1)
             for ri in range(NA_TILE_ROWS)], axis=0)

    outs = []
    for hh in range(2):
        qh = jnp.where(lo if hh == 0 else jnp.logical_not(lo), q, jnp.zeros_like(q))
        s = [_dot_nt(qh, ks[i][...]) + bias(hh, i) + mask_ref[:, i * kb:(i + 1) * kb] for i in range(4)]
        s.append(_dot_nt(qh, kc_ref[...]))
        m = s[0].max(axis=1, keepdims=True)
        for si in s[1:]:
            m = jnp.maximum(m, si.max(axis=1, keepdims=True))
        p = [jnp.exp(si - m) for si in s]
        l = p[0].sum(axis=1, keepdims=True)
        for pi in p[1:]:
            l = l + pi.sum(axis=1, keepdims=True)
        acc = _dot(p[4].astype(BF16), vc_ref[...])
        for i in range(4):
            acc = acc + _dot(p[i].astype(BF16), vs[i][...])
        outs.append(acc / l)
    o_ref[...] = jnp.where(lo, outs[0], outs[1]).astype(o_ref.dtype)


def neighbourhood_attention(q, k, v, k_ctx, v_ctx, rpb):
    l, w = q.shape
    rows = l // GRID_W
    n_tiles = rows // NA_TILE_ROWS
    assert n_tiles >= 3 and rows % NA_TILE_ROWS == 0
    n_ctx = k_ctx.shape[0]
    tq = NA_TILE_ROWS * GRID_W
    kb = NA_KEY_BLOCK_ROWS * GRID_W
    colb2, rowmask = _na_bias_tables(rpb, rows)
    pair_w = 2 * HEAD_DIM

    def kv_spec(i):
        def imap(p, t):
            return (_na_key_start(t, rows) // NA_KEY_BLOCK_ROWS + i, p)
        return pl.BlockSpec((kb, pair_w), imap)

    def mask_map(p, t):
        return (jnp.where(t == 0, 0, jnp.where(t == n_tiles - 1, 2, 1)), 0, 0)

    return pl.pallas_call(
        functools.partial(_na_kernel, rows=rows),
        grid=(w // pair_w, n_tiles),
        in_specs=[pl.BlockSpec((tq, pair_w), lambda p, t: (t, p))]
                 + [kv_spec(i) for i in range(4)] + [kv_spec(i) for i in range(4)]
                 + [pl.BlockSpec((n_ctx, pair_w), lambda p, t: (0, p)),
                    pl.BlockSpec((n_ctx, pair_w), lambda p, t: (0, p)),
                    pl.BlockSpec((None,) + colb2.shape[1:], lambda p, t: (p, 0, 0, 0, 0)),
                    pl.BlockSpec((None, tq, NA_KEY_ROWS * GRID_W), mask_map)],
        out_specs=pl.BlockSpec((tq, pair_w), lambda p, t: (t, p)),
        out_shape=jax.ShapeDtypeStruct((l, w), BF16),
        compiler_params=_cparams(("parallel", "parallel")),
        name="neighbourhood_attention",
    )(q, k, k, k, k, v, v, v, v, k_ctx, v_ctx, colb2, rowmask)


def _ctx_attn_kernel(q_ref, k_ref, v_ref, o_ref):
    q = q_ref[...]
    lo = _pair_masks(q.shape)
    outs = []
    for hh in range(2):
        qh = jnp.where(lo if hh == 0 else jnp.logical_not(lo), q, jnp.zeros_like(q))
        s = _dot_nt(qh, k_ref[...])
        p = jnp.exp(s - s.max(axis=1, keepdims=True))
        outs.append(_dot(p.astype(BF16), v_ref[...]) / p.sum(axis=1, keepdims=True))
    o_ref[...] = jnp.where(lo, outs[0], outs[1]).astype(o_ref.dtype)


def context_attention(q, k, v):
    n, w = q.shape
    pair_w = 2 * HEAD_DIM
    spec = pl.BlockSpec((n, pair_w), lambda p: (0, p))
    return pl.pallas_call(
        _ctx_attn_kernel, grid=(w // pair_w,), in_specs=[spec, spec, spec], out_specs=spec,
        out_shape=jax.ShapeDtypeStruct((n, w), BF16),
        compiler_params=_cparams(("parallel",)), name="context_attention",
    )(q, k, v)


def _shortconv_kernel(u_ref, up_ref, un_ref, w_ref, b_ref, x0_ref, z_ref, *, n_tiles):
    i = pl.program_id(0)
    u = u_ref[...]
    tm = u.shape[0]
    prev_row = jnp.where(i > 0, up_ref[SUBLANES - 1:SUBLANES, :], 0.0)
    next_row = jnp.where(i < n_tiles - 1, un_ref[0:1, :], 0.0)
    row = lax.broadcasted_iota(I32, u.shape, 0)
    u_dn = jnp.where(row == 0, prev_row, pltpu.roll(u, 1, 0))
    u_up = jnp.where(row == tm - 1, next_row, pltpu.roll(u, tm - 1, 0))
    y = u_dn * w_ref[0:1, :] + u * w_ref[1:2, :] + u_up * w_ref[2:3, :] + b_ref[...]
    c = HY_WIDTH
    x0_ref[...] = y[:, :c]
    z_ref[...] = y[:, c:2 * c] * y[:, 2 * c:]


def hyena_gate(u, conv_w, conv_b):
    l, w3 = u.shape
    tm = _row_tile(l, 512)
    n_tiles = l // tm
    per = tm // SUBLANES
    last = l // SUBLANES - 1
    kern = functools.partial(_shortconv_kernel, n_tiles=n_tiles)
    return pl.pallas_call(
        kern,
        grid=(n_tiles,),
        in_specs=[pl.BlockSpec((tm, w3), lambda i: (i, 0)),
                  pl.BlockSpec((SUBLANES, w3), lambda i: (jnp.maximum(i * per - 1, 0), 0)),
                  pl.BlockSpec((SUBLANES, w3), lambda i: (jnp.minimum((i + 1) * per, last), 0)),
                  pl.BlockSpec((3, w3), lambda i: (0, 0)),
                  pl.BlockSpec((1, w3), lambda i: (0, 0))],
        out_specs=[pl.BlockSpec((tm, HY_WIDTH), lambda i: (i, 0))] * 2,
        out_shape=[jax.ShapeDtypeStruct((l, HY_WIDTH), F32)] * 2,
        compiler_params=_cparams(("parallel",)),
        name="hyena_gate",
    )(u, u, u, conv_w, conv_b.reshape(1, w3))


def _filter_kernel(bands_ref, w1t_ref, w1c_ref, w1s_ref, b1_ref, fr_ref, w2_ref, b2_ref, w3_ref, dl_ref,
                   taps_ref, asum_ref, *, l, tp):
    i = pl.program_id(0)
    hid_w = w2_ref.shape[0]
    c = HY_WIDTH
    denom = float(max(l - 1, 1))

    def pos(width):
        return (lax.broadcasted_iota(I32, (tp, width), 0) + i * tp).astype(F32)

    ang = (2.0 * math.pi / l) * pos(HY_BANDS) * bands_ref[...]
    pre = ((pos(hid_w) / denom) * w1t_ref[...]
           + _dot(jnp.cos(ang), w1c_ref[...], precision=HIGHEST)
           + _dot(-jnp.sin(ang), w1s_ref[...], precision=HIGHEST) + b1_ref[...])
    hid = jnp.sin(fr_ref[...] * pre)
    hid = jnp.sin(fr_ref[...] * (_dot(hid, w2_ref[...], precision=HIGHEST) + b2_ref[...]))
    taps = _dot(hid, w3_ref[...], precision=HIGHEST)
    pc = pos(c)
    window = jnp.exp(-(pc / denom) * dl_ref[...])
    fwd = taps[:, :c] * window
    bwd = jnp.where(pc == 0.0, 0.0, taps[:, c:] * window)
    taps_ref[:, :c] = fwd
    taps_ref[:, c:] = bwd

    @pl.when(i == 0)
    def _():
        asum_ref[...] = jnp.zeros_like(asum_ref)

    asum_ref[...] += jnp.sum(jnp.abs(fwd) + jnp.abs(bwd), axis=0, keepdims=True)


def hyena_filter_taps(l, f_w1, f_b1, f_freq, f_w2, f_b2, f_w3):
    c = HY_WIDTH
    hid = f_w2.shape[0]
    tp = _row_tile(l, 1024)
    bands = jnp.linspace(1e-4, HY_BANDS - 1, HY_BANDS, dtype=F32).reshape(1, HY_BANDS)
    deltas = jnp.abs(jnp.linspace(math.log(HY_DECAY_TARGET) / HY_DECAY_PCT_MAX,
                                  math.log(HY_DECAY_TARGET) / HY_DECAY_PCT_MIN, c, dtype=F32)).reshape(1, c)
    fixed = lambda i: (0, 0)
    full = lambda a: pl.BlockSpec(a.shape, fixed)
    args = (bands, f_w1[0:1], f_w1[1:1 + HY_BANDS], f_w1[1 + HY_BANDS:], f_b1.reshape(1, hid),
            f_freq.reshape(1, hid), f_w2, f_b2.reshape(1, hid), f_w3, deltas)
    kern = functools.partial(_filter_kernel, l=l, tp=tp)
    return pl.pallas_call(
        kern,
        grid=(l // tp,),
        in_specs=[full(a) for a in args],
        out_specs=[pl.BlockSpec((tp, 2 * c), lambda i: (i, 0)), pl.BlockSpec((1, c), fixed)],
        out_shape=[jax.ShapeDtypeStruct((l, 2 * c), F32), jax.ShapeDtypeStruct((1, c), F32)],
        compiler_params=_cparams(("arbitrary",)),
        name="hyena_filter",
    )(*args)


def _dft_tables(l):
    n = 2 * l
    n1 = DFT_N1
    n2 = n // n1
    k1 = jnp.arange(n1)[:, None]
    m1 = jnp.arange(n1 // 2)[None, :]
    ph1 = (2.0 * math.pi / n1) * ((k1 * m1) % n1).astype(F32)
    d1 = jnp.stack([jnp.cos(ph1), -jnp.sin(ph1)], axis=1).reshape(2 * n1, n1 // 2)
    d1_inv = d1.T
    j2 = jnp.arange(n2)
    ph2 = (2.0 * math.pi / n2) * ((j2[:, None] * j2[None, :]) % n2).astype(F32)
    cs, sn = jnp.cos(ph2), jnp.sin(ph2)
    f2 = jnp.concatenate([jnp.concatenate([cs, sn], axis=1), jnp.concatenate([-sn, cs], axis=1)], axis=0)
    pht = (2.0 * math.pi / n) * ((jnp.arange(n1)[:, None] * j2[None, :]) % n).astype(F32)[:, :, None]
    return d1.astype(BF16), d1_inv.astype(BF16), f2.astype(BF16), f2.T.astype(BF16), jnp.cos(pht), -jnp.sin(pht)


def _dft1_kernel(d_ref, x_ref, o_ref):
    n1h, m, c = x_ref.shape
    y = _dot(d_ref[...], x_ref[...].reshape(n1h * m, c).astype(BF16))
    o_ref[...] = y.reshape(o_ref.shape).astype(o_ref.dtype)


def dft_stage1(x, d1, n2):
    l, c = x.shape
    n1h = d1.shape[1]
    m = SUBLANES
    dk = jnp.kron(d1.astype(F32), jnp.eye(m, dtype=F32)).astype(BF16)
    out = pl.pallas_call(
        _dft1_kernel,
        grid=(n2 // m,),
        in_specs=[pl.BlockSpec(dk.shape, lambda j: (0, 0)), pl.BlockSpec((n1h, m, c), lambda j: (0, j, 0))],
        out_specs=pl.BlockSpec((d1.shape[0], m, c), lambda j: (0, j, 0)),
        out_shape=jax.ShapeDtypeStruct((d1.shape[0], n2, c), F32),
        compiler_params=_cparams(("parallel",)),
        name="dft_stage1",
    )(dk, x.reshape(n1h, n2, c))
    return out.reshape(d1.shape[0] // 2, 2, n2, c)


def _twiddled_stage2(f_ref, a_ref, twr_ref, twi_ref):
    shape = a_ref.shape[1:]
    twr = jnp.broadcast_to(twr_ref[...], shape)
    twi = jnp.broadcast_to(twi_ref[...], shape)
    ar, ai = a_ref[0].astype(F32), a_ref[1].astype(F32)
    a = jnp.concatenate([ar * twr - ai * twi, ar * twi + ai * twr], axis=0).astype(BF16)
    return _dot(f_ref[...], a), twr, twi


def _filter_spectrum_kernel(f_ref, a_ref, twr_ref, twi_ref, h_ref):
    n2 = a_ref.shape[1]
    c = h_ref.shape[2]
    x, _, _ = _twiddled_stage2(f_ref, a_ref, twr_ref, twi_ref)
    h_ref[0] = x[:n2, :c] + x[:n2, c:]
    h_ref[1] = x[n2:, :c] - x[n2:, c:]


def filter_spectrum(a_taps, f2, twr, twi):
    n1, _, n2, c2 = a_taps.shape
    c = c2 // 2
    tw = pl.BlockSpec((None, n2, 1), lambda i: (i, 0, 0))
    return pl.pallas_call(
        _filter_spectrum_kernel,
        grid=(n1,),
        in_specs=[pl.BlockSpec((2 * n2, 2 * n2), lambda i: (0, 0)),
                  pl.BlockSpec((None, 2, n2, c2), lambda i: (i, 0, 0, 0)), tw, tw],
        out_specs=pl.BlockSpec((None, 2, n2, c), lambda i: (i, 0, 0, 0)),
        out_shape=jax.ShapeDtypeStruct((n1, 2, n2, c), F32),
        compiler_params=_cparams(("parallel",)),
        name="filter_spectrum",
    )(f2, a_taps, twr, twi)


def _spectral_mix_kernel(f_ref, fi_ref, a_ref, h_ref, twr_ref, twi_ref, o_ref):
    n2 = a_ref.shape[1]
    x, twr, twi = _twiddled_stage2(f_ref, a_ref, twr_ref, twi_ref)
    xr, xi = x[:n2], x[n2:]
    hr, hi = h_ref[0], h_ref[1]
    y = jnp.concatenate([xr * hr - xi * hi, xr * hi + xi * hr], axis=0).astype(BF16)
    b = _dot(fi_ref[...], y)
    br, bi = b[:n2], b[n2:]
    o_ref[0] = (br * twr + bi * twi).astype(o_ref.dtype)
    o_ref[1] = (bi * twr - br * twi).astype(o_ref.dtype)


def spectral_mix(a_z, h, f2, f2_inv, twr, twi):
    n1, _, n2, c = a_z.shape
    blk = pl.BlockSpec((None, 2, n2, c), lambda i: (i, 0, 0, 0))
    mat = pl.BlockSpec((2 * n2, 2 * n2), lambda i: (0, 0))
    tw = pl.BlockSpec((None, n2, 1), lambda i: (i, 0, 0))
    return pl.pallas_call(
        _spectral_mix_kernel,
        grid=(n1,),
        in_specs=[mat, mat, blk, blk, tw, tw],
        out_specs=blk,
        out_shape=jax.ShapeDtypeStruct((n1, 2, n2, c), BF16),
        compiler_params=_cparams(("parallel",)),
        name="spectral_mix",
    )(f2, f2_inv, a_z, h, twr, twi)


def _hyena_out_kernel(di_ref, b_ref, x0_ref, z_ref, inv_ref, skip_ref, o_ref, *, inv_n):
    rows, m, c = b_ref.shape
    conv = (_dot(di_ref[...], b_ref[...].reshape(rows * m, c)) * inv_n).reshape(o_ref.shape)
    o_ref[...] = (x0_ref[...] * (conv * inv_ref[...] + z_ref[...] * skip_ref[...])).astype(o_ref.dtype)


def hyena_output(b, d1_inv, x0, z, inv_norm, skip):
    n1, _, n2, c = b.shape
    l = x0.shape[0]
    n1h = n1 // 2
    m = 2 * SUBLANES
    dk = jnp.kron(d1_inv.astype(F32), jnp.eye(m, dtype=F32)).astype(BF16)
    kern = functools.partial(_hyena_out_kernel, inv_n=1.0 / (2 * l))
    tile = pl.BlockSpec((n1h, m, c), lambda j: (0, j, 0))
    vec = pl.BlockSpec((1, 1, c), lambda j: (0, 0, 0))
    out = pl.pallas_call(
        kern,
        grid=(n2 // m,),
        in_specs=[pl.BlockSpec(dk.shape, lambda j: (0, 0)),
                  pl.BlockSpec((2 * n1, m, c), lambda j: (0, j, 0)),
                  tile, tile, vec, vec],
        out_specs=tile,
        out_shape=jax.ShapeDtypeStruct((n1h, n2, c), BF16),
        compiler_params=_cparams(("parallel",)),
        name="hyena_output",
    )(dk, b.reshape(2 * n1, n2, c), x0.reshape(n1h, n2, c), z.reshape(n1h, n2, c),
      inv_norm.reshape(1, 1, c), skip.reshape(1, 1, c))
    return out.reshape(l, c)


def _small_conv_kernel(d_ref, di_ref, z_ref, taps_ref, x0_ref, inv_ref, skip_ref, o_ref, *, inv_n):
    c = z_ref.shape[1]
    n = d_ref.shape[0] // 2
    zs = _dot(d_ref[...], z_ref[...], precision=HIGHEST)
    ts = _dot(d_ref[...], taps_ref[...], precision=HIGHEST)
    hr = ts[:n, :c] + ts[:n, c:]
    hi = ts[n:, :c] - ts[n:, c:]
    zr, zi = zs[:n], zs[n:]
    y = jnp.concatenate([zr * hr - zi * hi, zr * hi + zi * hr], axis=0)
    conv = _dot(di_ref[...], y, precision=HIGHEST) * inv_n
    o_ref[...] = (x0_ref[...] * (conv * inv_ref[...] + z_ref[...] * skip_ref[...])).astype(o_ref.dtype)


def hyena_output_short(z, taps, x0, inv_norm, skip):
    l, c = z.shape
    n = 2 * l
    ph = (2.0 * math.pi / n) * ((jnp.arange(n)[:, None] * jnp.arange(l)[None, :]) % n).astype(F32)
    d = jnp.concatenate([jnp.cos(ph), -jnp.sin(ph)], axis=0)
    di = jnp.concatenate([jnp.cos(ph), -jnp.sin(ph)], axis=0).T
    args = (d, di, z, taps, x0, inv_norm, skip.reshape(1, c))
    kern = functools.partial(_small_conv_kernel, inv_n=1.0 / n)
    return pl.pallas_call(
        kern,
        grid=(1,),
        in_specs=[pl.BlockSpec(a.shape, lambda i: (0, 0)) for a in args],
        out_specs=pl.BlockSpec((l, c), lambda i: (0, 0)),
        out_shape=jax.ShapeDtypeStruct((l, c), BF16),
        compiler_params=_cparams(("arbitrary",)),
        name="hyena_output_short",
    )(*args)


def hyena_long(u, conv_w, conv_b, f_w1, f_b1, f_freq, f_w2, f_b2, f_w3, skip):
    l = u.shape[0]
    x0, z = hyena_gate(u, conv_w, conv_b)
    taps, asum = hyena_filter_taps(l, f_w1, f_b1, f_freq, f_w2, f_b2, f_w3)
    inv_norm = 1.0 / asum
    if 2 * l < DFT_N1 * SUBLANES * 2:
        return hyena_output_short(z, taps, x0, inv_norm, skip)
    n2 = 2 * l // DFT_N1
    d1, d1_inv, f2, f2_inv, twr, twi = _dft_tables(l)
    h = filter_spectrum(dft_stage1(taps, d1, n2), f2, twr, twi)
    b = spectral_mix(dft_stage1(z, d1, n2), h, f2, f2_inv, twr, twi)
    return hyena_output(b, d1_inv, x0, z, inv_norm, skip)


def _head_sumsq(x, bd):
    sq = x * x
    hi = sq.astype(BF16)
    lo = (sq - hi.astype(F32)).astype(BF16)
    return _dot(hi, bd) + _dot(lo, bd)


def _qk_prep_kernel(x_ref, gain_ref, bd_ref, *rest, rope, scale, transposed):
    x = x_ref[...]
    w = x.shape[1]
    ms = _head_sumsq(x, bd_ref[...]) * (1.0 / HEAD_DIM)
    xn = x * lax.rsqrt(ms + RMS_EPS) * gain_ref[...]
    if rope:
        cos_ref, sin_ref, o_ref = rest
        reps = w // cos_ref.shape[1]
        cos = jnp.tile(cos_ref[...], (1, reps)) if reps > 1 else cos_ref[...]
        sin = jnp.tile(sin_ref[...], (1, reps)) if reps > 1 else sin_ref[...]
        lane = lax.broadcasted_iota(I32, x.shape, 1)
        partner = jnp.where(lane % 2 == 0, pltpu.roll(xn, w - 1, 1), pltpu.roll(xn, 1, 1))
        xn = xn * cos + partner * sin
    else:
        (o_ref,) = rest
    if scale != 1.0:
        xn = xn * scale
    if transposed:
        xn = xn.T
    o_ref[...] = xn.astype(o_ref.dtype)


def _rope_tables(l):
    half = HEAD_DIM // 2
    inv_freq = ROPE_THETA ** (-jnp.arange(0, half, 2, dtype=F32) / half)
    t = jnp.arange(l)
    row = (t // GRID_W).astype(F32)
    col = (t % GRID_W).astype(F32)
    ang = jnp.concatenate([jnp.repeat(row[:, None] * inv_freq[None], 2, axis=1),
                           jnp.repeat(col[:, None] * inv_freq[None], 2, axis=1)], axis=1)
    sign = jnp.where(jnp.arange(HEAD_DIM) % 2 == 0, -1.0, 1.0).astype(F32)
    cos = jnp.tile(jnp.cos(ang), (1, 2))
    sin = jnp.tile(jnp.sin(ang) * sign[None], (1, 2))
    return cos, sin


def qk_prep(x, gain, rope_tabs, scale, transposed=False):
    l, w = x.shape
    tm = _row_tile(l, 512)
    head = jnp.arange(w) // HEAD_DIM
    bd = (head[:, None] == head[None, :]).astype(BF16)
    gain_t = jnp.tile(gain.reshape(1, HEAD_DIM), (1, w // HEAD_DIM))
    row = lambda i: (i, 0)
    fixed = lambda i: (0, 0)
    in_specs = [pl.BlockSpec((tm, w), row), pl.BlockSpec((1, w), fixed), pl.BlockSpec((w, w), fixed)]
    args = [x, gain_t, bd]
    if rope_tabs is not None:
        in_specs += [pl.BlockSpec((tm, 2 * HEAD_DIM), row)] * 2
        args += list(rope_tabs)
    kern = functools.partial(_qk_prep_kernel, rope=rope_tabs is not None, scale=scale, transposed=transposed)
    if transposed:
        out_spec, out_shape = pl.BlockSpec((w, tm), lambda i: (0, i)), (w, l)
    else:
        out_spec, out_shape = pl.BlockSpec((tm, w), row), (l, w)
    return pl.pallas_call(
        kern, grid=(l // tm,), in_specs=in_specs, out_specs=out_spec,
        out_shape=jax.ShapeDtypeStruct(out_shape, BF16),
        compiler_params=_cparams(("parallel",)), name="qk_prep",
    )(*args)


def _flash_kernel(qt_ref, k_ref, vt_ref, o_ref, qs_ref, s_ref, m_ref, l_ref, acc_ref, *, tk, nk):
    dh = HEAD_DIM
    rep = qt_ref.shape[0] // dh
    tq = qt_ref.shape[1]
    for r in range(rep):
        qs_ref[:, r * tq:(r + 1) * tq] = qt_ref[r * dh:(r + 1) * dh, :]
    m_ref[...] = jnp.full(m_ref.shape, NEG_BIG, F32)
    l_ref[...] = jnp.zeros(l_ref.shape, F32)
    acc_ref[...] = jnp.zeros(acc_ref.shape, F32)

    def scores(j):
        start = pl.multiple_of(jnp.minimum(j, nk - 1) * tk, tk)
        return _dot(k_ref[pl.ds(start, tk), :], qs_ref[...])

    def absorb(j, s):
        start = pl.multiple_of(j * tk, tk)
        m_old = m_ref[...]
        m_new = jnp.maximum(m_old, s.max(axis=0, keepdims=True))
        alpha = jnp.exp2(m_old - m_new)
        p = jnp.exp2(s - m_new)
        l_ref[...] = alpha * l_ref[...] + p.sum(axis=0, keepdims=True)
        acc_ref[...] = alpha * acc_ref[...] + _dot(vt_ref[:, pl.ds(start, tk)], p.astype(BF16))
        m_ref[...] = m_new

    s_ref[0] = scores(0)

    def body(i, carry):
        j = 2 * i
        s_ref[1] = scores(j + 1)
        absorb(j, s_ref[0])
        s_ref[0] = scores(j + 2)
        absorb(j + 1, s_ref[1])
        return carry

    lax.fori_loop(0, nk // 2, body, 0)
    if nk % 2:
        absorb(nk - 1, s_ref[0])
    out = acc_ref[...] / l_ref[...]
    for r in range(rep):
        o_ref[:, r * dh:(r + 1) * dh] = out[:, r * tq:(r + 1) * tq].T.astype(o_ref.dtype)


def _kv_chunk(lk):
    for tiles in (5, 4, 3, 2, 1):
        if lk % (tiles * 256) == 0:
            return tiles * 256
    raise ValueError(lk)


def gqa_attention(q_t, k_hm, v_t):
    wq, l = q_t.shape
    hkv, lk, dh = k_hm.shape
    wg = wq // hkv
    rep = wg // dh
    tq = _row_tile(l, 256)
    tk = _kv_chunk(lk)
    kern = functools.partial(_flash_kernel, tk=tk, nk=lk // tk)
    return pl.pallas_call(
        kern,
        grid=(hkv, l // tq),
        in_specs=[pl.BlockSpec((wg, tq), lambda g, i: (g, i)),
                  pl.BlockSpec((None, lk, dh), lambda g, i: (g, 0, 0)),
                  pl.BlockSpec((None, dh, lk), lambda g, i: (g, 0, 0))],
        out_specs=pl.BlockSpec((tq, wg), lambda g, i: (i, g)),
        out_shape=jax.ShapeDtypeStruct((l, wq), BF16),
        scratch_shapes=[pltpu.VMEM((dh, rep * tq), BF16), pltpu.VMEM((2, tk, rep * tq), F32),
                        pltpu.VMEM((1, rep * tq), F32), pltpu.VMEM((1, rep * tq), F32),
                        pltpu.VMEM((dh, rep * tq), F32)],
        compiler_params=_cparams(("parallel", "parallel")),
        name="gqa_attention",
    )(q_t, k_hm, v_t)


def _cmul(ar, ai, br, bi):
    return ar * br - ai * bi, ar * bi + ai * br


def _s5_operators(a_re, a_im, log_dt, b_re, b_im, c_re, c_im):
    t = S5_CHUNK
    gs = S5_GROUP
    hp = dict(precision=HIGHEST)
    dt = jnp.exp(log_dt)[..., None]
    zr, zi = a_re * dt, a_im * dt
    er = jnp.exp(zr)
    abr, abi = er * jnp.cos(zi), er * jnp.sin(zi)
    den = a_re * a_re + a_im * a_im
    fr = ((abr - 1.0) * a_re + abi * a_im) / den
    fi = (abi * a_re - (abr - 1.0) * a_im) / den
    bbr, bbi = _cmul(fr[..., None], fi[..., None], b_re, b_im)
    tau = jnp.arange(t + 1, dtype=F32)
    pr = jnp.exp(zr[..., None] * tau) * jnp.cos(zi[..., None] * tau)
    pi = jnp.exp(zr[..., None] * tau) * jnp.sin(zi[..., None] * tau)
    car, cai = _cmul(c_re[..., None], c_im[..., None], pr[:, :, None, :, :t], pi[:, :, None, :, :t])
    ktap = (jnp.einsum('dgqpt,dgpk->dgtqk', car, bbr, **hp) - jnp.einsum('dgqpt,dgpk->dgtqk', cai, bbi, **hp))
    ktp = jnp.concatenate([jnp.zeros_like(ktap), ktap], axis=2)
    win = jnp.stack([ktp[:, :, t - i:2 * t - i] for i in range(t)], axis=2)
    m_tot = (win[0] + win[1].transpose(0, 2, 1, 3, 4)).transpose(0, 1, 4, 2, 3)
    pw_r = jnp.stack([pr[0, ..., t - 1::-1], pr[1, ..., :t]])
    pw_i = jnp.stack([pi[0, ..., t - 1::-1], pi[1, ..., :t]])
    wr, wi = _cmul(pw_r[..., None], pw_i[..., None], bbr[:, :, :, None, :], bbi[:, :, :, None, :])
    w_in = jnp.stack([wr[0], wi[0], wr[1], wi[1]]).transpose(1, 3, 4, 0, 2)
    pv_r = jnp.stack([pr[0, ..., 1:], pr[1, ..., t:0:-1]])
    pv_i = jnp.stack([pi[0, ..., 1:], pi[1, ..., t:0:-1]])
    vr, vi = _cmul(c_re[..., None], c_im[..., None], pv_r[:, :, None], pv_i[:, :, None])
    v_out = jnp.stack([vr[0], -vi[0], vr[1], -vi[1]]).transpose(1, 0, 3, 4, 2)

    nb = S5_GROUPS // S5_LANE_GROUPS
    lg = S5_LANE_GROUPS

    def spread(width, n_outer):
        c = jnp.arange(n_outer * width)
        lane = (c // width)[None, :] * (lg * width) + jnp.arange(lg)[:, None] * width + (c % width)[None, :]
        return (lane[:, :, None] == jnp.arange(n_outer * lg * width)[None, None, :]).astype(F32)

    def expand(x, pattern, width, n_outer):
        x = x.reshape((nb, lg) + x.shape[1:-2] + (n_outer * width,)).astype(BF16).astype(F32)
        return jnp.einsum(pattern, x, spread(width, n_outer)).astype(BF16)

    m_op = expand(m_tot, 'bgikc,gcl->bigkl', gs, t).reshape(nb, t * LANES, t * LANES)
    w_op = expand(w_in, 'bgikc,gcl->bigkl', S5_STATE, 4).reshape(nb, t * LANES, 4 * lg * S5_STATE)
    v_op = expand(v_out, 'bgrpc,gcl->brgpl', gs, t).reshape(nb, 4 * lg * S5_STATE, t * LANES)
    return m_op, w_op, v_op, pr[..., t], pi[..., t]


def _s5_layout_kernel(*refs):
    *u_refs, o_ref = refs
    rows = o_ref.shape[0]
    t = S5_CHUNK
    for b, u_ref in enumerate(u_refs):
        for i in range(t):
            o_ref[:, (b * t + i) * LANES:(b * t + i + 1) * LANES] = u_ref[pl.ds(i, rows, stride=t), :].astype(o_ref.dtype)


def _s5_state_in_kernel(u_ref, w_ref, *e_refs):
    e = _dot(u_ref[...], w_ref[...])
    q = e.shape[1] // len(e_refs)
    for part, e_ref in enumerate(e_refs):
        e_ref[...] = e[:, part * q:(part + 1) * q]


def _s5_scan_kernel(ar_ref, ai_ref, er_ref, ei_ref, sr_ref, si_ref, cr_ref, ci_ref, *, reverse):
    @pl.when(pl.program_id(0) == 0)
    def _():
        cr_ref[...] = jnp.zeros_like(cr_ref)
        ci_ref[...] = jnp.zeros_like(ci_ref)

    ar, ai = ar_ref[...], ai_ref[...]
    n = er_ref.shape[0]

    def body(k, carry):
        c = n - 1 - k if reverse else k
        sr, si = carry
        sr_ref[c] = sr
        si_ref[c] = si
        return ar * sr - ai * si + er_ref[c], ar * si + ai * sr + ei_ref[c]

    sr, si = lax.fori_loop(0, n, body, (cr_ref[...], ci_ref[...]))
    cr_ref[...] = sr
    ci_ref[...] = si


def _s5_out_kernel(u_ref, fr_ref, fi_ref, br_ref, bi_ref, m_ref, v_ref, y_ref):
    s = jnp.concatenate([fr_ref[...], fi_ref[...], br_ref[...], bi_ref[...]], axis=1).astype(BF16)
    y_ref[...] = _dot(u_ref[...], m_ref[...]) + _dot(s, v_ref[...])


def _s5_readout_kernel(y_ref, u_ref, d_ref, w_ref, b_ref, o_ref, ynat_ref):
    rows = y_ref.shape[0]
    t = S5_CHUNK
    nb = ynat_ref.shape[0]
    for b in range(nb):
        for i in range(t):
            ynat_ref[b, pl.ds(i, rows, stride=t), :] = y_ref[:, (b * t + i) * LANES:(b * t + i + 1) * LANES]
    y = jnp.concatenate([ynat_ref[b] for b in range(nb)], axis=1) + d_ref[...] * u_ref[...]
    y = 0.5 * y * (1.0 + jnp.tanh(math.sqrt(2.0 / math.pi) * (y + 0.044715 * (y * y * y))))
    gate = jax.nn.sigmoid(_dot(y.astype(BF16), w_ref[...]) + b_ref[...])
    o_ref[...] = (y * gate).astype(o_ref.dtype)


def s5_mix(u_ctx, u_lat, a_re, a_im, log_dt, b_re, b_im, c_re, c_im, d_skip, glu_w, glu_b):
    t = S5_CHUNK
    n_ctx, w = u_ctx.shape
    l = u_lat.shape[0]
    n_tok = n_ctx + l
    nch = n_tok // t
    tc = n_ctx // t
    n_tiles = nch // tc
    assert n_ctx == tc * t and tc % (2 * SUBLANES) == 0 and l % (tc * t) == 0
    nb = w // LANES
    cw = t * LANES
    sw = S5_LANE_GROUPS * S5_STATE
    n_state = S5_GROUPS * S5_STATE
    m_op, w_op, v_op, atr, ati = _s5_operators(a_re, a_im, log_dt, b_re, b_im, c_re, c_im)

    u_all = jnp.concatenate([u_ctx, u_lat], axis=0)
    u_ch = pl.pallas_call(
        _s5_layout_kernel,
        grid=(n_tiles,),
        in_specs=[pl.BlockSpec((tc * t, LANES), functools.partial(lambda b, i: (i, b), b)) for b in range(nb)],
        out_specs=pl.BlockSpec((tc, nb * cw), lambda i: (i, 0)),
        out_shape=jax.ShapeDtypeStruct((nch, nb * cw), BF16),
        compiler_params=_cparams(("parallel",)),
        name="s5_layout",
    )(*([u_all] * nb))

    u_blk = pl.BlockSpec((nch, cw), lambda b: (0, b))
    s_blk = pl.BlockSpec((nch, sw), lambda b: (0, b))
    states_in = pl.pallas_call(
        _s5_state_in_kernel,
        grid=(nb,),
        in_specs=[u_blk, pl.BlockSpec((None, cw, 4 * sw), lambda b: (b, 0, 0))],
        out_specs=[s_blk] * 4,
        out_shape=[jax.ShapeDtypeStruct((nch, n_state), F32)] * 4,
        compiler_params=_cparams(("parallel",)),
        name="s5_state_in",
    )(u_ch, w_op)

    slab = n_state // SUBLANES
    vec = pl.BlockSpec((SUBLANES, slab), lambda s: (0, 0))
    orders = (lambda s: (s, 0, 0),
              lambda s: (jnp.where(s == 0, 0, n_tiles - s), 0, 0))
    states = []
    for d in range(2):
        blk = pl.BlockSpec((tc, SUBLANES, slab), orders[d])
        s_re, s_im = pl.pallas_call(
            functools.partial(_s5_scan_kernel, reverse=bool(d)),
            grid=(n_tiles,),
            in_specs=[vec, vec, blk, blk],
            out_specs=[blk, blk],
            out_shape=[jax.ShapeDtypeStruct((nch, SUBLANES, slab), F32)] * 2,
            scratch_shapes=[pltpu.VMEM((SUBLANES, slab), F32)] * 2,
            compiler_params=_cparams(("arbitrary",)),
            name="s5_scan",
        )(atr[d].reshape(SUBLANES, slab), ati[d].reshape(SUBLANES, slab),
          states_in[2 * d].reshape(nch, SUBLANES, slab), states_in[2 * d + 1].reshape(nch, SUBLANES, slab))
        states += [s_re.reshape(nch, n_state), s_im.reshape(nch, n_state)]

    oc = cw // 4
    y_ch = pl.pallas_call(
        _s5_out_kernel,
        grid=(nb, cw // oc),
        in_specs=[pl.BlockSpec((nch, cw), lambda b, j: (0, b))] + [pl.BlockSpec((nch, sw), lambda b, j: (0, b))] * 4
                 + [pl.BlockSpec((None, cw, oc), lambda b, j: (b, 0, j)),
                    pl.BlockSpec((None, 4 * sw, oc), lambda b, j: (b, 0, j))],
        out_specs=pl.BlockSpec((nch, oc), lambda b, j: (0, b * (cw // oc) + j)),
        out_shape=jax.ShapeDtypeStruct((nch, nb * cw), F32),
        compiler_params=_cparams(("parallel", "parallel")),
        name="s5_out",
    )(u_ch, *states, m_op, v_op)

    row = lambda i: (i, 0)
    fixed = lambda i: (0, 0)
    return pl.pallas_call(
        _s5_readout_kernel,
        grid=(l // (tc * t),),
        in_specs=[pl.BlockSpec((tc, nb * cw), lambda i: (i + n_ctx // (tc * t), 0)), pl.BlockSpec((tc * t, w), row),
                  pl.BlockSpec((1, w), fixed), pl.BlockSpec((w, w), fixed), pl.BlockSpec((1, w), fixed)],
        out_specs=pl.BlockSpec((tc * t, w), row),
        out_shape=jax.ShapeDtypeStruct((l, w), BF16),
        scratch_shapes=[pltpu.VMEM((nb, tc * t, LANES), F32)],
        compiler_params=_cparams(("parallel",)),
        name="s5_readout",
    )(y_ch, u_lat, d_skip.reshape(1, w), glu_w.astype(BF16), glu_b.reshape(1, w))


def _first_max(vals, lane):
    m = vals.max(axis=1, keepdims=True)
    idx = jnp.where(vals == m, lane, jnp.int32(1 << 20)).min(axis=1, keepdims=True)
    return m, idx


def _stream_specs(n_ctx_tiles, tm, d):
    return (pl.BlockSpec((tm, d), lambda i: (jnp.clip(i, 0, max(n_ctx_tiles - 1, 0)), 0)),
            pl.BlockSpec((tm, d), lambda i: (jnp.maximum(i - n_ctx_tiles, 0), 0)))


def _stream_tile(xc_ref, xl_ref, n_ctx_tiles):
    if n_ctx_tiles == 0:
        return xl_ref[...]
    return jnp.where(pl.program_id(0) < n_ctx_tiles, xc_ref[...], xl_ref[...])


def _store_row_slabs(ref, x):
    rows = x.shape[0]
    for s in range(ROW_SLAB):
        ref[pl.ds(s, rows, stride=ROW_SLAB), :] = x[:, s * LANES:(s + 1) * LANES]


def _router_kernel(xc_ref, xl_ref, sc_ref, sh_ref, rw_ref, rb_ref, tri_ref,
                   hf_ref, te_ref, gt_ref, rk_ref, cnt_ref, run_ref, *, n_ctx_tiles):
    @pl.when(pl.program_id(0) == 0)
    def _():
        run_ref[...] = jnp.zeros_like(run_ref)

    hf = _stream_tile(xc_ref, xl_ref, n_ctx_tiles) * (1.0 + sc_ref[...]) + sh_ref[...]
    _store_row_slabs(hf_ref, hf)
    tm = hf.shape[0]
    scores = jax.nn.sigmoid(_dot(hf, rw_ref[...], precision=HIGHEST))
    biased = scores + rb_ref[...]
    lane = lax.broadcasted_iota(I32, (tm, N_EXPERTS), 1)
    grp = lane // (N_EXPERTS // N_EXPERT_GROUPS)
    lane_o = lax.broadcasted_iota(I32, (tm, LANES), 1)
    neg = jnp.float32(-jnp.inf)

    group_score = jnp.full((tm, LANES), neg, F32)
    for g in range(N_EXPERT_GROUPS):
        vals = jnp.where(grp == g, biased, neg)
        m1, i1 = _first_max(vals, lane)
        m2 = jnp.where(lane == i1, neg, vals).max(axis=1, keepdims=True)
        group_score = jnp.where(lane_o == g, m1 + m2, group_score)
    keep = jnp.zeros((tm, N_EXPERTS), F32)
    for _ in range(TOPK_GROUPS):
        _, gi = _first_max(group_score, lane_o)
        keep = jnp.where(grp == gi, 1.0, keep)
        group_score = jnp.where(lane_o == gi, neg, group_score)

    masked = jnp.where(keep > 0.0, biased, neg)
    member = jnp.zeros((tm, N_EXPERTS), F32)
    e_cols, g_cols = [], []
    for _ in range(TOP_K):
        _, ei = _first_max(masked, lane)
        hit = lane == ei
        g_cols.append(jnp.where(hit, scores, 0.0).sum(axis=1, keepdims=True))
        masked = jnp.where(hit, neg, masked)
        member = jnp.where(hit, 1.0, member)
        e_cols.append(ei)
    g_sum = g_cols[0]
    for gk in g_cols[1:]:
        g_sum = g_sum + gk

    before = _dot(tri_ref[...], member.astype(BF16)) + run_ref[...]
    te = jnp.zeros((tm, LANES), I32)
    rk = jnp.zeros((tm, LANES), I32)
    gt = jnp.zeros((tm, LANES), F32)
    for k in range(TOP_K):
        rank = jnp.where(lane == e_cols[k], before, 0.0).sum(axis=1, keepdims=True)
        te = jnp.where(lane_o == k, e_cols[k], te)
        rk = jnp.where(lane_o == k, rank.astype(I32), rk)
        gt = jnp.where(lane_o == k, ROUTED_SCALE * g_cols[k] / g_sum, gt)
    te_ref[...] = te
    rk_ref[...] = rk
    gt_ref[...] = gt
    run_ref[...] += member.sum(axis=0, keepdims=True)
    cnt_ref[...] = run_ref[...]


def moe_route(x_ctx, x_lat, sc2, sh2, n_ctx_tiles, router_w, router_bias):
    d = x_lat.shape[1]
    tm = MOE_TILE
    n = n_ctx_tiles * tm + x_lat.shape[0]
    tri = (jnp.arange(tm)[None, :] < jnp.arange(tm)[:, None]).astype(BF16)
    row = lambda i: (i, 0)
    fixed = lambda i: (0, 0)
    seg = lambda i: (jnp.where(i < n_ctx_tiles, 1, 0), 0, 0)
    kern = functools.partial(_router_kernel, n_ctx_tiles=n_ctx_tiles)
    return pl.pallas_call(
        kern,
        grid=(n // tm,),
        in_specs=[*_stream_specs(n_ctx_tiles, tm, d), pl.BlockSpec((None, 1, d), seg), pl.BlockSpec((None, 1, d), seg),
                  pl.BlockSpec((d, N_EXPERTS), fixed), pl.BlockSpec((1, N_EXPERTS), fixed),
                  pl.BlockSpec((tm, tm), fixed)],
        out_specs=[pl.BlockSpec((tm * ROW_SLAB, LANES), row), pl.BlockSpec((tm, LANES), row),
                   pl.BlockSpec((tm, LANES), row), pl.BlockSpec((tm, LANES), row),
                   pl.BlockSpec((1, N_EXPERTS), fixed)],
        out_shape=[jax.ShapeDtypeStruct((n * ROW_SLAB, LANES), F32), jax.ShapeDtypeStruct((n, LANES), I32),
                   jax.ShapeDtypeStruct((n, LANES), F32), jax.ShapeDtypeStruct((n, LANES), I32),
                   jax.ShapeDtypeStruct((1, N_EXPERTS), F32)],
        scratch_shapes=[pltpu.VMEM((1, N_EXPERTS), F32)],
        compiler_params=_cparams(("arbitrary",)),
        name="moe_route",
    )(x_ctx, x_lat, sc2, sh2, router_w, router_bias.reshape(1, N_EXPERTS), tri)


def _slots_kernel(te_ref, rk_ref, start_ref, o_ref):
    tm = te_ref.shape[0]
    lane = lax.broadcasted_iota(I32, (tm, N_EXPERTS), 1)
    lane_o = lax.broadcasted_iota(I32, (tm, LANES), 1)
    te = te_ref[...]
    out = rk_ref[...]
    for k in range(TOP_K):
        first = jnp.where(lane == te[:, k:k + 1], start_ref[...], 0.0).sum(axis=1, keepdims=True)
        out = jnp.where(lane_o == k, out + first.astype(I32), out)
    o_ref[...] = out


def moe_slots(te, rk, start):
    n = te.shape[0]
    tm = MOE_TILE
    row = lambda i: (i, 0)
    out = pl.pallas_call(
        _slots_kernel,
        grid=(n // tm,),
        in_specs=[pl.BlockSpec((tm, LANES), row), pl.BlockSpec((tm, LANES), row),
                  pl.BlockSpec((1, N_EXPERTS), lambda i: (0, 0))],
        out_specs=pl.BlockSpec((tm, LANES), row),
        out_shape=jax.ShapeDtypeStruct((n, LANES), I32),
        compiler_params=_cparams(("parallel",)),
        name="moe_slots",
    )(te, rk, start.astype(F32).reshape(1, N_EXPERTS))
    return out[:, :TOP_K].reshape(-1)


def _slab_rows(ref, row, n_rows):
    first = row * ROW_SLAB
    if not isinstance(first, int):
        first = pl.multiple_of(first, ROW_SLAB)
    return ref.at[pl.ds(first, n_rows * ROW_SLAB), :]


def _dispatch_kernel(dest_ref, hf_ref, xs_hbm, zbuf, sem, zsem, *, n_assign):
    n_rows = dest_ref.shape[0]

    @pl.when(pl.program_id(0) == 0)
    def _():
        zbuf[...] = jnp.zeros_like(zbuf)
        tail = pltpu.make_async_copy(zbuf, _slab_rows(xs_hbm, n_assign, EXPERT_BLOCK), zsem)
        tail.start()
        tail.wait()

    def body(r, carry):
        src = _slab_rows(hf_ref, r, 1)
        for k in range(TOP_K):
            pltpu.make_async_copy(src, _slab_rows(xs_hbm, dest_ref[r * TOP_K + k], 1), sem).start(priority=k % 2)
        return carry
    lax.fori_loop(0, n_rows // TOP_K, body, 0)
    for _ in range(TOP_K):
        pltpu.make_async_copy(hf_ref, hf_ref, sem).wait()


def moe_dispatch(dest, hf_slabs):
    n_assign = dest.shape[0]
    tm = MOE_TILE
    n_rows = tm * TOP_K
    kern = functools.partial(_dispatch_kernel, n_assign=n_assign)
    return pl.pallas_call(
        kern,
        grid=(n_assign // n_rows,),
        in_specs=[pl.BlockSpec((n_rows,), lambda i: (i,), memory_space=pltpu.SMEM),
                  pl.BlockSpec((tm * ROW_SLAB, LANES), lambda i: (i, 0))],
        out_specs=pl.BlockSpec(memory_space=pl.ANY),
        out_shape=jax.ShapeDtypeStruct(((n_assign + EXPERT_BLOCK) * ROW_SLAB, LANES), F32),
        scratch_shapes=[pltpu.VMEM((EXPERT_BLOCK * ROW_SLAB, LANES), F32), pltpu.SemaphoreType.DMA(()),
                        pltpu.SemaphoreType.DMA(())],
        compiler_params=_cparams(("arbitrary",)),
        name="moe_dispatch",
    )(dest, hf_slabs)


def _row_gather(idx_ref, base, count, src_hbm, dst, sem):
    group = 8

    def body(g, carry):
        for k in range(group):
            j = g * group + k
            cp = pltpu.make_async_copy(_slab_rows(src_hbm, idx_ref[base + j], 1), _slab_rows(dst, j, 1), sem)
            cp.start(priority=k % 2)
        return carry
    lax.fori_loop(0, count // group, body, 0)


def _wait_rows(dst, sem):
    pltpu.make_async_copy(dst, dst, sem).wait()


def _gathered_rows(buf, first, rows, stride):
    return jnp.concatenate(
        [buf[pl.ds(first * ROW_SLAB + s, rows, stride=stride * ROW_SLAB), :] for s in range(ROW_SLAB)], axis=1)


def _swiglu(x, wg, wu, wd):
    gate = _dot(x, wg)
    up = _dot(x, wu)
    return _dot((gate * jax.nn.sigmoid(gate) * up).astype(BF16), wd)


def _valid_row_copies(ybuf, ys_hbm, row0, valid, sem):
    out = [(valid == EXPERT_BLOCK,
            pltpu.make_async_copy(_slab_rows(ybuf, 0, EXPERT_BLOCK), _slab_rows(ys_hbm, row0, EXPERT_BLOCK), sem))]
    part = valid < EXPERT_BLOCK
    size = EXPERT_BLOCK // 2
    while size >= 1:
        off = valid & ~(2 * size - 1)
        out.append((part & ((valid & size) != 0),
                    pltpu.make_async_copy(_slab_rows(ybuf, off, size), _slab_rows(ys_hbm, row0 + off, size), sem)))
        size //= 2
    return out


def _expert_kernel(be_ref, r0_ref, nv_ref, ws_ref, nx_ref, e1_ref, xs_hbm, wg_hbm, wu_hbm, wd_hbm, ys_hbm,
                   xbuf, ybuf, wg_st, wu_st, wd_st, wg_bf, wu_bf, wd_bf, sem_in, sem_out, sem_w, *, layer):
    b = pl.program_id(0)
    n_blocks = pl.num_programs(0)
    slot = b % 2

    def weight_copies(e, ws):
        return [pltpu.make_async_copy(hbm.at[layer, e], st.at[ws], sem_w.at[ws])
                for hbm, st in ((wg_hbm, wg_st), (wu_hbm, wu_st), (wd_hbm, wd_st))]

    @pl.when(b == 0)
    def _():
        for cp in weight_copies(be_ref[0], 0):
            cp.start()

        @pl.when(e1_ref[0] >= 0)
        def _():
            for cp in weight_copies(e1_ref[0], 1):
                cp.start()

    @pl.when(ws_ref[b] >= 0)
    def _():
        ws = ws_ref[b]
        for cp in weight_copies(be_ref[b], ws):
            cp.wait()
        wg_bf[...] = wg_st[ws].astype(BF16)
        wu_bf[...] = wu_st[ws].astype(BF16)
        wd_bf[...] = wd_st[ws].astype(BF16)

        @pl.when(nx_ref[b] >= 0)
        def _():
            for cp in weight_copies(nx_ref[b], (ws + EXPERT_WEIGHT_SLOTS - 1) % EXPERT_WEIGHT_SLOTS):
                cp.start()

    def fetch(blk, sl):
        return pltpu.make_async_copy(_slab_rows(xs_hbm, r0_ref[blk], EXPERT_BLOCK), xbuf.at[sl], sem_in.at[sl])

    def drain(blk, sl):
        for cond, cp in _valid_row_copies(ybuf.at[sl], ys_hbm, r0_ref[blk], nv_ref[blk], sem_out.at[sl]):
            pl.when(cond)(cp.wait)

    @pl.when(b == 0)
    def _():
        for k in range(EXPERT_LOOKAHEAD):
            pl.when(nv_ref[k] > 0)(fetch(k, k).start)

    nxt = jnp.minimum(b + EXPERT_LOOKAHEAD, n_blocks - 1)

    @pl.when((b + EXPERT_LOOKAHEAD < n_blocks) & (nv_ref[nxt] > 0))
    def _():
        fetch(nxt, nxt % (EXPERT_LOOKAHEAD + 1)).start()

    @pl.when(b >= 2)
    def _():
        drain(jnp.maximum(b - 2, 0), slot)

    @pl.when(nv_ref[b] > 0)
    def _():
        xslot = b % (EXPERT_LOOKAHEAD + 1)
        fetch(b, xslot).wait()
        x = _gathered_rows(xbuf.at[xslot], 0, EXPERT_BLOCK, 1).astype(BF16)
        y = _swiglu(x, wg_bf[...], wu_bf[...], wd_bf[...])
        _store_row_slabs(ybuf.at[slot], y)
        for cond, cp in _valid_row_copies(ybuf.at[slot], ys_hbm, r0_ref[b], nv_ref[b], sem_out.at[slot]):
            pl.when(cond)(cp.start)

    @pl.when(b == n_blocks - 1)
    def _():
        drain(jnp.maximum(b - 1, 0), 1 - slot)
        drain(b, slot)


def moe_experts(xs_slabs, block_e, block_row0, block_valid, block_wslot, block_next_e, second_e,
                w_gate, w_up, w_down, layer):
    n_blocks = block_e.shape[0]
    d, ff = w_gate.shape[2:]
    blk_rows = EXPERT_BLOCK * ROW_SLAB
    n_assign = xs_slabs.shape[0] // ROW_SLAB - EXPERT_BLOCK
    any_spec = pl.BlockSpec(memory_space=pl.ANY)
    grid_spec = pltpu.PrefetchScalarGridSpec(
        num_scalar_prefetch=6,
        grid=(n_blocks,),
        in_specs=[any_spec] * 4,
        out_specs=any_spec,
        scratch_shapes=[pltpu.VMEM((EXPERT_LOOKAHEAD + 1, blk_rows, LANES), F32), pltpu.VMEM((2, blk_rows, LANES), F32),
                        pltpu.VMEM((EXPERT_WEIGHT_SLOTS, d, ff), F32), pltpu.VMEM((EXPERT_WEIGHT_SLOTS, d, ff), F32),
                        pltpu.VMEM((EXPERT_WEIGHT_SLOTS, ff, d), F32),
                        pltpu.VMEM((d, ff), BF16), pltpu.VMEM((d, ff), BF16), pltpu.VMEM((ff, d), BF16),
                        pltpu.SemaphoreType.DMA((EXPERT_LOOKAHEAD + 1,)), pltpu.SemaphoreType.DMA((2,)),
                        pltpu.SemaphoreType.DMA((EXPERT_WEIGHT_SLOTS,))],
    )
    return pl.pallas_call(
        functools.partial(_expert_kernel, layer=layer),
        grid_spec=grid_spec,
        out_shape=jax.ShapeDtypeStruct((n_assign * ROW_SLAB, LANES), F32),
        compiler_params=_cparams(("arbitrary",)),
        name="moe_experts",
    )(block_e, block_row0, block_valid, block_wslot, block_next_e, second_e, xs_slabs, w_gate, w_up, w_down)


def _combine_kernel(cur_ref, nxt_ref, ys_hbm, xc_ref, xl_ref, hf_ref, gt_ref, g2_ref, sg_ref, su_ref, sd_ref,
                    lg_ref, lb_ref, *rest, n_ctx_tiles):
    *o_refs, ybuf, gate_ref, routed_ref, sem = rest
    i = pl.program_id(0)
    n_tiles = pl.num_programs(0)
    tm = xl_ref.shape[0]
    n_rows = tm * TOP_K
    slot = i % 2

    @pl.when(i == 0)
    def _():
        _row_gather(cur_ref, 0, n_rows, ys_hbm, ybuf.at[0], sem.at[0])

    @pl.when(i + 1 < n_tiles)
    def _():
        _row_gather(nxt_ref, 0, n_rows, ys_hbm, ybuf.at[1 - slot], sem.at[1 - slot])

    hf = _gathered_rows(hf_ref, 0, tm, 1).astype(BF16)
    y = _swiglu(hf, sg_ref[...], su_ref[...], sd_ref[...])

    _wait_rows(ybuf.at[slot], sem.at[slot])
    for k in range(TOP_K):
        gate_ref[k] = jnp.broadcast_to(gt_ref[:, k:k + 1], (tm, LANES))
    cur = ybuf.at[slot]

    def weigh(r, carry):
        acc = jnp.zeros((ROW_SLAB, LANES), F32)
        for k in range(TOP_K):
            g = jnp.broadcast_to(gate_ref[k, pl.ds(r, 1), :], (ROW_SLAB, LANES))
            acc = acc + g * _slab_rows(cur, r * TOP_K + k, 1)[...]
        _slab_rows(routed_ref, r, 1)[...] = acc
        return carry
    lax.fori_loop(0, tm, weigh, 0, unroll=2)
    y = y + _gathered_rows(routed_ref, 0, tm, 1)
    r = DEEPNORM_ALPHA * _stream_tile(xc_ref, xl_ref, n_ctx_tiles) + g2_ref[...] * y
    res = _layer_norm_rows(r, lg_ref[...], lb_ref[...])
    if n_ctx_tiles == 0:
        o_refs[0][...] = res
    else:
        oc_ref, ol_ref = o_refs

        @pl.when(i < n_ctx_tiles)
        def _():
            oc_ref[...] = res

        @pl.when(i >= n_ctx_tiles)
        def _():
            ol_ref[...] = res


def moe_combine(dest, ys_slabs, x_ctx, x_lat, hf_slabs, gate, g2, n_ctx_tiles, sh_gate, sh_up, sh_down, ln_g, ln_b):
    d = x_lat.shape[1]
    tm = MOE_TILE
    n_tiles = n_ctx_tiles + x_lat.shape[0] // tm
    n_rows = tm * TOP_K
    ff = sh_gate.shape[1]
    row = lambda i: (i, 0)
    fixed = lambda i: (0, 0)
    seg = lambda i: (jnp.where(i < n_ctx_tiles, 1, 0), 0, 0)
    ctx_spec, lat_spec = _stream_specs(n_ctx_tiles, tm, d)
    lat_out = jax.ShapeDtypeStruct(x_lat.shape, F32)
    if n_ctx_tiles == 0:
        out_specs, out_shape = [lat_spec], [lat_out]
    else:
        out_specs, out_shape = [ctx_spec, lat_spec], [jax.ShapeDtypeStruct(x_ctx.shape, F32), lat_out]
    kern = functools.partial(_combine_kernel, n_ctx_tiles=n_ctx_tiles)
    return pl.pallas_call(
        kern,
        grid=(n_tiles,),
        in_specs=[pl.BlockSpec((n_rows,), lambda i: (i,), memory_space=pltpu.SMEM),
                  pl.BlockSpec((n_rows,), lambda i: (jnp.minimum(i + 1, n_tiles - 1),), memory_space=pltpu.SMEM),
                  pl.BlockSpec(memory_space=pl.ANY),
                  ctx_spec, lat_spec, pl.BlockSpec((tm * ROW_SLAB, LANES), row), pl.BlockSpec((tm, LANES), row),
                  pl.BlockSpec((None, 1, d), seg),
                  pl.BlockSpec((d, ff), fixed), pl.BlockSpec((d, ff), fixed), pl.BlockSpec((ff, d), fixed),
                  pl.BlockSpec((1, d), fixed), pl.BlockSpec((1, d), fixed)],
        out_specs=out_specs,
        out_shape=out_shape,
        scratch_shapes=[pltpu.VMEM((2, n_rows * ROW_SLAB, LANES), F32), pltpu.VMEM((TOP_K, tm, LANES), F32),
                        pltpu.VMEM((tm * ROW_SLAB, LANES), F32), pltpu.SemaphoreType.DMA((2,))],
        compiler_params=_cparams(("arbitrary",)),
        name="moe_combine",
    )(dest, dest, ys_slabs, x_ctx, x_lat, hf_slabs, gate, g2,
      sh_gate.astype(BF16), sh_up.astype(BF16), sh_down.astype(BF16), ln_g.reshape(1, d), ln_b.reshape(1, d))


def moe_layer(x_ctx, x_lat, sc2, sh2, g2, n_ctx_tiles, layer, router_w, router_bias, w_gate, w_up, w_down,
              sh_gate, sh_up, sh_down, ln_g, ln_b):
    n = n_ctx_tiles * MOE_TILE + x_lat.shape[0]
    hf, te, gt, rk, cnt = moe_route(x_ctx, x_lat, sc2, sh2, n_ctx_tiles, router_w, router_bias)
    counts = cnt[0].astype(I32)
    start = jnp.cumsum(counts) - counts
    experts = jnp.arange(N_EXPERTS, dtype=I32)
    dest = moe_slots(te, rk, start)
    nb = (counts + EXPERT_BLOCK - 1) // EXPERT_BLOCK
    blk_end = jnp.cumsum(nb)
    blk_start = blk_end - nb
    n_blocks = n * TOP_K // EXPERT_BLOCK + N_EXPERTS
    blk = jnp.arange(n_blocks, dtype=I32)
    bb = jnp.minimum(blk, blk_end[-1] - 1)[:, None]
    own = (blk_start[None, :] <= bb) & (bb < blk_end[None, :])
    sel = lambda v: jnp.sum(jnp.where(own, v[None, :], 0), axis=1)
    j = bb[:, 0] - sel(blk_start)
    block_e = sel(experts)
    block_row0 = sel(start) + j * EXPERT_BLOCK
    block_valid = jnp.where(blk < blk_end[-1], jnp.clip(sel(counts) - j * EXPERT_BLOCK, 0, EXPERT_BLOCK), 0)
    used = nb > 0
    ordinal = jnp.cumsum(used.astype(I32)) - 1
    later = jnp.where(used[None, :] & (experts[None, :] > experts[:, None]), experts[None, :], N_EXPERTS)
    next1 = jnp.min(later, axis=1)
    next2 = jnp.min(jnp.where(later > next1[:, None], later, N_EXPERTS), axis=1)
    none = lambda e: jnp.where(e < N_EXPERTS, e, -1)
    first = (blk < blk_end[-1]) & (j == 0)
    block_wslot = jnp.where(first, sel(ordinal) % EXPERT_WEIGHT_SLOTS, -1)
    block_next_e = jnp.where(first, sel(none(next2)), -1)
    second_e = jnp.sum(jnp.where(experts == block_e[0], none(next1), 0)).reshape(1)
    xs = moe_dispatch(dest, hf)
    ys = moe_experts(xs, block_e, block_row0, block_valid, block_wslot, block_next_e, second_e,
                     w_gate, w_up, w_down, layer)
    return moe_combine(dest, ys, x_ctx, x_lat, hf, gt, g2, n_ctx_tiles, sh_gate, sh_up, sh_down, ln_g, ln_b)


def kernel(x, c, ctx, c_ctx, w_mod, b_mod, ln_mix_g, ln_mix_b, ln_ffn_g, ln_ffn_b, ab_w_in, ab_w_out, na_rpb,
           hy_conv_w, hy_conv_b, hy_f_w1, hy_f_b1, hy_f_freq, hy_f_w2, hy_f_b2, hy_f_w3, hy_skip, cd_w_in, cd_w_out,
           q_norm_g, k_norm_g, s5_a_re, s5_a_im, s5_log_dt, s5_b_re, s5_b_im, s5_c_re, s5_c_im, s5_d, s5_glu_w,
           s5_glu_b, router_w, router_bias, exp_w_gate, exp_w_up, exp_w_down, sh_w_gate, sh_w_up, sh_w_down):
    b, l, d = x.shape
    assert b == 1
    n_ctx = ctx.shape[1]
    assert n_ctx == MOE_TILE
    xs = x[0]
    cs = ctx[0]
    cmat = jnp.zeros((SUBLANES, d), F32).at[0].set(c[0]).at[1].set(c_ctx)
    mods = modulation_all(cmat, w_mod, b_mod).reshape(DEPTH, SUBLANES, 6, d)
    qscale = HEAD_DIM ** -0.5

    for i in range(DEPTH):
        need_ctx = i < DEPTH - 1
        m = mods[i]
        sh1, sc1, g1, sh2, sc2, g2 = [m[0:1, t] for t in range(6)]
        csh1, csc1, cg1, csh2, csc2, cg2 = [m[1:2, t] for t in range(6)]
        j = i // 2
        if i % 2 == 0:
            filt = (hy_conv_w[j], hy_conv_b[j], hy_f_w1[j], hy_f_b1[j], hy_f_freq[j], hy_f_w2[j], hy_f_b2[j],
                    hy_f_w3[j], hy_skip[j])
            splits = (NA_WIDTH, NA_WIDTH, NA_WIDTH, 3 * HY_WIDTH)
            dts = (BF16, BF16, BF16, F32)
            scl = (qscale, 1.0, 1.0, 1.0)
            q_l, k_l, v_l, u_l = mod_project(xs, sc1, sh1, ab_w_in[j], splits, dts, scl)
            q_c, k_c, v_c, u_c = mod_project(cs, csc1, csh1, ab_w_in[j], splits, dts, scl)
            a_lat = neighbourhood_attention(q_l, k_l, v_l, k_c, v_c, na_rpb[j])
            y_hy = hyena_long(u_l, *filt)
            xs_new = outproj_ln(a_lat, y_hy, ab_w_out[j], xs, g1, ln_mix_g[i], ln_mix_b[i])
            if need_ctx:
                a_ctx = context_attention(q_c, k_c, v_c)
                yc_hy = hyena_long(u_c, *filt)
                cs = outproj_ln(a_ctx, yc_hy, ab_w_out[j], cs, cg1, ln_mix_g[i], ln_mix_b[i])
            xs = xs_new
        else:
            splits = (GQA_WIDTH, GQA_KV_WIDTH, GQA_KV_WIDTH, S5_WIDTH)
            q_l, k_l, v_l, u_l = mod_project(xs, sc1, sh1, cd_w_in[j], splits, (F32, F32, BF16, F32))
            k_c, v_c, u_c = mod_project(cs, csc1, csh1, cd_w_in[j][:, GQA_WIDTH:], splits[1:], (F32, BF16, F32))
            tabs = _rope_tables(l)
            qn_t = qk_prep(q_l, q_norm_g[j], tabs, qscale * math.log2(math.e), transposed=True)
            kn = qk_prep(k_l, k_norm_g[j], tabs, 1.0)
            kcn = qk_prep(k_c, k_norm_g[j], None, 1.0)
            k_all = jnp.concatenate([kn, kcn], axis=0)
            v_all = jnp.concatenate([v_l, v_c], axis=0)
            k_hm = k_all.reshape(-1, GQA_KV_HEADS, HEAD_DIM).transpose(1, 0, 2)
            v_t = v_all.T.reshape(GQA_KV_HEADS, HEAD_DIM, -1)
            att = gqa_attention(qn_t, k_hm, v_t)
            ssm = s5_mix(u_c, u_l, s5_a_re[j], s5_a_im[j], s5_log_dt[j], s5_b_re[j], s5_b_im[j],
                         s5_c_re[j], s5_c_im[j], s5_d[j], s5_glu_w[j], s5_glu_b[j])
            xs = outproj_ln(att, ssm, cd_w_out[j], xs, g1, ln_mix_g[i], ln_mix_b[i])
            assert not need_ctx

        moe_w = (i, router_w[i], router_bias[i], exp_w_gate, exp_w_up, exp_w_down,
                 sh_w_gate[i], sh_w_up[i], sh_w_down[i], ln_ffn_g[i], ln_ffn_b[i])
        stack2 = lambda lat, cx: jnp.stack([lat, cx])
        mod2 = (stack2(sc2, csc2), stack2(sh2, csh2), stack2(g2, cg2))
        if need_ctx:
            cs, xs = moe_layer(cs, xs, *mod2, n_ctx // MOE_TILE, *moe_w)
        else:
            (xs,) = moe_layer(xs, xs, *mod2, 0, *moe_w)
    return xs.reshape(b, l, d)
```
